```python
import jax
import jax.numpy as jnp
from jax import lax
import numpy as np

D_MODEL = 1024
BATCH = 4
SEQ = 8192
DEPTH = 2

D_ATTN = D_MODEL // 2
N_ATTN_HEADS = 8
HEAD_DIM = D_ATTN // N_ATTN_HEADS
N_KV = 2
HEADS_PER_KV = N_ATTN_HEADS // N_KV
D_REC = D_MODEL - D_ATTN
N_REC_BLOCKS = 8
REC_BW = D_REC // N_REC_BLOCKS
L_CMP = 32
STRIDE = 16
L_SLC = 64
N_SEL = 16
N_LOCAL = 2
W_WIN = 512
CMP_HID = 2 * HEAD_DIM
NSA_QB = 64
R_CMP = L_CMP // STRIDE
R_SLC = L_SLC // STRIDE
N_GATES = 3
CONV_W = 4
C_LRU = 8.0
D_FF = 2816
N_EXPERTS = 8
TOP_K = 2
D_FF_EXPERT = 3584
MOE_BLOCK = 256
N_DENSE = (DEPTH + 1) // 2
N_MOE = DEPTH // 2
PLE_DIM = 256
RMS_EPS = 1e-6
NEG_INF = -1e30
KV_W = N_KV * HEAD_DIM
IN_SPLITS = (D_ATTN, KV_W, KV_W, KV_W, KV_W, KV_W, KV_W, N_ATTN_HEADS * N_GATES, D_REC, D_REC)
IN_COLS = sum(IN_SPLITS)

kernel_name = 'hybrid_nsa_rglru_moe_ple'


def rmsnorm(x, g):
    x32 = x.astype(jnp.float32)
    y = x32 * lax.rsqrt(jnp.mean(x32 * x32, axis=-1, keepdims=True) + RMS_EPS)
    return (y * g.astype(jnp.float32)).astype(x.dtype)


def alibi_slopes(n):
    return 2.0 ** (-8.0 * jnp.arange(1, n + 1, dtype=jnp.float32) / n)


def masked_softmax(s, mask):
    s = jnp.where(mask, s, NEG_INF)
    e = jnp.where(mask, jnp.exp(s - jnp.max(s, axis=-1, keepdims=True)), 0.0)
    return e / jnp.maximum(jnp.sum(e, axis=-1, keepdims=True), 1e-30)


def compress_blocks(kv, pos, w1, w2):
    b_, g_, t_, _ = kv.shape
    n_chunk = t_ // STRIDE
    n_cmp = n_chunk - R_CMP + 1
    chunks = kv.reshape(b_, g_, n_chunk, STRIDE, HEAD_DIM)
    blocks = jnp.concatenate([chunks[:, :, j:j + n_cmp] for j in range(R_CMP)], axis=3)
    flat = (blocks + pos).reshape(b_, g_, n_cmp, L_CMP * HEAD_DIM)
    return jax.nn.gelu(flat @ w1) @ w2


def nsa_attention(q, k_cmp, v_cmp, k_slc, v_slc, k_win, v_win, gates, cmp_pos, cmp_w1, cmp_w2):
    b_, t_ = q.shape[0], q.shape[1]
    n_qb = t_ // NSA_QB
    n_slc = t_ // L_SLC
    n_chunk = t_ // STRIDE
    k_top = min(N_SEL, n_slc)
    scale = HEAD_DIM ** -0.5
    grp = lambda a: a.transpose(0, 2, 1, 3)
    kc = compress_blocks(grp(k_cmp), cmp_pos[0], cmp_w1[0], cmp_w2[0])
    vc = compress_blocks(grp(v_cmp), cmp_pos[1], cmp_w1[1], cmp_w2[1])
    n_cmp = kc.shape[2]
    cmp_end = jnp.arange(n_cmp) * STRIDE + (L_CMP - 1)
    ks_blocks = grp(k_slc).reshape(b_, N_KV, n_slc, L_SLC, HEAD_DIM)
    vs_blocks = grp(v_slc).reshape(b_, N_KV, n_slc, L_SLC, HEAD_DIM)
    win_pad = ((0, 0), (0, 0), (W_WIN, 0), (0, 0))
    kw_pad = jnp.pad(grp(k_win), win_pad)
    vw_pad = jnp.pad(grp(v_win), win_pad)
    qg = q.reshape(b_, t_, N_KV, HEADS_PER_KV, HEAD_DIM).transpose(0, 2, 3, 1, 4)
    gg = jax.nn.sigmoid(gates.astype(jnp.float32)).reshape(b_, t_, N_KV, HEADS_PER_KV, N_GATES).transpose(0, 2, 3, 1, 4)
    q_blocks = jnp.moveaxis(qg.reshape(b_, N_KV, HEADS_PER_KV, n_qb, NSA_QB, HEAD_DIM), 3, 0)
    g_blocks = jnp.moveaxis(gg.reshape(b_, N_KV, HEADS_PER_KV, n_qb, NSA_QB, N_GATES), 3, 0)
    slopes = alibi_slopes(N_ATTN_HEADS).reshape(N_KV, HEADS_PER_KV, 1, 1)
    gather_blocks = jax.vmap(jax.vmap(lambda blocks, idx: blocks[idx]))
    blk_ids = jnp.arange(n_slc)
    win_offsets = jnp.arange(W_WIN + NSA_QB)

    def block_fn(args):
        qb, gb, qi = args
        s0 = qi * NSA_QB
        t = s0 + jnp.arange(NSA_QB)
        dist_c = t[:, None] - cmp_end[None, :]
        s_c = jnp.einsum('bghqd,bgcd->bghqc', qb, kc).astype(jnp.float32) * scale - slopes * dist_c.astype(jnp.float32)
        p_c = masked_softmax(s_c, dist_c >= 0)
        o_c = jnp.einsum('bghqc,bgcd->bghqd', p_c.astype(vc.dtype), vc)
        p_grp = jnp.pad(jnp.sum(p_c, axis=2), ((0, 0), (0, 0), (0, 0), (R_CMP - 1, R_CMP - 1)))
        p_chunk = sum(p_grp[..., R_CMP - 1 - n:R_CMP - 1 - n + n_chunk] for n in range(R_CMP))
        p_slc = p_chunk.reshape(b_, N_KV, NSA_QB, n_slc, R_SLC).sum(axis=-1)
        cur = t // L_SLC
        valid = blk_ids[None, :] <= cur[:, None]
        forced = valid & ((blk_ids[None, :] == 0) | (blk_ids[None, :] > cur[:, None] - N_LOCAL))
        score = jnp.where(forced, jnp.inf, jnp.where(valid, p_slc, -jnp.inf))
        _, sel = lax.top_k(score, k_top)
        ks_sel = gather_blocks(ks_blocks, sel)
        vs_sel = gather_blocks(vs_blocks, sel)
        pos_s = sel[..., None] * L_SLC + jnp.arange(L_SLC)
        dist_s = t[:, None, None] - pos_s
        mask_s = (sel <= cur[:, None])[..., None] & (dist_s >= 0)
        s_s = jnp.einsum('bghqd,bgqkld->bghqkl', qb, ks_sel).astype(jnp.float32) * scale - slopes[..., None] * dist_s[:, :, None].astype(jnp.float32)
        n_keys = k_top * L_SLC
        p_s = masked_softmax(s_s.reshape(b_, N_KV, HEADS_PER_KV, NSA_QB, n_keys), mask_s.reshape(b_, N_KV, 1, NSA_QB, n_keys))
        o_s = jnp.einsum('bghqn,bgqnd->bghqd', p_s.astype(vs_sel.dtype), vs_sel.reshape(b_, N_KV, NSA_QB, n_keys, HEAD_DIM))
        kwin = lax.dynamic_slice_in_dim(kw_pad, s0, W_WIN + NSA_QB, axis=2)
        vwin = lax.dynamic_slice_in_dim(vw_pad, s0, W_WIN + NSA_QB, axis=2)
        pos_w = s0 - W_WIN + win_offsets
        dist_w = t[:, None] - pos_w[None, :]
        mask_w = (pos_w[None, :] >= 0) & (dist_w >= 0) & (dist_w < W_WIN)
        s_w = jnp.einsum('bghqd,bgkd->bghqk', qb, kwin).astype(jnp.float32) * scale - slopes * dist_w.astype(jnp.float32)
        p_w = masked_softmax(s_w, mask_w)
        o_w = jnp.einsum('bghqk,bgkd->bghqd', p_w.astype(vwin.dtype), vwin)
        out = gb[..., 0:1] * o_c + gb[..., 1:2] * o_s + gb[..., 2:3] * o_w
        return out.astype(qb.dtype)

    out = lax.map(block_fn, (q_blocks, g_blocks, jnp.arange(n_qb)))
    out = jnp.moveaxis(out, 0, 3).reshape(b_, N_KV, HEADS_PER_KV, t_, HEAD_DIM)
    return out.transpose(0, 3, 1, 2, 4).reshape(b_, t_, D_ATTN)


def rglru_mixer(xr, yg, conv_w, conv_b, wa, ba, wx, bx, lam):
    b_, t_, c_ = xr.shape
    xc = lax.conv_general_dilated(xr, conv_w[:, None, :], window_strides=(1,), padding=[(CONV_W - 1, 0)],
                                  dimension_numbers=('NWC', 'WIO', 'NWC'), feature_group_count=c_) + conv_b
    xb = xc.reshape(b_, t_, N_REC_BLOCKS, REC_BW)
    r = jax.nn.sigmoid((jnp.einsum('btnc,ncd->btnd', xb, wa).reshape(b_, t_, c_) + ba).astype(jnp.float32))
    i = jax.nn.sigmoid((jnp.einsum('btnc,ncd->btnd', xb, wx).reshape(b_, t_, c_) + bx).astype(jnp.float32))
    log_a = -C_LRU * r * jax.nn.softplus(-lam.astype(jnp.float32))
    a = jnp.exp(log_a)
    b = jnp.sqrt(-jnp.expm1(2.0 * log_a)) * i * xc.astype(jnp.float32)

    def combine(left, right):
        a_l, b_l = left
        a_r, b_r = right
        return a_l * a_r, a_r * b_l + b_r

    _, h = lax.associative_scan(combine, (a, b), axis=1)
    return (h * jax.nn.gelu(yg.astype(jnp.float32))).astype(xr.dtype)


def dense_swiglu(h, wg, wu, wd):
    return (jax.nn.silu(h @ wg) * (h @ wu)) @ wd


def moe_swiglu(h, router_w, wg, wu, wd):
    n_tok, d_ = h.shape
    nk = n_tok * TOP_K
    n_blocks = -(-nk // MOE_BLOCK) + N_EXPERTS
    n_slots = n_blocks * MOE_BLOCK
    logits = (h @ router_w).astype(jnp.float32)
    top_logit, top_e = lax.top_k(logits, TOP_K)
    gates = jax.nn.softmax(top_logit, axis=-1).astype(h.dtype)
    flat_e = top_e.reshape(-1)
    flat_tok = jnp.arange(nk, dtype=jnp.int32) // TOP_K
    flat_g = gates.reshape(-1)
    order = jnp.argsort(flat_e, stable=True)
    e_sorted = flat_e[order]
    counts = jnp.bincount(flat_e, length=N_EXPERTS)
    starts = jnp.cumsum(counts) - counts
    padded = (counts + MOE_BLOCK - 1) // MOE_BLOCK * MOE_BLOCK
    pad_ends = jnp.cumsum(padded)
    pad_starts = pad_ends - padded
    dest = pad_starts[e_sorted] + jnp.arange(nk) - starts[e_sorted]
    slot_tok = jnp.full((n_slots,), n_tok, jnp.int32).at[dest].set(flat_tok[order])
    slot_g = jnp.zeros((n_slots,), h.dtype).at[dest].set(flat_g[order])
    block_e = jnp.minimum(jnp.searchsorted(pad_ends, jnp.arange(n_blocks) * MOE_BLOCK, side='right'), N_EXPERTS - 1)
    h_pad = jnp.concatenate([h, jnp.zeros((1, d_), h.dtype)], axis=0)
    xb = h_pad[slot_tok].reshape(n_blocks, MOE_BLOCK, d_)

    def expert_block(args):
        xblk, e = args
        return (jax.nn.silu(xblk @ wg[e]) * (xblk @ wu[e])) @ wd[e]

    yb = lax.map(expert_block, (xb, block_e)).reshape(n_slots, d_)
    y = jax.ops.segment_sum(yb * slot_g[:, None], slot_tok, num_segments=n_tok + 1)
    return y[:n_tok]


def setup_inputs(seed: int = 0) -> dict:
    key = jax.random.key(seed)
    k = jax.random.split(key, 32)
    f32 = jnp.float32
    nrm = lambda kk, shape, fan_in: jax.random.normal(kk, shape, f32) * (fan_in ** -0.5)
    gain = lambda kk, shape: 1.0 + 0.05 * jax.random.normal(kk, shape, f32)
    small = lambda kk, shape: 0.01 * jax.random.normal(kk, shape, f32)
    a0 = jax.random.uniform(k[13], (DEPTH, D_REC), f32, 0.9, 0.999)
    sig = a0 ** (1.0 / C_LRU)
    lru_lambda = jnp.log(sig) - jnp.log1p(-sig)
    return {
        'x': jax.random.normal(k[0], (BATCH, SEQ, D_MODEL), f32),
        'p': jax.random.normal(k[1], (DEPTH, BATCH, SEQ, PLE_DIM), f32),
        'attn_norm': gain(k[2], (DEPTH, D_MODEL)),
        'w_in': nrm(k[3], (DEPTH, D_MODEL, IN_COLS), D_MODEL),
        'cmp_pos': 0.02 * jax.random.normal(k[4], (DEPTH, 2, L_CMP, HEAD_DIM), f32),
        'cmp_w1': nrm(k[5], (DEPTH, 2, L_CMP * HEAD_DIM, CMP_HID), L_CMP * HEAD_DIM),
        'cmp_w2': nrm(k[6], (DEPTH, 2, CMP_HID, HEAD_DIM), CMP_HID),
        'conv_w': nrm(k[7], (DEPTH, CONV_W, D_REC), CONV_W),
        'conv_b': small(k[8], (DEPTH, D_REC)),
        'lru_wa': nrm(k[9], (DEPTH, N_REC_BLOCKS, REC_BW, REC_BW), REC_BW),
        'lru_ba': small(k[10], (DEPTH, D_REC)),
        'lru_wx': nrm(k[11], (DEPTH, N_REC_BLOCKS, REC_BW, REC_BW), REC_BW),
        'lru_bx': small(k[12], (DEPTH, D_REC)),
        'lru_lambda': lru_lambda,
        'out_norm_attn': gain(k[14], (DEPTH, D_ATTN)),
        'out_norm_rec': gain(k[15], (DEPTH, D_REC)),
        'w_out': nrm(k[16], (DEPTH, D_MODEL, D_MODEL), D_MODEL),
        'ffn_norm': gain(k[17], (DEPTH, D_MODEL)),
        'dense_w_gate': nrm(k[18], (N_DENSE, D_MODEL, D_FF), D_MODEL),
        'dense_w_up': nrm(k[19], (N_DENSE, D_MODEL, D_FF), D_MODEL),
        'dense_w_down': nrm(k[20], (N_DENSE, D_FF, D_MODEL), D_FF),
        'router_w': nrm(k[21], (N_MOE, D_MODEL, N_EXPERTS), D_MODEL),
        'moe_w_gate': nrm(k[22], (N_MOE, N_EXPERTS, D_MODEL, D_FF_EXPERT), D_MODEL),
        'moe_w_up': nrm(k[23], (N_MOE, N_EXPERTS, D_MODEL, D_FF_EXPERT), D_MODEL),
        'moe_w_down': nrm(k[24], (N_MOE, N_EXPERTS, D_FF_EXPERT, D_MODEL), D_FF_EXPERT),
        'ple_norm': gain(k[25], (DEPTH, D_MODEL)),
        'ple_w_gate': nrm(k[26], (DEPTH, D_MODEL, D_MODEL), D_MODEL),
        'ple_w_proj': nrm(k[27], (DEPTH, PLE_DIM, D_MODEL), PLE_DIM),
        'final_norm': gain(k[28], (D_MODEL,)),
    }


def reference(x, p, attn_norm, w_in, cmp_pos, cmp_w1, cmp_w2, conv_w, conv_b, lru_wa, lru_ba, lru_wx, lru_bx,
              lru_lambda, out_norm_attn, out_norm_rec, w_out, ffn_norm, dense_w_gate, dense_w_up, dense_w_down,
              router_w, moe_w_gate, moe_w_up, moe_w_down, ple_norm, ple_w_gate, ple_w_proj, final_norm):
    b_, t_, _ = x.shape
    split_idx = np.cumsum(IN_SPLITS)[:-1].tolist()
    for i in range(DEPTH):
        h = rmsnorm(x, attn_norm[i])
        q, kc, vc, ks_, vs_, kw, vw, g, xr, yg = jnp.split(h @ w_in[i], split_idx, axis=-1)
        kvh = (b_, t_, N_KV, HEAD_DIM)
        o_attn = nsa_attention(q.reshape(b_, t_, N_ATTN_HEADS, HEAD_DIM), kc.reshape(kvh), vc.reshape(kvh),
                               ks_.reshape(kvh), vs_.reshape(kvh), kw.reshape(kvh), vw.reshape(kvh),
                               g.reshape(b_, t_, N_ATTN_HEADS, N_GATES), cmp_pos[i], cmp_w1[i], cmp_w2[i])
        o_rec = rglru_mixer(xr, yg, conv_w[i], conv_b[i], lru_wa[i], lru_ba[i], lru_wx[i], lru_bx[i], lru_lambda[i])
        mixed = jnp.concatenate([rmsnorm(o_attn, out_norm_attn[i]), rmsnorm(o_rec, out_norm_rec[i])], axis=-1)
        x = x + mixed @ w_out[i]
        h = rmsnorm(x, ffn_norm[i])
        j = i // 2
        if i % 2 == 0:
            y = dense_swiglu(h, dense_w_gate[j], dense_w_up[j], dense_w_down[j])
        else:
            y = moe_swiglu(h.reshape(b_ * t_, D_MODEL), router_w[j], moe_w_gate[j], moe_w_up[j],
                           moe_w_down[j]).reshape(b_, t_, D_MODEL)
        x = x + y
        gate = jax.nn.sigmoid(rmsnorm(x, ple_norm[i]) @ ple_w_gate[i])
        x = x + gate * (p[i] @ ple_w_proj[i])
    return rmsnorm(x, final_norm)
```

```python
import functools

import jax
import jax.numpy as jnp
from jax import lax
from jax.experimental import pallas as pl
from jax.experimental.pallas import tpu as pltpu

F32 = jnp.float32
BF16 = jnp.bfloat16
I32 = jnp.int32

N_ATTN_HEADS = 8
HEAD_DIM = 64
N_KV = 2
HEADS_PER_KV = N_ATTN_HEADS // N_KV
D_ATTN = N_ATTN_HEADS * HEAD_DIM
KV_W = N_KV * HEAD_DIM
N_GATES = 3
L_CMP = 32
STRIDE = 16
L_SLC = 64
N_SEL = 16
N_LOCAL = 2
W_WIN = 512
CMP_HID = 2 * HEAD_DIM
CONV_W = 4
C_LRU = 8.0
N_EXPERTS = 8
RMS_EPS = 1e-6
ATTN_SCALE = HEAD_DIM ** -0.5

V7X_LANES = 128
V7X_VMEM_LIMIT_BYTES = 56 * 1024 * 1024

TQ = 128
KT_SLC = 256
KT_WIN = 128
N_WIN_TILES = W_WIN // KT_WIN + 1
MASK_NEG = -1e30
SEL_NEG = -1.0e4
ROW_TILE = 512
RGLRU_CHUNK = 256
MOE_ROWS = 512
MOE_TOK_CHUNK = 512
CMB_ROWS = 256


def _cparams(semantics, vmem=None):
    return pltpu.CompilerParams(dimension_semantics=semantics, vmem_limit_bytes=vmem)


def _rms(x, g):
    ms = jnp.mean(x * x, axis=-1, keepdims=True)
    return x * lax.rsqrt(ms + RMS_EPS) * g


def _gelu_tanh(x):
    c = 0.7978845608028654
    return x * (0.5 * (1.0 + jnp.tanh(c * (x + 0.044715 * (x * x * x)))))


def _sigmoid(x):
    return 1.0 / (1.0 + jnp.exp(-x))


def _silu(x):
    return x * _sigmoid(x)


N_TOK_COLS = 2 * KV_W + 2 * KV_W + 2 * 512
N_TR_ROWS = D_ATTN + 2 * KV_W + 32


def _proj_in_kernel(x_ref, g_ref, wtok_ref, wtr_ref,
                    kcvc_ref, ks_ref, kw_ref, xr_ref, yg_ref, qT_ref, vsT_ref, vwT_ref, gT_ref):
    hn = _rms(x_ref[0], g_ref[...]).astype(BF16)
    tok = jnp.dot(hn, wtok_ref[...], preferred_element_type=F32)
    kcvc_ref[0] = tok[:, 0:256]
    ks_ref[0] = tok[:, 256:384].astype(BF16)
    kw_ref[0] = tok[:, 384:512].astype(BF16)
    xr_ref[0] = tok[:, 512:1024]
    yg_ref[0] = tok[:, 1024:1536]
    tr = lax.dot_general(wtr_ref[...], hn, (((1,), (1,)), ((), ())),
                         preferred_element_type=F32)
    qT_ref[0] = (tr[0:512] * ATTN_SCALE).astype(BF16)
    vsT_ref[0] = tr[512:640].astype(BF16)
    vwT_ref[0] = tr[640:768].astype(BF16)
    gT_ref[0] = _sigmoid(tr[768:800])


def _proj_in(x, norm_g, w_in):
    b_, t_, d_ = x.shape
    tm = min(ROW_TILE, t_)
    q, kc, vc, ks, vs, kw, vw, g, xr, yg = jnp.split(
        w_in, [512, 640, 768, 896, 1024, 1152, 1280, 1304, 1816], axis=1)
    wtok = jnp.concatenate([kc, vc, ks, kw, xr, yg], axis=1).astype(BF16)
    g4 = g.reshape(d_, N_KV, HEADS_PER_KV, N_GATES).transpose(0, 1, 3, 2)
    g4 = jnp.pad(g4.reshape(d_, N_KV, 12), ((0, 0), (0, 0), (0, 4))).reshape(d_, 32)
    wtr = jnp.concatenate([q, vs, vw, g4], axis=1).T.astype(BF16)
    nt = t_ // tm
    row = lambda shape: pl.BlockSpec((1, tm, shape), lambda b, i: (b, i, 0))
    col = lambda shape: pl.BlockSpec((1, shape, tm), lambda b, i: (b, 0, i))
    full = lambda a: pl.BlockSpec(a.shape, lambda b, i: (0,) * a.ndim)
    g2 = norm_g.reshape(1, d_)
    outs = pl.pallas_call(
        _proj_in_kernel,
        name="proj_in",
        grid=(b_, nt),
        in_specs=[row(d_), full(g2), full(wtok), full(wtr)],
        out_specs=[row(256), row(128), row(128), row(512), row(512),
                   col(512), col(128), col(128), col(32)],
        out_shape=[
            jax.ShapeDtypeStruct((b_, t_, 256), F32),
            jax.ShapeDtypeStruct((b_, t_, 128), BF16),
            jax.ShapeDtypeStruct((b_, t_, 128), BF16),
            jax.ShapeDtypeStruct((b_, t_, 512), F32),
            jax.ShapeDtypeStruct((b_, t_, 512), F32),
            jax.ShapeDtypeStruct((b_, 512, t_), BF16),
            jax.ShapeDtypeStruct((b_, 128, t_), BF16),
            jax.ShapeDtypeStruct((b_, 128, t_), BF16),
            jax.ShapeDtypeStruct((b_, 32, t_), F32),
        ],
        compiler_params=_cparams(("parallel", "parallel"), V7X_VMEM_LIMIT_BYTES),
    )(x, g2, wtok, wtr)
    return outs


def _compress_kernel(xc_ref, pa_ref, pb_ref, w1a_ref, w1b_ref, w2_ref, out_ref):
    xc = xc_ref[0].astype(BF16)
    n_chunk = xc.shape[0]
    a = jnp.dot(xc, w1a_ref[...], preferred_element_type=F32)
    bm = jnp.dot(xc, w1b_ref[...], preferred_element_type=F32)
    posc = (jnp.dot(pa_ref[...], w1a_ref[...], preferred_element_type=F32)
            + jnp.dot(pb_ref[...], w1b_ref[...], preferred_element_type=F32))[0:1]
    row = lax.broadcasted_iota(I32, bm.shape, 0)
    bm_up = jnp.where(row < n_chunk - 1, pltpu.roll(bm, n_chunk - 1, axis=0), 0.0)
    hid = _gelu_tanh(a + bm_up + posc).astype(BF16)
    out_ref[0] = jnp.dot(hid, w2_ref[...], preferred_element_type=F32)


def _compress(kcvc, cmp_pos, cmp_w1, cmp_w2):
    b_, t_, _ = kcvc.shape
    n_chunk = t_ // STRIDE
    xc = kcvc.reshape(b_, n_chunk, STRIDE * 256)
    eye = jnp.eye(2 * N_KV, dtype=F32)
    w1 = cmp_w1.reshape(2, 2, STRIDE, HEAD_DIM, CMP_HID)

    def expand(half):
        w = w1[:, half]
        full = jnp.einsum('wldo,wx,gy->lxgdwyo', w, jnp.eye(2, dtype=F32), jnp.eye(2, dtype=F32))
        return full.reshape(STRIDE * 256, 4 * CMP_HID).astype(BF16)

    del eye
    w1a, w1b = expand(0), expand(1)
    pos = cmp_pos.reshape(2, 2, STRIDE, HEAD_DIM)

    def posrow(half):
        p = jnp.broadcast_to(pos[:, half][:, None], (2, N_KV, STRIDE, HEAD_DIM))
        p = p.transpose(2, 0, 1, 3).reshape(1, STRIDE * 256)
        return jnp.pad(p, ((0, 7), (0, 0))).astype(BF16)

    pa, pb = posrow(0), posrow(1)
    w2 = jnp.einsum('whd,wx,gy->wghxyd', cmp_w2, jnp.eye(2, dtype=F32), jnp.eye(2, dtype=F32))
    w2 = w2.reshape(4 * CMP_HID, 4 * HEAD_DIM).astype(BF16)
    full = lambda a: pl.BlockSpec(a.shape, lambda b: (0,) * a.ndim)
    return pl.pallas_call(
        _compress_kernel,
        name="compress_kv",
        grid=(b_,),
        in_specs=[pl.BlockSpec((1, n_chunk, STRIDE * 256), lambda b: (b, 0, 0)),
                  full(pa), full(pb), full(w1a), full(w1b), full(w2)],
        out_specs=pl.BlockSpec((1, n_chunk, 256), lambda b: (b, 0, 0)),
        out_shape=jax.ShapeDtypeStruct((b_, n_chunk, 256), F32),
        compiler_params=_cparams(("parallel",), V7X_VMEM_LIMIT_BYTES),
    )(xc, pa, pb, w1a, w1b, w2)


def _head_lanes(rows):
    r = rows.shape[0] // HEADS_PER_KV
    return jnp.concatenate([rows[h * r:(h + 1) * r, :] for h in range(HEADS_PER_KV)], axis=1)


def _attn_kernel(qT_ref, gT_ref, kc_ref, vcT_ref, ks_ref, vsT_ref, kw_ref, vwT_ref, mt_ref, o_ref):
    g = pl.program_id(1)
    i = pl.program_id(2)
    n_cmp = kc_ref.shape[2]
    n_slc = mt_ref.shape[0]
    nl = HEADS_PER_KV * TQ

    q4 = _head_lanes(qT_ref[0])
    lane = lax.broadcasted_iota(I32, (1, nl), 1)
    head = g * HEADS_PER_KV + lane // TQ + 1
    slope = lax.bitcast_convert_type((127 - head) << 23, F32)
    t_loc = lane % TQ
    t_row = i * TQ + t_loc

    def extra_rows(n_rows, rows):
        ridx = lax.broadcasted_iota(I32, (n_rows, nl), 0)
        out = jnp.zeros((n_rows, nl), F32)
        for k, r in enumerate(rows):
            out = jnp.where(ridx == k, r, out)
        return out

    c_ref = i // 2
    bc = jnp.concatenate(
        [q4, extra_rows(64, [256.0 * slope, 16.0 * slope,
                             -256.0 * slope * c_ref.astype(F32)]).astype(BF16)], axis=0)
    s_c = jnp.dot(kc_ref[0, 0], bc, preferred_element_type=F32)
    c_idx = lax.broadcasted_iota(I32, (n_cmp, nl), 0)
    ok_c = (c_idx * STRIDE + (L_CMP - 1)) <= t_row
    s_c = jnp.where(ok_c, s_c, MASK_NEG)
    m_c = jnp.max(s_c, axis=0, keepdims=True)
    e_c = jnp.where(ok_c, jnp.exp(s_c - m_c), 0.0)
    l_c = jnp.sum(e_c, axis=0, keepdims=True)
    p_c = e_c * (1.0 / jnp.maximum(l_c, 1e-30))
    o_c = jnp.dot(vcT_ref[0, 0], p_c.astype(BF16), preferred_element_type=F32)

    p_grp = (p_c[:, 0:TQ] + p_c[:, TQ:2 * TQ]) + (p_c[:, 2 * TQ:3 * TQ] + p_c[:, 3 * TQ:4 * TQ])
    p1 = p_grp.astype(BF16)
    r1 = p_grp - p1.astype(F32)
    p2 = r1.astype(BF16)
    p3 = (r1 - p2.astype(F32)).astype(BF16)
    mt = mt_ref[...]
    p_slc = (jnp.dot(mt, p1, preferred_element_type=F32) + jnp.dot(mt, p2, preferred_element_type=F32)
             + jnp.dot(mt, p3, preferred_element_type=F32))
    s_idx = lax.broadcasted_iota(I32, (n_slc, TQ), 0)
    cur = (i * TQ + lax.broadcasted_iota(I32, (1, TQ), 1)) // L_SLC
    valid = s_idx <= cur
    forced = valid & ((s_idx == 0) | (s_idx > cur - N_LOCAL))
    score0 = jnp.where(forced, 3.0e38, jnp.where(valid, p_slc, -1.0))
    s_idx_f = s_idx.astype(F32)

    def pick(_, carry):
        score, picked = carry
        m = jnp.max(score, axis=0, keepdims=True)
        idx = jnp.min(jnp.where(score == m, s_idx_f, 1.0e9), axis=0, keepdims=True)
        hit = s_idx_f == idx
        return jnp.where(hit, -2.0, score), jnp.where(hit, 1.0, picked)

    all_sel = (i * TQ + TQ - 1) // L_SLC < N_SEL
    n_pick = jnp.where(all_sel, 0, N_SEL)
    _, picked = lax.fori_loop(0, n_pick, pick, (score0, jnp.zeros((n_slc, TQ), F32)))
    sel = valid & (all_sel | (picked > 0.5))

    s_ref = (i * TQ) // L_SLC
    sel4 = jnp.concatenate([sel] * HEADS_PER_KV, axis=1)
    blk_off = ((lax.broadcasted_iota(I32, (n_slc, nl), 0) - s_ref) * L_SLC).astype(F32)
    blk_rows = jnp.where(sel4, slope * blk_off, SEL_NEG).astype(BF16)
    bs = jnp.concatenate([q4, blk_rows, extra_rows(64, [slope]).astype(BF16)], axis=0)

    def slc_tile(j, carry, causal):
        m, l, acc = carry
        k0 = pl.multiple_of(j * KT_SLC, KT_SLC)
        s = jnp.dot(ks_ref[0, 0, pl.ds(k0, KT_SLC), :], bs, preferred_element_type=F32)
        if causal:
            pos = k0 + lax.broadcasted_iota(I32, (KT_SLC, nl), 0)
            s = jnp.where(pos <= t_row, s, MASK_NEG)
        m_new = jnp.maximum(m, jnp.max(s, axis=0, keepdims=True))
        alpha = jnp.exp(m - m_new)
        p = jnp.exp(s - m_new)
        l = alpha * l + jnp.sum(p, axis=0, keepdims=True)
        acc = alpha * acc + jnp.dot(vsT_ref[0, 0, :, pl.ds(k0, KT_SLC)], p.astype(BF16),
                                    preferred_element_type=F32)
        return m_new, l, acc

    n_full = (i * TQ) // KT_SLC
    init = (jnp.full((1, nl), MASK_NEG, F32), jnp.zeros((1, nl), F32), jnp.zeros((HEAD_DIM, nl), F32))
    carry = lax.fori_loop(0, n_full, lambda j, c: slc_tile(j, c, False), init)
    _, l_s, acc_s = slc_tile(n_full, carry, True)
    o_s = acc_s * (1.0 / l_s)

    bw = jnp.concatenate([q4, extra_rows(64, [slope]).astype(BF16)], axis=0)
    r_idx = lax.broadcasted_iota(I32, (KT_WIN, nl), 0)
    s_tiles = []
    for a in range(N_WIN_TILES):
        tile = i - (N_WIN_TILES - 1) + a
        k0 = pl.multiple_of(jnp.maximum(tile, 0) * KT_WIN, KT_WIN)
        s = jnp.dot(kw_ref[0, 0, pl.ds(k0, KT_WIN), :], bw, preferred_element_type=F32)
        s = s + slope * float(KT_WIN * (a - (N_WIN_TILES - 1)))
        ok = tile >= 0
        if a == 0:
            ok = ok & (r_idx > t_loc)
        elif a == N_WIN_TILES - 1:
            ok = r_idx <= t_loc
        s_tiles.append(jnp.where(ok, s, MASK_NEG))
    m_w = s_tiles[0].max(axis=0, keepdims=True)
    for s in s_tiles[1:]:
        m_w = jnp.maximum(m_w, s.max(axis=0, keepdims=True))
    l_w = jnp.zeros((1, nl), F32)
    acc_w = jnp.zeros((HEAD_DIM, nl), F32)
    for a, s in enumerate(s_tiles):
        tile = i - (N_WIN_TILES - 1) + a
        k0 = pl.multiple_of(jnp.maximum(tile, 0) * KT_WIN, KT_WIN)
        p = jnp.exp(s - m_w)
        l_w = l_w + jnp.sum(p, axis=0, keepdims=True)
        acc_w = acc_w + jnp.dot(vwT_ref[0, 0, :, pl.ds(k0, KT_WIN)], p.astype(BF16),
                                preferred_element_type=F32)
    o_w = acc_w * (1.0 / l_w)

    gt = gT_ref[0]
    gate = lambda k: _head_lanes(gt[k * HEADS_PER_KV:(k + 1) * HEADS_PER_KV, :])
    out = gate(0) * o_c + gate(1) * o_s + gate(2) * o_w
    for h in range(HEADS_PER_KV):
        o_ref[0, h * HEAD_DIM:(h + 1) * HEAD_DIM, :] = out[:, h * TQ:(h + 1) * TQ]


def _attention(qT, gT, kcmp, ks, vsT, kw, vwT):
    b_, _, t_ = qT.shape
    n_cmp = t_ // STRIDE
    n_slc = t_ // L_SLC
    pos = jnp.arange(t_, dtype=I32)
    grp = lambda a: a.reshape(b_, t_, N_KV, HEAD_DIM).transpose(0, 2, 1, 3)
    slc_cols = jnp.concatenate(
        [jax.nn.one_hot(pos // L_SLC, n_slc, dtype=BF16), (pos % L_SLC).astype(BF16)[:, None],
         jnp.zeros((t_, 63), BF16)], axis=1)
    ks_aug = jnp.concatenate(
        [grp(ks), jnp.broadcast_to(slc_cols, (b_, N_KV) + slc_cols.shape)], axis=-1)
    win_cols = jnp.concatenate([(pos % KT_WIN).astype(BF16)[:, None], jnp.zeros((t_, 63), BF16)], axis=1)
    kw_aug = jnp.concatenate(
        [grp(kw), jnp.broadcast_to(win_cols, (b_, N_KV) + win_cols.shape)], axis=-1)
    c = jnp.arange(n_cmp, dtype=I32)
    cmp_cols = jnp.concatenate(
        [(c // 16).astype(BF16)[:, None], (c % 16).astype(BF16)[:, None], jnp.ones((n_cmp, 1), BF16),
         jnp.zeros((n_cmp, 61), BF16)], axis=1)
    kc4 = kcmp[..., 0:KV_W].reshape(b_, n_cmp, N_KV, HEAD_DIM).transpose(0, 2, 1, 3).astype(BF16)
    kc_aug = jnp.concatenate(
        [kc4, jnp.broadcast_to(cmp_cols, (b_, N_KV) + cmp_cols.shape)], axis=-1)
    vcT = kcmp[..., KV_W:2 * KV_W].reshape(b_, n_cmp, N_KV, HEAD_DIM).transpose(0, 2, 3, 1).astype(BF16)
    vsT4 = vsT.reshape(b_, N_KV, HEAD_DIM, t_)
    vwT4 = vwT.reshape(b_, N_KV, HEAD_DIM, t_)
    s = jnp.arange(n_slc, dtype=I32)[:, None]
    cc = c[None, :]
    r_slc = L_SLC // STRIDE
    mt = (((cc >= r_slc * s) & (cc < r_slc * s + r_slc)).astype(F32)
          + ((cc + 1 >= r_slc * s) & (cc + 1 < r_slc * s + r_slc)).astype(F32))
    mt = jnp.where(cc < n_cmp - 1, mt, 0.0).astype(BF16)
    kv_spec = lambda a: pl.BlockSpec((1, 1) + a.shape[2:], lambda b, g, i: (b, g, 0, 0))
    return pl.pallas_call(
        _attn_kernel,
        name="nsa_attention",
        grid=(b_, N_KV, t_ // TQ),
        in_specs=[pl.BlockSpec((1, HEADS_PER_KV * HEAD_DIM, TQ), lambda b, g, i: (b, g, i)),
                  pl.BlockSpec((1, 16, TQ), lambda b, g, i: (b, g, i)),
                  kv_spec(kc_aug), kv_spec(vcT), kv_spec(ks_aug), kv_spec(vsT4),
                  kv_spec(kw_aug), kv_spec(vwT4),
                  pl.BlockSpec(mt.shape, lambda b, g, i: (0, 0))],
        out_specs=pl.BlockSpec((1, HEADS_PER_KV * HEAD_DIM, TQ), lambda b, g, i: (b, g, i)),
        out_shape=jax.ShapeDtypeStruct((b_, D_ATTN, t_), F32),
        compiler_params=_cparams(("parallel", "parallel", "arbitrary"), V7X_VMEM_LIMIT_BYTES),
    )(qT, gT, kc_aug, vcT, ks_aug, vsT4, kw_aug, vwT4, mt)


def _rglru_kernel(xr_ref, yg_ref, cw_ref, cb_ref, wa_ref, ba_ref, wx_ref, bx_ref, lam_ref,
                  o_ref, tail_ref, h_ref):
    tc = xr_ref.shape[1]

    @pl.when(pl.program_id(1) == 0)
    def _():
        tail_ref[...] = jnp.zeros_like(tail_ref)
        h_ref[...] = jnp.zeros_like(h_ref)

    x = xr_ref[0]
    tail = tail_ref[...]
    row8 = lax.broadcasted_iota(I32, tail.shape, 0)
    cw = cw_ref[...]
    xc = x * cw[CONV_W - 1:CONV_W, :] + cb_ref[...]
    for d in range(1, CONV_W):
        xs = pltpu.roll(x, d, axis=0)
        head = jnp.where(row8 < d, pltpu.roll(tail, d, axis=0), xs[0:8])
        xs = jnp.concatenate([head, xs[8:]], axis=0)
        xc = xc + xs * cw[CONV_W - 1 - d:CONV_W - d, :]
    tail_ref[...] = x[tc - 8:tc]

    xb = xc.astype(BF16)
    r = _sigmoid(jnp.dot(xb, wa_ref[...], preferred_element_type=F32) + ba_ref[...])
    gi = _sigmoid(jnp.dot(xb, wx_ref[...], preferred_element_type=F32) + bx_ref[...])
    z = -lam_ref[...]
    softplus = jnp.maximum(z, 0.0) + jnp.log1p(jnp.exp(-jnp.abs(z)))
    log_a = (-C_LRU * r) * softplus
    a = jnp.exp(log_a)
    two = 2.0 * log_a
    neg_expm1 = jnp.where(two > -1e-3, -two * (1.0 + two * (0.5 + two * (1.0 / 6.0))), 1.0 - jnp.exp(two))
    bb = jnp.sqrt(neg_expm1) * gi * xc

    row = lax.broadcasted_iota(I32, a.shape, 0)
    d = 1
    while d < tc:
        a_sh = jnp.where(row >= d, pltpu.roll(a, d, axis=0), 1.0)
        b_sh = jnp.where(row >= d, pltpu.roll(bb, d, axis=0), 0.0)
        bb = a * b_sh + bb
        a = a * a_sh
        d *= 2
    h = a * h_ref[0:1, :] + bb
    h_ref[...] = jnp.broadcast_to(h[tc - 1:tc, :], h_ref.shape)
    o_ref[0] = h * _gelu_tanh(yg_ref[0])


def _block_diag(w):
    n, c, d = w.shape
    return jnp.einsum('ncd,nm->ncmd', w, jnp.eye(n, dtype=w.dtype)).reshape(n * c, n * d)


def _rglru(xr, yg, conv_w, conv_b, wa, ba, wx, bx, lam):
    b_, t_, c_ = xr.shape
    tc = min(RGLRU_CHUNK, t_)
    wa_bd = _block_diag(wa).astype(BF16)
    wx_bd = _block_diag(wx).astype(BF16)
    vec = lambda v: v.reshape(1, c_)
    full = lambda a: pl.BlockSpec(a.shape, lambda b, i: (0,) * a.ndim)
    blk = pl.BlockSpec((1, tc, c_), lambda b, i: (b, i, 0))
    args = (xr, yg, conv_w, vec(conv_b), wa_bd, vec(ba), wx_bd, vec(bx), vec(lam))
    return pl.pallas_call(
        _rglru_kernel,
        name="rglru",
        grid=(b_, t_ // tc),
        in_specs=[blk, blk] + [full(a) for a in args[2:]],
        out_specs=blk,
        out_shape=jax.ShapeDtypeStruct((b_, t_, c_), F32),
        scratch_shapes=[pltpu.VMEM((8, c_), F32), pltpu.VMEM((8, c_), F32)],
        compiler_params=_cparams(("parallel", "arbitrary"), V7X_VMEM_LIMIT_BYTES),
    )(*args)


def _out_proj_kernel(oaT_ref, orec_ref, x_ref, ga_ref, gr_ref, wa_ref, wr_ref, o_ref):
    oaT = oaT_ref[0]
    ms = jnp.mean(oaT * oaT, axis=0, keepdims=True)
    na = (oaT * lax.rsqrt(ms + RMS_EPS)).T * ga_ref[...]
    nr = _rms(orec_ref[0], gr_ref[...])
    y = (jnp.dot(na.astype(BF16), wa_ref[...], preferred_element_type=F32)
         + jnp.dot(nr.astype(BF16), wr_ref[...], preferred_element_type=F32))
    o_ref[0] = x_ref[0] + y


def _out_proj(oaT, orec, x, g_attn, g_rec, w_out):
    b_, t_, d_ = x.shape
    tm = min(256, t_)
    wa = w_out[:D_ATTN].astype(BF16)
    wr = w_out[D_ATTN:].astype(BF16)
    ga = g_attn.reshape(1, -1)
    gr = g_rec.reshape(1, -1)
    full = lambda a: pl.BlockSpec(a.shape, lambda b, i: (0,) * a.ndim)
    return pl.pallas_call(
        _out_proj_kernel,
        name="out_proj",
        grid=(b_, t_ // tm),
        in_specs=[pl.BlockSpec((1, D_ATTN, tm), lambda b, i: (b, 0, i)),
                  pl.BlockSpec((1, tm, orec.shape[2]), lambda b, i: (b, i, 0)),
                  pl.BlockSpec((1, tm, d_), lambda b, i: (b, i, 0)),
                  full(ga), full(gr), full(wa), full(wr)],
        out_specs=pl.BlockSpec((1, tm, d_), lambda b, i: (b, i, 0)),
        out_shape=jax.ShapeDtypeStruct((b_, t_, d_), F32),
        compiler_params=_cparams(("parallel", "parallel"), V7X_VMEM_LIMIT_BYTES),
    )(oaT, orec, x, ga, gr, wa, wr)


def _ffn_kernel(be_ref, x_ref, g_ref, wg_ref, wu_ref, wd_ref, o_ref, hn_ref, acc_ref, *, dense):
    del be_ref
    f = pl.program_id(1)

    @pl.when(f == 0)
    def _():
        if dense:
            x = x_ref[...]
            hn_ref[...] = _rms(x, g_ref[...]).astype(BF16)
            acc_ref[...] = x
        else:
            hn_ref[...] = x_ref[...]
            acc_ref[...] = jnp.zeros_like(acc_ref)

    hn = hn_ref[...]
    gate = jnp.dot(hn, wg_ref[0], preferred_element_type=F32)
    up = jnp.dot(hn, wu_ref[0], preferred_element_type=F32)
    act = (_silu(gate) * up).astype(BF16)
    acc_ref[...] += jnp.dot(act, wd_ref[0], preferred_element_type=F32)

    @pl.when(f == pl.num_programs(1) - 1)
    def _():
        o_ref[...] = acc_ref[...].astype(o_ref.dtype)


def _ffn(x, norm_g, wg, wu, wd, block_e, *, dense, tf):
    n, d_ = x.shape
    tm = min(ROW_TILE if dense else MOE_ROWS, n)
    ff = wg.shape[2]
    assert ff % tf == 0 and n % tm == 0
    if block_e is None:
        block_e = jnp.zeros((n // tm,), I32)
    grid_spec = pltpu.PrefetchScalarGridSpec(
        num_scalar_prefetch=1,
        grid=(n // tm, ff // tf),
        in_specs=[pl.BlockSpec((tm, d_), lambda i, f, be: (i, 0)),
                  pl.BlockSpec((1, d_), lambda i, f, be: (0, 0)),
                  pl.BlockSpec((1, d_, tf), lambda i, f, be: (be[i], 0, f)),
                  pl.BlockSpec((1, d_, tf), lambda i, f, be: (be[i], 0, f)),
                  pl.BlockSpec((1, tf, d_), lambda i, f, be: (be[i], f, 0))],
        out_specs=pl.BlockSpec((tm, d_), lambda i, f, be: (i, 0)),
        scratch_shapes=[pltpu.VMEM((tm, d_), BF16), pltpu.VMEM((tm, d_), F32)],
    )
    return pl.pallas_call(
        functools.partial(_ffn_kernel, dense=dense),
        name="ffn_dense" if dense else "ffn_expert",
        grid_spec=grid_spec,
        out_shape=jax.ShapeDtypeStruct((n, d_), F32 if dense else BF16),
        compiler_params=_cparams(("parallel", "arbitrary"), V7X_VMEM_LIMIT_BYTES),
    )(block_e, x, norm_g.reshape(1, d_), wg, wu, wd)


def _router_kernel(x_ref, g_ref, wr_ref, h_ref, e_ref, p_ref):
    hn = _rms(x_ref[...], g_ref[...])
    h_ref[...] = hn.astype(BF16)
    logits = lax.dot_general(wr_ref[...], hn, (((1,), (1,)), ((), ())),
                             precision=lax.Precision.HIGHEST, preferred_element_type=F32)
    eidx = lax.broadcasted_iota(I32, logits.shape, 0)
    m1 = jnp.max(logits, axis=0, keepdims=True)
    i1 = jnp.min(jnp.where(logits == m1, eidx, N_EXPERTS), axis=0, keepdims=True)
    rest = jnp.where(eidx == i1, -jnp.inf, logits)
    m2 = jnp.max(rest, axis=0, keepdims=True)
    i2 = jnp.min(jnp.where(rest == m2, eidx, N_EXPERTS), axis=0, keepdims=True)
    e2 = jnp.exp(m2 - m1)
    inv = 1.0 / (1.0 + e2)
    e_ref[...] = jnp.concatenate([i1, i2], axis=0)
    p_ref[...] = jnp.concatenate([inv, e2 * inv], axis=0)


def _router(x, norm_g, router_w):
    n, d_ = x.shape
    tm = min(ROW_TILE, n)
    wr = router_w.T
    return pl.pallas_call(
        _router_kernel,
        name="router",
        grid=(n // tm,),
        in_specs=[pl.BlockSpec((tm, d_), lambda i: (i, 0)),
                  pl.BlockSpec((1, d_), lambda i: (0, 0)),
                  pl.BlockSpec(wr.shape, lambda i: (0, 0))],
        out_specs=[pl.BlockSpec((tm, d_), lambda i: (i, 0)),
                   pl.BlockSpec((2, tm), lambda i: (0, i)),
                   pl.BlockSpec((2, tm), lambda i: (0, i))],
        out_shape=[jax.ShapeDtypeStruct((n, d_), BF16),
                   jax.ShapeDtypeStruct((2, n), I32),
                   jax.ShapeDtypeStruct((2, n), F32)],
        compiler_params=_cparams(("parallel",), V7X_VMEM_LIMIT_BYTES),
    )(x, norm_g.reshape(1, d_), wr)


def _gather_kernel(j_ref, c_ref, v_ref, dest_ref, h_ref, zeros_ref, o_ref, acc_ref):
    del c_ref, zeros_ref
    w = pl.program_id(0)
    nw = pl.num_programs(0)
    j = j_ref[w]
    first = (w == 0) | (j != j_ref[jnp.maximum(w - 1, 0)])
    last = (w == nw - 1) | (j != j_ref[jnp.minimum(w + 1, nw - 1)])

    @pl.when(first)
    def _():
        acc_ref[...] = jnp.zeros_like(acc_ref)

    @pl.when(v_ref[w] == 1)
    def _():
        dest = dest_ref[...]
        slot = j * acc_ref.shape[0] + lax.broadcasted_iota(I32, (acc_ref.shape[0], dest.shape[1]), 0)
        onehot = jnp.where((slot == dest[0:1, :]) | (slot == dest[1:2, :]), 1.0, 0.0).astype(BF16)
        acc_ref[...] += jnp.dot(onehot, h_ref[...], preferred_element_type=F32)

    @pl.when(last)
    def _():
        o_ref[...] = acc_ref[...].astype(BF16)


def _gather(h, dest, j_list, c_list, v_list, n_slots):
    n, d_ = h.shape
    zeros = jnp.zeros((n_slots, d_), BF16)
    grid_spec = pltpu.PrefetchScalarGridSpec(
        num_scalar_prefetch=3,
        grid=(j_list.shape[0],),
        in_specs=[pl.BlockSpec((2, MOE_TOK_CHUNK), lambda w, j, c, v: (0, c[w])),
                  pl.BlockSpec((MOE_TOK_CHUNK, d_), lambda w, j, c, v: (c[w], 0)),
                  pl.BlockSpec(memory_space=pl.ANY)],
        out_specs=pl.BlockSpec((MOE_ROWS, d_), lambda w, j, c, v: (j[w], 0)),
        scratch_shapes=[pltpu.VMEM((MOE_ROWS, d_), F32)],
    )
    return pl.pallas_call(
        _gather_kernel,
        name="moe_gather",
        grid_spec=grid_spec,
        out_shape=jax.ShapeDtypeStruct((n_slots, d_), BF16),
        input_output_aliases={5: 0},
        compiler_params=_cparams(("arbitrary",), V7X_VMEM_LIMIT_BYTES),
    )(j_list, c_list, v_list, dest, h, zeros)


N_CMB_OPS = 2 * N_EXPERTS


def _combine_kernel(jt_ref, vt_ref, x_ref, dest_ref, gate_ref, *rest):
    yb_refs = rest[:N_CMB_OPS]
    o_ref = rest[N_CMB_OPS]
    c = pl.program_id(0)
    o_ref[...] = x_ref[...]
    dest = dest_ref[...]
    gate = gate_ref[...]
    lane = lax.broadcasted_iota(I32, (dest.shape[0], CMB_ROWS), 1)
    for k in range(N_CMB_OPS):
        @pl.when(vt_ref[c * N_CMB_OPS + k] == 1)
        def _(k=k):
            slot = jt_ref[c * N_CMB_OPS + k] * CMB_ROWS + lane
            w = (jnp.where(slot == dest[:, 0:1], gate[:, 0:1], 0.0)
                 + jnp.where(slot == dest[:, 1:2], gate[:, 1:2], 0.0)).astype(BF16)
            o_ref[...] += jnp.dot(w, yb_refs[k][...], preferred_element_type=F32)


def _combine(x, yb, dest_t, gate_t, jt, vt):
    n, d_ = x.shape
    yb_spec = lambda k: pl.BlockSpec((CMB_ROWS, d_), lambda c, jt, vt, k=k: (jt[c * N_CMB_OPS + k], 0))
    grid_spec = pltpu.PrefetchScalarGridSpec(
        num_scalar_prefetch=2,
        grid=(n // CMB_ROWS,),
        in_specs=[pl.BlockSpec((CMB_ROWS, d_), lambda c, jt, vt: (c, 0)),
                  pl.BlockSpec((CMB_ROWS, 2), lambda c, jt, vt: (c, 0)),
                  pl.BlockSpec((CMB_ROWS, 2), lambda c, jt, vt: (c, 0))]
                 + [yb_spec(k) for k in range(N_CMB_OPS)],
        out_specs=pl.BlockSpec((CMB_ROWS, d_), lambda c, jt, vt: (c, 0)),
    )
    return pl.pallas_call(
        _combine_kernel,
        name="moe_combine",
        grid_spec=grid_spec,
        out_shape=jax.ShapeDtypeStruct((n, d_), F32),
        compiler_params=_cparams(("parallel",), V7X_VMEM_LIMIT_BYTES),
    )(jt, vt, x, dest_t, gate_t, *([yb] * N_CMB_OPS))


def _moe_plan(top_e, n_slots):
    n = top_e.shape[1]
    e0, e1 = top_e[0], top_e[1]
    oh = jax.nn.one_hot(e0, N_EXPERTS, dtype=I32) + jax.nn.one_hot(e1, N_EXPERTS, dtype=I32)
    csum_incl = jnp.cumsum(oh, axis=0)
    csum = csum_incl - oh
    counts = csum_incl[-1]
    padded = (counts + MOE_ROWS - 1) // MOE_ROWS * MOE_ROWS
    pad_ends = jnp.cumsum(padded)
    pad_starts = pad_ends - padded
    rank0 = jnp.take_along_axis(csum, e0[:, None], axis=1)[:, 0]
    rank1 = jnp.take_along_axis(csum, e1[:, None], axis=1)[:, 0]
    dest = jnp.stack([pad_starts[e0] + rank0, pad_starts[e1] + rank1]).astype(I32)
    n_blk = n_slots // MOE_ROWS
    block_e = jnp.minimum(jnp.searchsorted(pad_ends, jnp.arange(n_blk, dtype=I32) * MOE_ROWS, side='right'),
                          N_EXPERTS - 1).astype(I32)

    def ranges(chunk):
        nc = n // chunk
        cb = jnp.concatenate([csum[::chunk], counts[None]], axis=0)
        lo = pad_starts[None] + cb[:-1]
        hi = pad_starts[None] + cb[1:]
        return nc, lo, hi

    nc, lo, hi = ranges(MOE_TOK_CHUNK)
    ja = lo // MOE_ROWS
    jb = (hi - 1) // MOE_ROWS
    va = hi > lo
    vb = va & (jb > ja)
    j_list = jnp.stack([ja, jb], axis=-1).transpose(1, 0, 2).reshape(-1)
    v_list = jnp.stack([va, vb], axis=-1).transpose(1, 0, 2).reshape(-1)
    c_list = jnp.broadcast_to(jnp.arange(nc, dtype=I32)[None, :, None], (N_EXPERTS, nc, 2)).reshape(-1)
    idx = lax.cummax(jnp.where(v_list, jnp.arange(v_list.shape[0], dtype=I32), -1), axis=0)
    has = idx >= 0
    idx = jnp.maximum(idx, 0)
    j_list = jnp.where(has, j_list[idx], 0).astype(I32)
    c_list = jnp.where(has, c_list[idx], 0).astype(I32)
    v_list = v_list.astype(I32)

    _, lo2, hi2 = ranges(CMB_ROWS)
    ja2 = lo2 // CMB_ROWS
    jb2 = (hi2 - 1) // CMB_ROWS
    va2 = hi2 > lo2
    vb2 = va2 & (jb2 > ja2)
    jt = jnp.stack([ja2, jb2], axis=-1).reshape(-1)
    vt = jnp.stack([va2, vb2], axis=-1).reshape(-1)
    jt = jnp.where(vt, jt, 0).astype(I32)
    return dest, block_e, j_list, c_list, v_list, jt, vt.astype(I32)


def _moe(x, norm_g, router_w, wg, wu, wd):
    n, d_ = x.shape
    n_slots = (2 * n // MOE_ROWS + N_EXPERTS) * MOE_ROWS
    h, top_e, top_p = _router(x, norm_g, router_w)
    dest, block_e, j_list, c_list, v_list, jt, vt = _moe_plan(top_e, n_slots)
    xs = _gather(h, dest, j_list, c_list, v_list, n_slots)
    yb = _ffn(xs, norm_g, wg, wu, wd, block_e, dense=False, tf=896)
    return _combine(x, yb, dest.T, top_p.T, jt, vt)


def _ple_kernel(x_ref, p_ref, g_ref, wg_ref, wp_ref, fg_ref, o_ref, *, final):
    x = x_ref[...]
    gate = _sigmoid(jnp.dot(_rms(x, g_ref[...]).astype(BF16), wg_ref[...], preferred_element_type=F32))
    proj = jnp.dot(p_ref[...].astype(BF16), wp_ref[...], preferred_element_type=F32)
    y = x + gate * proj
    if final:
        y = _rms(y, fg_ref[...])
    o_ref[...] = y


def _ple(x, p, norm_g, w_gate, w_proj, final_g, *, final):
    n, d_ = x.shape
    tm = min(ROW_TILE, n)
    wg = w_gate.astype(BF16)
    wp = w_proj.astype(BF16)
    full = lambda a: pl.BlockSpec(a.shape, lambda i: (0,) * a.ndim)
    g2 = norm_g.reshape(1, d_)
    fg = final_g.reshape(1, d_)
    return pl.pallas_call(
        functools.partial(_ple_kernel, final=final),
        name="ple_final" if final else "ple",
        grid=(n // tm,),
        in_specs=[pl.BlockSpec((tm, d_), lambda i: (i, 0)),
                  pl.BlockSpec((tm, p.shape[1]), lambda i: (i, 0)),
                  full(g2), full(wg), full(wp), full(fg)],
        out_specs=pl.BlockSpec((tm, d_), lambda i: (i, 0)),
        out_shape=jax.ShapeDtypeStruct((n, d_), F32),
        compiler_params=_cparams(("parallel",), V7X_VMEM_LIMIT_BYTES),
    )(x, p, g2, wg, wp, fg)


def kernel(x, p, attn_norm, w_in, cmp_pos, cmp_w1, cmp_w2, conv_w, conv_b, lru_wa, lru_ba, lru_wx, lru_bx,
           lru_lambda, out_norm_attn, out_norm_rec, w_out, ffn_norm, dense_w_gate, dense_w_up, dense_w_down,
           router_w, moe_w_gate, moe_w_up, moe_w_down, ple_norm, ple_w_gate, ple_w_proj, final_norm):
    b_, t_, d_ = x.shape
    depth = w_in.shape[0]
    n = b_ * t_
    for i in range(depth):
        kcvc, ks, kw, xr, yg, qT, vsT, vwT, gT = _proj_in(x, attn_norm[i], w_in[i])
        kcmp = _compress(kcvc, cmp_pos[i], cmp_w1[i], cmp_w2[i])
        oaT = _attention(qT, gT, kcmp, ks, vsT, kw, vwT)
        orec = _rglru(xr, yg, conv_w[i], conv_b[i], lru_wa[i], lru_ba[i], lru_wx[i], lru_bx[i], lru_lambda[i])
        x = _out_proj(oaT, orec, x, out_norm_attn[i], out_norm_rec[i], w_out[i])
        x2 = x.reshape(n, d_)
        j = i // 2
        if i % 2 == 0:
            x2 = _ffn(x2, ffn_norm[i], dense_w_gate[j][None].astype(BF16), dense_w_up[j][None].astype(BF16),
                      dense_w_down[j][None].astype(BF16), None, dense=True, tf=1408)
        else:
            x2 = _moe(x2, ffn_norm[i], router_w[j], moe_w_gate[j].astype(BF16), moe_w_up[j].astype(BF16),
                      moe_w_down[j].astype(BF16))
        x2 = _ple(x2, p[i].reshape(n, -1), ple_norm[i], ple_w_gate[i], ple_w_proj[i], final_norm,
                  final=(i == depth - 1))
        x = x2.reshape(b_, t_, d_)
    return x
```

```python
import functools

import jax
import jax.numpy as jnp
from jax import lax
from jax.experimental import pallas as pl
from jax.experimental.pallas import tpu as pltpu

F32 = jnp.float32
BF16 = jnp.bfloat16
I32 = jnp.int32

N_ATTN_HEADS = 8
HEAD_DIM = 64
N_KV = 2
HEADS_PER_KV = N_ATTN_HEADS // N_KV
D_ATTN = N_ATTN_HEADS * HEAD_DIM
KV_W = N_KV * HEAD_DIM
N_GATES = 3
L_CMP = 32
STRIDE = 16
L_SLC = 64
N_SEL = 16
N_LOCAL = 2
W_WIN = 512
CMP_HID = 2 * HEAD_DIM
CONV_W = 4
C_LRU = 8.0
N_EXPERTS = 8
RMS_EPS = 1e-6
ATTN_SCALE = HEAD_DIM ** -0.5

V7X_LANES = 128
V7X_VMEM_LIMIT_BYTES = 56 * 1024 * 1024

TQ = 128
KT_SLC = 256
KT_WIN = 128
N_WIN_TILES = W_WIN // KT_WIN + 1
MASK_NEG = -1e30
SEL_NEG = -1.0e4
ROW_TILE = 512
RGLRU_CHUNK = 256
MOE_ROWS = 512
MOE_TOK_CHUNK = 512
CMB_ROWS = 256


def _cparams(semantics, vmem=None):
    return pltpu.CompilerParams(dimension_semantics=semantics, vmem_limit_bytes=vmem)


def _rms(x, g):
    ms = jnp.mean(x * x, axis=-1, keepdims=True)
    return x * lax.rsqrt(ms + RMS_EPS) * g


def _gelu_tanh(x):
    c = 0.7978845608028654
    return x * (0.5 * (1.0 + jnp.tanh(c * (x + 0.044715 * (x * x * x)))))


def _sigmoid(x):
    return 1.0 / (1.0 + jnp.exp(-x))


def _silu(x):
    return x * _sigmoid(x)


N_TOK_COLS = 2 * KV_W + 2 * KV_W + 2 * 512
N_TR_ROWS = D_ATTN + 2 * KV_W + 32


def _proj_in_kernel(x_ref, g_ref, wtok_ref, wtr_ref,
                    kcvc_ref, ks_ref, kw_ref, xr_ref, yg_ref, qT_ref, vsT_ref, vwT_ref, gT_ref):
    hn = _rms(x_ref[0], g_ref[...]).astype(BF16)
    tok = jnp.dot(hn, wtok_ref[...], preferred_element_type=F32)
    kcvc_ref[0] = tok[:, 0:256]
    ks_ref[0] = tok[:, 256:384].astype(BF16)
    kw_ref[0] = tok[:, 384:512].astype(BF16)
    xr_ref[0] = tok[:, 512:1024]
    yg_ref[0] = tok[:, 1024:1536]
    tr = lax.dot_general(wtr_ref[...], hn, (((1,), (1,)), ((), ())),
                         preferred_element_type=F32)
    qT_ref[0] = (tr[0:512] * ATTN_SCALE).astype(BF16)
    vsT_ref[0] = tr[512:640].astype(BF16)
    vwT_ref[0] = tr[640:768].astype(BF16)
    gT_ref[0] = _sigmoid(tr[768:800])


def _proj_in(x, norm_g, w_in):
    b_, t_, d_ = x.shape
    tm = min(ROW_TILE, t_)
    q, kc, vc, ks, vs, kw, vw, g, xr, yg = jnp.split(
        w_in, [512, 640, 768, 896, 1024, 1152, 1280, 1304, 1816], axis=1)
    wtok = jnp.concatenate([kc, vc, ks, kw, xr, yg], axis=1).astype(BF16)
    g4 = g.reshape(d_, N_KV, HEADS_PER_KV, N_GATES).transpose(0, 1, 3, 2)
    g4 = jnp.pad(g4.reshape(d_, N_KV, 12), ((0, 0), (0, 0), (0, 4))).reshape(d_, 32)
    wtr = jnp.concatenate([q, vs, vw, g4], axis=1).T.astype(BF16)
    nt = t_ // tm
    row = lambda shape: pl.BlockSpec((1, tm, shape), lambda b, i: (b, i, 0))
    col = lambda shape: pl.BlockSpec((1, shape, tm), lambda b, i: (b, 0, i))
    full = lambda a: pl.BlockSpec(a.shape, lambda b, i: (0,) * a.ndim)
    g2 = norm_g.reshape(1, d_)
    outs = pl.pallas_call(
        _proj_in_kernel,
        name="proj_in",
        grid=(b_, nt),
        in_specs=[row(d_), full(g2), full(wtok), full(wtr)],
        out_specs=[row(256), row(128), row(128), row(512), row(512),
                   col(512), col(128), col(128), col(32)],
        out_shape=[
            jax.ShapeDtypeStruct((b_, t_, 256), F32),
            jax.ShapeDtypeStruct((b_, t_, 128), BF16),
            jax.ShapeDtypeStruct((b_, t_, 128), BF16),
            jax.ShapeDtypeStruct((b_, t_, 512), F32),
            jax.ShapeDtypeStruct((b_, t_, 512), F32),
            jax.ShapeDtypeStruct((b_, 512, t_), BF16),
            jax.ShapeDtypeStruct((b_, 128, t_), BF16),
            jax.ShapeDtypeStruct((b_, 128, t_), BF16),
            jax.ShapeDtypeStruct((b_, 32, t_), F32),
        ],
        compiler_params=_cparams(("parallel", "parallel"), V7X_VMEM_LIMIT_BYTES),
    )(x, g2, wtok, wtr)
    return outs


def _compress_kernel(xc_ref, pa_ref, pb_ref, w1a_ref, w1b_ref, w2_ref, out_ref):
    xc = xc_ref[0].astype(BF16)
    n_chunk = xc.shape[0]
    a = jnp.dot(xc, w1a_ref[...], preferred_element_type=F32)
    bm = jnp.dot(xc, w1b_ref[...], preferred_element_type=F32)
    posc = (jnp.dot(pa_ref[...], w1a_ref[...], preferred_element_type=F32)
            + jnp.dot(pb_ref[...], w1b_ref[...], preferred_element_type=F32))[0:1]
    row = lax.broadcasted_iota(I32, bm.shape, 0)
    bm_up = jnp.where(row < n_chunk - 1, pltpu.roll(bm, n_chunk - 1, axis=0), 0.0)
    hid = _gelu_tanh(a + bm_up + posc).astype(BF16)
    out_ref[0] = jnp.dot(hid, w2_ref[...], preferred_element_type=F32)


def _compress(kcvc, cmp_pos, cmp_w1, cmp_w2):
    b_, t_, _ = kcvc.shape
    n_chunk = t_ // STRIDE
    xc = kcvc.reshape(b_, n_chunk, STRIDE * 256)
    eye = jnp.eye(2 * N_KV, dtype=F32)
    w1 = cmp_w1.reshape(2, 2, STRIDE, HEAD_DIM, CMP_HID)

    def expand(half):
        w = w1[:, half]
        full = jnp.einsum('wldo,wx,gy->lxgdwyo', w, jnp.eye(2, dtype=F32), jnp.eye(2, dtype=F32))
        return full.reshape(STRIDE * 256, 4 * CMP_HID).astype(BF16)

    del eye
    w1a, w1b = expand(0), expand(1)
    pos = cmp_pos.reshape(2, 2, STRIDE, HEAD_DIM)

    def posrow(half):
        p = jnp.broadcast_to(pos[:, half][:, None], (2, N_KV, STRIDE, HEAD_DIM))
        p = p.transpose(2, 0, 1, 3).reshape(1, STRIDE * 256)
        return jnp.pad(p, ((0, 7), (0, 0))).astype(BF16)

    pa, pb = posrow(0), posrow(1)
    w2 = jnp.einsum('whd,wx,gy->wghxyd', cmp_w2, jnp.eye(2, dtype=F32), jnp.eye(2, dtype=F32))
    w2 = w2.reshape(4 * CMP_HID, 4 * HEAD_DIM).astype(BF16)
    full = lambda a: pl.BlockSpec(a.shape, lambda b: (0,) * a.ndim)
    return pl.pallas_call(
        _compress_kernel,
        name="compress_kv",
        grid=(b_,),
        in_specs=[pl.BlockSpec((1, n_chunk, STRIDE * 256), lambda b: (b, 0, 0)),
                  full(pa), full(pb), full(w1a), full(w1b), full(w2)],
        out_specs=pl.BlockSpec((1, n_chunk, 256), lambda b: (b, 0, 0)),
        out_shape=jax.ShapeDtypeStruct((b_, n_chunk, 256), F32),
        compiler_params=_cparams(("parallel",), V7X_VMEM_LIMIT_BYTES),
    )(xc, pa, pb, w1a, w1b, w2)


def _head_lanes(rows):
    r = rows.shape[0] // HEADS_PER_KV
    return jnp.concatenate([rows[h * r:(h + 1) * r, :] for h in range(HEADS_PER_KV)], axis=1)


def _attn_kernel(qT_ref, gT_ref, kc_ref, vcT_ref, ks_ref, vsT_ref, kw_ref, vwT_ref, mt_ref, o_ref, s_scr):
    i = pl.program_id(1)
    n_cmp = kc_ref.shape[2]
    n_slc = mt_ref.shape[0]
    nl = HEADS_PER_KV * TQ
    hd_rows = HEADS_PER_KV * HEAD_DIM

    lane = lax.broadcasted_iota(I32, (1, nl), 1)
    t_loc = lane % TQ
    t_row = i * TQ + t_loc

    def extra_rows(n_rows, rows):
        ridx = lax.broadcasted_iota(I32, (16, nl), 0)
        out = jnp.zeros((16, nl), F32)
        for k, r in enumerate(rows):
            out = jnp.where(ridx == k, r, out)
        return jnp.concatenate([out.astype(BF16), jnp.zeros((n_rows - 16, nl), BF16)], axis=0)

    ok_c = (lax.broadcasted_iota(I32, (n_cmp, nl), 0) * STRIDE + (L_CMP - 1)) <= t_row
    r_idx = lax.broadcasted_iota(I32, (KT_WIN, nl), 0)
    s_idx = lax.broadcasted_iota(I32, (n_slc, TQ), 0)
    cur = (i * TQ + lax.broadcasted_iota(I32, (1, TQ), 1)) // L_SLC
    valid = s_idx <= cur
    forced = valid & ((s_idx == 0) | (s_idx > cur - N_LOCAL))
    s_idx_f = s_idx.astype(F32)
    c_ref = (i // 2).astype(F32)
    s_ref = (i * TQ) // L_SLC
    blk_off = ((lax.broadcasted_iota(I32, (n_slc, nl), 0) - s_ref) * L_SLC).astype(F32)
    n_full = (i * TQ) // KT_SLC
    mt = mt_ref[...]

    def pick(_, carry):
        score, picked = carry
        m = jnp.max(score, axis=0, keepdims=True)
        idx = jnp.min(jnp.where(score == m, s_idx_f, 1.0e9), axis=0, keepdims=True)
        hit = s_idx_f == idx
        return jnp.where(hit, -2.0, score), jnp.where(hit, 1.0, picked)

    def stage_a(g, bs, tile, slot, causal):
        k0 = pl.multiple_of(tile * KT_SLC, KT_SLC)
        s = jnp.dot(ks_ref[0, g, pl.ds(k0, KT_SLC), :], bs, preferred_element_type=F32)
        if causal:
            pos = k0 + lax.broadcasted_iota(I32, (KT_SLC, nl), 0)
            s = jnp.where(pos <= t_row, s, MASK_NEG)
        s_scr[g, slot] = s
        return jnp.max(s, axis=0, keepdims=True)

    def stage_b(g, tile, slot, m, l, acc, m_tile):
        k0 = pl.multiple_of(tile * KT_SLC, KT_SLC)
        m_new = jnp.maximum(m, m_tile)
        alpha = jnp.exp(m - m_new)
        p = jnp.exp(s_scr[g, slot] - m_new)
        l = alpha * l + jnp.sum(p, axis=0, keepdims=True)
        acc = alpha * acc + jnp.dot(vsT_ref[0, g, :, pl.ds(k0, KT_SLC)], p.astype(BF16),
                                    preferred_element_type=F32)
        return m_new, l, acc

    def prepare(g):
        q4 = _head_lanes(qT_ref[0, g * hd_rows:(g + 1) * hd_rows, :])
        head = g * HEADS_PER_KV + lane // TQ + 1
        slope = lax.bitcast_convert_type((127 - head) << 23, F32)

        bc = jnp.concatenate(
            [q4, extra_rows(64, [256.0 * slope, 16.0 * slope, -256.0 * slope * c_ref])],
            axis=0)
        s_c = jnp.dot(kc_ref[0, g], bc, preferred_element_type=F32)
        s_c = jnp.where(ok_c, s_c, MASK_NEG)
        m_c = jnp.max(s_c, axis=0, keepdims=True)
        e_c = jnp.exp(s_c - m_c)
        l_c = jnp.sum(e_c, axis=0, keepdims=True)
        inv_c = jnp.where(m_c > 0.5 * MASK_NEG, 1.0 / jnp.maximum(l_c, 1e-30), 0.0)
        p_c = e_c * inv_c
        o_c = jnp.dot(vcT_ref[0, g], p_c.astype(BF16), preferred_element_type=F32)

        bw = jnp.concatenate([q4, extra_rows(64, [slope])], axis=0)
        s_tiles = []
        for a in range(N_WIN_TILES):
            tile = i - (N_WIN_TILES - 1) + a
            k0 = pl.multiple_of(jnp.maximum(tile, 0) * KT_WIN, KT_WIN)
            s = jnp.dot(kw_ref[0, g, pl.ds(k0, KT_WIN), :], bw, preferred_element_type=F32)
            s = s + slope * float(KT_WIN * (a - (N_WIN_TILES - 1)))
            ok = tile >= 0
            if a == 0:
                ok = ok & (r_idx > t_loc)
            elif a == N_WIN_TILES - 1:
                ok = r_idx <= t_loc
            s_tiles.append(jnp.where(ok, s, MASK_NEG))
        m_w = s_tiles[0].max(axis=0, keepdims=True)
        for s in s_tiles[1:]:
            m_w = jnp.maximum(m_w, s.max(axis=0, keepdims=True))
        l_w = jnp.zeros((1, nl), F32)
        acc_w = jnp.zeros((HEAD_DIM, nl), F32)
        for a, s in enumerate(s_tiles):
            tile = i - (N_WIN_TILES - 1) + a
            k0 = pl.multiple_of(jnp.maximum(tile, 0) * KT_WIN, KT_WIN)
            p = jnp.exp(s - m_w)
            l_w = l_w + jnp.sum(p, axis=0, keepdims=True)
            acc_w = acc_w + jnp.dot(vwT_ref[0, g, :, pl.ds(k0, KT_WIN)], p.astype(BF16),
                                    preferred_element_type=F32)
        o_w = acc_w * (1.0 / l_w)

        p_grp = (p_c[:, 0:TQ] + p_c[:, TQ:2 * TQ]) + (p_c[:, 2 * TQ:3 * TQ] + p_c[:, 3 * TQ:4 * TQ])
        p1 = p_grp.astype(BF16)
        r1 = p_grp - p1.astype(F32)
        p2 = r1.astype(BF16)
        p3 = (r1 - p2.astype(F32)).astype(BF16)
        p_slc = (jnp.dot(mt, p1, preferred_element_type=F32) + jnp.dot(mt, p2, preferred_element_type=F32)
                 + jnp.dot(mt, p3, preferred_element_type=F32))
        score0 = jnp.where(forced, -2.0, jnp.where(valid, p_slc, -1.0))
        _, picked = lax.fori_loop(0, N_SEL - (N_LOCAL + 1), pick, (score0, forced.astype(F32)), unroll=True)
        sel = valid & (picked > 0.5)

        sel4 = jnp.concatenate([sel] * HEADS_PER_KV, axis=1)
        blk_rows = jnp.where(sel4, slope * blk_off, SEL_NEG).astype(BF16)
        bs = jnp.concatenate([q4, blk_rows, extra_rows(64, [slope])], axis=0)
        m_tile0 = stage_a(g, bs, n_full, 0, True)
        gt = gT_ref[0, g * 16:(g + 1) * 16, :]
        gates = [_head_lanes(gt[k * HEADS_PER_KV:(k + 1) * HEADS_PER_KV, :]) for k in range(N_GATES)]
        return dict(bs=bs, base=gates[0] * o_c + gates[2] * o_w, g_s=gates[1], m_tile0=m_tile0)

    grp = [prepare(g) for g in range(N_KV)]

    def body(k, carry):
        slot = k & 1
        tile_b = jnp.where(k == 0, n_full, k - 1)
        out = []
        for g in range(N_KV):
            m, l, acc, m_tile = carry[g]
            m, l, acc = stage_b(g, tile_b, slot, m, l, acc, m_tile)
            out.append((m, l, acc, stage_a(g, grp[g]["bs"], k, 1 - slot, False)))
        return tuple(out)

    init = tuple((jnp.full((1, nl), MASK_NEG, F32), jnp.zeros((1, nl), F32),
                  jnp.zeros((HEAD_DIM, nl), F32), grp[g]["m_tile0"]) for g in range(N_KV))
    carry = lax.fori_loop(0, n_full, body, init)
    tile_last = jnp.where(n_full == 0, 0, n_full - 1)
    for g in range(N_KV):
        m, l, acc, m_tile = carry[g]
        _, l, acc = stage_b(g, tile_last, n_full & 1, m, l, acc, m_tile)
        out = grp[g]["base"] + grp[g]["g_s"] * (acc * (1.0 / l))
        for h in range(HEADS_PER_KV):
            r0 = g * hd_rows + h * HEAD_DIM
            o_ref[0, r0:r0 + HEAD_DIM, :] = out[:, h * TQ:(h + 1) * TQ]


def _attention(qT, gT, kcmp, ks, vsT, kw, vwT):
    b_, _, t_ = qT.shape
    n_cmp = t_ // STRIDE
    n_slc = t_ // L_SLC
    pos = jnp.arange(t_, dtype=I32)
    grp = lambda a: a.reshape(b_, t_, N_KV, HEAD_DIM).transpose(0, 2, 1, 3)
    slc_cols = jnp.concatenate(
        [jax.nn.one_hot(pos // L_SLC, n_slc, dtype=BF16), (pos % L_SLC).astype(BF16)[:, None],
         jnp.zeros((t_, 63), BF16)], axis=1)
    ks_aug = jnp.concatenate(
        [grp(ks), jnp.broadcast_to(slc_cols, (b_, N_KV) + slc_cols.shape)], axis=-1)
    win_cols = jnp.concatenate([(pos % KT_WIN).astype(BF16)[:, None], jnp.zeros((t_, 63), BF16)], axis=1)
    kw_aug = jnp.concatenate(
        [grp(kw), jnp.broadcast_to(win_cols, (b_, N_KV) + win_cols.shape)], axis=-1)
    c = jnp.arange(n_cmp, dtype=I32)
    cmp_cols = jnp.concatenate(
        [(c // 16).astype(BF16)[:, None], (c % 16).astype(BF16)[:, None], jnp.ones((n_cmp, 1), BF16),
         jnp.zeros((n_cmp, 61), BF16)], axis=1)
    kc4 = kcmp[..., 0:KV_W].reshape(b_, n_cmp, N_KV, HEAD_DIM).transpose(0, 2, 1, 3).astype(BF16)
    kc_aug = jnp.concatenate(
        [kc4, jnp.broadcast_to(cmp_cols, (b_, N_KV) + cmp_cols.shape)], axis=-1)
    vcT = kcmp[..., KV_W:2 * KV_W].reshape(b_, n_cmp, N_KV, HEAD_DIM).transpose(0, 2, 3, 1).astype(BF16)
    vsT4 = vsT.reshape(b_, N_KV, HEAD_DIM, t_)
    vwT4 = vwT.reshape(b_, N_KV, HEAD_DIM, t_)
    s = jnp.arange(n_slc, dtype=I32)[:, None]
    cc = c[None, :]
    r_slc = L_SLC // STRIDE
    mt = (((cc >= r_slc * s) & (cc < r_slc * s + r_slc)).astype(F32)
          + ((cc + 1 >= r_slc * s) & (cc + 1 < r_slc * s + r_slc)).astype(F32))
    mt = jnp.where(cc < n_cmp - 1, mt, 0.0).astype(BF16)
    kv_spec = lambda a: pl.BlockSpec((1,) + a.shape[1:], lambda b, i: (b, 0, 0, 0))
    return pl.pallas_call(
        _attn_kernel,
        name="nsa_attention",
        grid=(b_, t_ // TQ),
        in_specs=[pl.BlockSpec((1, D_ATTN, TQ), lambda b, i: (b, 0, i)),
                  pl.BlockSpec((1, 32, TQ), lambda b, i: (b, 0, i)),
                  kv_spec(kc_aug), kv_spec(vcT), kv_spec(ks_aug), kv_spec(vsT4),
                  kv_spec(kw_aug), kv_spec(vwT4),
                  pl.BlockSpec(mt.shape, lambda b, i: (0, 0))],
        out_specs=pl.BlockSpec((1, D_ATTN, TQ), lambda b, i: (b, 0, i)),
        out_shape=jax.ShapeDtypeStruct((b_, D_ATTN, t_), F32),
        scratch_shapes=[pltpu.VMEM((N_KV, 2, KT_SLC, HEADS_PER_KV * TQ), F32)],
        compiler_params=_cparams(("parallel", "arbitrary"), V7X_VMEM_LIMIT_BYTES),
    )(qT, gT, kc_aug, vcT, ks_aug, vsT4, kw_aug, vwT4, mt)


def _rglru_kernel(xr_ref, yg_ref, cw_ref, cb_ref, wa_ref, ba_ref, wx_ref, bx_ref, lam_ref,
                  o_ref, tail_ref, h_ref):
    tc = xr_ref.shape[1]

    @pl.when(pl.program_id(1) == 0)
    def _():
        tail_ref[...] = jnp.zeros_like(tail_ref)
        h_ref[...] = jnp.zeros_like(h_ref)

    x = xr_ref[0]
    tail = tail_ref[...]
    row8 = lax.broadcasted_iota(I32, tail.shape, 0)
    cw = cw_ref[...]
    xc = x * cw[CONV_W - 1:CONV_W, :] + cb_ref[...]
    for d in range(1, CONV_W):
        xs = pltpu.roll(x, d, axis=0)
        head = jnp.where(row8 < d, pltpu.roll(tail, d, axis=0), xs[0:8])
        xs = jnp.concatenate([head, xs[8:]], axis=0)
        xc = xc + xs * cw[CONV_W - 1 - d:CONV_W - d, :]
    tail_ref[...] = x[tc - 8:tc]

    xb = xc.astype(BF16)
    r = _sigmoid(jnp.dot(xb, wa_ref[...], preferred_element_type=F32) + ba_ref[...])
    gi = _sigmoid(jnp.dot(xb, wx_ref[...], preferred_element_type=F32) + bx_ref[...])
    z = -lam_ref[...]
    softplus = jnp.maximum(z, 0.0) + jnp.log1p(jnp.exp(-jnp.abs(z)))
    log_a = (-C_LRU * r) * softplus
    a = jnp.exp(log_a)
    two = 2.0 * log_a
    neg_expm1 = jnp.where(two > -1e-3, -two * (1.0 + two * (0.5 + two * (1.0 / 6.0))), 1.0 - jnp.exp(two))
    bb = jnp.sqrt(neg_expm1) * gi * xc

    row = lax.broadcasted_iota(I32, a.shape, 0)
    d = 1
    while d < tc:
        a_sh = jnp.where(row >= d, pltpu.roll(a, d, axis=0), 1.0)
        b_sh = jnp.where(row >= d, pltpu.roll(bb, d, axis=0), 0.0)
        bb = a * b_sh + bb
        a = a * a_sh
        d *= 2
    h = a * h_ref[0:1, :] + bb
    h_ref[...] = jnp.broadcast_to(h[tc - 1:tc, :], h_ref.shape)
    o_ref[0] = h * _gelu_tanh(yg_ref[0])


def _block_diag(w):
    n, c, d = w.shape
    return jnp.einsum('ncd,nm->ncmd', w, jnp.eye(n, dtype=w.dtype)).reshape(n * c, n * d)


def _rglru(xr, yg, conv_w, conv_b, wa, ba, wx, bx, lam):
    b_, t_, c_ = xr.shape
    tc = min(RGLRU_CHUNK, t_)
    wa_bd = _block_diag(wa).astype(BF16)
    wx_bd = _block_diag(wx).astype(BF16)
    vec = lambda v: v.reshape(1, c_)
    full = lambda a: pl.BlockSpec(a.shape, lambda b, i: (0,) * a.ndim)
    blk = pl.BlockSpec((1, tc, c_), lambda b, i: (b, i, 0))
    args = (xr, yg, conv_w, vec(conv_b), wa_bd, vec(ba), wx_bd, vec(bx), vec(lam))
    return pl.pallas_call(
        _rglru_kernel,
        name="rglru",
        grid=(b_, t_ // tc),
        in_specs=[blk, blk] + [full(a) for a in args[2:]],
        out_specs=blk,
        out_shape=jax.ShapeDtypeStruct((b_, t_, c_), F32),
        scratch_shapes=[pltpu.VMEM((8, c_), F32), pltpu.VMEM((8, c_), F32)],
        compiler_params=_cparams(("parallel", "arbitrary"), V7X_VMEM_LIMIT_BYTES),
    )(*args)


def _out_proj_kernel(oaT_ref, orec_ref, x_ref, ga_ref, gr_ref, wa_ref, wr_ref, o_ref):
    oaT = oaT_ref[0]
    ms = jnp.mean(oaT * oaT, axis=0, keepdims=True)
    na = (oaT * lax.rsqrt(ms + RMS_EPS)).T * ga_ref[...]
    nr = _rms(orec_ref[0], gr_ref[...])
    y = (jnp.dot(na.astype(BF16), wa_ref[...], preferred_element_type=F32)
         + jnp.dot(nr.astype(BF16), wr_ref[...], preferred_element_type=F32))
    o_ref[0] = x_ref[0] + y


def _out_proj(oaT, orec, x, g_attn, g_rec, w_out):
    b_, t_, d_ = x.shape
    tm = min(256, t_)
    wa = w_out[:D_ATTN].astype(BF16)
    wr = w_out[D_ATTN:].astype(BF16)
    ga = g_attn.reshape(1, -1)
    gr = g_rec.reshape(1, -1)
    full = lambda a: pl.BlockSpec(a.shape, lambda b, i: (0,) * a.ndim)
    return pl.pallas_call(
        _out_proj_kernel,
        name="out_proj",
        grid=(b_, t_ // tm),
        in_specs=[pl.BlockSpec((1, D_ATTN, tm), lambda b, i: (b, 0, i)),
                  pl.BlockSpec((1, tm, orec.shape[2]), lambda b, i: (b, i, 0)),
                  pl.BlockSpec((1, tm, d_), lambda b, i: (b, i, 0)),
                  full(ga), full(gr), full(wa), full(wr)],
        out_specs=pl.BlockSpec((1, tm, d_), lambda b, i: (b, i, 0)),
        out_shape=jax.ShapeDtypeStruct((b_, t_, d_), F32),
        compiler_params=_cparams(("parallel", "parallel"), V7X_VMEM_LIMIT_BYTES),
    )(oaT, orec, x, ga, gr, wa, wr)


def _ffn_kernel(be_ref, x_ref, g_ref, wg_ref, wu_ref, wd_ref, o_ref, hn_ref, acc_ref, *, dense):
    del be_ref
    f = pl.program_id(1)

    @pl.when(f == 0)
    def _():
        if dense:
            x = x_ref[...]
            hn_ref[...] = _rms(x, g_ref[...]).astype(BF16)
            acc_ref[...] = x
        else:
            hn_ref[...] = x_ref[...]
            acc_ref[...] = jnp.zeros_like(acc_ref)

    hn = hn_ref[...]
    gate = jnp.dot(hn, wg_ref[0], preferred_element_type=F32)
    up = jnp.dot(hn, wu_ref[0], preferred_element_type=F32)
    act = (_silu(gate) * up).astype(BF16)
    acc_ref[...] += jnp.dot(act, wd_ref[0], preferred_element_type=F32)

    @pl.when(f == pl.num_programs(1) - 1)
    def _():
        o_ref[...] = acc_ref[...].astype(o_ref.dtype)


def _ffn(x, norm_g, wg, wu, wd, block_e, *, dense, tf):
    n, d_ = x.shape
    tm = min(ROW_TILE if dense else MOE_ROWS, n)
    ff = wg.shape[2]
    assert ff % tf == 0 and n % tm == 0
    if block_e is None:
        block_e = jnp.zeros((n // tm,), I32)
    grid_spec = pltpu.PrefetchScalarGridSpec(
        num_scalar_prefetch=1,
        grid=(n // tm, ff // tf),
        in_specs=[pl.BlockSpec((tm, d_), lambda i, f, be: (i, 0)),
                  pl.BlockSpec((1, d_), lambda i, f, be: (0, 0)),
                  pl.BlockSpec((1, d_, tf), lambda i, f, be: (be[i], 0, f)),
                  pl.BlockSpec((1, d_, tf), lambda i, f, be: (be[i], 0, f)),
                  pl.BlockSpec((1, tf, d_), lambda i, f, be: (be[i], f, 0))],
        out_specs=pl.BlockSpec((tm, d_), lambda i, f, be: (i, 0)),
        scratch_shapes=[pltpu.VMEM((tm, d_), BF16), pltpu.VMEM((tm, d_), F32)],
    )
    return pl.pallas_call(
        functools.partial(_ffn_kernel, dense=dense),
        name="ffn_dense" if dense else "ffn_expert",
        grid_spec=grid_spec,
        out_shape=jax.ShapeDtypeStruct((n, d_), F32 if dense else BF16),
        compiler_params=_cparams(("parallel", "arbitrary"), V7X_VMEM_LIMIT_BYTES),
    )(block_e, x, norm_g.reshape(1, d_), wg, wu, wd)


def _router_kernel(x_ref, g_ref, wr_ref, h_ref, e_ref, p_ref):
    hn = _rms(x_ref[...], g_ref[...])
    h_ref[...] = hn.astype(BF16)
    logits = lax.dot_general(wr_ref[...], hn, (((1,), (1,)), ((), ())),
                             precision=lax.Precision.HIGHEST, preferred_element_type=F32)
    eidx = lax.broadcasted_iota(I32, logits.shape, 0)
    m1 = jnp.max(logits, axis=0, keepdims=True)
    i1 = jnp.min(jnp.where(logits == m1, eidx, N_EXPERTS), axis=0, keepdims=True)
    rest = jnp.where(eidx == i1, -jnp.inf, logits)
    m2 = jnp.max(rest, axis=0, keepdims=True)
    i2 = jnp.min(jnp.where(rest == m2, eidx, N_EXPERTS), axis=0, keepdims=True)
    e2 = jnp.exp(m2 - m1)
    inv = 1.0 / (1.0 + e2)
    e_ref[...] = jnp.concatenate([i1, i2], axis=0)
    p_ref[...] = jnp.concatenate([inv, e2 * inv], axis=0)


def _router(x, norm_g, router_w):
    n, d_ = x.shape
    tm = min(ROW_TILE, n)
    wr = router_w.T
    return pl.pallas_call(
        _router_kernel,
        name="router",
        grid=(n // tm,),
        in_specs=[pl.BlockSpec((tm, d_), lambda i: (i, 0)),
                  pl.BlockSpec((1, d_), lambda i: (0, 0)),
                  pl.BlockSpec(wr.shape, lambda i: (0, 0))],
        out_specs=[pl.BlockSpec((tm, d_), lambda i: (i, 0)),
                   pl.BlockSpec((2, tm), lambda i: (0, i)),
                   pl.BlockSpec((2, tm), lambda i: (0, i))],
        out_shape=[jax.ShapeDtypeStruct((n, d_), BF16),
                   jax.ShapeDtypeStruct((2, n), I32),
                   jax.ShapeDtypeStruct((2, n), F32)],
        compiler_params=_cparams(("parallel",), V7X_VMEM_LIMIT_BYTES),
    )(x, norm_g.reshape(1, d_), wr)


def _gather_kernel(j_ref, c_ref, v_ref, dest_ref, h_ref, zeros_ref, o_ref, acc_ref):
    del c_ref, zeros_ref
    w = pl.program_id(0)
    nw = pl.num_programs(0)
    j = j_ref[w]
    first = (w == 0) | (j != j_ref[jnp.maximum(w - 1, 0)])
    last = (w == nw - 1) | (j != j_ref[jnp.minimum(w + 1, nw - 1)])

    @pl.when(first)
    def _():
        acc_ref[...] = jnp.zeros_like(acc_ref)

    @pl.when(v_ref[w] == 1)
    def _():
        dest = dest_ref[...]
        slot = j * acc_ref.shape[0] + lax.broadcasted_iota(I32, (acc_ref.shape[0], dest.shape[1]), 0)
        onehot = jnp.where((slot == dest[0:1, :]) | (slot == dest[1:2, :]), 1.0, 0.0).astype(BF16)
        acc_ref[...] += jnp.dot(onehot, h_ref[...], preferred_element_type=F32)

    @pl.when(last)
    def _():
        o_ref[...] = acc_ref[...].astype(BF16)


def _gather(h, dest, j_list, c_list, v_list, n_slots):
    n, d_ = h.shape
    zeros = jnp.zeros((n_slots, d_), BF16)
    grid_spec = pltpu.PrefetchScalarGridSpec(
        num_scalar_prefetch=3,
        grid=(j_list.shape[0],),
        in_specs=[pl.BlockSpec((2, MOE_TOK_CHUNK), lambda w, j, c, v: (0, c[w])),
                  pl.BlockSpec((MOE_TOK_CHUNK, d_), lambda w, j, c, v: (c[w], 0)),
                  pl.BlockSpec(memory_space=pl.ANY)],
        out_specs=pl.BlockSpec((MOE_ROWS, d_), lambda w, j, c, v: (j[w], 0)),
        scratch_shapes=[pltpu.VMEM((MOE_ROWS, d_), F32)],
    )
    return pl.pallas_call(
        _gather_kernel,
        name="moe_gather",
        grid_spec=grid_spec,
        out_shape=jax.ShapeDtypeStruct((n_slots, d_), BF16),
        input_output_aliases={5: 0},
        compiler_params=_cparams(("arbitrary",), V7X_VMEM_LIMIT_BYTES),
    )(j_list, c_list, v_list, dest, h, zeros)


N_CMB_OPS = 2 * N_EXPERTS


def _combine_kernel(jt_ref, vt_ref, x_ref, dest_ref, gate_ref, *rest):
    yb_refs = rest[:N_CMB_OPS]
    o_ref = rest[N_CMB_OPS]
    c = pl.program_id(0)
    o_ref[...] = x_ref[...]
    dest = dest_ref[...]
    gate = gate_ref[...]
    lane = lax.broadcasted_iota(I32, (dest.shape[0], CMB_ROWS), 1)
    for k in range(N_CMB_OPS):
        @pl.when(vt_ref[c * N_CMB_OPS + k] == 1)
        def _(k=k):
            slot = jt_ref[c * N_CMB_OPS + k] * CMB_ROWS + lane
            w = (jnp.where(slot == dest[:, 0:1], gate[:, 0:1], 0.0)
                 + jnp.where(slot == dest[:, 1:2], gate[:, 1:2], 0.0)).astype(BF16)
            o_ref[...] += jnp.dot(w, yb_refs[k][...], preferred_element_type=F32)


def _combine(x, yb, dest_t, gate_t, jt, vt):
    n, d_ = x.shape
    yb_spec = lambda k: pl.BlockSpec((CMB_ROWS, d_), lambda c, jt, vt, k=k: (jt[c * N_CMB_OPS + k], 0))
    grid_spec = pltpu.PrefetchScalarGridSpec(
        num_scalar_prefetch=2,
        grid=(n // CMB_ROWS,),
        in_specs=[pl.BlockSpec((CMB_ROWS, d_), lambda c, jt, vt: (c, 0)),
                  pl.BlockSpec((CMB_ROWS, 2), lambda c, jt, vt: (c, 0)),
                  pl.BlockSpec((CMB_ROWS, 2), lambda c, jt, vt: (c, 0))]
                 + [yb_spec(k) for k in range(N_CMB_OPS)],
        out_specs=pl.BlockSpec((CMB_ROWS, d_), lambda c, jt, vt: (c, 0)),
    )
    return pl.pallas_call(
        _combine_kernel,
        name="moe_combine",
        grid_spec=grid_spec,
        out_shape=jax.ShapeDtypeStruct((n, d_), F32),
        compiler_params=_cparams(("parallel",), V7X_VMEM_LIMIT_BYTES),
    )(jt, vt, x, dest_t, gate_t, *([yb] * N_CMB_OPS))


def _moe_plan(top_e, n_slots):
    n = top_e.shape[1]
    e0, e1 = top_e[0], top_e[1]
    oh = jax.nn.one_hot(e0, N_EXPERTS, dtype=I32) + jax.nn.one_hot(e1, N_EXPERTS, dtype=I32)
    csum_incl = jnp.cumsum(oh, axis=0)
    csum = csum_incl - oh
    counts = csum_incl[-1]
    padded = (counts + MOE_ROWS - 1) // MOE_ROWS * MOE_ROWS
    pad_ends = jnp.cumsum(padded)
    pad_starts = pad_ends - padded
    rank0 = jnp.take_along_axis(csum, e0[:, None], axis=1)[:, 0]
    rank1 = jnp.take_along_axis(csum, e1[:, None], axis=1)[:, 0]
    dest = jnp.stack([pad_starts[e0] + rank0, pad_starts[e1] + rank1]).astype(I32)
    n_blk = n_slots // MOE_ROWS
    blk_start = jnp.arange(n_blk, dtype=I32) * MOE_ROWS
    block_e = jnp.minimum(jnp.sum((pad_ends[None, :] <= blk_start[:, None]).astype(I32), axis=1),
                          N_EXPERTS - 1).astype(I32)

    def ranges(chunk):
        nc = n // chunk
        cb = jnp.concatenate([csum[::chunk], counts[None]], axis=0)
        lo = pad_starts[None] + cb[:-1]
        hi = pad_starts[None] + cb[1:]
        return nc, lo, hi

    nc, lo, hi = ranges(MOE_TOK_CHUNK)
    ja = lo // MOE_ROWS
    jb = (hi - 1) // MOE_ROWS
    va = hi > lo
    vb = va & (jb > ja)
    j_list = jnp.stack([ja, jb], axis=-1).transpose(1, 0, 2).reshape(-1)
    v_list = jnp.stack([va, vb], axis=-1).transpose(1, 0, 2).reshape(-1)
    c_list = jnp.broadcast_to(jnp.arange(nc, dtype=I32)[None, :, None], (N_EXPERTS, nc, 2)).reshape(-1)
    idx = lax.cummax(jnp.where(v_list, jnp.arange(v_list.shape[0], dtype=I32), -1), axis=0)
    has = idx >= 0
    idx = jnp.maximum(idx, 0)
    j_list = jnp.where(has, j_list[idx], 0).astype(I32)
    c_list = jnp.where(has, c_list[idx], 0).astype(I32)
    v_list = v_list.astype(I32)

    _, lo2, hi2 = ranges(CMB_ROWS)
    ja2 = lo2 // CMB_ROWS
    jb2 = (hi2 - 1) // CMB_ROWS
    va2 = hi2 > lo2
    vb2 = va2 & (jb2 > ja2)
    jt = jnp.stack([ja2, jb2], axis=-1).reshape(-1)
    vt = jnp.stack([va2, vb2], axis=-1).reshape(-1)
    jt = jnp.where(vt, jt, 0).astype(I32)
    return dest, block_e, j_list, c_list, v_list, jt, vt.astype(I32)


def _moe(x, norm_g, router_w, wg, wu, wd):
    n, d_ = x.shape
    n_slots = (2 * n // MOE_ROWS + N_EXPERTS) * MOE_ROWS
    h, top_e, top_p = _router(x, norm_g, router_w)
    dest, block_e, j_list, c_list, v_list, jt, vt = _moe_plan(top_e, n_slots)
    xs = _gather(h, dest, j_list, c_list, v_list, n_slots)
    yb = _ffn(xs, norm_g, wg, wu, wd, block_e, dense=False, tf=896)
    return _combine(x, yb, dest.T, top_p.T, jt, vt)


def _ple_kernel(x_ref, p_ref, g_ref, wg_ref, wp_ref, fg_ref, o_ref, *, final):
    x = x_ref[...]
    gate = _sigmoid(jnp.dot(_rms(x, g_ref[...]).astype(BF16), wg_ref[...], preferred_element_type=F32))
    proj = jnp.dot(p_ref[...].astype(BF16), wp_ref[...], preferred_element_type=F32)
    y = x + gate * proj
    if final:
        y = _rms(y, fg_ref[...])
    o_ref[...] = y


def _ple(x, p, norm_g, w_gate, w_proj, final_g, *, final):
    n, d_ = x.shape
    tm = min(ROW_TILE, n)
    wg = w_gate.astype(BF16)
    wp = w_proj.astype(BF16)
    full = lambda a: pl.BlockSpec(a.shape, lambda i: (0,) * a.ndim)
    g2 = norm_g.reshape(1, d_)
    fg = final_g.reshape(1, d_)
    return pl.pallas_call(
        functools.partial(_ple_kernel, final=final),
        name="ple_final" if final else "ple",
        grid=(n // tm,),
        in_specs=[pl.BlockSpec((tm, d_), lambda i: (i, 0)),
                  pl.BlockSpec((tm, p.shape[1]), lambda i: (i, 0)),
                  full(g2), full(wg), full(wp), full(fg)],
        out_specs=pl.BlockSpec((tm, d_), lambda i: (i, 0)),
        out_shape=jax.ShapeDtypeStruct((n, d_), F32),
        compiler_params=_cparams(("parallel",), V7X_VMEM_LIMIT_BYTES),
    )(x, p, g2, wg, wp, fg)


def kernel(x, p, attn_norm, w_in, cmp_pos, cmp_w1, cmp_w2, conv_w, conv_b, lru_wa, lru_ba, lru_wx, lru_bx,
           lru_lambda, out_norm_attn, out_norm_rec, w_out, ffn_norm, dense_w_gate, dense_w_up, dense_w_down,
           router_w, moe_w_gate, moe_w_up, moe_w_down, ple_norm, ple_w_gate, ple_w_proj, final_norm):
    b_, t_, d_ = x.shape
    depth = w_in.shape[0]
    n = b_ * t_
    for i in range(depth):
        kcvc, ks, kw, xr, yg, qT, vsT, vwT, gT = _proj_in(x, attn_norm[i], w_in[i])
        kcmp = _compress(kcvc, cmp_pos[i], cmp_w1[i], cmp_w2[i])
        oaT = _attention(qT, gT, kcmp, ks, vsT, kw, vwT)
        orec = _rglru(xr, yg, conv_w[i], conv_b[i], lru_wa[i], lru_ba[i], lru_wx[i], lru_bx[i], lru_lambda[i])
        x = _out_proj(oaT, orec, x, out_norm_attn[i], out_norm_rec[i], w_out[i])
        x2 = x.reshape(n, d_)
        j = i // 2
        if i % 2 == 0:
            x2 = _ffn(x2, ffn_norm[i], dense_w_gate[j][None].astype(BF16), dense_w_up[j][None].astype(BF16),
                      dense_w_down[j][None].astype(BF16), None, dense=True, tf=1408)
        else:
            x2 = _moe(x2, ffn_norm[i], router_w[j], moe_w_gate[j].astype(BF16), moe_w_up[j].astype(BF16),
                      moe_w_down[j].astype(BF16))
        x2 = _ple(x2, p[i].reshape(n, -1), ple_norm[i], ple_w_gate[i], ple_w_proj[i], final_norm,
                  final=(i == depth - 1))
        x = x2.reshape(b_, t_, d_)
    return x
```

```python
import functools

import jax
import jax.numpy as jnp
from jax import lax
from jax.experimental import pallas as pl
from jax.experimental.pallas import tpu as pltpu

F32 = jnp.float32
BF16 = jnp.bfloat16
I32 = jnp.int32

N_ATTN_HEADS = 8
HEAD_DIM = 64
N_KV = 2
HEADS_PER_KV = N_ATTN_HEADS // N_KV
D_ATTN = N_ATTN_HEADS * HEAD_DIM
KV_W = N_KV * HEAD_DIM
N_GATES = 3
L_CMP = 32
STRIDE = 16
L_SLC = 64
N_SEL = 16
N_LOCAL = 2
W_WIN = 512
CMP_HID = 2 * HEAD_DIM
CONV_W = 4
C_LRU = 8.0
N_EXPERTS = 8
RMS_EPS = 1e-6
ATTN_SCALE = HEAD_DIM ** -0.5
LOG2E = 1.4426950408889634
LOG2E_HI = 1.4453125
LOG2E_LO = LOG2E - LOG2E_HI

V7X_LANES = 128
V7X_VMEM_LIMIT_BYTES = 56 * 1024 * 1024

TQ = 128
KT_SLC = 256
KT_WIN = 128
N_WIN_TILES = W_WIN // KT_WIN + 1
V_ROWS = HEAD_DIM + 16
MASK_NEG = -1e30
SEL_NEG = -1.0e4
ROW_TILE = 512
RGLRU_CHUNK = 256
MOE_ROWS = 512
MOE_TOK_CHUNK = 512
CMB_ROWS = 256


def _cparams(semantics, vmem=None):
    return pltpu.CompilerParams(dimension_semantics=semantics, vmem_limit_bytes=vmem)


def _rms(x, g):
    ms = jnp.mean(x * x, axis=-1, keepdims=True)
    return x * lax.rsqrt(ms + RMS_EPS) * g


def _gelu_tanh(x):
    c = 0.7978845608028654
    return x * (0.5 * (1.0 + jnp.tanh(c * (x + 0.044715 * (x * x * x)))))


def _sigmoid(x):
    return 1.0 / (1.0 + jnp.exp(-x))


def _silu(x):
    return x * _sigmoid(x)


N_TOK_COLS = 2 * KV_W + 2 * KV_W + 2 * 512
N_TR_ROWS = D_ATTN + 2 * KV_W + 32


def _proj_in_kernel(x_ref, g_ref, wtok_ref, wtr_ref,
                    kcvc_ref, ks_ref, kw_ref, xr_ref, yg_ref, qT_ref, vsT_ref, vwT_ref, gT_ref):
    hn = _rms(x_ref[0], g_ref[...]).astype(BF16)
    tok = jnp.dot(hn, wtok_ref[...], preferred_element_type=F32)
    kcvc_ref[0] = tok[:, 0:256]
    ks_ref[0] = tok[:, 256:384].astype(BF16)
    kw_ref[0] = tok[:, 384:512].astype(BF16)
    xr_ref[0] = tok[:, 512:1024]
    yg_ref[0] = tok[:, 1024:1536]
    tr = lax.dot_general(wtr_ref[...], hn, (((1,), (1,)), ((), ())),
                         preferred_element_type=F32)
    qT_ref[0] = (tr[0:512] * (ATTN_SCALE * LOG2E)).astype(BF16)
    vsT_ref[0] = tr[512:640].astype(BF16)
    vwT_ref[0] = tr[640:768].astype(BF16)
    gT_ref[0] = _sigmoid(tr[768:800])


def _proj_in(x, norm_g, w_in):
    b_, t_, d_ = x.shape
    tm = min(ROW_TILE, t_)
    q, kc, vc, ks, vs, kw, vw, g, xr, yg = jnp.split(
        w_in, [512, 640, 768, 896, 1024, 1152, 1280, 1304, 1816], axis=1)
    wtok = jnp.concatenate([kc, vc, ks, kw, xr, yg], axis=1).astype(BF16)
    g4 = g.reshape(d_, N_KV, HEADS_PER_KV, N_GATES).transpose(0, 1, 3, 2)
    g4 = jnp.pad(g4.reshape(d_, N_KV, 12), ((0, 0), (0, 0), (0, 4))).reshape(d_, 32)
    wtr = jnp.concatenate([q, vs, vw, g4], axis=1).T.astype(BF16)
    nt = t_ // tm
    row = lambda shape: pl.BlockSpec((1, tm, shape), lambda b, i: (b, i, 0))
    col = lambda shape: pl.BlockSpec((1, shape, tm), lambda b, i: (b, 0, i))
    full = lambda a: pl.BlockSpec(a.shape, lambda b, i: (0,) * a.ndim)
    g2 = norm_g.reshape(1, d_)
    outs = pl.pallas_call(
        _proj_in_kernel,
        name="proj_in",
        grid=(b_, nt),
        in_specs=[row(d_), full(g2), full(wtok), full(wtr)],
        out_specs=[row(256), row(128), row(128), row(512), row(512),
                   col(512), col(128), col(128), col(32)],
        out_shape=[
            jax.ShapeDtypeStruct((b_, t_, 256), F32),
            jax.ShapeDtypeStruct((b_, t_, 128), BF16),
            jax.ShapeDtypeStruct((b_, t_, 128), BF16),
            jax.ShapeDtypeStruct((b_, t_, 512), F32),
            jax.ShapeDtypeStruct((b_, t_, 512), F32),
            jax.ShapeDtypeStruct((b_, 512, t_), BF16),
            jax.ShapeDtypeStruct((b_, 128, t_), BF16),
            jax.ShapeDtypeStruct((b_, 128, t_), BF16),
            jax.ShapeDtypeStruct((b_, 32, t_), F32),
        ],
        compiler_params=_cparams(("parallel", "parallel"), V7X_VMEM_LIMIT_BYTES),
    )(x, g2, wtok, wtr)
    return outs


def _compress_kernel(xc_ref, pa_ref, pb_ref, w1a_ref, w1b_ref, w2_ref, out_ref):
    xc = xc_ref[0].astype(BF16)
    n_chunk = xc.shape[0]
    a = jnp.dot(xc, w1a_ref[...], preferred_element_type=F32)
    bm = jnp.dot(xc, w1b_ref[...], preferred_element_type=F32)
    posc = (jnp.dot(pa_ref[...], w1a_ref[...], preferred_element_type=F32)
            + jnp.dot(pb_ref[...], w1b_ref[...], preferred_element_type=F32))[0:1]
    row = lax.broadcasted_iota(I32, bm.shape, 0)
    bm_up = jnp.where(row < n_chunk - 1, pltpu.roll(bm, n_chunk - 1, axis=0), 0.0)
    hid = _gelu_tanh(a + bm_up + posc).astype(BF16)
    out_ref[0] = jnp.dot(hid, w2_ref[...], preferred_element_type=F32)


def _compress(kcvc, cmp_pos, cmp_w1, cmp_w2):
    b_, t_, _ = kcvc.shape
    n_chunk = t_ // STRIDE
    xc = kcvc.reshape(b_, n_chunk, STRIDE * 256)
    eye = jnp.eye(2 * N_KV, dtype=F32)
    w1 = cmp_w1.reshape(2, 2, STRIDE, HEAD_DIM, CMP_HID)

    def expand(half):
        w = w1[:, half]
        full = jnp.einsum('wldo,wx,gy->lxgdwyo', w, jnp.eye(2, dtype=F32), jnp.eye(2, dtype=F32))
        return full.reshape(STRIDE * 256, 4 * CMP_HID).astype(BF16)

    del eye
    w1a, w1b = expand(0), expand(1)
    pos = cmp_pos.reshape(2, 2, STRIDE, HEAD_DIM)

    def posrow(half):
        p = jnp.broadcast_to(pos[:, half][:, None], (2, N_KV, STRIDE, HEAD_DIM))
        p = p.transpose(2, 0, 1, 3).reshape(1, STRIDE * 256)
        return jnp.pad(p, ((0, 7), (0, 0))).astype(BF16)

    pa, pb = posrow(0), posrow(1)
    w2 = jnp.einsum('whd,wx,gy->wghxyd', cmp_w2, jnp.eye(2, dtype=F32), jnp.eye(2, dtype=F32))
    w2 = w2.reshape(4 * CMP_HID, 4 * HEAD_DIM).astype(BF16)
    full = lambda a: pl.BlockSpec(a.shape, lambda b: (0,) * a.ndim)
    return pl.pallas_call(
        _compress_kernel,
        name="compress_kv",
        grid=(b_,),
        in_specs=[pl.BlockSpec((1, n_chunk, STRIDE * 256), lambda b: (b, 0, 0)),
                  full(pa), full(pb), full(w1a), full(w1b), full(w2)],
        out_specs=pl.BlockSpec((1, n_chunk, 256), lambda b: (b, 0, 0)),
        out_shape=jax.ShapeDtypeStruct((b_, n_chunk, 256), F32),
        compiler_params=_cparams(("parallel",), V7X_VMEM_LIMIT_BYTES),
    )(xc, pa, pb, w1a, w1b, w2)


def _head_lanes(rows):
    r = rows.shape[0] // HEADS_PER_KV
    return jnp.concatenate([rows[h * r:(h + 1) * r, :] for h in range(HEADS_PER_KV)], axis=1)


def _attn_kernel(qT_ref, gT_ref, kc_ref, vcT_ref, ks_ref, vsT_ref, kw_ref, vwT_ref, mt_ref, o_ref, s_scr):
    i = pl.program_id(1)
    n_cmp = kc_ref.shape[2]
    n_slc = mt_ref.shape[0]
    nl = HEADS_PER_KV * TQ
    hd_rows = HEADS_PER_KV * HEAD_DIM

    lane = lax.broadcasted_iota(I32, (1, nl), 1)
    t_loc = lane % TQ
    t_row = i * TQ + t_loc

    def extra_rows(n_rows, rows):
        ridx = lax.broadcasted_iota(I32, (16, nl), 0)
        out = jnp.zeros((16, nl), F32)
        for k, r in enumerate(rows):
            out = jnp.where(ridx == k, r, out)
        return jnp.concatenate([out.astype(BF16), jnp.zeros((n_rows - 16, nl), BF16)], axis=0)

    ok_c = (lax.broadcasted_iota(I32, (n_cmp, nl), 0) * STRIDE + (L_CMP - 1)) <= t_row
    r_idx = lax.broadcasted_iota(I32, (KT_WIN, nl), 0)
    s_idx = lax.broadcasted_iota(I32, (n_slc, TQ), 0)
    cur = (i * TQ + lax.broadcasted_iota(I32, (1, TQ), 1)) // L_SLC
    valid = s_idx <= cur
    forced = valid & ((s_idx == 0) | (s_idx > cur - N_LOCAL))
    s_idx_f = jnp.concatenate([s_idx.astype(F32)] * N_KV, axis=1)
    n_full = (i * TQ) // KT_SLC
    mt = mt_ref[...]

    def pick(_, score):
        m = jnp.max(score, axis=0, keepdims=True)
        idx = jnp.min(jnp.where(score == m, s_idx_f, 1.0e9), axis=0, keepdims=True)
        return jnp.where(s_idx_f == idx, -2.0, score)

    def stage_a(g, bs, tile, slot, causal):
        k0 = pl.multiple_of(tile * KT_SLC, KT_SLC)
        s = jnp.dot(ks_ref[0, g, pl.ds(k0, KT_SLC), :], bs, preferred_element_type=F32)
        if causal:
            pos = k0 + lax.broadcasted_iota(I32, (KT_SLC, nl), 0)
            s = jnp.where(pos <= t_row, s, MASK_NEG)
        s_scr[g, slot] = s
        return jnp.max(s, axis=0, keepdims=True)

    def stage_b(g, tile, slot, m, acc, m_tile):
        k0 = pl.multiple_of(tile * KT_SLC, KT_SLC)
        m_new = jnp.maximum(m, m_tile)
        p = jnp.exp2(s_scr[g, slot] - m_new)
        acc = jnp.exp2(m - m_new) * acc + jnp.dot(vsT_ref[0, g, :, pl.ds(k0, KT_SLC)], p.astype(BF16),
                                                  preferred_element_type=F32)
        return m_new, acc

    groups = range(N_KV)

    def setup(g):
        q4 = _head_lanes(qT_ref[0, g * hd_rows:(g + 1) * hd_rows, :])
        head = g * HEADS_PER_KV + lane // TQ + 1
        slope = lax.bitcast_convert_type((127 - head) << 23, F32)
        return dict(q4=q4, slope=slope, s_hi=slope * LOG2E_HI, s_lo=slope * LOG2E_LO)

    st = [setup(g) for g in groups]

    def cmp_scores(g):
        s_hi, s_lo = st[g]["s_hi"], st[g]["s_lo"]
        bc = jnp.concatenate(
            [st[g]["q4"], extra_rows(64, [256.0 * s_hi, 256.0 * s_lo, 16.0 * s_hi, 16.0 * s_lo])], axis=0)
        return jnp.dot(kc_ref[0, g], bc, preferred_element_type=F32)

    def cmp_softmax(g, s_c):
        s_c = jnp.where(ok_c, s_c, MASK_NEG)
        m_c = jnp.max(s_c, axis=0, keepdims=True)
        e_c = jnp.exp2(s_c - m_c)
        l_c = jnp.sum(e_c, axis=0, keepdims=True)
        inv_c = jnp.where(m_c > 0.5 * MASK_NEG, 1.0 / jnp.maximum(l_c, 1e-30), 0.0)
        p_c = e_c * inv_c
        o_c = jnp.dot(vcT_ref[0, g], p_c.astype(BF16), preferred_element_type=F32)
        p_grp = (p_c[:, 0:TQ] + p_c[:, TQ:2 * TQ]) + (p_c[:, 2 * TQ:3 * TQ] + p_c[:, 3 * TQ:4 * TQ])
        p1 = p_grp.astype(BF16)
        r1 = p_grp - p1.astype(F32)
        p2 = r1.astype(BF16)
        p3 = (r1 - p2.astype(F32)).astype(BF16)
        p_slc = (jnp.dot(mt, p1, preferred_element_type=F32) + jnp.dot(mt, p2, preferred_element_type=F32)
                 + jnp.dot(mt, p3, preferred_element_type=F32))
        return o_c, p_slc

    def win_scores(g):
        slope = st[g]["slope"]
        bw = jnp.concatenate([st[g]["q4"], extra_rows(64, [st[g]["s_hi"], st[g]["s_lo"]])], axis=0)
        s_tiles, c_tiles = [], []
        for a in range(N_WIN_TILES):
            tile = i - (N_WIN_TILES - 1) + a
            k0 = pl.multiple_of(jnp.maximum(tile, 0) * KT_WIN, KT_WIN)
            s = jnp.dot(kw_ref[0, g, pl.ds(k0, KT_WIN), :], bw, preferred_element_type=F32)
            if a == 0:
                s = jnp.where(r_idx > t_loc, s, MASK_NEG)
            elif a == N_WIN_TILES - 1:
                s = jnp.where(r_idx <= t_loc, s, MASK_NEG)
            s_tiles.append(s)
            c_a = slope * (LOG2E * KT_WIN * (a - (N_WIN_TILES - 1)))
            c_tiles.append(jnp.where(tile >= 0, c_a, MASK_NEG))
        return s_tiles, c_tiles

    def win_softmax(g, s_tiles, c_tiles):
        m_w = s_tiles[0].max(axis=0, keepdims=True) + c_tiles[0]
        for s, c_a in zip(s_tiles[1:], c_tiles[1:]):
            m_w = jnp.maximum(m_w, s.max(axis=0, keepdims=True) + c_a)
        acc_w = jnp.zeros((V_ROWS, nl), F32)
        for a, (s, c_a) in enumerate(zip(s_tiles, c_tiles)):
            tile = i - (N_WIN_TILES - 1) + a
            k0 = pl.multiple_of(jnp.maximum(tile, 0) * KT_WIN, KT_WIN)
            p = jnp.exp2(s - (m_w - c_a))
            acc_w = acc_w + jnp.dot(vwT_ref[0, g, :, pl.ds(k0, KT_WIN)], p.astype(BF16),
                                    preferred_element_type=F32)
        return acc_w[0:HEAD_DIM] * (1.0 / acc_w[HEAD_DIM:HEAD_DIM + 1])

    s_cmp = [cmp_scores(g) for g in groups]
    s_win = [win_scores(g) for g in groups]
    cmp_out = [cmp_softmax(g, s_cmp[g]) for g in groups]
    o_win = [win_softmax(g, *s_win[g]) for g in groups]

    score0 = jnp.concatenate(
        [jnp.where(forced, -2.0, jnp.where(valid, cmp_out[g][1], -1.0)) for g in groups], axis=1)
    score = lax.fori_loop(0, N_SEL - (N_LOCAL + 1), pick, score0, unroll=True)

    def select(g):
        sel = valid & (score[:, g * TQ:(g + 1) * TQ] < -1.5)
        sel4 = jnp.concatenate([sel] * HEADS_PER_KV, axis=1)
        blk_rows = jnp.where(sel4, 0.0, SEL_NEG).astype(BF16)
        s_hi, s_lo = st[g]["s_hi"], st[g]["s_lo"]
        bs = jnp.concatenate(
            [st[g]["q4"], blk_rows, extra_rows(64, [L_SLC * s_hi, L_SLC * s_lo, s_hi, s_lo])], axis=0)
        m_tile0 = stage_a(g, bs, n_full, 0, True)
        gt = gT_ref[0, g * 16:(g + 1) * 16, :]
        gates = [_head_lanes(gt[k * HEADS_PER_KV:(k + 1) * HEADS_PER_KV, :]) for k in range(N_GATES)]
        return dict(bs=bs, base=gates[0] * cmp_out[g][0] + gates[2] * o_win[g], g_s=gates[1], m_tile0=m_tile0)

    grp = [select(g) for g in groups]

    order = lambda k: jnp.where(k == 0, n_full, k - 1)

    def half(carry, tile_b, slot_b, tile_a):
        m_next = [c[2] if tile_a is None else stage_a(g, grp[g]["bs"], tile_a, 1 - slot_b, False)
                  for g, c in enumerate(carry)]
        out = []
        for g in range(N_KV):
            m, acc, m_tile = carry[g]
            out.append(stage_b(g, tile_b, slot_b, m, acc, m_tile) + (m_next[g],))
        return tuple(out)

    def body(p, carry):
        k = 2 * p
        return half(half(carry, order(k), 0, k), k, 1, k + 1)

    init = tuple((jnp.full((1, nl), MASK_NEG, F32), jnp.zeros((V_ROWS, nl), F32), grp[g]["m_tile0"])
                 for g in range(N_KV))
    carry = lax.fori_loop(0, n_full // 2, body, init)
    k_tail = 2 * (n_full // 2)
    carry = lax.cond(n_full % 2 == 1,
                     lambda c: half(half(c, order(k_tail), 0, k_tail), k_tail, 1, None),
                     lambda c: half(c, order(k_tail), 0, None), carry)
    for g in range(N_KV):
        _, acc, _ = carry[g]
        o_s = acc[0:HEAD_DIM] * (1.0 / acc[HEAD_DIM:HEAD_DIM + 1])
        out = grp[g]["base"] + grp[g]["g_s"] * o_s
        for h in range(HEADS_PER_KV):
            r0 = g * hd_rows + h * HEAD_DIM
            o_ref[0, r0:r0 + HEAD_DIM, :] = out[:, h * TQ:(h + 1) * TQ]


def _attention(qT, gT, kcmp, ks, vsT, kw, vwT):
    b_, _, t_ = qT.shape
    n_cmp = t_ // STRIDE
    n_slc = t_ // L_SLC
    pos = jnp.arange(t_, dtype=I32)
    grp = lambda a: a.reshape(b_, t_, N_KV, HEAD_DIM).transpose(0, 2, 1, 3)
    def pos_cols(vals, width):
        cols = [v.astype(BF16)[:, None] for v in vals for _ in range(2)]
        return jnp.concatenate(cols + [jnp.zeros((vals[0].shape[0], width - len(cols)), BF16)], axis=1)

    slc_cols = jnp.concatenate(
        [jax.nn.one_hot(pos // L_SLC, n_slc, dtype=BF16), pos_cols([pos // L_SLC, pos % L_SLC], 64)],
        axis=1)
    ks_aug = jnp.concatenate(
        [grp(ks), jnp.broadcast_to(slc_cols, (b_, N_KV) + slc_cols.shape)], axis=-1)
    win_cols = pos_cols([pos % KT_WIN], 64)
    kw_aug = jnp.concatenate(
        [grp(kw), jnp.broadcast_to(win_cols, (b_, N_KV) + win_cols.shape)], axis=-1)
    c = jnp.arange(n_cmp, dtype=I32)
    cmp_cols = pos_cols([c // 16, c % 16], 64)
    kc4 = kcmp[..., 0:KV_W].reshape(b_, n_cmp, N_KV, HEAD_DIM).transpose(0, 2, 1, 3).astype(BF16)
    kc_aug = jnp.concatenate(
        [kc4, jnp.broadcast_to(cmp_cols, (b_, N_KV) + cmp_cols.shape)], axis=-1)
    vcT = kcmp[..., KV_W:2 * KV_W].reshape(b_, n_cmp, N_KV, HEAD_DIM).transpose(0, 2, 3, 1).astype(BF16)
    ones_rows = jnp.ones((b_, N_KV, V_ROWS - HEAD_DIM, t_), BF16)
    vsT4 = jnp.concatenate([vsT.reshape(b_, N_KV, HEAD_DIM, t_), ones_rows], axis=2)
    vwT4 = jnp.concatenate([vwT.reshape(b_, N_KV, HEAD_DIM, t_), ones_rows], axis=2)
    s = jnp.arange(n_slc, dtype=I32)[:, None]
    cc = c[None, :]
    r_slc = L_SLC // STRIDE
    mt = (((cc >= r_slc * s) & (cc < r_slc * s + r_slc)).astype(F32)
          + ((cc + 1 >= r_slc * s) & (cc + 1 < r_slc * s + r_slc)).astype(F32))
    mt = jnp.where(cc < n_cmp - 1, mt, 0.0).astype(BF16)
    kv_spec = lambda a: pl.BlockSpec((1,) + a.shape[1:], lambda b, i: (b, 0, 0, 0))
    return pl.pallas_call(
        _attn_kernel,
        name="nsa_attention",
        grid=(b_, t_ // TQ),
        in_specs=[pl.BlockSpec((1, D_ATTN, TQ), lambda b, i: (b, 0, i)),
                  pl.BlockSpec((1, 32, TQ), lambda b, i: (b, 0, i)),
                  kv_spec(kc_aug), kv_spec(vcT), kv_spec(ks_aug), kv_spec(vsT4),
                  kv_spec(kw_aug), kv_spec(vwT4),
                  pl.BlockSpec(mt.shape, lambda b, i: (0, 0))],
        out_specs=pl.BlockSpec((1, D_ATTN, TQ), lambda b, i: (b, 0, i)),
        out_shape=jax.ShapeDtypeStruct((b_, D_ATTN, t_), F32),
        scratch_shapes=[pltpu.VMEM((N_KV, 2, KT_SLC, HEADS_PER_KV * TQ), F32)],
        compiler_params=_cparams(("parallel", "arbitrary"), V7X_VMEM_LIMIT_BYTES),
    )(qT, gT, kc_aug, vcT, ks_aug, vsT4, kw_aug, vwT4, mt)


def _rglru_kernel(xr_ref, yg_ref, cw_ref, cb_ref, wa_ref, ba_ref, wx_ref, bx_ref, lam_ref,
                  o_ref, tail_ref, h_ref):
    tc = xr_ref.shape[1]

    @pl.when(pl.program_id(1) == 0)
    def _():
        tail_ref[...] = jnp.zeros_like(tail_ref)
        h_ref[...] = jnp.zeros_like(h_ref)

    x = xr_ref[0]
    tail = tail_ref[...]
    row8 = lax.broadcasted_iota(I32, tail.shape, 0)
    cw = cw_ref[...]
    xc = x * cw[CONV_W - 1:CONV_W, :] + cb_ref[...]
    for d in range(1, CONV_W):
        xs = pltpu.roll(x, d, axis=0)
        head = jnp.where(row8 < d, pltpu.roll(tail, d, axis=0), xs[0:8])
        xs = jnp.concatenate([head, xs[8:]], axis=0)
        xc = xc + xs * cw[CONV_W - 1 - d:CONV_W - d, :]
    tail_ref[...] = x[tc - 8:tc]

    xb = xc.astype(BF16)
    r = _sigmoid(jnp.dot(xb, wa_ref[...], preferred_element_type=F32) + ba_ref[...])
    gi = _sigmoid(jnp.dot(xb, wx_ref[...], preferred_element_type=F32) + bx_ref[...])
    z = -lam_ref[...]
    softplus = jnp.maximum(z, 0.0) + jnp.log1p(jnp.exp(-jnp.abs(z)))
    log_a = (-C_LRU * r) * softplus
    a = jnp.exp(log_a)
    two = 2.0 * log_a
    neg_expm1 = jnp.where(two > -1e-3, -two * (1.0 + two * (0.5 + two * (1.0 / 6.0))), 1.0 - jnp.exp(two))
    bb = jnp.sqrt(neg_expm1) * gi * xc

    row = lax.broadcasted_iota(I32, a.shape, 0)
    d = 1
    while d < tc:
        a_sh = jnp.where(row >= d, pltpu.roll(a, d, axis=0), 1.0)
        b_sh = jnp.where(row >= d, pltpu.roll(bb, d, axis=0), 0.0)
        bb = a * b_sh + bb
        a = a * a_sh
        d *= 2
    h = a * h_ref[0:1, :] + bb
    h_ref[...] = jnp.broadcast_to(h[tc - 1:tc, :], h_ref.shape)
    o_ref[0] = h * _gelu_tanh(yg_ref[0])


def _block_diag(w):
    n, c, d = w.shape
    return jnp.einsum('ncd,nm->ncmd', w, jnp.eye(n, dtype=w.dtype)).reshape(n * c, n * d)


def _rglru(xr, yg, conv_w, conv_b, wa, ba, wx, bx, lam):
    b_, t_, c_ = xr.shape
    tc = min(RGLRU_CHUNK, t_)
    wa_bd = _block_diag(wa).astype(BF16)
    wx_bd = _block_diag(wx).astype(BF16)
    vec = lambda v: v.reshape(1, c_)
    full = lambda a: pl.BlockSpec(a.shape, lambda b, i: (0,) * a.ndim)
    blk = pl.BlockSpec((1, tc, c_), lambda b, i: (b, i, 0))
    args = (xr, yg, conv_w, vec(conv_b), wa_bd, vec(ba), wx_bd, vec(bx), vec(lam))
    return pl.pallas_call(
        _rglru_kernel,
        name="rglru",
        grid=(b_, t_ // tc),
        in_specs=[blk, blk] + [full(a) for a in args[2:]],
        out_specs=blk,
        out_shape=jax.ShapeDtypeStruct((b_, t_, c_), F32),
        scratch_shapes=[pltpu.VMEM((8, c_), F32), pltpu.VMEM((8, c_), F32)],
        compiler_params=_cparams(("parallel", "arbitrary"), V7X_VMEM_LIMIT_BYTES),
    )(*args)


def _out_proj_kernel(oaT_ref, orec_ref, x_ref, ga_ref, gr_ref, wa_ref, wr_ref, o_ref):
    oaT = oaT_ref[0]
    ms = jnp.mean(oaT * oaT, axis=0, keepdims=True)
    na = (oaT * lax.rsqrt(ms + RMS_EPS)).T * ga_ref[...]
    nr = _rms(orec_ref[0], gr_ref[...])
    y = (jnp.dot(na.astype(BF16), wa_ref[...], preferred_element_type=F32)
         + jnp.dot(nr.astype(BF16), wr_ref[...], preferred_element_type=F32))
    o_ref[0] = x_ref[0] + y


def _out_proj(oaT, orec, x, g_attn, g_rec, w_out):
    b_, t_, d_ = x.shape
    tm = min(256, t_)
    wa = w_out[:D_ATTN].astype(BF16)
    wr = w_out[D_ATTN:].astype(BF16)
    ga = g_attn.reshape(1, -1)
    gr = g_rec.reshape(1, -1)
    full = lambda a: pl.BlockSpec(a.shape, lambda b, i: (0,) * a.ndim)
    return pl.pallas_call(
        _out_proj_kernel,
        name="out_proj",
        grid=(b_, t_ // tm),
        in_specs=[pl.BlockSpec((1, D_ATTN, tm), lambda b, i: (b, 0, i)),
                  pl.BlockSpec((1, tm, orec.shape[2]), lambda b, i: (b, i, 0)),
                  pl.BlockSpec((1, tm, d_), lambda b, i: (b, i, 0)),
                  full(ga), full(gr), full(wa), full(wr)],
        out_specs=pl.BlockSpec((1, tm, d_), lambda b, i: (b, i, 0)),
        out_shape=jax.ShapeDtypeStruct((b_, t_, d_), F32),
        compiler_params=_cparams(("parallel", "parallel"), V7X_VMEM_LIMIT_BYTES),
    )(oaT, orec, x, ga, gr, wa, wr)


def _ffn_kernel(be_ref, x_ref, g_ref, wg_ref, wu_ref, wd_ref, o_ref, hn_ref, acc_ref, *, dense):
    del be_ref
    f = pl.program_id(1)

    @pl.when(f == 0)
    def _():
        if dense:
            x = x_ref[...]
            hn_ref[...] = _rms(x, g_ref[...]).astype(BF16)
            acc_ref[...] = x
        else:
            hn_ref[...] = x_ref[...]
            acc_ref[...] = jnp.zeros_like(acc_ref)

    hn = hn_ref[...]
    gate = jnp.dot(hn, wg_ref[0], preferred_element_type=F32)
    up = jnp.dot(hn, wu_ref[0], preferred_element_type=F32)
    act = (_silu(gate) * up).astype(BF16)
    acc_ref[...] += jnp.dot(act, wd_ref[0], preferred_element_type=F32)

    @pl.when(f == pl.num_programs(1) - 1)
    def _():
        o_ref[...] = acc_ref[...].astype(o_ref.dtype)


def _ffn(x, norm_g, wg, wu, wd, block_e, *, dense, tf):
    n, d_ = x.shape
    tm = min(ROW_TILE if dense else MOE_ROWS, n)
    ff = wg.shape[2]
    assert ff % tf == 0 and n % tm == 0
    if block_e is None:
        block_e = jnp.zeros((n // tm,), I32)
    grid_spec = pltpu.PrefetchScalarGridSpec(
        num_scalar_prefetch=1,
        grid=(n // tm, ff // tf),
        in_specs=[pl.BlockSpec((tm, d_), lambda i, f, be: (i, 0)),
                  pl.BlockSpec((1, d_), lambda i, f, be: (0, 0)),
                  pl.BlockSpec((1, d_, tf), lambda i, f, be: (be[i], 0, f)),
                  pl.BlockSpec((1, d_, tf), lambda i, f, be: (be[i], 0, f)),
                  pl.BlockSpec((1, tf, d_), lambda i, f, be: (be[i], f, 0))],
        out_specs=pl.BlockSpec((tm, d_), lambda i, f, be: (i, 0)),
        scratch_shapes=[pltpu.VMEM((tm, d_), BF16), pltpu.VMEM((tm, d_), F32)],
    )
    return pl.pallas_call(
        functools.partial(_ffn_kernel, dense=dense),
        name="ffn_dense" if dense else "ffn_expert",
        grid_spec=grid_spec,
        out_shape=jax.ShapeDtypeStruct((n, d_), F32 if dense else BF16),
        compiler_params=_cparams(("parallel", "arbitrary"), V7X_VMEM_LIMIT_BYTES),
    )(block_e, x, norm_g.reshape(1, d_), wg, wu, wd)


def _router_kernel(x_ref, g_ref, wr_ref, h_ref, e_ref, p_ref):
    hn = _rms(x_ref[...], g_ref[...])
    h_ref[...] = hn.astype(BF16)
    logits = lax.dot_general(wr_ref[...], hn, (((1,), (1,)), ((), ())),
                             precision=lax.Precision.HIGHEST, preferred_element_type=F32)
    eidx = lax.broadcasted_iota(I32, logits.shape, 0)
    m1 = jnp.max(logits, axis=0, keepdims=True)
    i1 = jnp.min(jnp.where(logits == m1, eidx, N_EXPERTS), axis=0, keepdims=True)
    rest = jnp.where(eidx == i1, -jnp.inf, logits)
    m2 = jnp.max(rest, axis=0, keepdims=True)
    i2 = jnp.min(jnp.where(rest == m2, eidx, N_EXPERTS), axis=0, keepdims=True)
    e2 = jnp.exp(m2 - m1)
    inv = 1.0 / (1.0 + e2)
    e_ref[...] = jnp.concatenate([i1, i2], axis=0)
    p_ref[...] = jnp.concatenate([inv, e2 * inv], axis=0)


def _router(x, norm_g, router_w):
    n, d_ = x.shape
    tm = min(ROW_TILE, n)
    wr = router_w.T
    return pl.pallas_call(
        _router_kernel,
        name="router",
        grid=(n // tm,),
        in_specs=[pl.BlockSpec((tm, d_), lambda i: (i, 0)),
                  pl.BlockSpec((1, d_), lambda i: (0, 0)),
                  pl.BlockSpec(wr.shape, lambda i: (0, 0))],
        out_specs=[pl.BlockSpec((tm, d_), lambda i: (i, 0)),
                   pl.BlockSpec((2, tm), lambda i: (0, i)),
                   pl.BlockSpec((2, tm), lambda i: (0, i))],
        out_shape=[jax.ShapeDtypeStruct((n, d_), BF16),
                   jax.ShapeDtypeStruct((2, n), I32),
                   jax.ShapeDtypeStruct((2, n), F32)],
        compiler_params=_cparams(("parallel",), V7X_VMEM_LIMIT_BYTES),
    )(x, norm_g.reshape(1, d_), wr)


def _gather_kernel(j_ref, c_ref, v_ref, dest_ref, h_ref, zeros_ref, o_ref, acc_ref):
    del c_ref, zeros_ref
    w = pl.program_id(0)
    nw = pl.num_programs(0)
    j = j_ref[w]
    first = (w == 0) | (j != j_ref[jnp.maximum(w - 1, 0)])
    last = (w == nw - 1) | (j != j_ref[jnp.minimum(w + 1, nw - 1)])

    @pl.when(first)
    def _():
        acc_ref[...] = jnp.zeros_like(acc_ref)

    @pl.when(v_ref[w] == 1)
    def _():
        dest = dest_ref[...]
        slot = j * acc_ref.shape[0] + lax.broadcasted_iota(I32, (acc_ref.shape[0], dest.shape[1]), 0)
        onehot = jnp.where((slot == dest[0:1, :]) | (slot == dest[1:2, :]), 1.0, 0.0).astype(BF16)
        acc_ref[...] += jnp.dot(onehot, h_ref[...], preferred_element_type=F32)

    @pl.when(last)
    def _():
        o_ref[...] = acc_ref[...].astype(BF16)


def _gather(h, dest, j_list, c_list, v_list, n_slots):
    n, d_ = h.shape
    zeros = jnp.zeros((n_slots, d_), BF16)
    grid_spec = pltpu.PrefetchScalarGridSpec(
        num_scalar_prefetch=3,
        grid=(j_list.shape[0],),
        in_specs=[pl.BlockSpec((2, MOE_TOK_CHUNK), lambda w, j, c, v: (0, c[w])),
                  pl.BlockSpec((MOE_TOK_CHUNK, d_), lambda w, j, c, v: (c[w], 0)),
                  pl.BlockSpec(memory_space=pl.ANY)],
        out_specs=pl.BlockSpec((MOE_ROWS, d_), lambda w, j, c, v: (j[w], 0)),
        scratch_shapes=[pltpu.VMEM((MOE_ROWS, d_), F32)],
    )
    return pl.pallas_call(
        _gather_kernel,
        name="moe_gather",
        grid_spec=grid_spec,
        out_shape=jax.ShapeDtypeStruct((n_slots, d_), BF16),
        input_output_aliases={5: 0},
        compiler_params=_cparams(("arbitrary",), V7X_VMEM_LIMIT_BYTES),
    )(j_list, c_list, v_list, dest, h, zeros)


N_CMB_OPS = 2 * N_EXPERTS


def _combine_kernel(jt_ref, vt_ref, x_ref, dest_ref, gate_ref, *rest):
    yb_refs = rest[:N_CMB_OPS]
    o_ref = rest[N_CMB_OPS]
    c = pl.program_id(0)
    o_ref[...] = x_ref[...]
    dest = dest_ref[...]
    gate = gate_ref[...]
    lane = lax.broadcasted_iota(I32, (dest.shape[0], CMB_ROWS), 1)
    for k in range(N_CMB_OPS):
        @pl.when(vt_ref[c * N_CMB_OPS + k] == 1)
        def _(k=k):
            slot = jt_ref[c * N_CMB_OPS + k] * CMB_ROWS + lane
            w = (jnp.where(slot == dest[:, 0:1], gate[:, 0:1], 0.0)
                 + jnp.where(slot == dest[:, 1:2], gate[:, 1:2], 0.0)).astype(BF16)
            o_ref[...] += jnp.dot(w, yb_refs[k][...], preferred_element_type=F32)


def _combine(x, yb, dest_t, gate_t, jt, vt):
    n, d_ = x.shape
    yb_spec = lambda k: pl.BlockSpec((CMB_ROWS, d_), lambda c, jt, vt, k=k: (jt[c * N_CMB_OPS + k], 0))
    grid_spec = pltpu.PrefetchScalarGridSpec(
        num_scalar_prefetch=2,
        grid=(n // CMB_ROWS,),
        in_specs=[pl.BlockSpec((CMB_ROWS, d_), lambda c, jt, vt: (c, 0)),
                  pl.BlockSpec((CMB_ROWS, 2), lambda c, jt, vt: (c, 0)),
                  pl.BlockSpec((CMB_ROWS, 2), lambda c, jt, vt: (c, 0))]
                 + [yb_spec(k) for k in range(N_CMB_OPS)],
        out_specs=pl.BlockSpec((CMB_ROWS, d_), lambda c, jt, vt: (c, 0)),
    )
    return pl.pallas_call(
        _combine_kernel,
        name="moe_combine",
        grid_spec=grid_spec,
        out_shape=jax.ShapeDtypeStruct((n, d_), F32),
        compiler_params=_cparams(("parallel",), V7X_VMEM_LIMIT_BYTES),
    )(jt, vt, x, dest_t, gate_t, *([yb] * N_CMB_OPS))


def _moe_plan(top_e, n_slots):
    n = top_e.shape[1]
    e0, e1 = top_e[0], top_e[1]
    oh = jax.nn.one_hot(e0, N_EXPERTS, dtype=I32) + jax.nn.one_hot(e1, N_EXPERTS, dtype=I32)
    csum_incl = jnp.cumsum(oh, axis=0)
    csum = csum_incl - oh
    counts = csum_incl[-1]
    padded = (counts + MOE_ROWS - 1) // MOE_ROWS * MOE_ROWS
    pad_ends = jnp.cumsum(padded)
    pad_starts = pad_ends - padded
    rank0 = jnp.take_along_axis(csum, e0[:, None], axis=1)[:, 0]
    rank1 = jnp.take_along_axis(csum, e1[:, None], axis=1)[:, 0]
    dest = jnp.stack([pad_starts[e0] + rank0, pad_starts[e1] + rank1]).astype(I32)
    n_blk = n_slots // MOE_ROWS
    blk_start = jnp.arange(n_blk, dtype=I32) * MOE_ROWS
    block_e = jnp.minimum(jnp.sum((pad_ends[None, :] <= blk_start[:, None]).astype(I32), axis=1),
                          N_EXPERTS - 1).astype(I32)

    def ranges(chunk):
        nc = n // chunk
        cb = jnp.concatenate([csum[::chunk], counts[None]], axis=0)
        lo = pad_starts[None] + cb[:-1]
        hi = pad_starts[None] + cb[1:]
        return nc, lo, hi

    nc, lo, hi = ranges(MOE_TOK_CHUNK)
    ja = lo // MOE_ROWS
    jb = (hi - 1) // MOE_ROWS
    va = hi > lo
    vb = va & (jb > ja)
    j_list = jnp.stack([ja, jb], axis=-1).transpose(1, 0, 2).reshape(-1)
    v_list = jnp.stack([va, vb], axis=-1).transpose(1, 0, 2).reshape(-1)
    c_list = jnp.broadcast_to(jnp.arange(nc, dtype=I32)[None, :, None], (N_EXPERTS, nc, 2)).reshape(-1)
    idx = lax.cummax(jnp.where(v_list, jnp.arange(v_list.shape[0], dtype=I32), -1), axis=0)
    has = idx >= 0
    idx = jnp.maximum(idx, 0)
    j_list = jnp.where(has, j_list[idx], 0).astype(I32)
    c_list = jnp.where(has, c_list[idx], 0).astype(I32)
    v_list = v_list.astype(I32)

    _, lo2, hi2 = ranges(CMB_ROWS)
    ja2 = lo2 // CMB_ROWS
    jb2 = (hi2 - 1) // CMB_ROWS
    va2 = hi2 > lo2
    vb2 = va2 & (jb2 > ja2)
    jt = jnp.stack([ja2, jb2], axis=-1).reshape(-1)
    vt = jnp.stack([va2, vb2], axis=-1).reshape(-1)
    jt = jnp.where(vt, jt, 0).astype(I32)
    return dest, block_e, j_list, c_list, v_list, jt, vt.astype(I32)


def _moe(x, norm_g, router_w, wg, wu, wd):
    n, d_ = x.shape
    n_slots = (2 * n // MOE_ROWS + N_EXPERTS) * MOE_ROWS
    h, top_e, top_p = _router(x, norm_g, router_w)
    dest, block_e, j_list, c_list, v_list, jt, vt = _moe_plan(top_e, n_slots)
    xs = _gather(h, dest, j_list, c_list, v_list, n_slots)
    yb = _ffn(xs, norm_g, wg, wu, wd, block_e, dense=False, tf=896)
    return _combine(x, yb, dest.T, top_p.T, jt, vt)


def _ple_kernel(x_ref, p_ref, g_ref, wg_ref, wp_ref, fg_ref, o_ref, *, final):
    x = x_ref[...]
    gate = _sigmoid(jnp.dot(_rms(x, g_ref[...]).astype(BF16), wg_ref[...], preferred_element_type=F32))
    proj = jnp.dot(p_ref[...].astype(BF16), wp_ref[...], preferred_element_type=F32)
    y = x + gate * proj
    if final:
        y = _rms(y, fg_ref[...])
    o_ref[...] = y


def _ple(x, p, norm_g, w_gate, w_proj, final_g, *, final):
    n, d_ = x.shape
    tm = min(ROW_TILE, n)
    wg = w_gate.astype(BF16)
    wp = w_proj.astype(BF16)
    full = lambda a: pl.BlockSpec(a.shape, lambda i: (0,) * a.ndim)
    g2 = norm_g.reshape(1, d_)
    fg = final_g.reshape(1, d_)
    return pl.pallas_call(
        functools.partial(_ple_kernel, final=final),
        name="ple_final" if final else "ple",
        grid=(n // tm,),
        in_specs=[pl.BlockSpec((tm, d_), lambda i: (i, 0)),
                  pl.BlockSpec((tm, p.shape[1]), lambda i: (i, 0)),
                  full(g2), full(wg), full(wp), full(fg)],
        out_specs=pl.BlockSpec((tm, d_), lambda i: (i, 0)),
        out_shape=jax.ShapeDtypeStruct((n, d_), F32),
        compiler_params=_cparams(("parallel",), V7X_VMEM_LIMIT_BYTES),
    )(x, p, g2, wg, wp, fg)


def kernel(x, p, attn_norm, w_in, cmp_pos, cmp_w1, cmp_w2, conv_w, conv_b, lru_wa, lru_ba, lru_wx, lru_bx,
           lru_lambda, out_norm_attn, out_norm_rec, w_out, ffn_norm, dense_w_gate, dense_w_up, dense_w_down,
           router_w, moe_w_gate, moe_w_up, moe_w_down, ple_norm, ple_w_gate, ple_w_proj, final_norm):
    b_, t_, d_ = x.shape
    depth = w_in.shape[0]
    n = b_ * t_
    for i in range(depth):
        kcvc, ks, kw, xr, yg, qT, vsT, vwT, gT = _proj_in(x, attn_norm[i], w_in[i])
        kcmp = _compress(kcvc, cmp_pos[i], cmp_w1[i], cmp_w2[i])
        oaT = _attention(qT, gT, kcmp, ks, vsT, kw, vwT)
        orec = _rglru(xr, yg, conv_w[i], conv_b[i], lru_wa[i], lru_ba[i], lru_wx[i], lru_bx[i], lru_lambda[i])
        x = _out_proj(oaT, orec, x, out_norm_attn[i], out_norm_rec[i], w_out[i])
        x2 = x.reshape(n, d_)
        j = i // 2
        if i % 2 == 0:
            x2 = _ffn(x2, ffn_norm[i], dense_w_gate[j][None].astype(BF16), dense_w_up[j][None].astype(BF16),
                      dense_w_down[j][None].astype(BF16), None, dense=True, tf=1408)
        else:
            x2 = _moe(x2, ffn_norm[i], router_w[j], moe_w_gate[j].astype(BF16), moe_w_up[j].astype(BF16),
                      moe_w_down[j].astype(BF16))
        x2 = _ple(x2, p[i].reshape(n, -1), ple_norm[i], ple_w_gate[i], ple_w_proj[i], final_norm,
                  final=(i == depth - 1))
        x = x2.reshape(b_, t_, d_)
    return x
```

```python
import functools

import jax
import jax.numpy as jnp
from jax import lax
from jax.experimental import pallas as pl
from jax.experimental.pallas import tpu as pltpu

F32 = jnp.float32
BF16 = jnp.bfloat16
I32 = jnp.int32

N_ATTN_HEADS = 8
HEAD_DIM = 64
N_KV = 2
HEADS_PER_KV = N_ATTN_HEADS // N_KV
D_ATTN = N_ATTN_HEADS * HEAD_DIM
KV_W = N_KV * HEAD_DIM
N_GATES = 3
L_CMP = 32
STRIDE = 16
L_SLC = 64
N_SEL = 16
N_LOCAL = 2
W_WIN = 512
CMP_HID = 2 * HEAD_DIM
CONV_W = 4
C_LRU = 8.0
N_EXPERTS = 8
RMS_EPS = 1e-6
ATTN_SCALE = HEAD_DIM ** -0.5
LOG2E = 1.4426950408889634
LOG2E_HI = 1.4453125
LOG2E_LO = LOG2E - LOG2E_HI

V7X_LANES = 128
V7X_VMEM_LIMIT_BYTES = 56 * 1024 * 1024

TQ = 128
KT_SLC = 256
KT_WIN = 128
N_WIN_TILES = W_WIN // KT_WIN + 1
V_ROWS = HEAD_DIM + 16
MASK_NEG = -1e30
SEL_NEG = -1.0e4
ROW_TILE = 512
RGLRU_CHUNK = 256
MOE_ROWS = 512
MOE_TOK_CHUNK = 512
CMB_ROWS = 256


def _cparams(semantics, vmem=None):
    return pltpu.CompilerParams(dimension_semantics=semantics, vmem_limit_bytes=vmem)


def _rms(x, g):
    ms = jnp.mean(x * x, axis=-1, keepdims=True)
    return x * lax.rsqrt(ms + RMS_EPS) * g


def _gelu_tanh(x):
    c = 0.7978845608028654
    return x * (0.5 * (1.0 + jnp.tanh(c * (x + 0.044715 * (x * x * x)))))


def _sigmoid(x):
    return 1.0 / (1.0 + jnp.exp(-x))


def _silu(x):
    return x * _sigmoid(x)


N_TOK_COLS = 2 * KV_W + 2 * KV_W + 2 * 512
N_TR_ROWS = D_ATTN + 2 * KV_W + 32


def _proj_in_kernel(x_ref, g_ref, wtok_ref, wtr_ref,
                    kcvc_ref, ks_ref, kw_ref, xr_ref, yg_ref, qT_ref, vsT_ref, vwT_ref, gT_ref):
    hn = _rms(x_ref[0], g_ref[...]).astype(BF16)
    tok = jnp.dot(hn, wtok_ref[...], preferred_element_type=F32)
    kcvc_ref[0] = tok[:, 0:256]
    ks_ref[0] = tok[:, 256:384].astype(BF16)
    kw_ref[0] = tok[:, 384:512].astype(BF16)
    xr_ref[0] = tok[:, 512:1024]
    yg_ref[0] = tok[:, 1024:1536]
    tr = lax.dot_general(wtr_ref[...], hn, (((1,), (1,)), ((), ())),
                         preferred_element_type=F32)
    qT_ref[0] = (tr[0:512] * (ATTN_SCALE * LOG2E)).astype(BF16)
    vsT_ref[0] = tr[512:640].astype(BF16)
    vwT_ref[0] = tr[640:768].astype(BF16)
    gT_ref[0] = _sigmoid(tr[768:800])


def _proj_in(x, norm_g, w_in):
    b_, t_, d_ = x.shape
    tm = min(ROW_TILE, t_)
    q, kc, vc, ks, vs, kw, vw, g, xr, yg = jnp.split(
        w_in, [512, 640, 768, 896, 1024, 1152, 1280, 1304, 1816], axis=1)
    wtok = jnp.concatenate([kc, vc, ks, kw, xr, yg], axis=1).astype(BF16)
    g4 = g.reshape(d_, N_KV, HEADS_PER_KV, N_GATES).transpose(0, 1, 3, 2)
    g4 = jnp.pad(g4.reshape(d_, N_KV, 12), ((0, 0), (0, 0), (0, 4))).reshape(d_, 32)
    wtr = jnp.concatenate([q, vs, vw, g4], axis=1).T.astype(BF16)
    nt = t_ // tm
    row = lambda shape: pl.BlockSpec((1, tm, shape), lambda b, i: (b, i, 0))
    col = lambda shape: pl.BlockSpec((1, shape, tm), lambda b, i: (b, 0, i))
    full = lambda a: pl.BlockSpec(a.shape, lambda b, i: (0,) * a.ndim)
    g2 = norm_g.reshape(1, d_)
    outs = pl.pallas_call(
        _proj_in_kernel,
        name="proj_in",
        grid=(b_, nt),
        in_specs=[row(d_), full(g2), full(wtok), full(wtr)],
        out_specs=[row(256), row(128), row(128), row(512), row(512),
                   col(512), col(128), col(128), col(32)],
        out_shape=[
            jax.ShapeDtypeStruct((b_, t_, 256), F32),
            jax.ShapeDtypeStruct((b_, t_, 128), BF16),
            jax.ShapeDtypeStruct((b_, t_, 128), BF16),
            jax.ShapeDtypeStruct((b_, t_, 512), F32),
            jax.ShapeDtypeStruct((b_, t_, 512), F32),
            jax.ShapeDtypeStruct((b_, 512, t_), BF16),
            jax.ShapeDtypeStruct((b_, 128, t_), BF16),
            jax.ShapeDtypeStruct((b_, 128, t_), BF16),
            jax.ShapeDtypeStruct((b_, 32, t_), F32),
        ],
        compiler_params=_cparams(("parallel", "parallel"), V7X_VMEM_LIMIT_BYTES),
    )(x, g2, wtok, wtr)
    return outs


def _compress_kernel(xc_ref, pa_ref, pb_ref, w1a_ref, w1b_ref, w2_ref, out_ref):
    xc = xc_ref[0].astype(BF16)
    n_chunk = xc.shape[0]
    a = jnp.dot(xc, w1a_ref[...], preferred_element_type=F32)
    bm = jnp.dot(xc, w1b_ref[...], preferred_element_type=F32)
    posc = (jnp.dot(pa_ref[...], w1a_ref[...], preferred_element_type=F32)
            + jnp.dot(pb_ref[...], w1b_ref[...], preferred_element_type=F32))[0:1]
    row = lax.broadcasted_iota(I32, bm.shape, 0)
    bm_up = jnp.where(row < n_chunk - 1, pltpu.roll(bm, n_chunk - 1, axis=0), 0.0)
    hid = _gelu_tanh(a + bm_up + posc).astype(BF16)
    out_ref[0] = jnp.dot(hid, w2_ref[...], preferred_element_type=F32)


def _compress(kcvc, cmp_pos, cmp_w1, cmp_w2):
    b_, t_, _ = kcvc.shape
    n_chunk = t_ // STRIDE
    xc = kcvc.reshape(b_, n_chunk, STRIDE * 256)
    eye = jnp.eye(2 * N_KV, dtype=F32)
    w1 = cmp_w1.reshape(2, 2, STRIDE, HEAD_DIM, CMP_HID)

    def expand(half):
        w = w1[:, half]
        full = jnp.einsum('wldo,wx,gy->lxgdwyo', w, jnp.eye(2, dtype=F32), jnp.eye(2, dtype=F32))
        return full.reshape(STRIDE * 256, 4 * CMP_HID).astype(BF16)

    del eye
    w1a, w1b = expand(0), expand(1)
    pos = cmp_pos.reshape(2, 2, STRIDE, HEAD_DIM)

    def posrow(half):
        p = jnp.broadcast_to(pos[:, half][:, None], (2, N_KV, STRIDE, HEAD_DIM))
        p = p.transpose(2, 0, 1, 3).reshape(1, STRIDE * 256)
        return jnp.pad(p, ((0, 7), (0, 0))).astype(BF16)

    pa, pb = posrow(0), posrow(1)
    w2 = jnp.einsum('whd,wx,gy->wghxyd', cmp_w2, jnp.eye(2, dtype=F32), jnp.eye(2, dtype=F32))
    w2 = w2.reshape(4 * CMP_HID, 4 * HEAD_DIM).astype(BF16)
    full = lambda a: pl.BlockSpec(a.shape, lambda b: (0,) * a.ndim)
    return pl.pallas_call(
        _compress_kernel,
        name="compress_kv",
        grid=(b_,),
        in_specs=[pl.BlockSpec((1, n_chunk, STRIDE * 256), lambda b: (b, 0, 0)),
                  full(pa), full(pb), full(w1a), full(w1b), full(w2)],
        out_specs=pl.BlockSpec((1, n_chunk, 256), lambda b: (b, 0, 0)),
        out_shape=jax.ShapeDtypeStruct((b_, n_chunk, 256), F32),
        compiler_params=_cparams(("parallel",), V7X_VMEM_LIMIT_BYTES),
    )(xc, pa, pb, w1a, w1b, w2)


def _head_lanes(rows):
    r = rows.shape[0] // HEADS_PER_KV
    return jnp.concatenate([rows[h * r:(h + 1) * r, :] for h in range(HEADS_PER_KV)], axis=1)


def _attn_kernel(qT_ref, gT_ref, kc_ref, vcT_ref, ks_ref, vsT_ref, kw_ref, vwT_ref, mt_ref, o_ref, s_scr,
                 tile_list):
    i = pl.program_id(1)
    n_cmp = kc_ref.shape[2]
    n_slc = mt_ref.shape[0]
    nl = HEADS_PER_KV * TQ
    hd_rows = HEADS_PER_KV * HEAD_DIM

    lane = lax.broadcasted_iota(I32, (1, nl), 1)
    t_loc = lane % TQ
    t_row = i * TQ + t_loc

    def extra_rows(n_rows, rows):
        ridx = lax.broadcasted_iota(I32, (16, nl), 0)
        out = jnp.zeros((16, nl), F32)
        for k, r in enumerate(rows):
            out = jnp.where(ridx == k, r, out)
        return jnp.concatenate([out.astype(BF16), jnp.zeros((n_rows - 16, nl), BF16)], axis=0)

    ok_c = (lax.broadcasted_iota(I32, (n_cmp, nl), 0) * STRIDE + (L_CMP - 1)) <= t_row
    r_idx = lax.broadcasted_iota(I32, (KT_WIN, nl), 0)
    s_idx = lax.broadcasted_iota(I32, (n_slc, TQ), 0)
    cur = (i * TQ + lax.broadcasted_iota(I32, (1, TQ), 1)) // L_SLC
    valid = s_idx <= cur
    forced = valid & ((s_idx == 0) | (s_idx > cur - N_LOCAL))
    s_idx_f = jnp.concatenate([s_idx.astype(F32)] * N_KV, axis=1)
    n_full = (i * TQ) // KT_SLC
    mt = mt_ref[...]

    def pick(_, score):
        m = jnp.max(score, axis=0, keepdims=True)
        idx = jnp.min(jnp.where(score == m, s_idx_f, 1.0e9), axis=0, keepdims=True)
        return jnp.where(s_idx_f == idx, -2.0, score)

    def stage_a(g, bs, tile, slot, causal):
        k0 = pl.multiple_of(tile * KT_SLC, KT_SLC)
        s = jnp.dot(ks_ref[0, g, pl.ds(k0, KT_SLC), :], bs, preferred_element_type=F32)
        if causal:
            pos = k0 + lax.broadcasted_iota(I32, (KT_SLC, nl), 0)
            s = jnp.where(pos <= t_row, s, MASK_NEG)
        s_scr[g, slot] = s
        return jnp.max(s, axis=0, keepdims=True)

    def stage_b(g, tile, slot, m, acc, m_tile):
        k0 = pl.multiple_of(tile * KT_SLC, KT_SLC)
        m_new = jnp.maximum(m, m_tile)
        p = jnp.exp2(s_scr[g, slot] - m_new)
        acc = jnp.exp2(m - m_new) * acc + jnp.dot(vsT_ref[0, g, :, pl.ds(k0, KT_SLC)], p.astype(BF16),
                                                  preferred_element_type=F32)
        return m_new, acc

    groups = range(N_KV)

    def setup(g):
        q4 = _head_lanes(qT_ref[0, g * hd_rows:(g + 1) * hd_rows, :])
        head = g * HEADS_PER_KV + lane // TQ + 1
        slope = lax.bitcast_convert_type((127 - head) << 23, F32)
        return dict(q4=q4, slope=slope, s_hi=slope * LOG2E_HI, s_lo=slope * LOG2E_LO)

    st = [setup(g) for g in groups]

    def cmp_scores(g):
        s_hi, s_lo = st[g]["s_hi"], st[g]["s_lo"]
        bc = jnp.concatenate(
            [st[g]["q4"], extra_rows(64, [256.0 * s_hi, 256.0 * s_lo, 16.0 * s_hi, 16.0 * s_lo])], axis=0)
        return jnp.dot(kc_ref[0, g], bc, preferred_element_type=F32)

    def cmp_softmax(g, s_c):
        s_c = jnp.where(ok_c, s_c, MASK_NEG)
        m_c = jnp.max(s_c, axis=0, keepdims=True)
        e_c = jnp.exp2(s_c - m_c)
        l_c = jnp.sum(e_c, axis=0, keepdims=True)
        inv_c = jnp.where(m_c > 0.5 * MASK_NEG, 1.0 / jnp.maximum(l_c, 1e-30), 0.0)
        p_c = e_c * inv_c
        o_c = jnp.dot(vcT_ref[0, g], p_c.astype(BF16), preferred_element_type=F32)
        p_grp = (p_c[:, 0:TQ] + p_c[:, TQ:2 * TQ]) + (p_c[:, 2 * TQ:3 * TQ] + p_c[:, 3 * TQ:4 * TQ])
        p1 = p_grp.astype(BF16)
        r1 = p_grp - p1.astype(F32)
        p2 = r1.astype(BF16)
        p3 = (r1 - p2.astype(F32)).astype(BF16)
        p_slc = (jnp.dot(mt, p1, preferred_element_type=F32) + jnp.dot(mt, p2, preferred_element_type=F32)
                 + jnp.dot(mt, p3, preferred_element_type=F32))
        return o_c, p_slc

    def win_scores(g):
        slope = st[g]["slope"]
        bw = jnp.concatenate([st[g]["q4"], extra_rows(64, [st[g]["s_hi"], st[g]["s_lo"]])], axis=0)
        s_tiles, c_tiles = [], []
        for a in range(N_WIN_TILES):
            tile = i - (N_WIN_TILES - 1) + a
            k0 = pl.multiple_of(jnp.maximum(tile, 0) * KT_WIN, KT_WIN)
            s = jnp.dot(kw_ref[0, g, pl.ds(k0, KT_WIN), :], bw, preferred_element_type=F32)
            if a == 0:
                s = jnp.where(r_idx > t_loc, s, MASK_NEG)
            elif a == N_WIN_TILES - 1:
                s = jnp.where(r_idx <= t_loc, s, MASK_NEG)
            s_tiles.append(s)
            c_a = slope * (LOG2E * KT_WIN * (a - (N_WIN_TILES - 1)))
            c_tiles.append(jnp.where(tile >= 0, c_a, MASK_NEG))
        return s_tiles, c_tiles

    def win_softmax(g, s_tiles, c_tiles):
        m_w = s_tiles[0].max(axis=0, keepdims=True) + c_tiles[0]
        for s, c_a in zip(s_tiles[1:], c_tiles[1:]):
            m_w = jnp.maximum(m_w, s.max(axis=0, keepdims=True) + c_a)
        acc_w = jnp.zeros((V_ROWS, nl), F32)
        for a, (s, c_a) in enumerate(zip(s_tiles, c_tiles)):
            tile = i - (N_WIN_TILES - 1) + a
            k0 = pl.multiple_of(jnp.maximum(tile, 0) * KT_WIN, KT_WIN)
            p = jnp.exp2(s - (m_w - c_a))
            acc_w = acc_w + jnp.dot(vwT_ref[0, g, :, pl.ds(k0, KT_WIN)], p.astype(BF16),
                                    preferred_element_type=F32)
        return acc_w[0:HEAD_DIM] * (1.0 / acc_w[HEAD_DIM:HEAD_DIM + 1])

    s_cmp = [cmp_scores(g) for g in groups]
    s_win = [win_scores(g) for g in groups]
    cmp_out = [cmp_softmax(g, s_cmp[g]) for g in groups]
    o_win = [win_softmax(g, *s_win[g]) for g in groups]

    score0 = jnp.concatenate(
        [jnp.where(forced, -2.0, jnp.where(valid, cmp_out[g][1], -1.0)) for g in groups], axis=1)
    score = lax.fori_loop(0, N_SEL - (N_LOCAL + 1), pick, score0, unroll=True)

    def select(g):
        sel = valid & (score[:, g * TQ:(g + 1) * TQ] < -1.5)
        sel4 = jnp.concatenate([sel] * HEADS_PER_KV, axis=1)
        blk_rows = jnp.where(sel4, 0.0, SEL_NEG).astype(BF16)
        s_hi, s_lo = st[g]["s_hi"], st[g]["s_lo"]
        bs = jnp.concatenate(
            [st[g]["q4"], blk_rows, extra_rows(64, [L_SLC * s_hi, L_SLC * s_lo, s_hi, s_lo])], axis=0)
        m_tile0 = stage_a(g, bs, n_full, 0, True)
        gt = gT_ref[0, g * 16:(g + 1) * 16, :]
        gates = [_head_lanes(gt[k * HEADS_PER_KV:(k + 1) * HEADS_PER_KV, :]) for k in range(N_GATES)]
        return dict(bs=bs, base=gates[0] * cmp_out[g][0] + gates[2] * o_win[g], g_s=gates[1], m_tile0=m_tile0)

    grp = [select(g) for g in groups]

    blocks_per_tile = KT_SLC // L_SLC
    n_tiles = n_slc // blocks_per_tile
    assert n_tiles <= 32
    valid2 = jnp.concatenate([valid] * N_KV, axis=1)
    sel_all = jnp.where(valid2 & (score < -1.5), 1.0, 0.0).astype(BF16)
    blk_cnt = lax.dot_general(jnp.ones((8, N_KV * TQ), BF16), sel_all, (((1,), (1,)), ((), ())),
                              preferred_element_type=F32)
    memb = jnp.where(lax.broadcasted_iota(I32, (n_slc, V7X_LANES), 0) // blocks_per_tile
                     == lax.broadcasted_iota(I32, (n_slc, V7X_LANES), 1), 1.0, 0.0).astype(BF16)
    tile_cnt = jnp.dot(jnp.where(blk_cnt > 0.5, 1.0, 0.0).astype(BF16), memb,
                       preferred_element_type=F32)[0:1]
    lane_t = lax.broadcasted_iota(I32, (1, V7X_LANES), 1)
    need = (tile_cnt > 0.5) & (lane_t < n_full)
    bit = lax.bitcast_convert_type(((lane_t & 15) + 127) << 23, F32)
    lo_bits = jnp.sum(jnp.where(need & (lane_t < 16), bit, 0.0)).astype(I32)
    hi_bits = jnp.sum(jnp.where(need & (lane_t >= 16), bit, 0.0)).astype(I32)
    n_sel = jnp.int32(0)
    for t in range(n_tiles - 1):
        tile_list[n_sel] = t
        n_sel = n_sel + (((lo_bits if t < 16 else hi_bits) >> (t % 16)) & 1)

    order = lambda k: jnp.where(k == 0, n_full, tile_list[jnp.maximum(k - 1, 0)])

    def half(carry, tile_b, slot_b, tile_a):
        m_next = [c[2] if tile_a is None else stage_a(g, grp[g]["bs"], tile_a, 1 - slot_b, False)
                  for g, c in enumerate(carry)]
        out = []
        for g in range(N_KV):
            m, acc, m_tile = carry[g]
            out.append(stage_b(g, tile_b, slot_b, m, acc, m_tile) + (m_next[g],))
        return tuple(out)

    def body(p, carry):
        k = 2 * p
        return half(half(carry, order(k), 0, order(k + 1)), order(k + 1), 1, order(k + 2))

    init = tuple((jnp.full((1, nl), MASK_NEG, F32), jnp.zeros((V_ROWS, nl), F32), grp[g]["m_tile0"])
                 for g in range(N_KV))
    carry = lax.fori_loop(0, n_sel // 2, body, init)
    k_tail = 2 * (n_sel // 2)
    carry = lax.cond(n_sel % 2 == 1,
                     lambda c: half(half(c, order(k_tail), 0, order(k_tail + 1)), order(k_tail + 1), 1, None),
                     lambda c: half(c, order(k_tail), 0, None), carry)
    for g in range(N_KV):
        _, acc, _ = carry[g]
        o_s = acc[0:HEAD_DIM] * (1.0 / acc[HEAD_DIM:HEAD_DIM + 1])
        out = grp[g]["base"] + grp[g]["g_s"] * o_s
        for h in range(HEADS_PER_KV):
            r0 = g * hd_rows + h * HEAD_DIM
            o_ref[0, r0:r0 + HEAD_DIM, :] = out[:, h * TQ:(h + 1) * TQ]


def _attention(qT, gT, kcmp, ks, vsT, kw, vwT):
    b_, _, t_ = qT.shape
    n_cmp = t_ // STRIDE
    n_slc = t_ // L_SLC
    pos = jnp.arange(t_, dtype=I32)
    grp = lambda a: a.reshape(b_, t_, N_KV, HEAD_DIM).transpose(0, 2, 1, 3)
    def pos_cols(vals, width):
        cols = [v.astype(BF16)[:, None] for v in vals for _ in range(2)]
        return jnp.concatenate(cols + [jnp.zeros((vals[0].shape[0], width - len(cols)), BF16)], axis=1)

    slc_cols = jnp.concatenate(
        [jax.nn.one_hot(pos // L_SLC, n_slc, dtype=BF16), pos_cols([pos // L_SLC, pos % L_SLC], 64)],
        axis=1)
    ks_aug = jnp.concatenate(
        [grp(ks), jnp.broadcast_to(slc_cols, (b_, N_KV) + slc_cols.shape)], axis=-1)
    win_cols = pos_cols([pos % KT_WIN], 64)
    kw_aug = jnp.concatenate(
        [grp(kw), jnp.broadcast_to(win_cols, (b_, N_KV) + win_cols.shape)], axis=-1)
    c = jnp.arange(n_cmp, dtype=I32)
    cmp_cols = pos_cols([c // 16, c % 16], 64)
    kc4 = kcmp[..., 0:KV_W].reshape(b_, n_cmp, N_KV, HEAD_DIM).transpose(0, 2, 1, 3).astype(BF16)
    kc_aug = jnp.concatenate(
        [kc4, jnp.broadcast_to(cmp_cols, (b_, N_KV) + cmp_cols.shape)], axis=-1)
    vcT = kcmp[..., KV_W:2 * KV_W].reshape(b_, n_cmp, N_KV, HEAD_DIM).transpose(0, 2, 3, 1).astype(BF16)
    ones_rows = jnp.ones((b_, N_KV, V_ROWS - HEAD_DIM, t_), BF16)
    vsT4 = jnp.concatenate([vsT.reshape(b_, N_KV, HEAD_DIM, t_), ones_rows], axis=2)
    vwT4 = jnp.concatenate([vwT.reshape(b_, N_KV, HEAD_DIM, t_), ones_rows], axis=2)
    s = jnp.arange(n_slc, dtype=I32)[:, None]
    cc = c[None, :]
    r_slc = L_SLC // STRIDE
    mt = (((cc >= r_slc * s) & (cc < r_slc * s + r_slc)).astype(F32)
          + ((cc + 1 >= r_slc * s) & (cc + 1 < r_slc * s + r_slc)).astype(F32))
    mt = jnp.where(cc < n_cmp - 1, mt, 0.0).astype(BF16)
    kv_spec = lambda a: pl.BlockSpec((1,) + a.shape[1:], lambda b, i: (b, 0, 0, 0))
    return pl.pallas_call(
        _attn_kernel,
        name="nsa_attention",
        grid=(b_, t_ // TQ),
        in_specs=[pl.BlockSpec((1, D_ATTN, TQ), lambda b, i: (b, 0, i)),
                  pl.BlockSpec((1, 32, TQ), lambda b, i: (b, 0, i)),
                  kv_spec(kc_aug), kv_spec(vcT), kv_spec(ks_aug), kv_spec(vsT4),
                  kv_spec(kw_aug), kv_spec(vwT4),
                  pl.BlockSpec(mt.shape, lambda b, i: (0, 0))],
        out_specs=pl.BlockSpec((1, D_ATTN, TQ), lambda b, i: (b, 0, i)),
        out_shape=jax.ShapeDtypeStruct((b_, D_ATTN, t_), F32),
        scratch_shapes=[pltpu.VMEM((N_KV, 2, KT_SLC, HEADS_PER_KV * TQ), F32), pltpu.SMEM((32,), I32)],
        compiler_params=_cparams(("parallel", "arbitrary"), V7X_VMEM_LIMIT_BYTES),
    )(qT, gT, kc_aug, vcT, ks_aug, vsT4, kw_aug, vwT4, mt)


def _rglru_kernel(xr_ref, yg_ref, cw_ref, cb_ref, wa_ref, ba_ref, wx_ref, bx_ref, lam_ref,
                  o_ref, tail_ref, h_ref):
    tc = xr_ref.shape[1]

    @pl.when(pl.program_id(1) == 0)
    def _():
        tail_ref[...] = jnp.zeros_like(tail_ref)
        h_ref[...] = jnp.zeros_like(h_ref)

    x = xr_ref[0]
    tail = tail_ref[...]
    row8 = lax.broadcasted_iota(I32, tail.shape, 0)
    cw = cw_ref[...]
    xc = x * cw[CONV_W - 1:CONV_W, :] + cb_ref[...]
    for d in range(1, CONV_W):
        xs = pltpu.roll(x, d, axis=0)
        head = jnp.where(row8 < d, pltpu.roll(tail, d, axis=0), xs[0:8])
        xs = jnp.concatenate([head, xs[8:]], axis=0)
        xc = xc + xs * cw[CONV_W - 1 - d:CONV_W - d, :]
    tail_ref[...] = x[tc - 8:tc]

    xb = xc.astype(BF16)
    r = _sigmoid(jnp.dot(xb, wa_ref[...], preferred_element_type=F32) + ba_ref[...])
    gi = _sigmoid(jnp.dot(xb, wx_ref[...], preferred_element_type=F32) + bx_ref[...])
    z = -lam_ref[...]
    softplus = jnp.maximum(z, 0.0) + jnp.log1p(jnp.exp(-jnp.abs(z)))
    log_a = (-C_LRU * r) * softplus
    a = jnp.exp(log_a)
    two = 2.0 * log_a
    neg_expm1 = jnp.where(two > -1e-3, -two * (1.0 + two * (0.5 + two * (1.0 / 6.0))), 1.0 - jnp.exp(two))
    bb = jnp.sqrt(neg_expm1) * gi * xc

    row = lax.broadcasted_iota(I32, a.shape, 0)
    d = 1
    while d < tc:
        a_sh = jnp.where(row >= d, pltpu.roll(a, d, axis=0), 1.0)
        b_sh = jnp.where(row >= d, pltpu.roll(bb, d, axis=0), 0.0)
        bb = a * b_sh + bb
        a = a * a_sh
        d *= 2
    h = a * h_ref[0:1, :] + bb
    h_ref[...] = jnp.broadcast_to(h[tc - 1:tc, :], h_ref.shape)
    o_ref[0] = h * _gelu_tanh(yg_ref[0])


def _block_diag(w):
    n, c, d = w.shape
    return jnp.einsum('ncd,nm->ncmd', w, jnp.eye(n, dtype=w.dtype)).reshape(n * c, n * d)


def _rglru(xr, yg, conv_w, conv_b, wa, ba, wx, bx, lam):
    b_, t_, c_ = xr.shape
    tc = min(RGLRU_CHUNK, t_)
    wa_bd = _block_diag(wa).astype(BF16)
    wx_bd = _block_diag(wx).astype(BF16)
    vec = lambda v: v.reshape(1, c_)
    full = lambda a: pl.BlockSpec(a.shape, lambda b, i: (0,) * a.ndim)
    blk = pl.BlockSpec((1, tc, c_), lambda b, i: (b, i, 0))
    args = (xr, yg, conv_w, vec(conv_b), wa_bd, vec(ba), wx_bd, vec(bx), vec(lam))
    return pl.pallas_call(
        _rglru_kernel,
        name="rglru",
        grid=(b_, t_ // tc),
        in_specs=[blk, blk] + [full(a) for a in args[2:]],
        out_specs=blk,
        out_shape=jax.ShapeDtypeStruct((b_, t_, c_), F32),
        scratch_shapes=[pltpu.VMEM((8, c_), F32), pltpu.VMEM((8, c_), F32)],
        compiler_params=_cparams(("parallel", "arbitrary"), V7X_VMEM_LIMIT_BYTES),
    )(*args)


def _out_proj_kernel(oaT_ref, orec_ref, x_ref, ga_ref, gr_ref, wa_ref, wr_ref, o_ref):
    oaT = oaT_ref[0]
    ms = jnp.mean(oaT * oaT, axis=0, keepdims=True)
    na = (oaT * lax.rsqrt(ms + RMS_EPS)).T * ga_ref[...]
    nr = _rms(orec_ref[0], gr_ref[...])
    y = (jnp.dot(na.astype(BF16), wa_ref[...], preferred_element_type=F32)
         + jnp.dot(nr.astype(BF16), wr_ref[...], preferred_element_type=F32))
    o_ref[0] = x_ref[0] + y


def _out_proj(oaT, orec, x, g_attn, g_rec, w_out):
    b_, t_, d_ = x.shape
    tm = min(256, t_)
    wa = w_out[:D_ATTN].astype(BF16)
    wr = w_out[D_ATTN:].astype(BF16)
    ga = g_attn.reshape(1, -1)
    gr = g_rec.reshape(1, -1)
    full = lambda a: pl.BlockSpec(a.shape, lambda b, i: (0,) * a.ndim)
    return pl.pallas_call(
        _out_proj_kernel,
        name="out_proj",
        grid=(b_, t_ // tm),
        in_specs=[pl.BlockSpec((1, D_ATTN, tm), lambda b, i: (b, 0, i)),
                  pl.BlockSpec((1, tm, orec.shape[2]), lambda b, i: (b, i, 0)),
                  pl.BlockSpec((1, tm, d_), lambda b, i: (b, i, 0)),
                  full(ga), full(gr), full(wa), full(wr)],
        out_specs=pl.BlockSpec((1, tm, d_), lambda b, i: (b, i, 0)),
        out_shape=jax.ShapeDtypeStruct((b_, t_, d_), F32),
        compiler_params=_cparams(("parallel", "parallel"), V7X_VMEM_LIMIT_BYTES),
    )(oaT, orec, x, ga, gr, wa, wr)


def _ffn_kernel(be_ref, x_ref, g_ref, wg_ref, wu_ref, wd_ref, o_ref, hn_ref, acc_ref, *, dense):
    del be_ref
    f = pl.program_id(1)

    @pl.when(f == 0)
    def _():
        if dense:
            x = x_ref[...]
            hn_ref[...] = _rms(x, g_ref[...]).astype(BF16)
            acc_ref[...] = x
        else:
            hn_ref[...] = x_ref[...]
            acc_ref[...] = jnp.zeros_like(acc_ref)

    hn = hn_ref[...]
    gate = jnp.dot(hn, wg_ref[0], preferred_element_type=F32)
    up = jnp.dot(hn, wu_ref[0], preferred_element_type=F32)
    act = (_silu(gate) * up).astype(BF16)
    acc_ref[...] += jnp.dot(act, wd_ref[0], preferred_element_type=F32)

    @pl.when(f == pl.num_programs(1) - 1)
    def _():
        o_ref[...] = acc_ref[...].astype(o_ref.dtype)


def _ffn(x, norm_g, wg, wu, wd, block_e, *, dense, tf):
    n, d_ = x.shape
    tm = min(ROW_TILE if dense else MOE_ROWS, n)
    ff = wg.shape[2]
    assert ff % tf == 0 and n % tm == 0
    if block_e is None:
        block_e = jnp.zeros((n // tm,), I32)
    wmode = dict(pipeline_mode=pl.Buffered(1)) if (wg.shape[0] == 1 and ff == tf) else {}
    grid_spec = pltpu.PrefetchScalarGridSpec(
        num_scalar_prefetch=1,
        grid=(n // tm, ff // tf),
        in_specs=[pl.BlockSpec((tm, d_), lambda i, f, be: (i, 0)),
                  pl.BlockSpec((1, d_), lambda i, f, be: (0, 0)),
                  pl.BlockSpec((1, d_, tf), lambda i, f, be: (be[i], 0, f), **wmode),
                  pl.BlockSpec((1, d_, tf), lambda i, f, be: (be[i], 0, f), **wmode),
                  pl.BlockSpec((1, tf, d_), lambda i, f, be: (be[i], f, 0), **wmode)],
        out_specs=pl.BlockSpec((tm, d_), lambda i, f, be: (i, 0)),
        scratch_shapes=[pltpu.VMEM((tm, d_), BF16), pltpu.VMEM((tm, d_), F32)],
    )
    return pl.pallas_call(
        functools.partial(_ffn_kernel, dense=dense),
        name="ffn_dense" if dense else "ffn_expert",
        grid_spec=grid_spec,
        out_shape=jax.ShapeDtypeStruct((n, d_), F32 if dense else BF16),
        compiler_params=_cparams(("parallel", "arbitrary"), V7X_VMEM_LIMIT_BYTES),
    )(block_e, x, norm_g.reshape(1, d_), wg, wu, wd)


def _router_kernel(x_ref, g_ref, wr_ref, h_ref, e_ref, p_ref):
    hn = _rms(x_ref[...], g_ref[...])
    h_ref[...] = hn.astype(BF16)
    logits = lax.dot_general(wr_ref[...], hn, (((1,), (1,)), ((), ())),
                             precision=lax.Precision.HIGHEST, preferred_element_type=F32)
    eidx = lax.broadcasted_iota(I32, logits.shape, 0)
    m1 = jnp.max(logits, axis=0, keepdims=True)
    i1 = jnp.min(jnp.where(logits == m1, eidx, N_EXPERTS), axis=0, keepdims=True)
    rest = jnp.where(eidx == i1, -jnp.inf, logits)
    m2 = jnp.max(rest, axis=0, keepdims=True)
    i2 = jnp.min(jnp.where(rest == m2, eidx, N_EXPERTS), axis=0, keepdims=True)
    e2 = jnp.exp(m2 - m1)
    inv = 1.0 / (1.0 + e2)
    e_ref[...] = jnp.concatenate([i1, i2], axis=0)
    p_ref[...] = jnp.concatenate([inv, e2 * inv], axis=0)


def _router(x, norm_g, router_w):
    n, d_ = x.shape
    tm = min(ROW_TILE, n)
    wr = router_w.T
    return pl.pallas_call(
        _router_kernel,
        name="router",
        grid=(n // tm,),
        in_specs=[pl.BlockSpec((tm, d_), lambda i: (i, 0)),
                  pl.BlockSpec((1, d_), lambda i: (0, 0)),
                  pl.BlockSpec(wr.shape, lambda i: (0, 0))],
        out_specs=[pl.BlockSpec((tm, d_), lambda i: (i, 0)),
                   pl.BlockSpec((2, tm), lambda i: (0, i)),
                   pl.BlockSpec((2, tm), lambda i: (0, i))],
        out_shape=[jax.ShapeDtypeStruct((n, d_), BF16),
                   jax.ShapeDtypeStruct((2, n), I32),
                   jax.ShapeDtypeStruct((2, n), F32)],
        compiler_params=_cparams(("parallel",), V7X_VMEM_LIMIT_BYTES),
    )(x, norm_g.reshape(1, d_), wr)


def _gather_kernel(j_ref, c_ref, v_ref, dest_ref, h_ref, zeros_ref, o_ref, acc_ref):
    del c_ref, zeros_ref
    w = pl.program_id(0)
    nw = pl.num_programs(0)
    j = j_ref[w]
    first = (w == 0) | (j != j_ref[jnp.maximum(w - 1, 0)])
    last = (w == nw - 1) | (j != j_ref[jnp.minimum(w + 1, nw - 1)])

    @pl.when(first)
    def _():
        acc_ref[...] = jnp.zeros_like(acc_ref)

    @pl.when(v_ref[w] == 1)
    def _():
        dest = dest_ref[...]
        slot = j * acc_ref.shape[0] + lax.broadcasted_iota(I32, (acc_ref.shape[0], dest.shape[1]), 0)
        onehot = jnp.where((slot == dest[0:1, :]) | (slot == dest[1:2, :]), 1.0, 0.0).astype(BF16)
        acc_ref[...] += jnp.dot(onehot, h_ref[...], preferred_element_type=F32)

    @pl.when(last)
    def _():
        o_ref[...] = acc_ref[...].astype(BF16)


def _gather(h, dest, j_list, c_list, v_list, n_slots):
    n, d_ = h.shape
    zeros = jnp.zeros((n_slots, d_), BF16)
    grid_spec = pltpu.PrefetchScalarGridSpec(
        num_scalar_prefetch=3,
        grid=(j_list.shape[0],),
        in_specs=[pl.BlockSpec((2, MOE_TOK_CHUNK), lambda w, j, c, v: (0, c[w])),
                  pl.BlockSpec((MOE_TOK_CHUNK, d_), lambda w, j, c, v: (c[w], 0)),
                  pl.BlockSpec(memory_space=pl.ANY)],
        out_specs=pl.BlockSpec((MOE_ROWS, d_), lambda w, j, c, v: (j[w], 0)),
        scratch_shapes=[pltpu.VMEM((MOE_ROWS, d_), F32)],
    )
    return pl.pallas_call(
        _gather_kernel,
        name="moe_gather",
        grid_spec=grid_spec,
        out_shape=jax.ShapeDtypeStruct((n_slots, d_), BF16),
        input_output_aliases={5: 0},
        compiler_params=_cparams(("arbitrary",), V7X_VMEM_LIMIT_BYTES),
    )(j_list, c_list, v_list, dest, h, zeros)


N_CMB_OPS = 2 * N_EXPERTS


def _combine_kernel(jt_ref, vt_ref, x_ref, dest_ref, gate_ref, *rest):
    yb_refs = rest[:N_CMB_OPS]
    o_ref = rest[N_CMB_OPS]
    c = pl.program_id(0)
    dest = dest_ref[...]
    gate = gate_ref[...]
    lane = lax.broadcasted_iota(I32, (dest.shape[0], CMB_ROWS), 1)
    acc = x_ref[...]
    for k in range(N_CMB_OPS):
        base = jnp.where(vt_ref[c * N_CMB_OPS + k] == 1, jt_ref[c * N_CMB_OPS + k] * CMB_ROWS, -2 * CMB_ROWS)
        rel = dest - base
        w = (jnp.where(lane == rel[:, 0:1], gate[:, 0:1], 0.0)
             + jnp.where(lane == rel[:, 1:2], gate[:, 1:2], 0.0)).astype(BF16)
        acc = acc + jnp.dot(w, yb_refs[k][...], preferred_element_type=F32)
    o_ref[...] = acc


def _combine(x, yb, dest_t, gate_t, jt, vt):
    n, d_ = x.shape
    yb_spec = lambda k: pl.BlockSpec((CMB_ROWS, d_), lambda c, jt, vt, k=k: (jt[c * N_CMB_OPS + k], 0))
    grid_spec = pltpu.PrefetchScalarGridSpec(
        num_scalar_prefetch=2,
        grid=(n // CMB_ROWS,),
        in_specs=[pl.BlockSpec((CMB_ROWS, d_), lambda c, jt, vt: (c, 0)),
                  pl.BlockSpec((CMB_ROWS, 2), lambda c, jt, vt: (c, 0)),
                  pl.BlockSpec((CMB_ROWS, 2), lambda c, jt, vt: (c, 0))]
                 + [yb_spec(k) for k in range(N_CMB_OPS)],
        out_specs=pl.BlockSpec((CMB_ROWS, d_), lambda c, jt, vt: (c, 0)),
    )
    return pl.pallas_call(
        _combine_kernel,
        name="moe_combine",
        grid_spec=grid_spec,
        out_shape=jax.ShapeDtypeStruct((n, d_), F32),
        compiler_params=_cparams(("parallel",), V7X_VMEM_LIMIT_BYTES),
    )(jt, vt, x, dest_t, gate_t, *([yb] * N_CMB_OPS))


def _moe_plan(top_e, n_slots):
    n = top_e.shape[1]
    e0, e1 = top_e[0], top_e[1]
    oh = jax.nn.one_hot(e0, N_EXPERTS, dtype=I32) + jax.nn.one_hot(e1, N_EXPERTS, dtype=I32)
    csum_incl = jnp.cumsum(oh, axis=0)
    csum = csum_incl - oh
    counts = csum_incl[-1]
    padded = (counts + MOE_ROWS - 1) // MOE_ROWS * MOE_ROWS
    pad_ends = jnp.cumsum(padded)
    pad_starts = pad_ends - padded
    rank0 = jnp.take_along_axis(csum, e0[:, None], axis=1)[:, 0]
    rank1 = jnp.take_along_axis(csum, e1[:, None], axis=1)[:, 0]
    dest = jnp.stack([pad_starts[e0] + rank0, pad_starts[e1] + rank1]).astype(I32)
    n_blk = n_slots // MOE_ROWS
    blk_start = jnp.arange(n_blk, dtype=I32) * MOE_ROWS
    block_e = jnp.minimum(jnp.sum((pad_ends[None, :] <= blk_start[:, None]).astype(I32), axis=1),
                          N_EXPERTS - 1).astype(I32)

    def ranges(chunk):
        nc = n // chunk
        cb = jnp.concatenate([csum[::chunk], counts[None]], axis=0)
        lo = pad_starts[None] + cb[:-1]
        hi = pad_starts[None] + cb[1:]
        return nc, lo, hi

    nc, lo, hi = ranges(MOE_TOK_CHUNK)
    ja = lo // MOE_ROWS
    jb = (hi - 1) // MOE_ROWS
    va = hi > lo
    vb = va & (jb > ja)
    j_list = jnp.stack([ja, jb], axis=-1).transpose(1, 0, 2).reshape(-1)
    v_list = jnp.stack([va, vb], axis=-1).transpose(1, 0, 2).reshape(-1)
    c_list = jnp.broadcast_to(jnp.arange(nc, dtype=I32)[None, :, None], (N_EXPERTS, nc, 2)).reshape(-1)
    idx = lax.cummax(jnp.where(v_list, jnp.arange(v_list.shape[0], dtype=I32), -1), axis=0)
    has = idx >= 0
    idx = jnp.maximum(idx, 0)
    j_list = jnp.where(has, j_list[idx], 0).astype(I32)
    c_list = jnp.where(has, c_list[idx], 0).astype(I32)
    v_list = v_list.astype(I32)

    _, lo2, hi2 = ranges(CMB_ROWS)
    ja2 = lo2 // CMB_ROWS
    jb2 = (hi2 - 1) // CMB_ROWS
    va2 = hi2 > lo2
    vb2 = va2 & (jb2 > ja2)
    jt = jnp.stack([ja2, jb2], axis=-1).reshape(-1)
    vt = jnp.stack([va2, vb2], axis=-1).reshape(-1)
    jt = jnp.where(vt, jt, 0).astype(I32)
    return dest, block_e, j_list, c_list, v_list, jt, vt.astype(I32)


def _moe(x, norm_g, router_w, wg, wu, wd):
    n, d_ = x.shape
    n_slots = (2 * n // MOE_ROWS + N_EXPERTS) * MOE_ROWS
    h, top_e, top_p = _router(x, norm_g, router_w)
    dest, block_e, j_list, c_list, v_list, jt, vt = _moe_plan(top_e, n_slots)
    xs = _gather(h, dest, j_list, c_list, v_list, n_slots)
    yb = _ffn(xs, norm_g, wg, wu, wd, block_e, dense=False, tf=wg.shape[2] // 2)
    return _combine(x, yb, dest.T, top_p.T, jt, vt)


def _ple_kernel(x_ref, p_ref, g_ref, wg_ref, wp_ref, fg_ref, o_ref, *, final):
    x = x_ref[...]
    gate = _sigmoid(jnp.dot(_rms(x, g_ref[...]).astype(BF16), wg_ref[...], preferred_element_type=F32))
    proj = jnp.dot(p_ref[...].astype(BF16), wp_ref[...], preferred_element_type=F32)
    y = x + gate * proj
    if final:
        y = _rms(y, fg_ref[...])
    o_ref[...] = y


def _ple(x, p, norm_g, w_gate, w_proj, final_g, *, final):
    n, d_ = x.shape
    tm = min(ROW_TILE, n)
    wg = w_gate.astype(BF16)
    wp = w_proj.astype(BF16)
    full = lambda a: pl.BlockSpec(a.shape, lambda i: (0,) * a.ndim)
    g2 = norm_g.reshape(1, d_)
    fg = final_g.reshape(1, d_)
    return pl.pallas_call(
        functools.partial(_ple_kernel, final=final),
        name="ple_final" if final else "ple",
        grid=(n // tm,),
        in_specs=[pl.BlockSpec((tm, d_), lambda i: (i, 0)),
                  pl.BlockSpec((tm, p.shape[1]), lambda i: (i, 0)),
                  full(g2), full(wg), full(wp), full(fg)],
        out_specs=pl.BlockSpec((tm, d_), lambda i: (i, 0)),
        out_shape=jax.ShapeDtypeStruct((n, d_), F32),
        compiler_params=_cparams(("parallel",), V7X_VMEM_LIMIT_BYTES),
    )(x, p, g2, wg, wp, fg)


def kernel(x, p, attn_norm, w_in, cmp_pos, cmp_w1, cmp_w2, conv_w, conv_b, lru_wa, lru_ba, lru_wx, lru_bx,
           lru_lambda, out_norm_attn, out_norm_rec, w_out, ffn_norm, dense_w_gate, dense_w_up, dense_w_down,
           router_w, moe_w_gate, moe_w_up, moe_w_down, ple_norm, ple_w_gate, ple_w_proj, final_norm):
    b_, t_, d_ = x.shape
    depth = w_in.shape[0]
    n = b_ * t_
    for i in range(depth):
        kcvc, ks, kw, xr, yg, qT, vsT, vwT, gT = _proj_in(x, attn_norm[i], w_in[i])
        kcmp = _compress(kcvc, cmp_pos[i], cmp_w1[i], cmp_w2[i])
        oaT = _attention(qT, gT, kcmp, ks, vsT, kw, vwT)
        orec = _rglru(xr, yg, conv_w[i], conv_b[i], lru_wa[i], lru_ba[i], lru_wx[i], lru_bx[i], lru_lambda[i])
        x = _out_proj(oaT, orec, x, out_norm_attn[i], out_norm_rec[i], w_out[i])
        x2 = x.reshape(n, d_)
        j = i // 2
        if i % 2 == 0:
            x2 = _ffn(x2, ffn_norm[i], dense_w_gate[j][None].astype(BF16), dense_w_up[j][None].astype(BF16),
                      dense_w_down[j][None].astype(BF16), None, dense=True, tf=dense_w_gate.shape[2])
        else:
            x2 = _moe(x2, ffn_norm[i], router_w[j], moe_w_gate[j].astype(BF16), moe_w_up[j].astype(BF16),
                      moe_w_down[j].astype(BF16))
        x2 = _ple(x2, p[i].reshape(n, -1), ple_norm[i], ple_w_gate[i], ple_w_proj[i], final_norm,
                  final=(i == depth - 1))
        x = x2.reshape(b_, t_, d_)
    return x
```

```python
import functools

import jax
import jax.numpy as jnp
from jax import lax
from jax.experimental import pallas as pl
from jax.experimental.pallas import tpu as pltpu

F32 = jnp.float32
BF16 = jnp.bfloat16
I32 = jnp.int32

N_ATTN_HEADS = 8
HEAD_DIM = 64
N_KV = 2
HEADS_PER_KV = N_ATTN_HEADS // N_KV
D_ATTN = N_ATTN_HEADS * HEAD_DIM
KV_W = N_KV * HEAD_DIM
N_GATES = 3
L_CMP = 32
STRIDE = 16
L_SLC = 64
N_SEL = 16
N_LOCAL = 2
W_WIN = 512
CMP_HID = 2 * HEAD_DIM
CONV_W = 4
C_LRU = 8.0
N_EXPERTS = 8
RMS_EPS = 1e-6
ATTN_SCALE = HEAD_DIM ** -0.5
LOG2E = 1.4426950408889634
LOG2E_HI = 1.4453125
LOG2E_LO = LOG2E - LOG2E_HI

V7X_LANES = 128
V7X_VMEM_LIMIT_BYTES = 56 * 1024 * 1024

TQ = 128
KT_SLC = 256
KT_WIN = 128
N_WIN_TILES = W_WIN // KT_WIN + 1
V_ROWS = HEAD_DIM + 16
MASK_NEG = -1e30
SEL_NEG = -1.0e4
ROW_TILE = 512
RGLRU_CHUNK = 256
MOE_ROWS = 512
MOE_TOK_CHUNK = 512
CMB_ROWS = 256


def _cparams(semantics, vmem=None):
    return pltpu.CompilerParams(dimension_semantics=semantics, vmem_limit_bytes=vmem)


def _rms(x, g):
    ms = jnp.mean(x * x, axis=-1, keepdims=True)
    return x * lax.rsqrt(ms + RMS_EPS) * g


def _gelu_tanh(x):
    c = 0.7978845608028654
    return x * (0.5 * (1.0 + jnp.tanh(c * (x + 0.044715 * (x * x * x)))))


def _sigmoid(x):
    return 1.0 / (1.0 + jnp.exp(-x))


def _silu(x):
    return x * _sigmoid(x)


N_TOK_COLS = 2 * KV_W + 2 * KV_W + 2 * 512
N_TR_ROWS = D_ATTN + 2 * KV_W + 32


def _proj_in_kernel(x_ref, g_ref, wtok_ref, wtr_ref,
                    kcvc_ref, ks_ref, kw_ref, xr_ref, yg_ref, qT_ref, vsT_ref, vwT_ref, gT_ref):
    hn = _rms(x_ref[0], g_ref[...]).astype(BF16)
    tok = jnp.dot(hn, wtok_ref[...], preferred_element_type=F32)
    kcvc_ref[0] = tok[:, 0:256]
    ks_ref[0] = tok[:, 256:384].astype(BF16)
    kw_ref[0] = tok[:, 384:512].astype(BF16)
    xr_ref[0] = tok[:, 512:1024]
    yg_ref[0] = tok[:, 1024:1536]
    tr = lax.dot_general(wtr_ref[...], hn, (((1,), (1,)), ((), ())),
                         preferred_element_type=F32)
    qT_ref[0] = (tr[0:512] * (ATTN_SCALE * LOG2E)).astype(BF16)
    vsT_ref[0] = tr[512:640].astype(BF16)
    vwT_ref[0] = tr[640:768].astype(BF16)
    gT_ref[0] = _sigmoid(tr[768:800])


def _proj_in(x, norm_g, w_in):
    b_, t_, d_ = x.shape
    tm = min(ROW_TILE, t_)
    q, kc, vc, ks, vs, kw, vw, g, xr, yg = jnp.split(
        w_in, [512, 640, 768, 896, 1024, 1152, 1280, 1304, 1816], axis=1)
    wtok = jnp.concatenate([kc, vc, ks, kw, xr, yg], axis=1).astype(BF16)
    g4 = g.reshape(d_, N_KV, HEADS_PER_KV, N_GATES).transpose(0, 1, 3, 2)
    g4 = jnp.pad(g4.reshape(d_, N_KV, 12), ((0, 0), (0, 0), (0, 4))).reshape(d_, 32)
    wtr = jnp.concatenate([q, vs, vw, g4], axis=1).T.astype(BF16)
    nt = t_ // tm
    row = lambda shape: pl.BlockSpec((1, tm, shape), lambda b, i: (b, i, 0))
    col = lambda shape: pl.BlockSpec((1, shape, tm), lambda b, i: (b, 0, i))
    full = lambda a: pl.BlockSpec(a.shape, lambda b, i: (0,) * a.ndim)
    g2 = norm_g.reshape(1, d_)
    outs = pl.pallas_call(
        _proj_in_kernel,
        name="proj_in",
        grid=(b_, nt),
        in_specs=[row(d_), full(g2), full(wtok), full(wtr)],
        out_specs=[row(256), row(128), row(128), row(512), row(512),
                   col(512), col(128), col(128), col(32)],
        out_shape=[
            jax.ShapeDtypeStruct((b_, t_, 256), F32),
            jax.ShapeDtypeStruct((b_, t_, 128), BF16),
            jax.ShapeDtypeStruct((b_, t_, 128), BF16),
            jax.ShapeDtypeStruct((b_, t_, 512), F32),
            jax.ShapeDtypeStruct((b_, t_, 512), F32),
            jax.ShapeDtypeStruct((b_, 512, t_), BF16),
            jax.ShapeDtypeStruct((b_, 128, t_), BF16),
            jax.ShapeDtypeStruct((b_, 128, t_), BF16),
            jax.ShapeDtypeStruct((b_, 32, t_), F32),
        ],
        compiler_params=_cparams(("parallel", "parallel"), V7X_VMEM_LIMIT_BYTES),
    )(x, g2, wtok, wtr)
    return outs


def _compress_kernel(xc_ref, pa_ref, pb_ref, w1a_ref, w1b_ref, w2_ref, out_ref):
    xc = xc_ref[0].astype(BF16)
    n_chunk = xc.shape[0]
    a = jnp.dot(xc, w1a_ref[...], preferred_element_type=F32)
    bm = jnp.dot(xc, w1b_ref[...], preferred_element_type=F32)
    posc = (jnp.dot(pa_ref[...], w1a_ref[...], preferred_element_type=F32)
            + jnp.dot(pb_ref[...], w1b_ref[...], preferred_element_type=F32))[0:1]
    row = lax.broadcasted_iota(I32, bm.shape, 0)
    bm_up = jnp.where(row < n_chunk - 1, pltpu.roll(bm, n_chunk - 1, axis=0), 0.0)
    hid = _gelu_tanh(a + bm_up + posc).astype(BF16)
    out_ref[0] = jnp.dot(hid, w2_ref[...], preferred_element_type=F32)


def _compress(kcvc, cmp_pos, cmp_w1, cmp_w2):
    b_, t_, _ = kcvc.shape
    n_chunk = t_ // STRIDE
    xc = kcvc.reshape(b_, n_chunk, STRIDE * 256)
    eye = jnp.eye(2 * N_KV, dtype=F32)
    w1 = cmp_w1.reshape(2, 2, STRIDE, HEAD_DIM, CMP_HID)

    def expand(half):
        w = w1[:, half]
        full = jnp.einsum('wldo,wx,gy->lxgdwyo', w, jnp.eye(2, dtype=F32), jnp.eye(2, dtype=F32))
        return full.reshape(STRIDE * 256, 4 * CMP_HID).astype(BF16)

    del eye
    w1a, w1b = expand(0), expand(1)
    pos = cmp_pos.reshape(2, 2, STRIDE, HEAD_DIM)

    def posrow(half):
        p = jnp.broadcast_to(pos[:, half][:, None], (2, N_KV, STRIDE, HEAD_DIM))
        p = p.transpose(2, 0, 1, 3).reshape(1, STRIDE * 256)
        return jnp.pad(p, ((0, 7), (0, 0))).astype(BF16)

    pa, pb = posrow(0), posrow(1)
    w2 = jnp.einsum('whd,wx,gy->wghxyd', cmp_w2, jnp.eye(2, dtype=F32), jnp.eye(2, dtype=F32))
    w2 = w2.reshape(4 * CMP_HID, 4 * HEAD_DIM).astype(BF16)
    full = lambda a: pl.BlockSpec(a.shape, lambda b: (0,) * a.ndim)
    return pl.pallas_call(
        _compress_kernel,
        name="compress_kv",
        grid=(b_,),
        in_specs=[pl.BlockSpec((1, n_chunk, STRIDE * 256), lambda b: (b, 0, 0)),
                  full(pa), full(pb), full(w1a), full(w1b), full(w2)],
        out_specs=pl.BlockSpec((1, n_chunk, 256), lambda b: (b, 0, 0)),
        out_shape=jax.ShapeDtypeStruct((b_, n_chunk, 256), F32),
        compiler_params=_cparams(("parallel",), V7X_VMEM_LIMIT_BYTES),
    )(xc, pa, pb, w1a, w1b, w2)


def _head_lanes(rows):
    r = rows.shape[0] // HEADS_PER_KV
    return jnp.concatenate([rows[h * r:(h + 1) * r, :] for h in range(HEADS_PER_KV)], axis=1)


def _attn_kernel(qT_ref, gT_ref, kc_ref, vcT_ref, ks_ref, vsT_ref, kw_ref, vwT_ref, mt_ref, o_ref, s_scr,
                 tile_list):
    i = pl.program_id(1)
    n_cmp = kc_ref.shape[2]
    n_slc = mt_ref.shape[0]
    nl = HEADS_PER_KV * TQ
    hd_rows = HEADS_PER_KV * HEAD_DIM

    lane = lax.broadcasted_iota(I32, (1, nl), 1)
    t_loc = lane % TQ
    t_row = i * TQ + t_loc

    def extra_rows(n_rows, rows):
        ridx = lax.broadcasted_iota(I32, (16, nl), 0)
        out = jnp.zeros((16, nl), F32)
        for k, r in enumerate(rows):
            out = jnp.where(ridx == k, r, out)
        return jnp.concatenate([out.astype(BF16), jnp.zeros((n_rows - 16, nl), BF16)], axis=0)

    ok_c = (lax.broadcasted_iota(I32, (n_cmp, nl), 0) * STRIDE + (L_CMP - 1)) <= t_row
    r_idx = lax.broadcasted_iota(I32, (KT_WIN, nl), 0)
    s_idx = lax.broadcasted_iota(I32, (n_slc, TQ), 0)
    cur = (i * TQ + lax.broadcasted_iota(I32, (1, TQ), 1)) // L_SLC
    valid = s_idx <= cur
    forced = valid & ((s_idx == 0) | (s_idx > cur - N_LOCAL))
    s_idx_f = jnp.concatenate([s_idx.astype(F32)] * N_KV, axis=1)
    n_full = (i * TQ) // KT_SLC
    mt = mt_ref[...]

    def pick(_, score):
        m = jnp.max(score, axis=0, keepdims=True)
        idx = jnp.min(jnp.where(score == m, s_idx_f, 1.0e9), axis=0, keepdims=True)
        return jnp.where(s_idx_f == idx, -2.0, score)

    def stage_a(g, bs, tile, slot, causal):
        k0 = pl.multiple_of(tile * KT_SLC, KT_SLC)
        s = jnp.dot(ks_ref[0, g, pl.ds(k0, KT_SLC), :], bs, preferred_element_type=F32)
        if causal:
            pos = k0 + lax.broadcasted_iota(I32, (KT_SLC, nl), 0)
            s = jnp.where(pos <= t_row, s, MASK_NEG)
        s_scr[g, slot] = s
        return jnp.max(s, axis=0, keepdims=True)

    def stage_b(g, tile, slot, m, acc, m_tile):
        k0 = pl.multiple_of(tile * KT_SLC, KT_SLC)
        m_new = jnp.maximum(m, m_tile)
        p = jnp.exp2(s_scr[g, slot] - m_new)
        acc = jnp.exp2(m - m_new) * acc + jnp.dot(vsT_ref[0, g, :, pl.ds(k0, KT_SLC)], p.astype(BF16),
                                                  preferred_element_type=F32)
        return m_new, acc

    groups = range(N_KV)

    def setup(g):
        q4 = _head_lanes(qT_ref[0, g * hd_rows:(g + 1) * hd_rows, :])
        head = g * HEADS_PER_KV + lane // TQ + 1
        slope = lax.bitcast_convert_type((127 - head) << 23, F32)
        return dict(q4=q4, slope=slope, s_hi=slope * LOG2E_HI, s_lo=slope * LOG2E_LO)

    st = [setup(g) for g in groups]

    def cmp_scores(g):
        s_hi, s_lo = st[g]["s_hi"], st[g]["s_lo"]
        bc = jnp.concatenate(
            [st[g]["q4"], extra_rows(64, [256.0 * s_hi, 256.0 * s_lo, 16.0 * s_hi, 16.0 * s_lo])], axis=0)
        return jnp.dot(kc_ref[0, g], bc, preferred_element_type=F32)

    def cmp_softmax(g, s_c):
        s_c = jnp.where(ok_c, s_c, MASK_NEG)
        m_c = jnp.max(s_c, axis=0, keepdims=True)
        e_c = jnp.exp2(s_c - m_c)
        l_c = jnp.sum(e_c, axis=0, keepdims=True)
        inv_c = jnp.where(m_c > 0.5 * MASK_NEG, 1.0 / jnp.maximum(l_c, 1e-30), 0.0)
        p_c = e_c * inv_c
        o_c = jnp.dot(vcT_ref[0, g], p_c.astype(BF16), preferred_element_type=F32)
        p_grp = (p_c[:, 0:TQ] + p_c[:, TQ:2 * TQ]) + (p_c[:, 2 * TQ:3 * TQ] + p_c[:, 3 * TQ:4 * TQ])
        p1 = p_grp.astype(BF16)
        r1 = p_grp - p1.astype(F32)
        p2 = r1.astype(BF16)
        p3 = (r1 - p2.astype(F32)).astype(BF16)
        p_slc = (jnp.dot(mt, p1, preferred_element_type=F32) + jnp.dot(mt, p2, preferred_element_type=F32)
                 + jnp.dot(mt, p3, preferred_element_type=F32))
        return o_c, p_slc

    def win_scores(g):
        slope = st[g]["slope"]
        bw = jnp.concatenate([st[g]["q4"], extra_rows(64, [st[g]["s_hi"], st[g]["s_lo"]])], axis=0)
        s_tiles, c_tiles = [], []
        for a in range(N_WIN_TILES):
            tile = i - (N_WIN_TILES - 1) + a
            k0 = pl.multiple_of(jnp.maximum(tile, 0) * KT_WIN, KT_WIN)
            s = jnp.dot(kw_ref[0, g, pl.ds(k0, KT_WIN), :], bw, preferred_element_type=F32)
            if a == 0:
                s = jnp.where(r_idx > t_loc, s, MASK_NEG)
            elif a == N_WIN_TILES - 1:
                s = jnp.where(r_idx <= t_loc, s, MASK_NEG)
            s_tiles.append(s)
            c_a = slope * (LOG2E * KT_WIN * (a - (N_WIN_TILES - 1)))
            c_tiles.append(jnp.where(tile >= 0, c_a, MASK_NEG))
        return s_tiles, c_tiles

    def win_softmax(g, s_tiles, c_tiles):
        m_w = s_tiles[0].max(axis=0, keepdims=True) + c_tiles[0]
        for s, c_a in zip(s_tiles[1:], c_tiles[1:]):
            m_w = jnp.maximum(m_w, s.max(axis=0, keepdims=True) + c_a)
        acc_w = jnp.zeros((V_ROWS, nl), F32)
        for a, (s, c_a) in enumerate(zip(s_tiles, c_tiles)):
            tile = i - (N_WIN_TILES - 1) + a
            k0 = pl.multiple_of(jnp.maximum(tile, 0) * KT_WIN, KT_WIN)
            p = jnp.exp2(s - (m_w - c_a))
            acc_w = acc_w + jnp.dot(vwT_ref[0, g, :, pl.ds(k0, KT_WIN)], p.astype(BF16),
                                    preferred_element_type=F32)
        return acc_w[0:HEAD_DIM] * (1.0 / acc_w[HEAD_DIM:HEAD_DIM + 1])

    s_cmp = [cmp_scores(g) for g in groups]
    s_win = [win_scores(g) for g in groups]
    cmp_out = [cmp_softmax(g, s_cmp[g]) for g in groups]
    o_win = [win_softmax(g, *s_win[g]) for g in groups]

    score0 = jnp.concatenate(
        [jnp.where(forced, -2.0, jnp.where(valid, cmp_out[g][1], -1.0)) for g in groups], axis=1)
    score = lax.fori_loop(0, N_SEL - (N_LOCAL + 1), pick, score0, unroll=True)

    def select(g):
        sel = valid & (score[:, g * TQ:(g + 1) * TQ] < -1.5)
        sel4 = jnp.concatenate([sel] * HEADS_PER_KV, axis=1)
        blk_rows = jnp.where(sel4, 0.0, SEL_NEG).astype(BF16)
        s_hi, s_lo = st[g]["s_hi"], st[g]["s_lo"]
        bs = jnp.concatenate(
            [st[g]["q4"], blk_rows, extra_rows(64, [L_SLC * s_hi, L_SLC * s_lo, s_hi, s_lo])], axis=0)
        m_tile0 = stage_a(g, bs, n_full, 0, True)
        gt = gT_ref[0, g * 16:(g + 1) * 16, :]
        gates = [_head_lanes(gt[k * HEADS_PER_KV:(k + 1) * HEADS_PER_KV, :]) for k in range(N_GATES)]
        return dict(bs=bs, base=gates[0] * cmp_out[g][0] + gates[2] * o_win[g], g_s=gates[1], m_tile0=m_tile0)

    grp = [select(g) for g in groups]

    blocks_per_tile = KT_SLC // L_SLC
    n_tiles = n_slc // blocks_per_tile
    assert n_tiles <= 32
    valid2 = jnp.concatenate([valid] * N_KV, axis=1)
    sel_all = jnp.where(valid2 & (score < -1.5), 1.0, 0.0).astype(BF16)
    blk_cnt = lax.dot_general(jnp.ones((8, N_KV * TQ), BF16), sel_all, (((1,), (1,)), ((), ())),
                              preferred_element_type=F32)
    memb = jnp.where(lax.broadcasted_iota(I32, (n_slc, V7X_LANES), 0) // blocks_per_tile
                     == lax.broadcasted_iota(I32, (n_slc, V7X_LANES), 1), 1.0, 0.0).astype(BF16)
    tile_cnt = jnp.dot(jnp.where(blk_cnt > 0.5, 1.0, 0.0).astype(BF16), memb,
                       preferred_element_type=F32)[0:1]
    lane_t = lax.broadcasted_iota(I32, (1, V7X_LANES), 1)
    need = (tile_cnt > 0.5) & (lane_t < n_full)
    bit = lax.bitcast_convert_type(((lane_t & 15) + 127) << 23, F32)
    lo_bits = jnp.sum(jnp.where(need & (lane_t < 16), bit, 0.0)).astype(I32)
    hi_bits = jnp.sum(jnp.where(need & (lane_t >= 16), bit, 0.0)).astype(I32)
    n_sel = jnp.int32(0)
    for t in range(n_tiles - 1):
        tile_list[n_sel] = t
        n_sel = n_sel + (((lo_bits if t < 16 else hi_bits) >> (t % 16)) & 1)

    order = lambda k: jnp.where(k == 0, n_full, tile_list[jnp.maximum(k - 1, 0)])

    def half(carry, tile_b, slot_b, tile_a):
        m_next = [c[2] if tile_a is None else stage_a(g, grp[g]["bs"], tile_a, 1 - slot_b, False)
                  for g, c in enumerate(carry)]
        out = []
        for g in range(N_KV):
            m, acc, m_tile = carry[g]
            out.append(stage_b(g, tile_b, slot_b, m, acc, m_tile) + (m_next[g],))
        return tuple(out)

    def body(p, carry):
        k = 2 * p
        return half(half(carry, order(k), 0, order(k + 1)), order(k + 1), 1, order(k + 2))

    init = tuple((jnp.full((1, nl), MASK_NEG, F32), jnp.zeros((V_ROWS, nl), F32), grp[g]["m_tile0"])
                 for g in range(N_KV))
    carry = lax.fori_loop(0, n_sel // 2, body, init)
    k_tail = 2 * (n_sel // 2)
    carry = lax.cond(n_sel % 2 == 1,
                     lambda c: half(half(c, order(k_tail), 0, order(k_tail + 1)), order(k_tail + 1), 1, None),
                     lambda c: half(c, order(k_tail), 0, None), carry)
    for g in range(N_KV):
        _, acc, _ = carry[g]
        o_s = acc[0:HEAD_DIM] * (1.0 / acc[HEAD_DIM:HEAD_DIM + 1])
        out = grp[g]["base"] + grp[g]["g_s"] * o_s
        for h in range(HEADS_PER_KV):
            r0 = g * hd_rows + h * HEAD_DIM
            o_ref[0, r0:r0 + HEAD_DIM, :] = out[:, h * TQ:(h + 1) * TQ]


def _attention(qT, gT, kcmp, ks, vsT, kw, vwT):
    b_, _, t_ = qT.shape
    n_cmp = t_ // STRIDE
    n_slc = t_ // L_SLC
    pos = jnp.arange(t_, dtype=I32)
    grp = lambda a: a.reshape(b_, t_, N_KV, HEAD_DIM).transpose(0, 2, 1, 3)
    def pos_cols(vals, width):
        cols = [v.astype(BF16)[:, None] for v in vals for _ in range(2)]
        return jnp.concatenate(cols + [jnp.zeros((vals[0].shape[0], width - len(cols)), BF16)], axis=1)

    slc_cols = jnp.concatenate(
        [jax.nn.one_hot(pos // L_SLC, n_slc, dtype=BF16), pos_cols([pos // L_SLC, pos % L_SLC], 64)],
        axis=1)
    ks_aug = jnp.concatenate(
        [grp(ks), jnp.broadcast_to(slc_cols, (b_, N_KV) + slc_cols.shape)], axis=-1)
    win_cols = pos_cols([pos % KT_WIN], 64)
    kw_aug = jnp.concatenate(
        [grp(kw), jnp.broadcast_to(win_cols, (b_, N_KV) + win_cols.shape)], axis=-1)
    c = jnp.arange(n_cmp, dtype=I32)
    cmp_cols = pos_cols([c // 16, c % 16], 64)
    kc4 = kcmp[..., 0:KV_W].reshape(b_, n_cmp, N_KV, HEAD_DIM).transpose(0, 2, 1, 3).astype(BF16)
    kc_aug = jnp.concatenate(
        [kc4, jnp.broadcast_to(cmp_cols, (b_, N_KV) + cmp_cols.shape)], axis=-1)
    vcT = kcmp[..., KV_W:2 * KV_W].reshape(b_, n_cmp, N_KV, HEAD_DIM).transpose(0, 2, 3, 1).astype(BF16)
    ones_rows = jnp.ones((b_, N_KV, V_ROWS - HEAD_DIM, t_), BF16)
    vsT4 = jnp.concatenate([vsT.reshape(b_, N_KV, HEAD_DIM, t_), ones_rows], axis=2)
    vwT4 = jnp.concatenate([vwT.reshape(b_, N_KV, HEAD_DIM, t_), ones_rows], axis=2)
    s = jnp.arange(n_slc, dtype=I32)[:, None]
    cc = c[None, :]
    r_slc = L_SLC // STRIDE
    mt = (((cc >= r_slc * s) & (cc < r_slc * s + r_slc)).astype(F32)
          + ((cc + 1 >= r_slc * s) & (cc + 1 < r_slc * s + r_slc)).astype(F32))
    mt = jnp.where(cc < n_cmp - 1, mt, 0.0).astype(BF16)
    kv_spec = lambda a: pl.BlockSpec((1,) + a.shape[1:], lambda b, i: (b, 0, 0, 0))
    return pl.pallas_call(
        _attn_kernel,
        name="nsa_attention",
        grid=(b_, t_ // TQ),
        in_specs=[pl.BlockSpec((1, D_ATTN, TQ), lambda b, i: (b, 0, i)),
                  pl.BlockSpec((1, 32, TQ), lambda b, i: (b, 0, i)),
                  kv_spec(kc_aug), kv_spec(vcT), kv_spec(ks_aug), kv_spec(vsT4),
                  kv_spec(kw_aug), kv_spec(vwT4),
                  pl.BlockSpec(mt.shape, lambda b, i: (0, 0))],
        out_specs=pl.BlockSpec((1, D_ATTN, TQ), lambda b, i: (b, 0, i)),
        out_shape=jax.ShapeDtypeStruct((b_, D_ATTN, t_), F32),
        scratch_shapes=[pltpu.VMEM((N_KV, 2, KT_SLC, HEADS_PER_KV * TQ), F32), pltpu.SMEM((32,), I32)],
        compiler_params=_cparams(("parallel", "arbitrary"), V7X_VMEM_LIMIT_BYTES),
    )(qT, gT, kc_aug, vcT, ks_aug, vsT4, kw_aug, vwT4, mt)


def _rglru_kernel(xr_ref, yg_ref, cw_ref, cb_ref, wa_ref, ba_ref, wx_ref, bx_ref, lam_ref,
                  o_ref, tail_ref, h_ref):
    tc = xr_ref.shape[1]

    @pl.when(pl.program_id(1) == 0)
    def _():
        tail_ref[...] = jnp.zeros_like(tail_ref)
        h_ref[...] = jnp.zeros_like(h_ref)

    x = xr_ref[0]
    tail = tail_ref[...]
    row8 = lax.broadcasted_iota(I32, tail.shape, 0)
    cw = cw_ref[...]
    xc = x * cw[CONV_W - 1:CONV_W, :] + cb_ref[...]
    for d in range(1, CONV_W):
        xs = pltpu.roll(x, d, axis=0)
        head = jnp.where(row8 < d, pltpu.roll(tail, d, axis=0), xs[0:8])
        xs = jnp.concatenate([head, xs[8:]], axis=0)
        xc = xc + xs * cw[CONV_W - 1 - d:CONV_W - d, :]
    tail_ref[...] = x[tc - 8:tc]

    xb = xc.astype(BF16)
    r = _sigmoid(jnp.dot(xb, wa_ref[...], preferred_element_type=F32) + ba_ref[...])
    gi = _sigmoid(jnp.dot(xb, wx_ref[...], preferred_element_type=F32) + bx_ref[...])
    z = -lam_ref[...]
    softplus = jnp.maximum(z, 0.0) + jnp.log1p(jnp.exp(-jnp.abs(z)))
    log_a = (-C_LRU * r) * softplus
    a = jnp.exp(log_a)
    two = 2.0 * log_a
    neg_expm1 = jnp.where(two > -1e-3, -two * (1.0 + two * (0.5 + two * (1.0 / 6.0))), 1.0 - jnp.exp(two))
    bb = jnp.sqrt(neg_expm1) * gi * xc

    row = lax.broadcasted_iota(I32, a.shape, 0)
    d = 1
    while d < tc:
        if d < 8:
            a_sh = jnp.where(row >= d, pltpu.roll(a, d, axis=0), 1.0)
            b_sh = jnp.where(row >= d, pltpu.roll(bb, d, axis=0), 0.0)
            bb = a * b_sh + bb
            a = a * a_sh
        else:
            bb = jnp.concatenate([bb[:d], a[d:] * bb[:-d] + bb[d:]], axis=0)
            a = jnp.concatenate([a[:d], a[d:] * a[:-d]], axis=0)
        d *= 2
    h = a * h_ref[0:1, :] + bb
    h_ref[...] = jnp.broadcast_to(h[tc - 1:tc, :], h_ref.shape)
    o_ref[0] = h * _gelu_tanh(yg_ref[0])


def _block_diag(w):
    n, c, d = w.shape
    return jnp.einsum('ncd,nm->ncmd', w, jnp.eye(n, dtype=w.dtype)).reshape(n * c, n * d)


def _rglru(xr, yg, conv_w, conv_b, wa, ba, wx, bx, lam):
    b_, t_, c_ = xr.shape
    tc = min(RGLRU_CHUNK, t_)
    wa_bd = _block_diag(wa).astype(BF16)
    wx_bd = _block_diag(wx).astype(BF16)
    vec = lambda v: v.reshape(1, c_)
    full = lambda a: pl.BlockSpec(a.shape, lambda b, i: (0,) * a.ndim)
    blk = pl.BlockSpec((1, tc, c_), lambda b, i: (b, i, 0))
    args = (xr, yg, conv_w, vec(conv_b), wa_bd, vec(ba), wx_bd, vec(bx), vec(lam))
    return pl.pallas_call(
        _rglru_kernel,
        name="rglru",
        grid=(b_, t_ // tc),
        in_specs=[blk, blk] + [full(a) for a in args[2:]],
        out_specs=blk,
        out_shape=jax.ShapeDtypeStruct((b_, t_, c_), F32),
        scratch_shapes=[pltpu.VMEM((8, c_), F32), pltpu.VMEM((8, c_), F32)],
        compiler_params=_cparams(("parallel", "arbitrary"), V7X_VMEM_LIMIT_BYTES),
    )(*args)


def _out_proj_kernel(oaT_ref, orec_ref, x_ref, ga_ref, gr_ref, wa_ref, wr_ref, o_ref):
    oaT = oaT_ref[0]
    ms = jnp.mean(oaT * oaT, axis=0, keepdims=True)
    na = (oaT * lax.rsqrt(ms + RMS_EPS)).T * ga_ref[...]
    nr = _rms(orec_ref[0], gr_ref[...])
    y = (jnp.dot(na.astype(BF16), wa_ref[...], preferred_element_type=F32)
         + jnp.dot(nr.astype(BF16), wr_ref[...], preferred_element_type=F32))
    o_ref[0] = x_ref[0] + y


def _out_proj(oaT, orec, x, g_attn, g_rec, w_out):
    b_, t_, d_ = x.shape
    tm = min(256, t_)
    wa = w_out[:D_ATTN].astype(BF16)
    wr = w_out[D_ATTN:].astype(BF16)
    ga = g_attn.reshape(1, -1)
    gr = g_rec.reshape(1, -1)
    full = lambda a: pl.BlockSpec(a.shape, lambda b, i: (0,) * a.ndim)
    return pl.pallas_call(
        _out_proj_kernel,
        name="out_proj",
        grid=(b_, t_ // tm),
        in_specs=[pl.BlockSpec((1, D_ATTN, tm), lambda b, i: (b, 0, i)),
                  pl.BlockSpec((1, tm, orec.shape[2]), lambda b, i: (b, i, 0)),
                  pl.BlockSpec((1, tm, d_), lambda b, i: (b, i, 0)),
                  full(ga), full(gr), full(wa), full(wr)],
        out_specs=pl.BlockSpec((1, tm, d_), lambda b, i: (b, i, 0)),
        out_shape=jax.ShapeDtypeStruct((b_, t_, d_), F32),
        compiler_params=_cparams(("parallel", "parallel"), V7X_VMEM_LIMIT_BYTES),
    )(oaT, orec, x, ga, gr, wa, wr)


def _ffn_kernel(be_ref, x_ref, g_ref, wg_ref, wu_ref, wd_ref, o_ref, hn_ref, acc_ref, *, dense):
    del be_ref
    f = pl.program_id(1)

    @pl.when(f == 0)
    def _():
        if dense:
            x = x_ref[...]
            hn_ref[...] = _rms(x, g_ref[...]).astype(BF16)
            acc_ref[...] = x
        else:
            hn_ref[...] = _unpack_bf16_pairs(x_ref[...])
            acc_ref[...] = jnp.zeros_like(acc_ref)

    hn = hn_ref[...]
    gate = jnp.dot(hn, wg_ref[0], preferred_element_type=F32)
    up = jnp.dot(hn, wu_ref[0], preferred_element_type=F32)
    act = (_silu(gate) * up).astype(BF16)
    acc_ref[...] += jnp.dot(act, wd_ref[0], preferred_element_type=F32)

    @pl.when(f == pl.num_programs(1) - 1)
    def _():
        o_ref[...] = acc_ref[...].astype(o_ref.dtype)


def _ffn(x, norm_g, wg, wu, wd, block_e, *, dense, tf):
    n = x.shape[0]
    d_ = wg.shape[1]
    tm = min(ROW_TILE if dense else MOE_ROWS, n)
    ff = wg.shape[2]
    assert ff % tf == 0 and n % tm == 0
    if block_e is None:
        block_e = jnp.zeros((n // tm,), I32)
    wmode = dict(pipeline_mode=pl.Buffered(1)) if (wg.shape[0] == 1 and ff == tf) else {}
    grid_spec = pltpu.PrefetchScalarGridSpec(
        num_scalar_prefetch=1,
        grid=(n // tm, ff // tf),
        in_specs=[pl.BlockSpec((tm, x.shape[1]), lambda i, f, be: (i, 0)),
                  pl.BlockSpec((1, d_), lambda i, f, be: (0, 0)),
                  pl.BlockSpec((1, d_, tf), lambda i, f, be: (be[i], 0, f), **wmode),
                  pl.BlockSpec((1, d_, tf), lambda i, f, be: (be[i], 0, f), **wmode),
                  pl.BlockSpec((1, tf, d_), lambda i, f, be: (be[i], f, 0), **wmode)],
        out_specs=pl.BlockSpec((tm, d_), lambda i, f, be: (i, 0)),
        scratch_shapes=[pltpu.VMEM((tm, d_), BF16), pltpu.VMEM((tm, d_), F32)],
    )
    return pl.pallas_call(
        functools.partial(_ffn_kernel, dense=dense),
        name="ffn_dense" if dense else "ffn_expert",
        grid_spec=grid_spec,
        out_shape=jax.ShapeDtypeStruct((n, d_), F32 if dense else BF16),
        compiler_params=_cparams(("parallel", "arbitrary"), V7X_VMEM_LIMIT_BYTES),
    )(block_e, x, norm_g.reshape(1, d_), wg, wu, wd)


def _pack_bf16_pairs(x):
    w = x.shape[1] // 2
    bits = lax.bitcast_convert_type(x.astype(BF16).astype(F32), jnp.uint32)
    return (bits[:, :w] >> 16) | bits[:, w:]


def _unpack_bf16_pairs(words):
    lo = lax.bitcast_convert_type(words << 16, F32)
    hi = lax.bitcast_convert_type(words & jnp.uint32(0xFFFF0000), F32)
    return jnp.concatenate([lo, hi], axis=1).astype(BF16)


def _router_kernel(x_ref, g_ref, wr_ref, h_ref, e_ref, p_ref):
    hn = _rms(x_ref[...], g_ref[...])
    h_ref[...] = _pack_bf16_pairs(hn)
    logits = lax.dot_general(wr_ref[...], hn, (((1,), (1,)), ((), ())),
                             precision=lax.Precision.HIGHEST, preferred_element_type=F32)
    eidx = lax.broadcasted_iota(I32, logits.shape, 0)
    m1 = jnp.max(logits, axis=0, keepdims=True)
    i1 = jnp.min(jnp.where(logits == m1, eidx, N_EXPERTS), axis=0, keepdims=True)
    rest = jnp.where(eidx == i1, -jnp.inf, logits)
    m2 = jnp.max(rest, axis=0, keepdims=True)
    i2 = jnp.min(jnp.where(rest == m2, eidx, N_EXPERTS), axis=0, keepdims=True)
    e2 = jnp.exp(m2 - m1)
    inv = 1.0 / (1.0 + e2)
    e_ref[...] = jnp.concatenate([i1, i2], axis=0)
    p_ref[...] = jnp.concatenate([inv, e2 * inv], axis=0)


def _router(x, norm_g, router_w):
    n, d_ = x.shape
    tm = min(ROW_TILE, n)
    wr = router_w.T
    return pl.pallas_call(
        _router_kernel,
        name="router",
        grid=(n // tm,),
        in_specs=[pl.BlockSpec((tm, d_), lambda i: (i, 0)),
                  pl.BlockSpec((1, d_), lambda i: (0, 0)),
                  pl.BlockSpec(wr.shape, lambda i: (0, 0))],
        out_specs=[pl.BlockSpec((tm, d_ // 2), lambda i: (i, 0)),
                   pl.BlockSpec((2, tm), lambda i: (0, i)),
                   pl.BlockSpec((2, tm), lambda i: (0, i))],
        out_shape=[jax.ShapeDtypeStruct((n, d_ // 2), jnp.uint32),
                   jax.ShapeDtypeStruct((2, n), I32),
                   jax.ShapeDtypeStruct((2, n), F32)],
        compiler_params=_cparams(("parallel",), V7X_VMEM_LIMIT_BYTES),
    )(x, norm_g.reshape(1, d_), wr)


SCATTER_UNROLL = 8


def _scatter_kernel(dest_ref, h_ref, zeros_ref, xs_ref, sem):
    del zeros_ref
    n_tok = h_ref.shape[0]

    def row_copy(t, k):
        return pltpu.make_async_copy(h_ref.at[pl.ds(t, 1), :], xs_ref.at[pl.ds(dest_ref[k, t], 1), :], sem)

    def issue(u, carry):
        for r in range(SCATTER_UNROLL):
            t = u * SCATTER_UNROLL + r
            row_copy(t, 0).start()
            row_copy(t, 1).start()
        return carry

    lax.fori_loop(0, n_tok // SCATTER_UNROLL, issue, 0)
    for _ in range(2):
        pltpu.make_async_copy(h_ref, xs_ref.at[pl.ds(0, n_tok), :], sem).wait()


def _scatter(h, dest, n_slots):
    n, w = h.shape
    tc = min(MOE_TOK_CHUNK, n)
    zeros = jnp.zeros((n_slots, w), h.dtype)
    return pl.pallas_call(
        _scatter_kernel,
        name="moe_scatter",
        grid=(n // tc,),
        in_specs=[pl.BlockSpec((2, tc), lambda c: (0, c), memory_space=pltpu.SMEM),
                  pl.BlockSpec((tc, w), lambda c: (c, 0)),
                  pl.BlockSpec(memory_space=pl.ANY)],
        out_specs=pl.BlockSpec(memory_space=pl.ANY),
        out_shape=jax.ShapeDtypeStruct((n_slots, w), h.dtype),
        scratch_shapes=[pltpu.SemaphoreType.DMA(())],
        input_output_aliases={2: 0},
        compiler_params=_cparams(("arbitrary",), V7X_VMEM_LIMIT_BYTES),
    )(dest, h, zeros)


N_CMB_OPS = 2 * N_EXPERTS


def _combine_kernel(jt_ref, vt_ref, x_ref, dest_ref, gate_ref, *rest):
    yb_refs = rest[:N_CMB_OPS]
    o_ref = rest[N_CMB_OPS]
    c = pl.program_id(0)
    dest = dest_ref[...]
    gate = gate_ref[...]
    lane = lax.broadcasted_iota(I32, (dest.shape[0], CMB_ROWS), 1)
    acc = x_ref[...]
    for k in range(N_CMB_OPS):
        base = jnp.where(vt_ref[c * N_CMB_OPS + k] == 1, jt_ref[c * N_CMB_OPS + k] * CMB_ROWS, -2 * CMB_ROWS)
        rel = dest - base
        w = (jnp.where(lane == rel[:, 0:1], gate[:, 0:1], 0.0)
             + jnp.where(lane == rel[:, 1:2], gate[:, 1:2], 0.0)).astype(BF16)
        acc = acc + jnp.dot(w, yb_refs[k][...], preferred_element_type=F32)
    o_ref[...] = acc


def _combine(x, yb, dest_t, gate_t, jt, vt):
    n, d_ = x.shape
    yb_spec = lambda k: pl.BlockSpec((CMB_ROWS, d_), lambda c, jt, vt, k=k: (jt[c * N_CMB_OPS + k], 0))
    grid_spec = pltpu.PrefetchScalarGridSpec(
        num_scalar_prefetch=2,
        grid=(n // CMB_ROWS,),
        in_specs=[pl.BlockSpec((CMB_ROWS, d_), lambda c, jt, vt: (c, 0)),
                  pl.BlockSpec((CMB_ROWS, 2), lambda c, jt, vt: (c, 0)),
                  pl.BlockSpec((CMB_ROWS, 2), lambda c, jt, vt: (c, 0))]
                 + [yb_spec(k) for k in range(N_CMB_OPS)],
        out_specs=pl.BlockSpec((CMB_ROWS, d_), lambda c, jt, vt: (c, 0)),
    )
    return pl.pallas_call(
        _combine_kernel,
        name="moe_combine",
        grid_spec=grid_spec,
        out_shape=jax.ShapeDtypeStruct((n, d_), F32),
        compiler_params=_cparams(("parallel",), V7X_VMEM_LIMIT_BYTES),
    )(jt, vt, x, dest_t, gate_t, *([yb] * N_CMB_OPS))


def _moe_plan(top_e, n_slots):
    n = top_e.shape[1]
    e0, e1 = top_e[0], top_e[1]
    oh = jax.nn.one_hot(e0, N_EXPERTS, dtype=I32) + jax.nn.one_hot(e1, N_EXPERTS, dtype=I32)
    csum_incl = jnp.cumsum(oh, axis=0)
    csum = csum_incl - oh
    counts = csum_incl[-1]
    padded = (counts + MOE_ROWS - 1) // MOE_ROWS * MOE_ROWS
    pad_ends = jnp.cumsum(padded)
    pad_starts = pad_ends - padded
    rank0 = jnp.take_along_axis(csum, e0[:, None], axis=1)[:, 0]
    rank1 = jnp.take_along_axis(csum, e1[:, None], axis=1)[:, 0]
    dest = jnp.stack([pad_starts[e0] + rank0, pad_starts[e1] + rank1]).astype(I32)
    n_blk = n_slots // MOE_ROWS
    blk_start = jnp.arange(n_blk, dtype=I32) * MOE_ROWS
    block_e = jnp.minimum(jnp.sum((pad_ends[None, :] <= blk_start[:, None]).astype(I32), axis=1),
                          N_EXPERTS - 1).astype(I32)

    cb = jnp.concatenate([csum[::CMB_ROWS], counts[None]], axis=0)
    lo = pad_starts[None] + cb[:-1]
    hi = pad_starts[None] + cb[1:]
    ja = lo // CMB_ROWS
    jb = (hi - 1) // CMB_ROWS
    va = hi > lo
    vb = va & (jb > ja)
    jt = jnp.stack([ja, jb], axis=-1).reshape(-1)
    vt = jnp.stack([va, vb], axis=-1).reshape(-1)
    jt = jnp.where(vt, jt, 0).astype(I32)
    return dest, block_e, jt, vt.astype(I32)


def _moe(x, norm_g, router_w, wg, wu, wd):
    n, d_ = x.shape
    n_slots = (2 * n // MOE_ROWS + N_EXPERTS) * MOE_ROWS
    h, top_e, top_p = _router(x, norm_g, router_w)
    dest, block_e, jt, vt = _moe_plan(top_e, n_slots)
    xs = _scatter(h, dest, n_slots)
    yb = _ffn(xs, norm_g, wg, wu, wd, block_e, dense=False, tf=wg.shape[2] // 2)
    return _combine(x, yb, dest.T, top_p.T, jt, vt)


def _ple_kernel(x_ref, p_ref, g_ref, wg_ref, wp_ref, fg_ref, o_ref, *, final):
    x = x_ref[...]
    gate = _sigmoid(jnp.dot(_rms(x, g_ref[...]).astype(BF16), wg_ref[...], preferred_element_type=F32))
    proj = jnp.dot(p_ref[...].astype(BF16), wp_ref[...], preferred_element_type=F32)
    y = x + gate * proj
    if final:
        y = _rms(y, fg_ref[...])
    o_ref[...] = y


def _ple(x, p, norm_g, w_gate, w_proj, final_g, *, final):
    n, d_ = x.shape
    tm = min(ROW_TILE, n)
    wg = w_gate.astype(BF16)
    wp = w_proj.astype(BF16)
    full = lambda a: pl.BlockSpec(a.shape, lambda i: (0,) * a.ndim)
    g2 = norm_g.reshape(1, d_)
    fg = final_g.reshape(1, d_)
    return pl.pallas_call(
        functools.partial(_ple_kernel, final=final),
        name="ple_final" if final else "ple",
        grid=(n // tm,),
        in_specs=[pl.BlockSpec((tm, d_), lambda i: (i, 0)),
                  pl.BlockSpec((tm, p.shape[1]), lambda i: (i, 0)),
                  full(g2), full(wg), full(wp), full(fg)],
        out_specs=pl.BlockSpec((tm, d_), lambda i: (i, 0)),
        out_shape=jax.ShapeDtypeStruct((n, d_), F32),
        compiler_params=_cparams(("parallel",), V7X_VMEM_LIMIT_BYTES),
    )(x, p, g2, wg, wp, fg)


def kernel(x, p, attn_norm, w_in, cmp_pos, cmp_w1, cmp_w2, conv_w, conv_b, lru_wa, lru_ba, lru_wx, lru_bx,
           lru_lambda, out_norm_attn, out_norm_rec, w_out, ffn_norm, dense_w_gate, dense_w_up, dense_w_down,
           router_w, moe_w_gate, moe_w_up, moe_w_down, ple_norm, ple_w_gate, ple_w_proj, final_norm):
    b_, t_, d_ = x.shape
    depth = w_in.shape[0]
    n = b_ * t_
    for i in range(depth):
        kcvc, ks, kw, xr, yg, qT, vsT, vwT, gT = _proj_in(x, attn_norm[i], w_in[i])
        kcmp = _compress(kcvc, cmp_pos[i], cmp_w1[i], cmp_w2[i])
        oaT = _attention(qT, gT, kcmp, ks, vsT, kw, vwT)
        orec = _rglru(xr, yg, conv_w[i], conv_b[i], lru_wa[i], lru_ba[i], lru_wx[i], lru_bx[i], lru_lambda[i])
        x = _out_proj(oaT, orec, x, out_norm_attn[i], out_norm_rec[i], w_out[i])
        x2 = x.reshape(n, d_)
        j = i // 2
        if i % 2 == 0:
            x2 = _ffn(x2, ffn_norm[i], dense_w_gate[j][None].astype(BF16), dense_w_up[j][None].astype(BF16),
                      dense_w_down[j][None].astype(BF16), None, dense=True, tf=dense_w_gate.shape[2])
        else:
            x2 = _moe(x2, ffn_norm[i], router_w[j], moe_w_gate[j].astype(BF16), moe_w_up[j].astype(BF16),
                      moe_w_down[j].astype(BF16))
        x2 = _ple(x2, p[i].reshape(n, -1), ple_norm[i], ple_w_gate[i], ple_w_proj[i], final_norm,
                  final=(i == depth - 1))
        x = x2.reshape(b_, t_, d_)
    return x
```

```python
import functools

import jax
import jax.numpy as jnp
from jax import lax
from jax.experimental import pallas as pl
from jax.experimental.pallas import tpu as pltpu

F32 = jnp.float32
BF16 = jnp.bfloat16
I32 = jnp.int32

N_ATTN_HEADS = 8
HEAD_DIM = 64
N_KV = 2
HEADS_PER_KV = N_ATTN_HEADS // N_KV
D_ATTN = N_ATTN_HEADS * HEAD_DIM
KV_W = N_KV * HEAD_DIM
N_GATES = 3
L_CMP = 32
STRIDE = 16
L_SLC = 64
N_SEL = 16
N_LOCAL = 2
W_WIN = 512
CMP_HID = 2 * HEAD_DIM
CONV_W = 4
C_LRU = 8.0
N_EXPERTS = 8
RMS_EPS = 1e-6
ATTN_SCALE = HEAD_DIM ** -0.5
LOG2E = 1.4426950408889634
LOG2E_HI = 1.4453125
LOG2E_LO = LOG2E - LOG2E_HI

V7X_LANES = 128
V7X_VMEM_LIMIT_BYTES = 56 * 1024 * 1024

TQ = 128
KT_SLC = 256
KT_WIN = 128
N_WIN_TILES = W_WIN // KT_WIN + 1
V_ROWS = HEAD_DIM + 16
MASK_NEG = -1e30
SEL_NEG = -1.0e4
ROW_TILE = 512
RGLRU_CHUNK = 256
MOE_ROWS = 512
MOE_TOK_CHUNK = 512
CMB_ROWS = 256


def _cparams(semantics, vmem=None):
    return pltpu.CompilerParams(dimension_semantics=semantics, vmem_limit_bytes=vmem)


def _rms(x, g):
    ms = jnp.mean(x * x, axis=-1, keepdims=True)
    return x * lax.rsqrt(ms + RMS_EPS) * g


def _gelu_tanh(x):
    c = 0.7978845608028654
    return x * (0.5 * (1.0 + jnp.tanh(c * (x + 0.044715 * (x * x * x)))))


def _sigmoid(x):
    return 1.0 / (1.0 + jnp.exp(-x))


def _silu(x):
    return x * _sigmoid(x)


N_TOK_COLS = 2 * KV_W + 2 * KV_W + 2 * 512
N_TR_ROWS = D_ATTN + 2 * KV_W + 32


def _proj_in_kernel(x_ref, g_ref, wtok_ref, wtr_ref,
                    kcvc_ref, ks_ref, kw_ref, xr_ref, yg_ref, qT_ref, vsT_ref, vwT_ref, gT_ref):
    hn = _rms(x_ref[0], g_ref[...]).astype(BF16)
    tok = jnp.dot(hn, wtok_ref[...], preferred_element_type=F32)
    kcvc_ref[0] = tok[:, 0:256]
    ks_ref[0] = tok[:, 256:384].astype(BF16)
    kw_ref[0] = tok[:, 384:512].astype(BF16)
    xr_ref[0] = tok[:, 512:1024]
    yg_ref[0] = tok[:, 1024:1536]
    tr = lax.dot_general(wtr_ref[...], hn, (((1,), (1,)), ((), ())),
                         preferred_element_type=F32)
    qT_ref[0] = (tr[0:512] * (ATTN_SCALE * LOG2E)).astype(BF16)
    vsT_ref[0] = tr[512:640].astype(BF16)
    vwT_ref[0] = tr[640:768].astype(BF16)
    gT_ref[0] = _sigmoid(tr[768:800])


def _proj_in(x, norm_g, w_in):
    b_, t_, d_ = x.shape
    tm = min(ROW_TILE, t_)
    q, kc, vc, ks, vs, kw, vw, g, xr, yg = jnp.split(
        w_in, [512, 640, 768, 896, 1024, 1152, 1280, 1304, 1816], axis=1)
    wtok = jnp.concatenate([kc, vc, ks, kw, xr, yg], axis=1).astype(BF16)
    g4 = g.reshape(d_, N_KV, HEADS_PER_KV, N_GATES).transpose(0, 1, 3, 2)
    g4 = jnp.pad(g4.reshape(d_, N_KV, 12), ((0, 0), (0, 0), (0, 4))).reshape(d_, 32)
    wtr = jnp.concatenate([q, vs, vw, g4], axis=1).T.astype(BF16)
    nt = t_ // tm
    row = lambda shape: pl.BlockSpec((1, tm, shape), lambda b, i: (b, i, 0))
    col = lambda shape: pl.BlockSpec((1, shape, tm), lambda b, i: (b, 0, i))
    full = lambda a: pl.BlockSpec(a.shape, lambda b, i: (0,) * a.ndim)
    g2 = norm_g.reshape(1, d_)
    outs = pl.pallas_call(
        _proj_in_kernel,
        name="proj_in",
        grid=(b_, nt),
        in_specs=[row(d_), full(g2), full(wtok), full(wtr)],
        out_specs=[row(256), row(128), row(128), row(512), row(512),
                   col(512), col(128), col(128), col(32)],
        out_shape=[
            jax.ShapeDtypeStruct((b_, t_, 256), F32),
            jax.ShapeDtypeStruct((b_, t_, 128), BF16),
            jax.ShapeDtypeStruct((b_, t_, 128), BF16),
            jax.ShapeDtypeStruct((b_, t_, 512), F32),
            jax.ShapeDtypeStruct((b_, t_, 512), F32),
            jax.ShapeDtypeStruct((b_, 512, t_), BF16),
            jax.ShapeDtypeStruct((b_, 128, t_), BF16),
            jax.ShapeDtypeStruct((b_, 128, t_), BF16),
            jax.ShapeDtypeStruct((b_, 32, t_), F32),
        ],
        compiler_params=_cparams(("parallel", "parallel"), V7X_VMEM_LIMIT_BYTES),
    )(x, g2, wtok, wtr)
    return outs


def _compress_kernel(xc_ref, pa_ref, pb_ref, w1a_ref, w1b_ref, w2_ref, out_ref):
    xc = xc_ref[0].astype(BF16)
    n_chunk = xc.shape[0]
    a = jnp.dot(xc, w1a_ref[...], preferred_element_type=F32)
    bm = jnp.dot(xc, w1b_ref[...], preferred_element_type=F32)
    posc = (jnp.dot(pa_ref[...], w1a_ref[...], preferred_element_type=F32)
            + jnp.dot(pb_ref[...], w1b_ref[...], preferred_element_type=F32))[0:1]
    row = lax.broadcasted_iota(I32, bm.shape, 0)
    bm_up = jnp.where(row < n_chunk - 1, pltpu.roll(bm, n_chunk - 1, axis=0), 0.0)
    hid = _gelu_tanh(a + bm_up + posc).astype(BF16)
    out_ref[0] = jnp.dot(hid, w2_ref[...], preferred_element_type=F32)


def _compress(kcvc, cmp_pos, cmp_w1, cmp_w2):
    b_, t_, _ = kcvc.shape
    n_chunk = t_ // STRIDE
    xc = kcvc.reshape(b_, n_chunk, STRIDE * 256)
    eye = jnp.eye(2 * N_KV, dtype=F32)
    w1 = cmp_w1.reshape(2, 2, STRIDE, HEAD_DIM, CMP_HID)

    def expand(half):
        w = w1[:, half]
        full = jnp.einsum('wldo,wx,gy->lxgdwyo', w, jnp.eye(2, dtype=F32), jnp.eye(2, dtype=F32))
        return full.reshape(STRIDE * 256, 4 * CMP_HID).astype(BF16)

    del eye
    w1a, w1b = expand(0), expand(1)
    pos = cmp_pos.reshape(2, 2, STRIDE, HEAD_DIM)

    def posrow(half):
        p = jnp.broadcast_to(pos[:, half][:, None], (2, N_KV, STRIDE, HEAD_DIM))
        p = p.transpose(2, 0, 1, 3).reshape(1, STRIDE * 256)
        return jnp.pad(p, ((0, 7), (0, 0))).astype(BF16)

    pa, pb = posrow(0), posrow(1)
    w2 = jnp.einsum('whd,wx,gy->wghxyd', cmp_w2, jnp.eye(2, dtype=F32), jnp.eye(2, dtype=F32))
    w2 = w2.reshape(4 * CMP_HID, 4 * HEAD_DIM).astype(BF16)
    full = lambda a: pl.BlockSpec(a.shape, lambda b: (0,) * a.ndim)
    return pl.pallas_call(
        _compress_kernel,
        name="compress_kv",
        grid=(b_,),
        in_specs=[pl.BlockSpec((1, n_chunk, STRIDE * 256), lambda b: (b, 0, 0)),
                  full(pa), full(pb), full(w1a), full(w1b), full(w2)],
        out_specs=pl.BlockSpec((1, n_chunk, 256), lambda b: (b, 0, 0)),
        out_shape=jax.ShapeDtypeStruct((b_, n_chunk, 256), F32),
        compiler_params=_cparams(("parallel",), V7X_VMEM_LIMIT_BYTES),
    )(xc, pa, pb, w1a, w1b, w2)


def _head_lanes(rows):
    r = rows.shape[0] // HEADS_PER_KV
    return jnp.concatenate([rows[h * r:(h + 1) * r, :] for h in range(HEADS_PER_KV)], axis=1)


def _attn_kernel(qT_ref, gT_ref, kc_ref, vcT_ref, ks_ref, vsT_ref, kw_ref, vwT_ref, mt_ref, o_ref, s_scr,
                 tile_list):
    i = pl.program_id(1)
    n_cmp = kc_ref.shape[2]
    n_slc = mt_ref.shape[0]
    nl = HEADS_PER_KV * TQ
    hd_rows = HEADS_PER_KV * HEAD_DIM

    lane = lax.broadcasted_iota(I32, (1, nl), 1)
    t_loc = lane % TQ
    t_row = i * TQ + t_loc

    def extra_rows(n_rows, rows):
        ridx = lax.broadcasted_iota(I32, (16, nl), 0)
        out = jnp.zeros((16, nl), F32)
        for k, r in enumerate(rows):
            out = jnp.where(ridx == k, r, out)
        return jnp.concatenate([out.astype(BF16), jnp.zeros((n_rows - 16, nl), BF16)], axis=0)

    ok_c = (lax.broadcasted_iota(I32, (n_cmp, nl), 0) * STRIDE + (L_CMP - 1)) <= t_row
    r_idx = lax.broadcasted_iota(I32, (KT_WIN, nl), 0)
    s_idx = lax.broadcasted_iota(I32, (n_slc, TQ), 0)
    cur = (i * TQ + lax.broadcasted_iota(I32, (1, TQ), 1)) // L_SLC
    valid = s_idx <= cur
    forced = valid & ((s_idx == 0) | (s_idx > cur - N_LOCAL))
    s_idx_f = s_idx.astype(F32)
    n_full = (i * TQ) // KT_SLC
    mt = mt_ref[...]

    def pick(_, score):
        m = jnp.max(score, axis=0, keepdims=True)
        idx = jnp.min(jnp.where(score == m, s_idx_f, 1.0e9), axis=0, keepdims=True)
        return jnp.where(s_idx_f == idx, -2.0, score)

    def stage_a(g, bs, tile, slot, causal):
        k0 = pl.multiple_of(tile * KT_SLC, KT_SLC)
        s = jnp.dot(ks_ref[0, g, pl.ds(k0, KT_SLC), :], bs, preferred_element_type=F32)
        if causal:
            pos = k0 + lax.broadcasted_iota(I32, (KT_SLC, nl), 0)
            s = jnp.where(pos <= t_row, s, MASK_NEG)
        s_scr[g, slot] = s
        return jnp.max(s, axis=0, keepdims=True)

    def stage_b(g, tile, slot, m, acc, m_tile):
        k0 = pl.multiple_of(tile * KT_SLC, KT_SLC)
        m_new = jnp.maximum(m, m_tile)
        p = jnp.exp2(s_scr[g, slot] - m_new)
        acc = jnp.exp2(m - m_new) * acc + jnp.dot(vsT_ref[0, g, :, pl.ds(k0, KT_SLC)], p.astype(BF16),
                                                  preferred_element_type=F32)
        return m_new, acc

    groups = range(N_KV)

    def setup(g):
        q4 = _head_lanes(qT_ref[0, g * hd_rows:(g + 1) * hd_rows, :])
        head = g * HEADS_PER_KV + lane // TQ + 1
        slope = lax.bitcast_convert_type((127 - head) << 23, F32)
        return dict(q4=q4, slope=slope, s_hi=slope * LOG2E_HI, s_lo=slope * LOG2E_LO)

    st = [setup(g) for g in groups]

    def cmp_scores(g):
        s_hi, s_lo = st[g]["s_hi"], st[g]["s_lo"]
        bc = jnp.concatenate(
            [st[g]["q4"], extra_rows(64, [256.0 * s_hi, 256.0 * s_lo, 16.0 * s_hi, 16.0 * s_lo])], axis=0)
        return jnp.dot(kc_ref[0, g], bc, preferred_element_type=F32)

    def cmp_softmax(g, s_c):
        s_c = jnp.where(ok_c, s_c, MASK_NEG)
        m_c = jnp.max(s_c, axis=0, keepdims=True)
        e_c = jnp.exp2(s_c - m_c)
        l_c = jnp.sum(e_c, axis=0, keepdims=True)
        inv_c = jnp.where(m_c > 0.5 * MASK_NEG, 1.0 / jnp.maximum(l_c, 1e-30), 0.0)
        p_c = e_c * inv_c
        o_c = jnp.dot(vcT_ref[0, g], p_c.astype(BF16), preferred_element_type=F32)
        p_grp = (p_c[:, 0:TQ] + p_c[:, TQ:2 * TQ]) + (p_c[:, 2 * TQ:3 * TQ] + p_c[:, 3 * TQ:4 * TQ])
        p1 = p_grp.astype(BF16)
        r1 = p_grp - p1.astype(F32)
        p2 = r1.astype(BF16)
        p3 = (r1 - p2.astype(F32)).astype(BF16)
        p_slc = (jnp.dot(mt, p1, preferred_element_type=F32) + jnp.dot(mt, p2, preferred_element_type=F32)
                 + jnp.dot(mt, p3, preferred_element_type=F32))
        return o_c, p_slc

    def win_scores(g):
        slope = st[g]["slope"]
        bw = jnp.concatenate([st[g]["q4"], extra_rows(64, [st[g]["s_hi"], st[g]["s_lo"]])], axis=0)
        s_tiles, c_tiles = [], []
        for a in range(N_WIN_TILES):
            tile = i - (N_WIN_TILES - 1) + a
            k0 = pl.multiple_of(jnp.maximum(tile, 0) * KT_WIN, KT_WIN)
            s = jnp.dot(kw_ref[0, g, pl.ds(k0, KT_WIN), :], bw, preferred_element_type=F32)
            if a == 0:
                s = jnp.where(r_idx > t_loc, s, MASK_NEG)
            elif a == N_WIN_TILES - 1:
                s = jnp.where(r_idx <= t_loc, s, MASK_NEG)
            s_tiles.append(s)
            c_a = slope * (LOG2E * KT_WIN * (a - (N_WIN_TILES - 1)))
            c_tiles.append(jnp.where(tile >= 0, c_a, MASK_NEG))
        return s_tiles, c_tiles

    def win_softmax(g, s_tiles, c_tiles):
        m_w = s_tiles[0].max(axis=0, keepdims=True) + c_tiles[0]
        for s, c_a in zip(s_tiles[1:], c_tiles[1:]):
            m_w = jnp.maximum(m_w, s.max(axis=0, keepdims=True) + c_a)
        acc_w = jnp.zeros((V_ROWS, nl), F32)
        for a, (s, c_a) in enumerate(zip(s_tiles, c_tiles)):
            tile = i - (N_WIN_TILES - 1) + a
            k0 = pl.multiple_of(jnp.maximum(tile, 0) * KT_WIN, KT_WIN)
            p = jnp.exp2(s - (m_w - c_a))
            acc_w = acc_w + jnp.dot(vwT_ref[0, g, :, pl.ds(k0, KT_WIN)], p.astype(BF16),
                                    preferred_element_type=F32)
        return acc_w[0:HEAD_DIM] * (1.0 / acc_w[HEAD_DIM:HEAD_DIM + 1])

    s_cmp = [cmp_scores(g) for g in groups]
    s_win = [win_scores(g) for g in groups]
    cmp_out, scores = [], []
    for g in groups:
        cmp_out.append(cmp_softmax(g, s_cmp[g]))
        score0 = jnp.where(forced, -2.0, jnp.where(valid, cmp_out[g][1], -1.0))
        scores.append(lax.fori_loop(0, N_SEL - (N_LOCAL + 1), pick, score0, unroll=True))
    o_win = [win_softmax(g, *s_win[g]) for g in groups]
    score = jnp.concatenate(scores, axis=1)

    def select(g):
        sel = valid & (score[:, g * TQ:(g + 1) * TQ] < -1.5)
        sel4 = jnp.concatenate([sel] * HEADS_PER_KV, axis=1)
        blk_rows = jnp.where(sel4, 0.0, SEL_NEG).astype(BF16)
        s_hi, s_lo = st[g]["s_hi"], st[g]["s_lo"]
        bs = jnp.concatenate(
            [st[g]["q4"], blk_rows, extra_rows(64, [L_SLC * s_hi, L_SLC * s_lo, s_hi, s_lo])], axis=0)
        m_tile0 = stage_a(g, bs, n_full, 0, True)
        gt = gT_ref[0, g * 16:(g + 1) * 16, :]
        gates = [_head_lanes(gt[k * HEADS_PER_KV:(k + 1) * HEADS_PER_KV, :]) for k in range(N_GATES)]
        return dict(bs=bs, base=gates[0] * cmp_out[g][0] + gates[2] * o_win[g], g_s=gates[1], m_tile0=m_tile0)

    grp = [select(g) for g in groups]

    blocks_per_tile = KT_SLC // L_SLC
    n_tiles = n_slc // blocks_per_tile
    assert n_tiles <= 32
    valid2 = jnp.concatenate([valid] * N_KV, axis=1)
    sel_all = jnp.where(valid2 & (score < -1.5), 1.0, 0.0).astype(BF16)
    blk_cnt = lax.dot_general(jnp.ones((8, N_KV * TQ), BF16), sel_all, (((1,), (1,)), ((), ())),
                              preferred_element_type=F32)
    memb = jnp.where(lax.broadcasted_iota(I32, (n_slc, V7X_LANES), 0) // blocks_per_tile
                     == lax.broadcasted_iota(I32, (n_slc, V7X_LANES), 1), 1.0, 0.0).astype(BF16)
    tile_cnt = jnp.dot(jnp.where(blk_cnt > 0.5, 1.0, 0.0).astype(BF16), memb,
                       preferred_element_type=F32)[0:1]
    lane_t = lax.broadcasted_iota(I32, (1, V7X_LANES), 1)
    need = (tile_cnt > 0.5) & (lane_t < n_full)
    bit = lax.bitcast_convert_type(((lane_t & 15) + 127) << 23, F32)
    lo_bits = jnp.sum(jnp.where(need & (lane_t < 16), bit, 0.0)).astype(I32)
    hi_bits = jnp.sum(jnp.where(need & (lane_t >= 16), bit, 0.0)).astype(I32)
    n_sel = jnp.int32(0)
    for t in range(n_tiles - 1):
        tile_list[n_sel] = t
        n_sel = n_sel + (((lo_bits if t < 16 else hi_bits) >> (t % 16)) & 1)

    order = lambda k: jnp.where(k == 0, n_full, tile_list[jnp.maximum(k - 1, 0)])

    def half(carry, tile_b, slot_b, tile_a):
        m_next = [c[2] if tile_a is None else stage_a(g, grp[g]["bs"], tile_a, 1 - slot_b, False)
                  for g, c in enumerate(carry)]
        out = []
        for g in range(N_KV):
            m, acc, m_tile = carry[g]
            out.append(stage_b(g, tile_b, slot_b, m, acc, m_tile) + (m_next[g],))
        return tuple(out)

    def body(p, carry):
        k = 2 * p
        return half(half(carry, order(k), 0, order(k + 1)), order(k + 1), 1, order(k + 2))

    init = tuple((jnp.full((1, nl), MASK_NEG, F32), jnp.zeros((V_ROWS, nl), F32), grp[g]["m_tile0"])
                 for g in range(N_KV))
    carry = lax.fori_loop(0, n_sel // 2, body, init)
    k_tail = 2 * (n_sel // 2)
    carry = lax.cond(n_sel % 2 == 1,
                     lambda c: half(half(c, order(k_tail), 0, order(k_tail + 1)), order(k_tail + 1), 1, None),
                     lambda c: half(c, order(k_tail), 0, None), carry)
    for g in range(N_KV):
        _, acc, _ = carry[g]
        o_s = acc[0:HEAD_DIM] * (1.0 / acc[HEAD_DIM:HEAD_DIM + 1])
        out = grp[g]["base"] + grp[g]["g_s"] * o_s
        for h in range(HEADS_PER_KV):
            r0 = g * hd_rows + h * HEAD_DIM
            o_ref[0, r0:r0 + HEAD_DIM, :] = out[:, h * TQ:(h + 1) * TQ]


def _attention(qT, gT, kcmp, ks, vsT, kw, vwT):
    b_, _, t_ = qT.shape
    n_cmp = t_ // STRIDE
    n_slc = t_ // L_SLC
    pos = jnp.arange(t_, dtype=I32)
    grp = lambda a: a.reshape(b_, t_, N_KV, HEAD_DIM).transpose(0, 2, 1, 3)
    def pos_cols(vals, width):
        cols = [v.astype(BF16)[:, None] for v in vals for _ in range(2)]
        return jnp.concatenate(cols + [jnp.zeros((vals[0].shape[0], width - len(cols)), BF16)], axis=1)

    slc_cols = jnp.concatenate(
        [jax.nn.one_hot(pos // L_SLC, n_slc, dtype=BF16), pos_cols([pos // L_SLC, pos % L_SLC], 64)],
        axis=1)
    ks_aug = jnp.concatenate(
        [grp(ks), jnp.broadcast_to(slc_cols, (b_, N_KV) + slc_cols.shape)], axis=-1)
    win_cols = pos_cols([pos % KT_WIN], 64)
    kw_aug = jnp.concatenate(
        [grp(kw), jnp.broadcast_to(win_cols, (b_, N_KV) + win_cols.shape)], axis=-1)
    c = jnp.arange(n_cmp, dtype=I32)
    cmp_cols = pos_cols([c // 16, c % 16], 64)
    kc4 = kcmp[..., 0:KV_W].reshape(b_, n_cmp, N_KV, HEAD_DIM).transpose(0, 2, 1, 3).astype(BF16)
    kc_aug = jnp.concatenate(
        [kc4, jnp.broadcast_to(cmp_cols, (b_, N_KV) + cmp_cols.shape)], axis=-1)
    vcT = kcmp[..., KV_W:2 * KV_W].reshape(b_, n_cmp, N_KV, HEAD_DIM).transpose(0, 2, 3, 1).astype(BF16)
    ones_rows = jnp.ones((b_, N_KV, V_ROWS - HEAD_DIM, t_), BF16)
    vsT4 = jnp.concatenate([vsT.reshape(b_, N_KV, HEAD_DIM, t_), ones_rows], axis=2)
    vwT4 = jnp.concatenate([vwT.reshape(b_, N_KV, HEAD_DIM, t_), ones_rows], axis=2)
    s = jnp.arange(n_slc, dtype=I32)[:, None]
    cc = c[None, :]
    r_slc = L_SLC // STRIDE
    mt = (((cc >= r_slc * s) & (cc < r_slc * s + r_slc)).astype(F32)
          + ((cc + 1 >= r_slc * s) & (cc + 1 < r_slc * s + r_slc)).astype(F32))
    mt = jnp.where(cc < n_cmp - 1, mt, 0.0).astype(BF16)
    kv_spec = lambda a: pl.BlockSpec((1,) + a.shape[1:], lambda b, i: (b, 0, 0, 0))
    return pl.pallas_call(
        _attn_kernel,
        name="nsa_attention",
        grid=(b_, t_ // TQ),
        in_specs=[pl.BlockSpec((1, D_ATTN, TQ), lambda b, i: (b, 0, i)),
                  pl.BlockSpec((1, 32, TQ), lambda b, i: (b, 0, i)),
                  kv_spec(kc_aug), kv_spec(vcT), kv_spec(ks_aug), kv_spec(vsT4),
                  kv_spec(kw_aug), kv_spec(vwT4),
                  pl.BlockSpec(mt.shape, lambda b, i: (0, 0))],
        out_specs=pl.BlockSpec((1, D_ATTN, TQ), lambda b, i: (b, 0, i)),
        out_shape=jax.ShapeDtypeStruct((b_, D_ATTN, t_), F32),
        scratch_shapes=[pltpu.VMEM((N_KV, 2, KT_SLC, HEADS_PER_KV * TQ), F32), pltpu.SMEM((32,), I32)],
        compiler_params=_cparams(("parallel", "arbitrary"), V7X_VMEM_LIMIT_BYTES),
    )(qT, gT, kc_aug, vcT, ks_aug, vsT4, kw_aug, vwT4, mt)


def _rglru_seg_pitch(tc):
    return tc // 8 + 8


def _rglru_kernel(xr_ref, yg_ref, cw_ref, cb_ref, wa_ref, ba_ref, wx_ref, bx_ref, lam_ref,
                  o_ref, tail_ref, h_ref, xs_scr, ys_scr, os_scr):
    tc = xr_ref.shape[1]

    @pl.when(pl.program_id(1) == 0)
    def _():
        tail_ref[...] = jnp.zeros_like(tail_ref)
        h_ref[...] = jnp.zeros_like(h_ref)

    n_v = tc // 8
    n_slab = xr_ref.shape[2] // V7X_LANES
    pitch = xs_scr.shape[1] // 8
    for c in range(n_slab):
        for s in range(8):
            lanes = slice(c * V7X_LANES, (c + 1) * V7X_LANES)
            xs_scr[c, s * pitch:s * pitch + n_v, :] = xr_ref[0, s * n_v:(s + 1) * n_v, lanes]
            ys_scr[c, s * pitch:s * pitch + n_v, :] = yg_ref[0, s * n_v:(s + 1) * n_v, lanes]

    def piece(scr, v):
        return jnp.concatenate([scr[c, pl.ds(v, 8, stride=pitch), :] for c in range(n_slab)], axis=1)

    x = [piece(xs_scr, v) for v in range(n_v)]
    row8 = lax.broadcasted_iota(I32, (8, x[0].shape[1]), 0)
    tail = tail_ref[...]
    before = {d: jnp.where(row8 == 0, tail[CONV_W - 1 - d:CONV_W - d, :], pltpu.roll(x[n_v - d], 1, axis=0))
              for d in range(1, CONV_W)}
    tail_ref[0:CONV_W - 1, :] = jnp.concatenate([x[n_v - d][7:8, :] for d in range(CONV_W - 1, 0, -1)], axis=0)
    cw = cw_ref[...]
    cb = cb_ref[...]
    xc = []
    for v in range(n_v):
        acc = x[v] * cw[CONV_W - 1:CONV_W, :] + cb
        for d in range(1, CONV_W):
            acc = acc + (x[v - d] if v >= d else before[d - v]) * cw[CONV_W - 1 - d:CONV_W - d, :]
        xc.append(acc)

    xb = jnp.concatenate(xc, axis=0).astype(BF16)
    r = _sigmoid(jnp.dot(xb, wa_ref[...], preferred_element_type=F32) + ba_ref[...])
    gi = _sigmoid(jnp.dot(xb, wx_ref[...], preferred_element_type=F32) + bx_ref[...])
    z = -lam_ref[...]
    softplus = jnp.maximum(z, 0.0) + jnp.log1p(jnp.exp(-jnp.abs(z)))
    log_a = (-C_LRU * r) * softplus
    a = jnp.exp(log_a)
    th = jnp.tanh(log_a)
    neg_expm1 = 2.0 * th / (th - 1.0)
    bb = jnp.sqrt(neg_expm1) * gi * jnp.concatenate(xc, axis=0)

    a_cum, b_cum = [a[0:8]], [bb[0:8]]
    for v in range(1, n_v):
        a_v = a[v * 8:(v + 1) * 8]
        b_cum.append(a_v * b_cum[-1] + bb[v * 8:(v + 1) * 8])
        a_cum.append(a_v * a_cum[-1])
    h_in = [h_ref[0:1, :]]
    for s in range(7):
        h_in.append(a_cum[-1][s:s + 1, :] * h_in[-1] + b_cum[-1][s:s + 1, :])
    h_ref[0:1, :] = a_cum[-1][7:8, :] * h_in[-1] + b_cum[-1][7:8, :]
    h_in = jnp.concatenate(h_in, axis=0)
    for v in range(n_v):
        out = (a_cum[v] * h_in + b_cum[v]) * _gelu_tanh(piece(ys_scr, v))
        for c in range(n_slab):
            os_scr[c, pl.ds(v, 8, stride=pitch), :] = out[:, c * V7X_LANES:(c + 1) * V7X_LANES]
    for s in range(8):
        o_ref[0, s * n_v:(s + 1) * n_v, :] = jnp.concatenate(
            [os_scr[c, s * pitch:s * pitch + n_v, :] for c in range(n_slab)], axis=1)


def _block_diag(w):
    n, c, d = w.shape
    return jnp.einsum('ncd,nm->ncmd', w, jnp.eye(n, dtype=w.dtype)).reshape(n * c, n * d)


def _rglru(xr, yg, conv_w, conv_b, wa, ba, wx, bx, lam):
    b_, t_, c_ = xr.shape
    tc = min(RGLRU_CHUNK, t_)
    wa_bd = _block_diag(wa).astype(BF16)
    wx_bd = _block_diag(wx).astype(BF16)
    vec = lambda v: v.reshape(1, c_)
    full = lambda a: pl.BlockSpec(a.shape, lambda b, i: (0,) * a.ndim)
    blk = pl.BlockSpec((1, tc, c_), lambda b, i: (b, i, 0))
    args = (xr, yg, conv_w, vec(conv_b), wa_bd, vec(ba), wx_bd, vec(bx), vec(lam))
    return pl.pallas_call(
        _rglru_kernel,
        name="rglru",
        grid=(b_, t_ // tc),
        in_specs=[blk, blk] + [full(a) for a in args[2:]],
        out_specs=blk,
        out_shape=jax.ShapeDtypeStruct((b_, t_, c_), F32),
        scratch_shapes=[pltpu.VMEM((8, c_), F32), pltpu.VMEM((8, c_), F32)]
                       + [pltpu.VMEM((c_ // V7X_LANES, 8 * _rglru_seg_pitch(tc), V7X_LANES), F32)] * 3,
        compiler_params=_cparams(("parallel", "arbitrary"), V7X_VMEM_LIMIT_BYTES),
    )(*args)


def _out_proj_kernel(oaT_ref, orec_ref, x_ref, ga_ref, gr_ref, wa_ref, wr_ref, o_ref):
    oaT = oaT_ref[0]
    ms = jnp.mean(oaT * oaT, axis=0, keepdims=True)
    na = (oaT * lax.rsqrt(ms + RMS_EPS)).T * ga_ref[...]
    nr = _rms(orec_ref[0], gr_ref[...])
    y = (jnp.dot(na.astype(BF16), wa_ref[...], preferred_element_type=F32)
         + jnp.dot(nr.astype(BF16), wr_ref[...], preferred_element_type=F32))
    o_ref[0] = x_ref[0] + y


def _out_proj(oaT, orec, x, g_attn, g_rec, w_out):
    b_, t_, d_ = x.shape
    tm = min(256, t_)
    wa = w_out[:D_ATTN].astype(BF16)
    wr = w_out[D_ATTN:].astype(BF16)
    ga = g_attn.reshape(1, -1)
    gr = g_rec.reshape(1, -1)
    full = lambda a: pl.BlockSpec(a.shape, lambda b, i: (0,) * a.ndim)
    return pl.pallas_call(
        _out_proj_kernel,
        name="out_proj",
        grid=(b_, t_ // tm),
        in_specs=[pl.BlockSpec((1, D_ATTN, tm), lambda b, i: (b, 0, i)),
                  pl.BlockSpec((1, tm, orec.shape[2]), lambda b, i: (b, i, 0)),
                  pl.BlockSpec((1, tm, d_), lambda b, i: (b, i, 0)),
                  full(ga), full(gr), full(wa), full(wr)],
        out_specs=pl.BlockSpec((1, tm, d_), lambda b, i: (b, i, 0)),
        out_shape=jax.ShapeDtypeStruct((b_, t_, d_), F32),
        compiler_params=_cparams(("parallel", "parallel"), V7X_VMEM_LIMIT_BYTES),
    )(oaT, orec, x, ga, gr, wa, wr)


def _ffn_kernel(be_ref, x_ref, g_ref, wg_ref, wu_ref, wd_ref, o_ref, hn_ref, acc_ref, *, dense):
    f = pl.program_id(1)
    used = pl.program_id(0) < be_ref[pl.num_programs(0)]

    @pl.when(f == 0)
    def _():
        if dense:
            x = x_ref[...]
            hn_ref[...] = _rms(x, g_ref[...]).astype(BF16)
            acc_ref[...] = x
        else:
            hn_ref[...] = _unpack_bf16_pairs(x_ref[...])
            acc_ref[...] = jnp.zeros_like(acc_ref)

    @pl.when(used)
    def _():
        hn = hn_ref[...]
        gate = jnp.dot(hn, wg_ref[0], preferred_element_type=F32)
        up = jnp.dot(hn, wu_ref[0], preferred_element_type=F32)
        act = (_silu(gate) * up).astype(BF16)
        acc_ref[...] += jnp.dot(act, wd_ref[0], preferred_element_type=F32)

    @pl.when(f == pl.num_programs(1) - 1)
    def _():
        o_ref[...] = acc_ref[...].astype(o_ref.dtype)


def _ffn(x, norm_g, wg, wu, wd, block_e, n_used=None, *, dense, tf):
    n = x.shape[0]
    d_ = wg.shape[1]
    tm = min(ROW_TILE if dense else MOE_ROWS, n)
    ff = wg.shape[2]
    assert ff % tf == 0 and n % tm == 0
    if block_e is None:
        block_e = jnp.zeros((n // tm,), I32)
    if n_used is None:
        n_used = jnp.int32(n // tm)
    n_blk, nf = n // tm, ff // tf
    be_all = jnp.concatenate([block_e, jnp.reshape(n_used, (1,)).astype(I32)])
    wmode = dict(pipeline_mode=pl.Buffered(1)) if (wg.shape[0] == 1 and ff == tf) else {}
    ftile = lambda i, f, be: jnp.where(i < be[n_blk], f, nf - 1)
    grid_spec = pltpu.PrefetchScalarGridSpec(
        num_scalar_prefetch=1,
        grid=(n_blk, nf),
        in_specs=[pl.BlockSpec((tm, x.shape[1]), lambda i, f, be: (i, 0)),
                  pl.BlockSpec((1, d_), lambda i, f, be: (0, 0)),
                  pl.BlockSpec((1, d_, tf), lambda i, f, be: (be[i], 0, ftile(i, f, be)), **wmode),
                  pl.BlockSpec((1, d_, tf), lambda i, f, be: (be[i], 0, ftile(i, f, be)), **wmode),
                  pl.BlockSpec((1, tf, d_), lambda i, f, be: (be[i], ftile(i, f, be), 0), **wmode)],
        out_specs=pl.BlockSpec((tm, d_), lambda i, f, be: (i, 0)),
        scratch_shapes=[pltpu.VMEM((tm, d_), BF16), pltpu.VMEM((tm, d_), F32)],
    )
    return pl.pallas_call(
        functools.partial(_ffn_kernel, dense=dense),
        name="ffn_dense" if dense else "ffn_expert",
        grid_spec=grid_spec,
        out_shape=jax.ShapeDtypeStruct((n, d_), F32 if dense else BF16),
        compiler_params=_cparams(("parallel", "arbitrary"), V7X_VMEM_LIMIT_BYTES),
    )(be_all, x, norm_g.reshape(1, d_), wg, wu, wd)


def _pack_bf16_pairs(x):
    w = x.shape[1] // 2
    bits = lax.bitcast_convert_type(x.astype(BF16).astype(F32), jnp.uint32)
    return (bits[:, :w] >> 16) | bits[:, w:]


def _unpack_bf16_pairs(words):
    lo = lax.bitcast_convert_type(words << 16, F32)
    hi = lax.bitcast_convert_type(words & jnp.uint32(0xFFFF0000), F32)
    return jnp.concatenate([lo, hi], axis=1).astype(BF16)


def _router_kernel(x_ref, g_ref, wr_ref, h_ref, e_ref, p_ref):
    hn = _rms(x_ref[...], g_ref[...])
    h_ref[...] = _pack_bf16_pairs(hn)
    logits = lax.dot_general(wr_ref[...], hn, (((1,), (1,)), ((), ())),
                             precision=lax.Precision.HIGHEST, preferred_element_type=F32)
    eidx = lax.broadcasted_iota(I32, logits.shape, 0)
    m1 = jnp.max(logits, axis=0, keepdims=True)
    i1 = jnp.min(jnp.where(logits == m1, eidx, N_EXPERTS), axis=0, keepdims=True)
    rest = jnp.where(eidx == i1, -jnp.inf, logits)
    m2 = jnp.max(rest, axis=0, keepdims=True)
    i2 = jnp.min(jnp.where(rest == m2, eidx, N_EXPERTS), axis=0, keepdims=True)
    e2 = jnp.exp(m2 - m1)
    inv = 1.0 / (1.0 + e2)
    e_ref[...] = jnp.concatenate([i1, i2], axis=0)
    p_ref[...] = jnp.concatenate([inv, e2 * inv], axis=0)


def _router(x, norm_g, router_w):
    n, d_ = x.shape
    tm = min(ROW_TILE, n)
    wr = router_w.T
    return pl.pallas_call(
        _router_kernel,
        name="router",
        grid=(n // tm,),
        in_specs=[pl.BlockSpec((tm, d_), lambda i: (i, 0)),
                  pl.BlockSpec((1, d_), lambda i: (0, 0)),
                  pl.BlockSpec(wr.shape, lambda i: (0, 0))],
        out_specs=[pl.BlockSpec((tm, d_ // 2), lambda i: (i, 0)),
                   pl.BlockSpec((2, tm), lambda i: (0, i)),
                   pl.BlockSpec((2, tm), lambda i: (0, i))],
        out_shape=[jax.ShapeDtypeStruct((n, d_ // 2), jnp.uint32),
                   jax.ShapeDtypeStruct((2, n), I32),
                   jax.ShapeDtypeStruct((2, n), F32)],
        compiler_params=_cparams(("parallel",), V7X_VMEM_LIMIT_BYTES),
    )(x, norm_g.reshape(1, d_), wr)


SCATTER_UNROLL = 8


def _scatter_kernel(dest_ref, h_ref, zeros_ref, xs_ref, sem):
    del zeros_ref
    n_tok = h_ref.shape[0]

    def row_copy(t, k):
        return pltpu.make_async_copy(h_ref.at[pl.ds(t, 1), :], xs_ref.at[pl.ds(dest_ref[k, t], 1), :], sem)

    def issue(u, carry):
        for r in range(SCATTER_UNROLL):
            t = u * SCATTER_UNROLL + r
            row_copy(t, 0).start()
            row_copy(t, 1).start()
        return carry

    lax.fori_loop(0, n_tok // SCATTER_UNROLL, issue, 0)
    for _ in range(2):
        pltpu.make_async_copy(h_ref, xs_ref.at[pl.ds(0, n_tok), :], sem).wait()


def _scatter(h, dest, n_slots):
    n, w = h.shape
    tc = min(MOE_TOK_CHUNK, n)
    zeros = jnp.zeros((n_slots, w), h.dtype)
    return pl.pallas_call(
        _scatter_kernel,
        name="moe_scatter",
        grid=(n // tc,),
        in_specs=[pl.BlockSpec((2, tc), lambda c: (0, c), memory_space=pltpu.SMEM),
                  pl.BlockSpec((tc, w), lambda c: (c, 0)),
                  pl.BlockSpec(memory_space=pl.ANY)],
        out_specs=pl.BlockSpec(memory_space=pl.ANY),
        out_shape=jax.ShapeDtypeStruct((n_slots, w), h.dtype),
        scratch_shapes=[pltpu.SemaphoreType.DMA(())],
        input_output_aliases={2: 0},
        compiler_params=_cparams(("arbitrary",), V7X_VMEM_LIMIT_BYTES),
    )(dest, h, zeros)


N_CMB_OPS = 2 * N_EXPERTS


def _combine_kernel(jt_ref, vt_ref, x_ref, dest_ref, gate_ref, *rest):
    yb_refs = rest[:N_CMB_OPS]
    o_ref = rest[N_CMB_OPS]
    c = pl.program_id(0)
    dest = dest_ref[...]
    gate = gate_ref[...]
    lane = lax.broadcasted_iota(I32, (dest.shape[0], CMB_ROWS), 1)
    acc = x_ref[...]
    for k in range(N_CMB_OPS):
        base = jnp.where(vt_ref[c * N_CMB_OPS + k] == 1, jt_ref[c * N_CMB_OPS + k] * CMB_ROWS, -2 * CMB_ROWS)
        rel = dest - base
        w = (jnp.where(lane == rel[:, 0:1], gate[:, 0:1], 0.0)
             + jnp.where(lane == rel[:, 1:2], gate[:, 1:2], 0.0)).astype(BF16)
        acc = acc + jnp.dot(w, yb_refs[k][...], preferred_element_type=F32)
    o_ref[...] = acc


def _combine(x, yb, dest_t, gate_t, jt, vt):
    n, d_ = x.shape
    yb_spec = lambda k: pl.BlockSpec((CMB_ROWS, d_), lambda c, jt, vt, k=k: (jt[c * N_CMB_OPS + k], 0))
    grid_spec = pltpu.PrefetchScalarGridSpec(
        num_scalar_prefetch=2,
        grid=(n // CMB_ROWS,),
        in_specs=[pl.BlockSpec((CMB_ROWS, d_), lambda c, jt, vt: (c, 0)),
                  pl.BlockSpec((CMB_ROWS, 2), lambda c, jt, vt: (c, 0)),
                  pl.BlockSpec((CMB_ROWS, 2), lambda c, jt, vt: (c, 0))]
                 + [yb_spec(k) for k in range(N_CMB_OPS)],
        out_specs=pl.BlockSpec((CMB_ROWS, d_), lambda c, jt, vt: (c, 0)),
    )
    return pl.pallas_call(
        _combine_kernel,
        name="moe_combine",
        grid_spec=grid_spec,
        out_shape=jax.ShapeDtypeStruct((n, d_), F32),
        compiler_params=_cparams(("parallel",), V7X_VMEM_LIMIT_BYTES),
    )(jt, vt, x, dest_t, gate_t, *([yb] * N_CMB_OPS))


def _moe_plan(top_e, n_slots):
    n = top_e.shape[1]
    e0, e1 = top_e[0], top_e[1]
    oh = jax.nn.one_hot(e0, N_EXPERTS, dtype=I32) + jax.nn.one_hot(e1, N_EXPERTS, dtype=I32)
    csum_incl = jnp.cumsum(oh, axis=0)
    csum = csum_incl - oh
    counts = csum_incl[-1]
    padded = (counts + MOE_ROWS - 1) // MOE_ROWS * MOE_ROWS
    pad_ends = jnp.cumsum(padded)
    pad_starts = pad_ends - padded
    rank0 = jnp.take_along_axis(csum, e0[:, None], axis=1)[:, 0]
    rank1 = jnp.take_along_axis(csum, e1[:, None], axis=1)[:, 0]
    dest = jnp.stack([pad_starts[e0] + rank0, pad_starts[e1] + rank1]).astype(I32)
    n_blk = n_slots // MOE_ROWS
    blk_start = jnp.arange(n_blk, dtype=I32) * MOE_ROWS
    block_e = jnp.minimum(jnp.sum((pad_ends[None, :] <= blk_start[:, None]).astype(I32), axis=1),
                          N_EXPERTS - 1).astype(I32)

    cb = jnp.concatenate([csum[::CMB_ROWS], counts[None]], axis=0)
    lo = pad_starts[None] + cb[:-1]
    hi = pad_starts[None] + cb[1:]
    ja = lo // CMB_ROWS
    jb = (hi - 1) // CMB_ROWS
    va = hi > lo
    vb = va & (jb > ja)
    jt = jnp.stack([ja, jb], axis=-1).reshape(-1)
    vt = jnp.stack([va, vb], axis=-1).reshape(-1)
    jt = jnp.where(vt, jt, 0).astype(I32)
    return dest, block_e, pad_ends[-1] // MOE_ROWS, jt, vt.astype(I32)


def _moe(x, norm_g, router_w, wg, wu, wd):
    n, d_ = x.shape
    n_slots = (2 * n // MOE_ROWS + N_EXPERTS) * MOE_ROWS
    h, top_e, top_p = _router(x, norm_g, router_w)
    dest, block_e, n_used, jt, vt = _moe_plan(top_e, n_slots)
    xs = _scatter(h, dest, n_slots)
    yb = _ffn(xs, norm_g, wg, wu, wd, block_e, n_used, dense=False, tf=wg.shape[2] // 2)
    return _combine(x, yb, dest.T, top_p.T, jt, vt)


def _ple_kernel(x_ref, p_ref, g_ref, wg_ref, wp_ref, fg_ref, o_ref, *, final):
    x = x_ref[...]
    gate = _sigmoid(jnp.dot(_rms(x, g_ref[...]).astype(BF16), wg_ref[...], preferred_element_type=F32))
    proj = jnp.dot(p_ref[...].astype(BF16), wp_ref[...], preferred_element_type=F32)
    y = x + gate * proj
    if final:
        y = _rms(y, fg_ref[...])
    o_ref[...] = y


def _ple(x, p, norm_g, w_gate, w_proj, final_g, *, final):
    n, d_ = x.shape
    tm = min(ROW_TILE, n)
    wg = w_gate.astype(BF16)
    wp = w_proj.astype(BF16)
    full = lambda a: pl.BlockSpec(a.shape, lambda i: (0,) * a.ndim)
    g2 = norm_g.reshape(1, d_)
    fg = final_g.reshape(1, d_)
    return pl.pallas_call(
        functools.partial(_ple_kernel, final=final),
        name="ple_final" if final else "ple",
        grid=(n // tm,),
        in_specs=[pl.BlockSpec((tm, d_), lambda i: (i, 0)),
                  pl.BlockSpec((tm, p.shape[1]), lambda i: (i, 0)),
                  full(g2), full(wg), full(wp), full(fg)],
        out_specs=pl.BlockSpec((tm, d_), lambda i: (i, 0)),
        out_shape=jax.ShapeDtypeStruct((n, d_), F32),
        compiler_params=_cparams(("parallel",), V7X_VMEM_LIMIT_BYTES),
    )(x, p, g2, wg, wp, fg)


def kernel(x, p, attn_norm, w_in, cmp_pos, cmp_w1, cmp_w2, conv_w, conv_b, lru_wa, lru_ba, lru_wx, lru_bx,
           lru_lambda, out_norm_attn, out_norm_rec, w_out, ffn_norm, dense_w_gate, dense_w_up, dense_w_down,
           router_w, moe_w_gate, moe_w_up, moe_w_down, ple_norm, ple_w_gate, ple_w_proj, final_norm):
    b_, t_, d_ = x.shape
    depth = w_in.shape[0]
    n = b_ * t_
    for i in range(depth):
        kcvc, ks, kw, xr, yg, qT, vsT, vwT, gT = _proj_in(x, attn_norm[i], w_in[i])
        kcmp = _compress(kcvc, cmp_pos[i], cmp_w1[i], cmp_w2[i])
        oaT = _attention(qT, gT, kcmp, ks, vsT, kw, vwT)
        orec = _rglru(xr, yg, conv_w[i], conv_b[i], lru_wa[i], lru_ba[i], lru_wx[i], lru_bx[i], lru_lambda[i])
        x = _out_proj(oaT, orec, x, out_norm_attn[i], out_norm_rec[i], w_out[i])
        x2 = x.reshape(n, d_)
        j = i // 2
        if i % 2 == 0:
            x2 = _ffn(x2, ffn_norm[i], dense_w_gate[j][None].astype(BF16), dense_w_up[j][None].astype(BF16),
                      dense_w_down[j][None].astype(BF16), None, dense=True, tf=dense_w_gate.shape[2])
        else:
            x2 = _moe(x2, ffn_norm[i], router_w[j], moe_w_gate[j].astype(BF16), moe_w_up[j].astype(BF16),
                      moe_w_down[j].astype(BF16))
        x2 = _ple(x2, p[i].reshape(n, -1), ple_norm[i], ple_w_gate[i], ple_w_proj[i], final_norm,
                  final=(i == depth - 1))
        x = x2.reshape(b_, t_, d_)
    return x
```

```python
import functools

import jax
import jax.numpy as jnp
from jax import lax
from jax.experimental import pallas as pl
from jax.experimental.pallas import tpu as pltpu

F32 = jnp.float32
BF16 = jnp.bfloat16
I32 = jnp.int32

N_ATTN_HEADS = 8
HEAD_DIM = 64
N_KV = 2
HEADS_PER_KV = N_ATTN_HEADS // N_KV
D_ATTN = N_ATTN_HEADS * HEAD_DIM
KV_W = N_KV * HEAD_DIM
N_GATES = 3
L_CMP = 32
STRIDE = 16
L_SLC = 64
N_SEL = 16
N_LOCAL = 2
W_WIN = 512
CMP_HID = 2 * HEAD_DIM
CONV_W = 4
C_LRU = 8.0
N_EXPERTS = 8
RMS_EPS = 1e-6
ATTN_SCALE = HEAD_DIM ** -0.5
LOG2E = 1.4426950408889634
LOG2E_HI = 1.4453125
LOG2E_LO = LOG2E - LOG2E_HI

V7X_LANES = 128
V7X_VMEM_LIMIT_BYTES = 56 * 1024 * 1024

TQ = 128
KT_SLC = 256
KT_WIN = 128
N_WIN_TILES = W_WIN // KT_WIN + 1
V_ROWS = HEAD_DIM + 16
MASK_NEG = -1e30
SEL_NEG = -1.0e4
ROW_TILE = 512
RGLRU_CHUNK = 256
MOE_ROWS = 512
MOE_TOK_CHUNK = 512
CMB_ROWS = 256


def _cparams(semantics, vmem=None):
    return pltpu.CompilerParams(dimension_semantics=semantics, vmem_limit_bytes=vmem)


def _rms(x, g):
    ms = jnp.mean(x * x, axis=-1, keepdims=True)
    return x * lax.rsqrt(ms + RMS_EPS) * g


def _gelu_tanh(x):
    c = 0.7978845608028654
    return x * (0.5 * (1.0 + jnp.tanh(c * (x + 0.044715 * (x * x * x)))))


def _sigmoid(x):
    return 1.0 / (1.0 + jnp.exp(-x))


def _silu(x):
    return x * _sigmoid(x)


N_TOK_COLS = 2 * KV_W + 2 * KV_W + 2 * 512
N_TR_ROWS = D_ATTN + 2 * KV_W + 32


def _proj_in_kernel(x_ref, g_ref, wtok_ref, wtr_ref,
                    kcvc_ref, ks_ref, kw_ref, xr_ref, yg_ref, qT_ref, vsT_ref, vwT_ref, gT_ref):
    hn = _rms(x_ref[0], g_ref[...]).astype(BF16)
    tok = jnp.dot(hn, wtok_ref[...], preferred_element_type=F32)
    kcvc_ref[0] = tok[:, 0:256]
    ks_ref[0] = tok[:, 256:384].astype(BF16)
    kw_ref[0] = tok[:, 384:512].astype(BF16)
    xr_ref[0] = tok[:, 512:1024]
    yg_ref[0] = tok[:, 1024:1536]
    tr = lax.dot_general(wtr_ref[...], hn, (((1,), (1,)), ((), ())),
                         preferred_element_type=F32)
    qT_ref[0] = (tr[0:512] * (ATTN_SCALE * LOG2E)).astype(BF16)
    vsT_ref[0] = tr[512:640].astype(BF16)
    vwT_ref[0] = tr[640:768].astype(BF16)
    gT_ref[0] = _sigmoid(tr[768:800])


def _proj_in(x, norm_g, w_in):
    b_, t_, d_ = x.shape
    tm = min(ROW_TILE, t_)
    q, kc, vc, ks, vs, kw, vw, g, xr, yg = jnp.split(
        w_in, [512, 640, 768, 896, 1024, 1152, 1280, 1304, 1816], axis=1)
    wtok = jnp.concatenate([kc, vc, ks, kw, xr, yg], axis=1).astype(BF16)
    g4 = g.reshape(d_, N_KV, HEADS_PER_KV, N_GATES).transpose(0, 1, 3, 2)
    g4 = jnp.pad(g4.reshape(d_, N_KV, 12), ((0, 0), (0, 0), (0, 4))).reshape(d_, 32)
    wtr = jnp.concatenate([q, vs, vw, g4], axis=1).T.astype(BF16)
    nt = t_ // tm
    row = lambda shape: pl.BlockSpec((1, tm, shape), lambda b, i: (b, i, 0))
    col = lambda shape: pl.BlockSpec((1, shape, tm), lambda b, i: (b, 0, i))
    full = lambda a: pl.BlockSpec(a.shape, lambda b, i: (0,) * a.ndim)
    g2 = norm_g.reshape(1, d_)
    outs = pl.pallas_call(
        _proj_in_kernel,
        name="proj_in",
        grid=(b_, nt),
        in_specs=[row(d_), full(g2), full(wtok), full(wtr)],
        out_specs=[row(256), row(128), row(128), row(512), row(512),
                   col(512), col(128), col(128), col(32)],
        out_shape=[
            jax.ShapeDtypeStruct((b_, t_, 256), F32),
            jax.ShapeDtypeStruct((b_, t_, 128), BF16),
            jax.ShapeDtypeStruct((b_, t_, 128), BF16),
            jax.ShapeDtypeStruct((b_, t_, 512), F32),
            jax.ShapeDtypeStruct((b_, t_, 512), F32),
            jax.ShapeDtypeStruct((b_, 512, t_), BF16),
            jax.ShapeDtypeStruct((b_, 128, t_), BF16),
            jax.ShapeDtypeStruct((b_, 128, t_), BF16),
            jax.ShapeDtypeStruct((b_, 32, t_), F32),
        ],
        compiler_params=_cparams(("parallel", "parallel"), V7X_VMEM_LIMIT_BYTES),
    )(x, g2, wtok, wtr)
    return outs


def _compress_kernel(xc_ref, pa_ref, pb_ref, w1a_ref, w1b_ref, w2_ref, out_ref):
    xc = xc_ref[0].astype(BF16)
    n_chunk = xc.shape[0]
    a = jnp.dot(xc, w1a_ref[...], preferred_element_type=F32)
    bm = jnp.dot(xc, w1b_ref[...], preferred_element_type=F32)
    posc = (jnp.dot(pa_ref[...], w1a_ref[...], preferred_element_type=F32)
            + jnp.dot(pb_ref[...], w1b_ref[...], preferred_element_type=F32))[0:1]
    row = lax.broadcasted_iota(I32, bm.shape, 0)
    bm_up = jnp.where(row < n_chunk - 1, pltpu.roll(bm, n_chunk - 1, axis=0), 0.0)
    hid = _gelu_tanh(a + bm_up + posc).astype(BF16)
    out_ref[0] = jnp.dot(hid, w2_ref[...], preferred_element_type=F32)


def _compress(kcvc, cmp_pos, cmp_w1, cmp_w2):
    b_, t_, _ = kcvc.shape
    n_chunk = t_ // STRIDE
    xc = kcvc.reshape(b_, n_chunk, STRIDE * 256)
    eye = jnp.eye(2 * N_KV, dtype=F32)
    w1 = cmp_w1.reshape(2, 2, STRIDE, HEAD_DIM, CMP_HID)

    def expand(half):
        w = w1[:, half]
        full = jnp.einsum('wldo,wx,gy->lxgdwyo', w, jnp.eye(2, dtype=F32), jnp.eye(2, dtype=F32))
        return full.reshape(STRIDE * 256, 4 * CMP_HID).astype(BF16)

    del eye
    w1a, w1b = expand(0), expand(1)
    pos = cmp_pos.reshape(2, 2, STRIDE, HEAD_DIM)

    def posrow(half):
        p = jnp.broadcast_to(pos[:, half][:, None], (2, N_KV, STRIDE, HEAD_DIM))
        p = p.transpose(2, 0, 1, 3).reshape(1, STRIDE * 256)
        return jnp.pad(p, ((0, 7), (0, 0))).astype(BF16)

    pa, pb = posrow(0), posrow(1)
    w2 = jnp.einsum('whd,wx,gy->wghxyd', cmp_w2, jnp.eye(2, dtype=F32), jnp.eye(2, dtype=F32))
    w2 = w2.reshape(4 * CMP_HID, 4 * HEAD_DIM).astype(BF16)
    full = lambda a: pl.BlockSpec(a.shape, lambda b: (0,) * a.ndim)
    return pl.pallas_call(
        _compress_kernel,
        name="compress_kv",
        grid=(b_,),
        in_specs=[pl.BlockSpec((1, n_chunk, STRIDE * 256), lambda b: (b, 0, 0)),
                  full(pa), full(pb), full(w1a), full(w1b), full(w2)],
        out_specs=pl.BlockSpec((1, n_chunk, 256), lambda b: (b, 0, 0)),
        out_shape=jax.ShapeDtypeStruct((b_, n_chunk, 256), F32),
        compiler_params=_cparams(("parallel",), V7X_VMEM_LIMIT_BYTES),
    )(xc, pa, pb, w1a, w1b, w2)


def _head_lanes(rows):
    r = rows.shape[0] // HEADS_PER_KV
    return jnp.concatenate([rows[h * r:(h + 1) * r, :] for h in range(HEADS_PER_KV)], axis=1)


def _attn_kernel(qT_ref, gT_ref, kc_ref, vcT_ref, ks_ref, vsT_ref, kw_ref, vwT_ref, mt_ref, o_ref, s_scr,
                 tile_list):
    i = pl.program_id(1)
    n_cmp = kc_ref.shape[2]
    n_slc = mt_ref.shape[0]
    nl = HEADS_PER_KV * TQ
    hd_rows = HEADS_PER_KV * HEAD_DIM

    lane = lax.broadcasted_iota(I32, (1, nl), 1)
    t_loc = lane % TQ
    t_row = i * TQ + t_loc

    def extra_rows(n_rows, rows):
        ridx = lax.broadcasted_iota(I32, (16, nl), 0)
        out = jnp.zeros((16, nl), F32)
        for k, r in enumerate(rows):
            out = jnp.where(ridx == k, r, out)
        return jnp.concatenate([out.astype(BF16), jnp.zeros((n_rows - 16, nl), BF16)], axis=0)

    ok_c = (lax.broadcasted_iota(I32, (n_cmp, nl), 0) * STRIDE + (L_CMP - 1)) <= t_row
    r_idx = lax.broadcasted_iota(I32, (KT_WIN, nl), 0)
    s_idx = lax.broadcasted_iota(I32, (n_slc, TQ), 0)
    cur = (i * TQ + lax.broadcasted_iota(I32, (1, TQ), 1)) // L_SLC
    valid = s_idx <= cur
    forced = valid & ((s_idx == 0) | (s_idx > cur - N_LOCAL))
    s_idx_f = s_idx.astype(F32)
    n_full = (i * TQ) // KT_SLC
    mt = mt_ref[...]

    def pick(_, score):
        m = jnp.max(score, axis=0, keepdims=True)
        idx = jnp.min(jnp.where(score == m, s_idx_f, 1.0e9), axis=0, keepdims=True)
        return jnp.where(s_idx_f == idx, -2.0, score)

    def stage_a(g, bs, tile, slot, causal):
        k0 = pl.multiple_of(tile * KT_SLC, KT_SLC)
        s = jnp.dot(ks_ref[0, g, pl.ds(k0, KT_SLC), :], bs, preferred_element_type=F32)
        if causal:
            pos = k0 + lax.broadcasted_iota(I32, (KT_SLC, nl), 0)
            s = jnp.where(pos <= t_row, s, MASK_NEG)
        s_scr[g, slot] = s
        return jnp.max(s, axis=0, keepdims=True)

    def stage_b(g, tile, slot, m, acc, m_tile):
        k0 = pl.multiple_of(tile * KT_SLC, KT_SLC)
        m_new = jnp.maximum(m, m_tile)
        p = jnp.exp2(s_scr[g, slot] - m_new)
        acc = jnp.exp2(m - m_new) * acc + jnp.dot(vsT_ref[0, g, :, pl.ds(k0, KT_SLC)], p.astype(BF16),
                                                  preferred_element_type=F32)
        return m_new, acc

    groups = range(N_KV)

    def setup(g):
        q4 = _head_lanes(qT_ref[0, g * hd_rows:(g + 1) * hd_rows, :])
        head = g * HEADS_PER_KV + lane // TQ + 1
        slope = lax.bitcast_convert_type((127 - head) << 23, F32)
        return dict(q4=q4, slope=slope, s_hi=slope * LOG2E_HI, s_lo=slope * LOG2E_LO)

    st = [setup(g) for g in groups]

    def cmp_scores(g):
        s_hi, s_lo = st[g]["s_hi"], st[g]["s_lo"]
        bc = jnp.concatenate(
            [st[g]["q4"], extra_rows(64, [256.0 * s_hi, 256.0 * s_lo, 16.0 * s_hi, 16.0 * s_lo])], axis=0)
        return jnp.dot(kc_ref[0, g], bc, preferred_element_type=F32)

    def cmp_softmax(g, s_c):
        s_c = jnp.where(ok_c, s_c, MASK_NEG)
        m_c = jnp.max(s_c, axis=0, keepdims=True)
        e_c = jnp.exp2(s_c - m_c)
        l_c = jnp.sum(e_c, axis=0, keepdims=True)
        inv_c = jnp.where(m_c > 0.5 * MASK_NEG, 1.0 / jnp.maximum(l_c, 1e-30), 0.0)
        p_c = e_c * inv_c
        o_c = jnp.dot(vcT_ref[0, g], p_c.astype(BF16), preferred_element_type=F32)
        p_grp = (p_c[:, 0:TQ] + p_c[:, TQ:2 * TQ]) + (p_c[:, 2 * TQ:3 * TQ] + p_c[:, 3 * TQ:4 * TQ])
        p1 = p_grp.astype(BF16)
        r1 = p_grp - p1.astype(F32)
        p2 = r1.astype(BF16)
        p3 = (r1 - p2.astype(F32)).astype(BF16)
        p_slc = (jnp.dot(mt, p1, preferred_element_type=F32) + jnp.dot(mt, p2, preferred_element_type=F32)
                 + jnp.dot(mt, p3, preferred_element_type=F32))
        return o_c, p_slc

    def win_scores(g):
        slope = st[g]["slope"]
        bw = jnp.concatenate([st[g]["q4"], extra_rows(64, [st[g]["s_hi"], st[g]["s_lo"]])], axis=0)
        s_tiles, c_tiles = [], []
        for a in range(N_WIN_TILES):
            tile = i - (N_WIN_TILES - 1) + a
            k0 = pl.multiple_of(jnp.maximum(tile, 0) * KT_WIN, KT_WIN)
            s = jnp.dot(kw_ref[0, g, pl.ds(k0, KT_WIN), :], bw, preferred_element_type=F32)
            if a == 0:
                s = jnp.where(r_idx > t_loc, s, MASK_NEG)
            elif a == N_WIN_TILES - 1:
                s = jnp.where(r_idx <= t_loc, s, MASK_NEG)
            s_tiles.append(s)
            c_a = slope * (LOG2E * KT_WIN * (a - (N_WIN_TILES - 1)))
            c_tiles.append(jnp.where(tile >= 0, c_a, MASK_NEG))
        return s_tiles, c_tiles

    def win_softmax(g, s_tiles, c_tiles):
        m_w = s_tiles[0].max(axis=0, keepdims=True) + c_tiles[0]
        for s, c_a in zip(s_tiles[1:], c_tiles[1:]):
            m_w = jnp.maximum(m_w, s.max(axis=0, keepdims=True) + c_a)
        acc_w = jnp.zeros((V_ROWS, nl), F32)
        for a, (s, c_a) in enumerate(zip(s_tiles, c_tiles)):
            tile = i - (N_WIN_TILES - 1) + a
            k0 = pl.multiple_of(jnp.maximum(tile, 0) * KT_WIN, KT_WIN)
            p = jnp.exp2(s - (m_w - c_a))
            acc_w = acc_w + jnp.dot(vwT_ref[0, g, :, pl.ds(k0, KT_WIN)], p.astype(BF16),
                                    preferred_element_type=F32)
        return acc_w[0:HEAD_DIM] * (1.0 / acc_w[HEAD_DIM:HEAD_DIM + 1])

    s_cmp = [cmp_scores(g) for g in groups]
    s_win = [win_scores(g) for g in groups]
    cmp_out, scores = [], []
    for g in groups:
        cmp_out.append(cmp_softmax(g, s_cmp[g]))
        score0 = jnp.where(forced, -2.0, jnp.where(valid, cmp_out[g][1], -1.0))
        scores.append(lax.fori_loop(0, N_SEL - (N_LOCAL + 1), pick, score0, unroll=True))
    o_win = [win_softmax(g, *s_win[g]) for g in groups]
    score = jnp.concatenate(scores, axis=1)

    def select(g):
        sel = valid & (score[:, g * TQ:(g + 1) * TQ] < -1.5)
        sel4 = jnp.concatenate([sel] * HEADS_PER_KV, axis=1)
        blk_rows = jnp.where(sel4, 0.0, SEL_NEG).astype(BF16)
        s_hi, s_lo = st[g]["s_hi"], st[g]["s_lo"]
        bs = jnp.concatenate(
            [st[g]["q4"], blk_rows, extra_rows(64, [L_SLC * s_hi, L_SLC * s_lo, s_hi, s_lo])], axis=0)
        m_tile0 = stage_a(g, bs, n_full, 0, True)
        gt = gT_ref[0, g * 16:(g + 1) * 16, :]
        gates = [_head_lanes(gt[k * HEADS_PER_KV:(k + 1) * HEADS_PER_KV, :]) for k in range(N_GATES)]
        return dict(bs=bs, base=gates[0] * cmp_out[g][0] + gates[2] * o_win[g], g_s=gates[1], m_tile0=m_tile0)

    grp = [select(g) for g in groups]

    blocks_per_tile = KT_SLC // L_SLC
    n_tiles = n_slc // blocks_per_tile
    assert n_tiles <= 32
    valid2 = jnp.concatenate([valid] * N_KV, axis=1)
    sel_all = jnp.where(valid2 & (score < -1.5), 1.0, 0.0).astype(BF16)
    blk_cnt = lax.dot_general(jnp.ones((8, N_KV * TQ), BF16), sel_all, (((1,), (1,)), ((), ())),
                              preferred_element_type=F32)
    memb = jnp.where(lax.broadcasted_iota(I32, (n_slc, V7X_LANES), 0) // blocks_per_tile
                     == lax.broadcasted_iota(I32, (n_slc, V7X_LANES), 1), 1.0, 0.0).astype(BF16)
    tile_cnt = jnp.dot(jnp.where(blk_cnt > 0.5, 1.0, 0.0).astype(BF16), memb,
                       preferred_element_type=F32)[0:1]
    lane_t = lax.broadcasted_iota(I32, (1, V7X_LANES), 1)
    need = (tile_cnt > 0.5) & (lane_t < n_full)
    bit = lax.bitcast_convert_type(((lane_t & 15) + 127) << 23, F32)
    lo_bits = jnp.sum(jnp.where(need & (lane_t < 16), bit, 0.0)).astype(I32)
    hi_bits = jnp.sum(jnp.where(need & (lane_t >= 16), bit, 0.0)).astype(I32)
    n_sel = jnp.int32(0)
    for t in range(n_tiles - 1):
        tile_list[n_sel] = t
        n_sel = n_sel + (((lo_bits if t < 16 else hi_bits) >> (t % 16)) & 1)

    order = lambda k: jnp.where(k == 0, n_full, tile_list[jnp.maximum(k - 1, 0)])

    def half(carry, tile_b, slot_b, tile_a):
        m_next = [c[2] if tile_a is None else stage_a(g, grp[g]["bs"], tile_a, 1 - slot_b, False)
                  for g, c in enumerate(carry)]
        out = []
        for g in range(N_KV):
            m, acc, m_tile = carry[g]
            out.append(stage_b(g, tile_b, slot_b, m, acc, m_tile) + (m_next[g],))
        return tuple(out)

    def body(p, carry):
        k = 2 * p
        return half(half(carry, order(k), 0, order(k + 1)), order(k + 1), 1, order(k + 2))

    init = tuple((jnp.full((1, nl), MASK_NEG, F32), jnp.zeros((V_ROWS, nl), F32), grp[g]["m_tile0"])
                 for g in range(N_KV))
    carry = lax.fori_loop(0, n_sel // 2, body, init)
    k_tail = 2 * (n_sel // 2)
    carry = lax.cond(n_sel % 2 == 1,
                     lambda c: half(half(c, order(k_tail), 0, order(k_tail + 1)), order(k_tail + 1), 1, None),
                     lambda c: half(c, order(k_tail), 0, None), carry)
    for g in range(N_KV):
        _, acc, _ = carry[g]
        o_s = acc[0:HEAD_DIM] * (1.0 / acc[HEAD_DIM:HEAD_DIM + 1])
        out = grp[g]["base"] + grp[g]["g_s"] * o_s
        for h in range(HEADS_PER_KV):
            r0 = g * hd_rows + h * HEAD_DIM
            o_ref[0, r0:r0 + HEAD_DIM, :] = out[:, h * TQ:(h + 1) * TQ]


def _attention(qT, gT, kcmp, ks, vsT, kw, vwT):
    b_, _, t_ = qT.shape
    n_cmp = t_ // STRIDE
    n_slc = t_ // L_SLC
    pos = jnp.arange(t_, dtype=I32)
    grp = lambda a: a.reshape(b_, t_, N_KV, HEAD_DIM).transpose(0, 2, 1, 3)
    def pos_cols(vals, width):
        cols = [v.astype(BF16)[:, None] for v in vals for _ in range(2)]
        return jnp.concatenate(cols + [jnp.zeros((vals[0].shape[0], width - len(cols)), BF16)], axis=1)

    slc_cols = jnp.concatenate(
        [jax.nn.one_hot(pos // L_SLC, n_slc, dtype=BF16), pos_cols([pos // L_SLC, pos % L_SLC], 64)],
        axis=1)
    ks_aug = jnp.concatenate(
        [grp(ks), jnp.broadcast_to(slc_cols, (b_, N_KV) + slc_cols.shape)], axis=-1)
    win_cols = pos_cols([pos % KT_WIN], 64)
    kw_aug = jnp.concatenate(
        [grp(kw), jnp.broadcast_to(win_cols, (b_, N_KV) + win_cols.shape)], axis=-1)
    c = jnp.arange(n_cmp, dtype=I32)
    cmp_cols = pos_cols([c // 16, c % 16], 64)
    kc4 = kcmp[..., 0:KV_W].reshape(b_, n_cmp, N_KV, HEAD_DIM).transpose(0, 2, 1, 3).astype(BF16)
    kc_aug = jnp.concatenate(
        [kc4, jnp.broadcast_to(cmp_cols, (b_, N_KV) + cmp_cols.shape)], axis=-1)
    vcT = kcmp[..., KV_W:2 * KV_W].reshape(b_, n_cmp, N_KV, HEAD_DIM).transpose(0, 2, 3, 1).astype(BF16)
    ones_rows = jnp.ones((b_, N_KV, V_ROWS - HEAD_DIM, t_), BF16)
    vsT4 = jnp.concatenate([vsT.reshape(b_, N_KV, HEAD_DIM, t_), ones_rows], axis=2)
    vwT4 = jnp.concatenate([vwT.reshape(b_, N_KV, HEAD_DIM, t_), ones_rows], axis=2)
    s = jnp.arange(n_slc, dtype=I32)[:, None]
    cc = c[None, :]
    r_slc = L_SLC // STRIDE
    mt = (((cc >= r_slc * s) & (cc < r_slc * s + r_slc)).astype(F32)
          + ((cc + 1 >= r_slc * s) & (cc + 1 < r_slc * s + r_slc)).astype(F32))
    mt = jnp.where(cc < n_cmp - 1, mt, 0.0).astype(BF16)
    kv_spec = lambda a: pl.BlockSpec((1,) + a.shape[1:], lambda b, i: (b, 0, 0, 0))
    return pl.pallas_call(
        _attn_kernel,
        name="nsa_attention",
        grid=(b_, t_ // TQ),
        in_specs=[pl.BlockSpec((1, D_ATTN, TQ), lambda b, i: (b, 0, i)),
                  pl.BlockSpec((1, 32, TQ), lambda b, i: (b, 0, i)),
                  kv_spec(kc_aug), kv_spec(vcT), kv_spec(ks_aug), kv_spec(vsT4),
                  kv_spec(kw_aug), kv_spec(vwT4),
                  pl.BlockSpec(mt.shape, lambda b, i: (0, 0))],
        out_specs=pl.BlockSpec((1, D_ATTN, TQ), lambda b, i: (b, 0, i)),
        out_shape=jax.ShapeDtypeStruct((b_, D_ATTN, t_), F32),
        scratch_shapes=[pltpu.VMEM((N_KV, 2, KT_SLC, HEADS_PER_KV * TQ), F32), pltpu.SMEM((32,), I32)],
        compiler_params=_cparams(("parallel", "arbitrary"), V7X_VMEM_LIMIT_BYTES),
    )(qT, gT, kc_aug, vcT, ks_aug, vsT4, kw_aug, vwT4, mt)


def _rglru_seg_pitch(tc):
    return tc // 8 + 8


def _rglru_kernel(xr_ref, yg_ref, cw_ref, cb_ref, wa_ref, ba_ref, wx_ref, bx_ref, lam_ref,
                  o_ref, tail_ref, h_ref, xs_scr, ys_scr, os_scr):
    tc = xr_ref.shape[1]

    @pl.when(pl.program_id(1) == 0)
    def _():
        tail_ref[...] = jnp.zeros_like(tail_ref)
        h_ref[...] = jnp.zeros_like(h_ref)

    n_v = tc // 8
    n_slab = xr_ref.shape[2] // V7X_LANES
    pitch = xs_scr.shape[1] // 8
    for c in range(n_slab):
        for s in range(8):
            lanes = slice(c * V7X_LANES, (c + 1) * V7X_LANES)
            xs_scr[c, s * pitch:s * pitch + n_v, :] = xr_ref[0, s * n_v:(s + 1) * n_v, lanes]
            ys_scr[c, s * pitch:s * pitch + n_v, :] = yg_ref[0, s * n_v:(s + 1) * n_v, lanes]

    def piece(scr, v):
        return jnp.concatenate([scr[c, pl.ds(v, 8, stride=pitch), :] for c in range(n_slab)], axis=1)

    x = [piece(xs_scr, v) for v in range(n_v)]
    row8 = lax.broadcasted_iota(I32, (8, x[0].shape[1]), 0)
    tail = tail_ref[...]
    before = {d: jnp.where(row8 == 0, tail[CONV_W - 1 - d:CONV_W - d, :], pltpu.roll(x[n_v - d], 1, axis=0))
              for d in range(1, CONV_W)}
    tail_ref[0:CONV_W - 1, :] = jnp.concatenate([x[n_v - d][7:8, :] for d in range(CONV_W - 1, 0, -1)], axis=0)
    cw = cw_ref[...]
    cb = cb_ref[...]
    xc = []
    for v in range(n_v):
        acc = x[v] * cw[CONV_W - 1:CONV_W, :] + cb
        for d in range(1, CONV_W):
            acc = acc + (x[v - d] if v >= d else before[d - v]) * cw[CONV_W - 1 - d:CONV_W - d, :]
        xc.append(acc)

    xb = jnp.concatenate(xc, axis=0).astype(BF16)
    r = _sigmoid(jnp.dot(xb, wa_ref[...], preferred_element_type=F32) + ba_ref[...])
    gi = _sigmoid(jnp.dot(xb, wx_ref[...], preferred_element_type=F32) + bx_ref[...])
    z = -lam_ref[...]
    softplus = jnp.maximum(z, 0.0) + jnp.log1p(jnp.exp(-jnp.abs(z)))
    log_a = (-C_LRU * r) * softplus
    a = jnp.exp(log_a)
    th = jnp.tanh(log_a)
    neg_expm1 = 2.0 * th / (th - 1.0)
    bb = jnp.sqrt(neg_expm1) * gi * jnp.concatenate(xc, axis=0)

    a_cum, b_cum = [a[0:8]], [bb[0:8]]
    for v in range(1, n_v):
        a_v = a[v * 8:(v + 1) * 8]
        b_cum.append(a_v * b_cum[-1] + bb[v * 8:(v + 1) * 8])
        a_cum.append(a_v * a_cum[-1])
    h_in = [h_ref[0:1, :]]
    for s in range(7):
        h_in.append(a_cum[-1][s:s + 1, :] * h_in[-1] + b_cum[-1][s:s + 1, :])
    h_ref[0:1, :] = a_cum[-1][7:8, :] * h_in[-1] + b_cum[-1][7:8, :]
    h_in = jnp.concatenate(h_in, axis=0)
    for v in range(n_v):
        out = (a_cum[v] * h_in + b_cum[v]) * _gelu_tanh(piece(ys_scr, v))
        for c in range(n_slab):
            os_scr[c, pl.ds(v, 8, stride=pitch), :] = out[:, c * V7X_LANES:(c + 1) * V7X_LANES]
    for s in range(8):
        o_ref[0, s * n_v:(s + 1) * n_v, :] = jnp.concatenate(
            [os_scr[c, s * pitch:s * pitch + n_v, :] for c in range(n_slab)], axis=1)


def _block_diag(w):
    n, c, d = w.shape
    return jnp.einsum('ncd,nm->ncmd', w, jnp.eye(n, dtype=w.dtype)).reshape(n * c, n * d)


def _rglru(xr, yg, conv_w, conv_b, wa, ba, wx, bx, lam):
    b_, t_, c_ = xr.shape
    tc = min(RGLRU_CHUNK, t_)
    wa_bd = _block_diag(wa).astype(BF16)
    wx_bd = _block_diag(wx).astype(BF16)
    vec = lambda v: v.reshape(1, c_)
    full = lambda a: pl.BlockSpec(a.shape, lambda b, i: (0,) * a.ndim)
    blk = pl.BlockSpec((1, tc, c_), lambda b, i: (b, i, 0))
    args = (xr, yg, conv_w, vec(conv_b), wa_bd, vec(ba), wx_bd, vec(bx), vec(lam))
    return pl.pallas_call(
        _rglru_kernel,
        name="rglru",
        grid=(b_, t_ // tc),
        in_specs=[blk, blk] + [full(a) for a in args[2:]],
        out_specs=blk,
        out_shape=jax.ShapeDtypeStruct((b_, t_, c_), F32),
        scratch_shapes=[pltpu.VMEM((8, c_), F32), pltpu.VMEM((8, c_), F32)]
                       + [pltpu.VMEM((c_ // V7X_LANES, 8 * _rglru_seg_pitch(tc), V7X_LANES), F32)] * 3,
        compiler_params=_cparams(("parallel", "arbitrary"), V7X_VMEM_LIMIT_BYTES),
    )(*args)


def _out_proj_kernel(oaT_ref, orec_ref, x_ref, ga_ref, gr_ref, wa_ref, wr_ref, o_ref):
    oaT = oaT_ref[0]
    ms = jnp.mean(oaT * oaT, axis=0, keepdims=True)
    na = (oaT * lax.rsqrt(ms + RMS_EPS)).T * ga_ref[...]
    nr = _rms(orec_ref[0], gr_ref[...])
    y = (jnp.dot(na.astype(BF16), wa_ref[...], preferred_element_type=F32)
         + jnp.dot(nr.astype(BF16), wr_ref[...], preferred_element_type=F32))
    o_ref[0] = x_ref[0] + y


def _out_proj(oaT, orec, x, g_attn, g_rec, w_out):
    b_, t_, d_ = x.shape
    tm = min(256, t_)
    wa = w_out[:D_ATTN].astype(BF16)
    wr = w_out[D_ATTN:].astype(BF16)
    ga = g_attn.reshape(1, -1)
    gr = g_rec.reshape(1, -1)
    full = lambda a: pl.BlockSpec(a.shape, lambda b, i: (0,) * a.ndim)
    return pl.pallas_call(
        _out_proj_kernel,
        name="out_proj",
        grid=(b_, t_ // tm),
        in_specs=[pl.BlockSpec((1, D_ATTN, tm), lambda b, i: (b, 0, i)),
                  pl.BlockSpec((1, tm, orec.shape[2]), lambda b, i: (b, i, 0)),
                  pl.BlockSpec((1, tm, d_), lambda b, i: (b, i, 0)),
                  full(ga), full(gr), full(wa), full(wr)],
        out_specs=pl.BlockSpec((1, tm, d_), lambda b, i: (b, i, 0)),
        out_shape=jax.ShapeDtypeStruct((b_, t_, d_), F32),
        compiler_params=_cparams(("parallel", "parallel"), V7X_VMEM_LIMIT_BYTES),
    )(oaT, orec, x, ga, gr, wa, wr)


def _ple_apply(x, p_ref, g_ref, wg_ref, wp_ref, fg_ref, *, final):
    gate = _sigmoid(jnp.dot(_rms(x, g_ref[...]).astype(BF16), wg_ref[...], preferred_element_type=F32))
    proj = jnp.dot(p_ref[...].astype(BF16), wp_ref[...], preferred_element_type=F32)
    y = x + gate * proj
    return _rms(y, fg_ref[...]) if final else y


def _ple_operands(p, norm_g, w_gate, w_proj, final_g, rows, row_map, const_map):
    d_ = w_gate.shape[0]
    arrays = (p, norm_g.reshape(1, d_), w_gate.astype(BF16), w_proj.astype(BF16), final_g.reshape(1, d_))
    once = dict(pipeline_mode=pl.Buffered(1))
    specs = [pl.BlockSpec((rows, p.shape[1]), row_map)] + [pl.BlockSpec(a.shape, const_map, **once)
                                                            for a in arrays[1:]]
    return arrays, specs


def _ffn_kernel(be_ref, x_ref, g_ref, wg_ref, wu_ref, wd_ref, *rest, dense, final):
    ple_refs, (o_ref, hn_ref, acc_ref) = rest[:-3], rest[-3:]
    f = pl.program_id(1)
    used = True if dense else pl.program_id(0) < be_ref[pl.num_programs(0)]

    @pl.when(f == 0)
    def _():
        if dense:
            x = x_ref[...]
            hn_ref[...] = _rms(x, g_ref[...]).astype(BF16)
            acc_ref[...] = x
        else:
            hn_ref[...] = _unpack_bf16_pairs(x_ref[...])
            acc_ref[...] = jnp.zeros_like(acc_ref)

    def accumulate():
        hn = hn_ref[...]
        gate = jnp.dot(hn, wg_ref[0], preferred_element_type=F32)
        up = jnp.dot(hn, wu_ref[0], preferred_element_type=F32)
        act = (_silu(gate) * up).astype(BF16)
        acc_ref[...] += jnp.dot(act, wd_ref[0], preferred_element_type=F32)

    if dense:
        accumulate()
    else:
        pl.when(used)(accumulate)

    @pl.when(f == pl.num_programs(1) - 1)
    def _():
        if dense:
            o_ref[...] = _ple_apply(acc_ref[...], *ple_refs, final=final)
        else:
            o_ref[...] = acc_ref[...].astype(o_ref.dtype)


def _ffn(x, norm_g, wg, wu, wd, block_e, n_used=None, *, dense, tf, ple=None, final=False):
    n = x.shape[0]
    d_ = wg.shape[1]
    tm = min(ROW_TILE if dense else MOE_ROWS, n)
    ff = wg.shape[2]
    assert ff % tf == 0 and n % tm == 0
    if block_e is None:
        block_e = jnp.zeros((n // tm,), I32)
    if n_used is None:
        n_used = jnp.int32(n // tm)
    n_blk, nf = n // tm, ff // tf
    be_all = jnp.concatenate([block_e, jnp.reshape(n_used, (1,)).astype(I32)])
    wmode = dict(pipeline_mode=pl.Buffered(1)) if (wg.shape[0] == 1 and ff == tf) else {}
    ftile = lambda i, f, be: jnp.where(i < be[n_blk], f, nf - 1)
    ple_arrays, ple_specs = ((), [])
    if dense:
        ple_arrays, ple_specs = _ple_operands(*ple, tm, lambda i, f, be: (i, 0), lambda i, f, be: (0, 0))
    grid_spec = pltpu.PrefetchScalarGridSpec(
        num_scalar_prefetch=1,
        grid=(n_blk, nf),
        in_specs=[pl.BlockSpec((tm, x.shape[1]), lambda i, f, be: (i, 0)),
                  pl.BlockSpec((1, d_), lambda i, f, be: (0, 0)),
                  pl.BlockSpec((1, d_, tf), lambda i, f, be: (be[i], 0, ftile(i, f, be)), **wmode),
                  pl.BlockSpec((1, d_, tf), lambda i, f, be: (be[i], 0, ftile(i, f, be)), **wmode),
                  pl.BlockSpec((1, tf, d_), lambda i, f, be: (be[i], ftile(i, f, be), 0), **wmode)] + ple_specs,
        out_specs=pl.BlockSpec((tm, d_), lambda i, f, be: (i, 0)),
        scratch_shapes=[pltpu.VMEM((tm, d_), BF16), pltpu.VMEM((tm, d_), F32)],
    )
    return pl.pallas_call(
        functools.partial(_ffn_kernel, dense=dense, final=final),
        name="ffn_dense" if dense else "ffn_expert",
        grid_spec=grid_spec,
        out_shape=jax.ShapeDtypeStruct((n, d_), F32 if dense else BF16),
        compiler_params=_cparams(("parallel", "arbitrary"), V7X_VMEM_LIMIT_BYTES),
    )(be_all, x, norm_g.reshape(1, d_), wg, wu, wd, *ple_arrays)


def _pack_bf16_pairs(x):
    w = x.shape[1] // 2
    bits = lax.bitcast_convert_type(x.astype(BF16).astype(F32), jnp.uint32)
    return (bits[:, :w] >> 16) | bits[:, w:]


def _unpack_bf16_pairs(words):
    lo = lax.bitcast_convert_type(words << 16, F32)
    hi = lax.bitcast_convert_type(words & jnp.uint32(0xFFFF0000), F32)
    return jnp.concatenate([lo, hi], axis=1).astype(BF16)


def _router_kernel(x_ref, g_ref, wr_ref, tri_ref, h_ref, e_ref, p_ref, r_ref, cb_ref, tot_ref, cnt_ref):
    @pl.when(pl.program_id(0) == 0)
    def _():
        cnt_ref[...] = jnp.zeros_like(cnt_ref)

    hn = _rms(x_ref[...], g_ref[...])
    h_ref[...] = _pack_bf16_pairs(hn)
    logits = lax.dot_general(wr_ref[...], hn, (((1,), (1,)), ((), ())),
                             precision=lax.Precision.HIGHEST, preferred_element_type=F32)
    eidx = lax.broadcasted_iota(I32, logits.shape, 0)
    m1 = jnp.max(logits, axis=0, keepdims=True)
    i1 = jnp.min(jnp.where(logits == m1, eidx, N_EXPERTS), axis=0, keepdims=True)
    rest = jnp.where(eidx == i1, -jnp.inf, logits)
    m2 = jnp.max(rest, axis=0, keepdims=True)
    i2 = jnp.min(jnp.where(rest == m2, eidx, N_EXPERTS), axis=0, keepdims=True)
    e2 = jnp.exp(m2 - m1)
    inv = 1.0 / (1.0 + e2)
    e_ref[...] = jnp.concatenate([i1, i2], axis=0)
    p_ref[...] = jnp.concatenate([inv, e2 * inv], axis=0)

    tm = logits.shape[1]
    chosen = jnp.where((eidx == i1) | (eidx == i2), 1.0, 0.0)
    incl = jnp.dot(chosen.astype(BF16), tri_ref[...], preferred_element_type=F32)
    before = incl - chosen + cnt_ref[:, 0:1]
    r_ref[...] = jnp.concatenate(
        [jnp.sum(jnp.where(eidx == i1, before, 0.0), axis=0, keepdims=True),
         jnp.sum(jnp.where(eidx == i2, before, 0.0), axis=0, keepdims=True)], axis=0).astype(I32)
    lane = lax.broadcasted_iota(I32, cb_ref.shape, 1)
    cb = jnp.zeros(cb_ref.shape, F32)
    for c in range(tm // CMB_ROWS):
        cb = jnp.where(lane == c, before[:, c * CMB_ROWS:c * CMB_ROWS + 1], cb)
    cb_ref[...] = cb.astype(I32)
    total = cnt_ref[...] + incl[:, tm - 1:tm]
    cnt_ref[...] = total
    tot_ref[...] = total.astype(I32)


def _router(x, norm_g, router_w):
    n, d_ = x.shape
    tm = min(ROW_TILE, n)
    assert tm % CMB_ROWS == 0
    wr = router_w.T
    tri = (jnp.arange(tm)[:, None] <= jnp.arange(tm)[None, :]).astype(BF16)
    h, top_e, top_p, rank, cb, tot = pl.pallas_call(
        _router_kernel,
        name="router",
        grid=(n // tm,),
        in_specs=[pl.BlockSpec((tm, d_), lambda i: (i, 0)),
                  pl.BlockSpec((1, d_), lambda i: (0, 0)),
                  pl.BlockSpec(wr.shape, lambda i: (0, 0)),
                  pl.BlockSpec(tri.shape, lambda i: (0, 0))],
        out_specs=[pl.BlockSpec((tm, d_ // 2), lambda i: (i, 0)),
                   pl.BlockSpec((2, tm), lambda i: (0, i)),
                   pl.BlockSpec((2, tm), lambda i: (0, i)),
                   pl.BlockSpec((2, tm), lambda i: (0, i)),
                   pl.BlockSpec((N_EXPERTS, V7X_LANES), lambda i: (0, i)),
                   pl.BlockSpec((N_EXPERTS, V7X_LANES), lambda i: (0, 0))],
        out_shape=[jax.ShapeDtypeStruct((n, d_ // 2), jnp.uint32),
                   jax.ShapeDtypeStruct((2, n), I32),
                   jax.ShapeDtypeStruct((2, n), F32),
                   jax.ShapeDtypeStruct((2, n), I32),
                   jax.ShapeDtypeStruct((N_EXPERTS, n // tm * V7X_LANES), I32),
                   jax.ShapeDtypeStruct((N_EXPERTS, V7X_LANES), I32)],
        scratch_shapes=[pltpu.VMEM((N_EXPERTS, V7X_LANES), F32)],
        compiler_params=_cparams(("arbitrary",), V7X_VMEM_LIMIT_BYTES),
    )(x, norm_g.reshape(1, d_), wr, tri)
    cb = cb.reshape(N_EXPERTS, n // tm, V7X_LANES)[:, :, :tm // CMB_ROWS].reshape(N_EXPERTS, n // CMB_ROWS).T
    return h, top_e, top_p, rank, cb, tot[:, 0]


SCATTER_UNROLL = 8


def _scatter_kernel(dest_ref, h_ref, zeros_ref, xs_ref, sem):
    del zeros_ref
    n_tok = h_ref.shape[0]

    def row_copy(t, k):
        return pltpu.make_async_copy(h_ref.at[pl.ds(t, 1), :], xs_ref.at[pl.ds(dest_ref[k, t], 1), :], sem)

    def issue(u, carry):
        for r in range(SCATTER_UNROLL):
            t = u * SCATTER_UNROLL + r
            row_copy(t, 0).start()
            row_copy(t, 1).start()
        return carry

    lax.fori_loop(0, n_tok // SCATTER_UNROLL, issue, 0)
    for _ in range(2):
        pltpu.make_async_copy(h_ref, xs_ref.at[pl.ds(0, n_tok), :], sem).wait()


def _scatter(h, dest, n_slots):
    n, w = h.shape
    tc = min(MOE_TOK_CHUNK, n)
    zeros = jnp.zeros((n_slots, w), h.dtype)
    return pl.pallas_call(
        _scatter_kernel,
        name="moe_scatter",
        grid=(n // tc,),
        in_specs=[pl.BlockSpec((2, tc), lambda c: (0, c), memory_space=pltpu.SMEM),
                  pl.BlockSpec((tc, w), lambda c: (c, 0)),
                  pl.BlockSpec(memory_space=pl.ANY)],
        out_specs=pl.BlockSpec(memory_space=pl.ANY),
        out_shape=jax.ShapeDtypeStruct((n_slots, w), h.dtype),
        scratch_shapes=[pltpu.SemaphoreType.DMA(())],
        input_output_aliases={2: 0},
        compiler_params=_cparams(("arbitrary",), V7X_VMEM_LIMIT_BYTES),
    )(dest, h, zeros)


N_CMB_OPS = 2 * N_EXPERTS


def _combine_kernel(jt_ref, vt_ref, x_ref, dest_ref, gate_ref, *rest, final):
    yb_refs = rest[:N_CMB_OPS]
    ple_refs = rest[N_CMB_OPS:-1]
    o_ref = rest[-1]
    c = pl.program_id(0)
    dest = dest_ref[...]
    gate = gate_ref[...]
    lane = lax.broadcasted_iota(I32, (dest.shape[0], CMB_ROWS), 1)
    acc = x_ref[...]
    for k in range(N_CMB_OPS):
        base = jnp.where(vt_ref[c * N_CMB_OPS + k] == 1, jt_ref[c * N_CMB_OPS + k] * CMB_ROWS, -2 * CMB_ROWS)
        rel = dest - base
        w = (jnp.where(lane == rel[:, 0:1], gate[:, 0:1], 0.0)
             + jnp.where(lane == rel[:, 1:2], gate[:, 1:2], 0.0)).astype(BF16)
        acc = acc + jnp.dot(w, yb_refs[k][...], preferred_element_type=F32)
    o_ref[...] = _ple_apply(acc, *ple_refs, final=final)


def _combine(x, yb, dest_t, gate_t, jt, vt, ple, final):
    n, d_ = x.shape
    ple_arrays, ple_specs = _ple_operands(*ple, CMB_ROWS, lambda c, jt, vt: (c, 0), lambda c, jt, vt: (0, 0))
    yb_spec = lambda k: pl.BlockSpec((CMB_ROWS, d_), lambda c, jt, vt, k=k: (jt[c * N_CMB_OPS + k], 0))
    grid_spec = pltpu.PrefetchScalarGridSpec(
        num_scalar_prefetch=2,
        grid=(n // CMB_ROWS,),
        in_specs=[pl.BlockSpec((CMB_ROWS, d_), lambda c, jt, vt: (c, 0)),
                  pl.BlockSpec((CMB_ROWS, 2), lambda c, jt, vt: (c, 0)),
                  pl.BlockSpec((CMB_ROWS, 2), lambda c, jt, vt: (c, 0))]
                 + [yb_spec(k) for k in range(N_CMB_OPS)] + ple_specs,
        out_specs=pl.BlockSpec((CMB_ROWS, d_), lambda c, jt, vt: (c, 0)),
    )
    return pl.pallas_call(
        functools.partial(_combine_kernel, final=final),
        name="moe_combine",
        grid_spec=grid_spec,
        out_shape=jax.ShapeDtypeStruct((n, d_), F32),
        compiler_params=_cparams(("parallel",), V7X_VMEM_LIMIT_BYTES),
    )(jt, vt, x, dest_t, gate_t, *([yb] * N_CMB_OPS), *ple_arrays)


def _moe_plan(top_e, rank, cb, counts, n_slots):
    padded = (counts + MOE_ROWS - 1) // MOE_ROWS * MOE_ROWS
    pad_ends = jnp.cumsum(padded)
    pad_starts = pad_ends - padded
    start_of = sum(jnp.where(top_e == e, pad_starts[e], 0) for e in range(N_EXPERTS))
    dest = (start_of + rank).astype(I32)
    n_blk = n_slots // MOE_ROWS
    blk_start = jnp.arange(n_blk, dtype=I32) * MOE_ROWS
    block_e = jnp.minimum(jnp.sum((pad_ends[None, :] <= blk_start[:, None]).astype(I32), axis=1),
                          N_EXPERTS - 1).astype(I32)

    cb = jnp.concatenate([cb, counts[None]], axis=0)
    lo = pad_starts[None] + cb[:-1]
    hi = pad_starts[None] + cb[1:]
    ja = lo // CMB_ROWS
    jb = (hi - 1) // CMB_ROWS
    va = hi > lo
    vb = va & (jb > ja)
    jt = jnp.stack([ja, jb], axis=-1).reshape(-1)
    vt = jnp.stack([va, vb], axis=-1).reshape(-1)
    jt = jnp.where(vt, jt, 0).astype(I32)
    return dest, block_e, pad_ends[-1] // MOE_ROWS, jt, vt.astype(I32)


def _moe(x, norm_g, router_w, wg, wu, wd, ple, final):
    n, d_ = x.shape
    n_slots = (2 * n // MOE_ROWS + N_EXPERTS) * MOE_ROWS
    h, top_e, top_p, rank, cb, counts = _router(x, norm_g, router_w)
    dest, block_e, n_used, jt, vt = _moe_plan(top_e, rank, cb, counts, n_slots)
    xs = _scatter(h, dest, n_slots)
    yb = _ffn(xs, norm_g, wg, wu, wd, block_e, n_used, dense=False, tf=wg.shape[2] // 2)
    return _combine(x, yb, dest.T, top_p.T, jt, vt, ple, final)


def kernel(x, p, attn_norm, w_in, cmp_pos, cmp_w1, cmp_w2, conv_w, conv_b, lru_wa, lru_ba, lru_wx, lru_bx,
           lru_lambda, out_norm_attn, out_norm_rec, w_out, ffn_norm, dense_w_gate, dense_w_up, dense_w_down,
           router_w, moe_w_gate, moe_w_up, moe_w_down, ple_norm, ple_w_gate, ple_w_proj, final_norm):
    b_, t_, d_ = x.shape
    depth = w_in.shape[0]
    n = b_ * t_
    for i in range(depth):
        kcvc, ks, kw, xr, yg, qT, vsT, vwT, gT = _proj_in(x, attn_norm[i], w_in[i])
        kcmp = _compress(kcvc, cmp_pos[i], cmp_w1[i], cmp_w2[i])
        oaT = _attention(qT, gT, kcmp, ks, vsT, kw, vwT)
        orec = _rglru(xr, yg, conv_w[i], conv_b[i], lru_wa[i], lru_ba[i], lru_wx[i], lru_bx[i], lru_lambda[i])
        x = _out_proj(oaT, orec, x, out_norm_attn[i], out_norm_rec[i], w_out[i])
        x2 = x.reshape(n, d_)
        j = i // 2
        ple = (p[i].reshape(n, -1), ple_norm[i], ple_w_gate[i], ple_w_proj[i], final_norm)
        final = i == depth - 1
        if i % 2 == 0:
            x2 = _ffn(x2, ffn_norm[i], dense_w_gate[j][None].astype(BF16), dense_w_up[j][None].astype(BF16),
                      dense_w_down[j][None].astype(BF16), None, dense=True, tf=dense_w_gate.shape[2],
                      ple=ple, final=final)
        else:
            x2 = _moe(x2, ffn_norm[i], router_w[j], moe_w_gate[j].astype(BF16), moe_w_up[j].astype(BF16),
                      moe_w_down[j].astype(BF16), ple, final)
        x = x2.reshape(b_, t_, d_)
    return x
```

```python
import functools

import jax
import jax.numpy as jnp
from jax import lax
from jax.experimental import pallas as pl
from jax.experimental.pallas import tpu as pltpu

F32 = jnp.float32
BF16 = jnp.bfloat16
I32 = jnp.int32

N_ATTN_HEADS = 8
HEAD_DIM = 64
N_KV = 2
HEADS_PER_KV = N_ATTN_HEADS // N_KV
D_ATTN = N_ATTN_HEADS * HEAD_DIM
KV_W = N_KV * HEAD_DIM
N_GATES = 3
L_CMP = 32
STRIDE = 16
L_SLC = 64
N_SEL = 16
N_LOCAL = 2
W_WIN = 512
CMP_HID = 2 * HEAD_DIM
CONV_W = 4
C_LRU = 8.0
N_EXPERTS = 8
RMS_EPS = 1e-6
ATTN_SCALE = HEAD_DIM ** -0.5
LOG2E = 1.4426950408889634
LOG2E_HI = 1.4453125
LOG2E_LO = LOG2E - LOG2E_HI

V7X_LANES = 128
V7X_VMEM_LIMIT_BYTES = 56 * 1024 * 1024

TQ = 128
KT_SLC = 256
KT_WIN = 128
N_WIN_TILES = W_WIN // KT_WIN + 1
V_ROWS = HEAD_DIM + 16
MASK_NEG = -1e30
SEL_NEG = -1.0e4
ROW_TILE = 512
RGLRU_CHUNK = 256
MOE_ROWS = 512
MOE_TOK_CHUNK = 512
CMB_ROWS = 256


def _cparams(semantics, vmem=None):
    return pltpu.CompilerParams(dimension_semantics=semantics, vmem_limit_bytes=vmem)


def _rms(x, g):
    ms = jnp.mean(x * x, axis=-1, keepdims=True)
    return x * lax.rsqrt(ms + RMS_EPS) * g


def _gelu_tanh(x):
    c = 0.7978845608028654
    return x * (0.5 * (1.0 + jnp.tanh(c * (x + 0.044715 * (x * x * x)))))


def _sigmoid(x):
    return 1.0 / (1.0 + jnp.exp(-x))


def _silu(x):
    return x * _sigmoid(x)


N_TOK_COLS = 2 * KV_W + 2 * KV_W + 2 * 512
N_TR_ROWS = D_ATTN + 2 * KV_W + 32


def _proj_in_kernel(x_ref, g_ref, wtok_ref, wtr_ref,
                    slc_ref, win_ref, kcvc_ref, ks_ref, kw_ref, xr_ref, yg_ref, qT_ref, vsT_ref, vwT_ref, gT_ref):
    hn = _rms(x_ref[0], g_ref[...]).astype(BF16)
    tok = jnp.dot(hn, wtok_ref[...], preferred_element_type=F32)
    kcvc_ref[0] = tok[:, 0:256]
    xr_ref[0] = tok[:, 512:1024]
    yg_ref[0] = tok[:, 1024:1536]
    lane = lax.broadcasted_iota(I32, (tok.shape[0], V7X_LANES), 1)
    slc_c = slc_ref[...].astype(F32)
    win_c = win_ref[...].astype(F32)
    for g in range(N_KV):
        in_g = lambda k2: jnp.where(lane < HEAD_DIM, k2 if g == 0 else pltpu.roll(k2, HEAD_DIM, axis=1), 0.0)
        ks_ref[0, g] = jnp.concatenate([in_g(tok[:, 256:384]) + slc_c[:, :V7X_LANES], slc_c[:, V7X_LANES:]],
                                       axis=1).astype(BF16)
        kw_ref[0, g] = (in_g(tok[:, 384:512]) + win_c).astype(BF16)
    tr = lax.dot_general(wtr_ref[...], hn, (((1,), (1,)), ((), ())),
                         preferred_element_type=F32)
    qT_ref[0] = (tr[0:512] * (ATTN_SCALE * LOG2E)).astype(BF16)
    ones = jnp.ones((V_ROWS - HEAD_DIM, tr.shape[1]), BF16)
    for g in range(N_KV):
        for v_ref, r0 in ((vsT_ref, 512), (vwT_ref, 640)):
            v_ref[0, g, 0:HEAD_DIM, :] = tr[r0 + g * HEAD_DIM:r0 + (g + 1) * HEAD_DIM].astype(BF16)
            v_ref[0, g, HEAD_DIM:V_ROWS, :] = ones
    gT_ref[0] = _sigmoid(tr[768:800])


def _key_position_columns(t_):
    pos = jnp.arange(t_, dtype=I32)
    dup = lambda vals: [v.astype(BF16)[:, None] for v in vals for _ in range(2)]
    pad_to = lambda cols, width: jnp.concatenate(
        cols + [jnp.zeros((t_, width - sum(c.shape[1] for c in cols)), BF16)], axis=1)
    lead = jnp.zeros((t_, HEAD_DIM), BF16)
    slc = pad_to([lead, jax.nn.one_hot(pos // L_SLC, t_ // L_SLC, dtype=BF16)]
                 + dup([pos // L_SLC, pos % L_SLC]), 2 * HEAD_DIM + t_ // L_SLC)
    win = pad_to([lead] + dup([pos % KT_WIN]), 2 * HEAD_DIM)
    return slc, win


def _proj_in(x, norm_g, w_in, slc_cols, win_cols):
    b_, t_, d_ = x.shape
    tm = min(ROW_TILE, t_)
    q, kc, vc, ks, vs, kw, vw, g, xr, yg = jnp.split(
        w_in, [512, 640, 768, 896, 1024, 1152, 1280, 1304, 1816], axis=1)
    wtok = jnp.concatenate([kc, vc, ks, kw, xr, yg], axis=1).astype(BF16)
    g4 = g.reshape(d_, N_KV, HEADS_PER_KV, N_GATES).transpose(0, 1, 3, 2)
    g4 = jnp.pad(g4.reshape(d_, N_KV, 12), ((0, 0), (0, 0), (0, 4))).reshape(d_, 32)
    wtr = jnp.concatenate([q, vs, vw, g4], axis=1).T.astype(BF16)
    nt = t_ // tm
    row = lambda shape: pl.BlockSpec((1, tm, shape), lambda b, i: (b, i, 0))
    col = lambda shape: pl.BlockSpec((1, shape, tm), lambda b, i: (b, 0, i))
    full = lambda a: pl.BlockSpec(a.shape, lambda b, i: (0,) * a.ndim)
    g2 = norm_g.reshape(1, d_)
    ws, ww = slc_cols.shape[1], win_cols.shape[1]
    grow = lambda width: pl.BlockSpec((1, N_KV, tm, width), lambda b, i: (b, 0, i, 0))
    gcol = pl.BlockSpec((1, N_KV, V_ROWS, tm), lambda b, i: (b, 0, 0, i))
    outs = pl.pallas_call(
        _proj_in_kernel,
        name="proj_in",
        grid=(b_, nt),
        in_specs=[row(d_), full(g2), full(wtok), full(wtr),
                  pl.BlockSpec((tm, ws), lambda b, i: (i, 0)), pl.BlockSpec((tm, ww), lambda b, i: (i, 0))],
        out_specs=[row(256), grow(ws), grow(ww), row(512), row(512), col(512), gcol, gcol, col(32)],
        out_shape=[
            jax.ShapeDtypeStruct((b_, t_, 256), F32),
            jax.ShapeDtypeStruct((b_, N_KV, t_, ws), BF16),
            jax.ShapeDtypeStruct((b_, N_KV, t_, ww), BF16),
            jax.ShapeDtypeStruct((b_, t_, 512), F32),
            jax.ShapeDtypeStruct((b_, t_, 512), F32),
            jax.ShapeDtypeStruct((b_, 512, t_), BF16),
            jax.ShapeDtypeStruct((b_, N_KV, V_ROWS, t_), BF16),
            jax.ShapeDtypeStruct((b_, N_KV, V_ROWS, t_), BF16),
            jax.ShapeDtypeStruct((b_, 32, t_), F32),
        ],
        compiler_params=_cparams(("parallel", "parallel"), V7X_VMEM_LIMIT_BYTES),
    )(x, g2, wtok, wtr, slc_cols, win_cols)
    return outs


def _compress_kernel(xc_ref, pa_ref, pb_ref, w1a_ref, w1b_ref, w2_ref, out_ref):
    xc = xc_ref[0].astype(BF16)
    n_chunk = xc.shape[0]
    a = jnp.dot(xc, w1a_ref[...], preferred_element_type=F32)
    bm = jnp.dot(xc, w1b_ref[...], preferred_element_type=F32)
    posc = (jnp.dot(pa_ref[...], w1a_ref[...], preferred_element_type=F32)
            + jnp.dot(pb_ref[...], w1b_ref[...], preferred_element_type=F32))[0:1]
    row = lax.broadcasted_iota(I32, bm.shape, 0)
    bm_up = jnp.where(row < n_chunk - 1, pltpu.roll(bm, n_chunk - 1, axis=0), 0.0)
    hid = _gelu_tanh(a + bm_up + posc).astype(BF16)
    out_ref[0] = jnp.dot(hid, w2_ref[...], preferred_element_type=F32)


def _compress(kcvc, cmp_pos, cmp_w1, cmp_w2):
    b_, t_, _ = kcvc.shape
    n_chunk = t_ // STRIDE
    xc = kcvc.reshape(b_, n_chunk, STRIDE * 256)
    eye = jnp.eye(2 * N_KV, dtype=F32)
    w1 = cmp_w1.reshape(2, 2, STRIDE, HEAD_DIM, CMP_HID)

    def expand(half):
        w = w1[:, half]
        full = jnp.einsum('wldo,wx,gy->lxgdwyo', w, jnp.eye(2, dtype=F32), jnp.eye(2, dtype=F32))
        return full.reshape(STRIDE * 256, 4 * CMP_HID).astype(BF16)

    del eye
    w1a, w1b = expand(0), expand(1)
    pos = cmp_pos.reshape(2, 2, STRIDE, HEAD_DIM)

    def posrow(half):
        p = jnp.broadcast_to(pos[:, half][:, None], (2, N_KV, STRIDE, HEAD_DIM))
        p = p.transpose(2, 0, 1, 3).reshape(1, STRIDE * 256)
        return jnp.pad(p, ((0, 7), (0, 0))).astype(BF16)

    pa, pb = posrow(0), posrow(1)
    w2 = jnp.einsum('whd,wx,gy->wghxyd', cmp_w2, jnp.eye(2, dtype=F32), jnp.eye(2, dtype=F32))
    w2 = w2.reshape(4 * CMP_HID, 4 * HEAD_DIM).astype(BF16)
    full = lambda a: pl.BlockSpec(a.shape, lambda b: (0,) * a.ndim)
    return pl.pallas_call(
        _compress_kernel,
        name="compress_kv",
        grid=(b_,),
        in_specs=[pl.BlockSpec((1, n_chunk, STRIDE * 256), lambda b: (b, 0, 0)),
                  full(pa), full(pb), full(w1a), full(w1b), full(w2)],
        out_specs=pl.BlockSpec((1, n_chunk, 256), lambda b: (b, 0, 0)),
        out_shape=jax.ShapeDtypeStruct((b_, n_chunk, 256), F32),
        compiler_params=_cparams(("parallel",), V7X_VMEM_LIMIT_BYTES),
    )(xc, pa, pb, w1a, w1b, w2)


def _head_lanes(rows):
    r = rows.shape[0] // HEADS_PER_KV
    return jnp.concatenate([rows[h * r:(h + 1) * r, :] for h in range(HEADS_PER_KV)], axis=1)


def _attn_kernel(qT_ref, gT_ref, kc_ref, vcT_ref, ks_ref, vsT_ref, kw_ref, vwT_ref, mt_ref, o_ref, s_scr,
                 tile_list):
    i = pl.program_id(1)
    n_cmp = kc_ref.shape[2]
    n_slc = mt_ref.shape[0]
    nl = HEADS_PER_KV * TQ
    hd_rows = HEADS_PER_KV * HEAD_DIM

    lane = lax.broadcasted_iota(I32, (1, nl), 1)
    t_loc = lane % TQ
    t_row = i * TQ + t_loc

    def extra_rows(n_rows, rows):
        ridx = lax.broadcasted_iota(I32, (16, nl), 0)
        out = jnp.zeros((16, nl), F32)
        for k, r in enumerate(rows):
            out = jnp.where(ridx == k, r, out)
        return jnp.concatenate([out.astype(BF16), jnp.zeros((n_rows - 16, nl), BF16)], axis=0)

    ok_c = (lax.broadcasted_iota(I32, (n_cmp, nl), 0) * STRIDE + (L_CMP - 1)) <= t_row
    r_idx = lax.broadcasted_iota(I32, (KT_WIN, nl), 0)
    s_idx = lax.broadcasted_iota(I32, (n_slc, TQ), 0)
    cur = (i * TQ + lax.broadcasted_iota(I32, (1, TQ), 1)) // L_SLC
    valid = s_idx <= cur
    forced = valid & ((s_idx == 0) | (s_idx > cur - N_LOCAL))
    s_idx_f = s_idx.astype(F32)
    n_full = (i * TQ) // KT_SLC
    mt = mt_ref[...]

    def pick(_, score):
        m = jnp.max(score, axis=0, keepdims=True)
        idx = jnp.min(jnp.where(score == m, s_idx_f, 1.0e9), axis=0, keepdims=True)
        return jnp.where(s_idx_f == idx, -2.0, score)

    def stage_a(g, bs, tile, slot, causal):
        k0 = pl.multiple_of(tile * KT_SLC, KT_SLC)
        s = jnp.dot(ks_ref[0, g, pl.ds(k0, KT_SLC), :], bs, preferred_element_type=F32)
        if causal:
            pos = k0 + lax.broadcasted_iota(I32, (KT_SLC, nl), 0)
            s = jnp.where(pos <= t_row, s, MASK_NEG)
        s_scr[g, slot] = s
        return jnp.max(s, axis=0, keepdims=True)

    def stage_b(g, tile, slot, m, acc, m_tile):
        k0 = pl.multiple_of(tile * KT_SLC, KT_SLC)
        m_new = jnp.maximum(m, m_tile)
        p = jnp.exp2(s_scr[g, slot] - m_new)
        acc = jnp.exp2(m - m_new) * acc + jnp.dot(vsT_ref[0, g, :, pl.ds(k0, KT_SLC)], p.astype(BF16),
                                                  preferred_element_type=F32)
        return m_new, acc

    groups = range(N_KV)

    def setup(g):
        q4 = _head_lanes(qT_ref[0, g * hd_rows:(g + 1) * hd_rows, :])
        head = g * HEADS_PER_KV + lane // TQ + 1
        slope = lax.bitcast_convert_type((127 - head) << 23, F32)
        return dict(q4=q4, slope=slope, s_hi=slope * LOG2E_HI, s_lo=slope * LOG2E_LO)

    st = [setup(g) for g in groups]

    def cmp_scores(g):
        s_hi, s_lo = st[g]["s_hi"], st[g]["s_lo"]
        bc = jnp.concatenate(
            [st[g]["q4"], extra_rows(64, [256.0 * s_hi, 256.0 * s_lo, 16.0 * s_hi, 16.0 * s_lo])], axis=0)
        return jnp.dot(kc_ref[0, g], bc, preferred_element_type=F32)

    def cmp_softmax(g, s_c):
        s_c = jnp.where(ok_c, s_c, MASK_NEG)
        m_c = jnp.max(s_c, axis=0, keepdims=True)
        e_c = jnp.exp2(s_c - m_c)
        l_c = jnp.sum(e_c, axis=0, keepdims=True)
        inv_c = jnp.where(m_c > 0.5 * MASK_NEG, 1.0 / jnp.maximum(l_c, 1e-30), 0.0)
        p_c = e_c * inv_c
        o_c = jnp.dot(vcT_ref[0, g], p_c.astype(BF16), preferred_element_type=F32)
        p_grp = (p_c[:, 0:TQ] + p_c[:, TQ:2 * TQ]) + (p_c[:, 2 * TQ:3 * TQ] + p_c[:, 3 * TQ:4 * TQ])
        p1 = p_grp.astype(BF16)
        r1 = p_grp - p1.astype(F32)
        p2 = r1.astype(BF16)
        p3 = (r1 - p2.astype(F32)).astype(BF16)
        p_slc = (jnp.dot(mt, p1, preferred_element_type=F32) + jnp.dot(mt, p2, preferred_element_type=F32)
                 + jnp.dot(mt, p3, preferred_element_type=F32))
        return o_c, p_slc

    def win_scores(g):
        slope = st[g]["slope"]
        bw = jnp.concatenate([st[g]["q4"], extra_rows(64, [st[g]["s_hi"], st[g]["s_lo"]])], axis=0)
        s_tiles, c_tiles = [], []
        for a in range(N_WIN_TILES):
            tile = i - (N_WIN_TILES - 1) + a
            k0 = pl.multiple_of(jnp.maximum(tile, 0) * KT_WIN, KT_WIN)
            s = jnp.dot(kw_ref[0, g, pl.ds(k0, KT_WIN), :], bw, preferred_element_type=F32)
            if a == 0:
                s = jnp.where(r_idx > t_loc, s, MASK_NEG)
            elif a == N_WIN_TILES - 1:
                s = jnp.where(r_idx <= t_loc, s, MASK_NEG)
            s_tiles.append(s)
            c_a = slope * (LOG2E * KT_WIN * (a - (N_WIN_TILES - 1)))
            c_tiles.append(jnp.where(tile >= 0, c_a, MASK_NEG))
        return s_tiles, c_tiles

    def win_softmax(g, s_tiles, c_tiles):
        m_w = s_tiles[0].max(axis=0, keepdims=True) + c_tiles[0]
        for s, c_a in zip(s_tiles[1:], c_tiles[1:]):
            m_w = jnp.maximum(m_w, s.max(axis=0, keepdims=True) + c_a)
        acc_w = jnp.zeros((V_ROWS, nl), F32)
        for a, (s, c_a) in enumerate(zip(s_tiles, c_tiles)):
            tile = i - (N_WIN_TILES - 1) + a
            k0 = pl.multiple_of(jnp.maximum(tile, 0) * KT_WIN, KT_WIN)
            p = jnp.exp2(s - (m_w - c_a))
            acc_w = acc_w + jnp.dot(vwT_ref[0, g, :, pl.ds(k0, KT_WIN)], p.astype(BF16),
                                    preferred_element_type=F32)
        return acc_w[0:HEAD_DIM] * (1.0 / acc_w[HEAD_DIM:HEAD_DIM + 1])

    s_cmp = [cmp_scores(g) for g in groups]
    s_win = [win_scores(g) for g in groups]
    cmp_out, scores = [], []
    for g in groups:
        cmp_out.append(cmp_softmax(g, s_cmp[g]))
        score0 = jnp.where(forced, -2.0, jnp.where(valid, cmp_out[g][1], -1.0))
        scores.append(lax.fori_loop(0, N_SEL - (N_LOCAL + 1), pick, score0, unroll=True))
    o_win = [win_softmax(g, *s_win[g]) for g in groups]
    score = jnp.concatenate(scores, axis=1)

    def select(g):
        sel = valid & (score[:, g * TQ:(g + 1) * TQ] < -1.5)
        sel4 = jnp.concatenate([sel] * HEADS_PER_KV, axis=1)
        blk_rows = jnp.where(sel4, 0.0, SEL_NEG).astype(BF16)
        s_hi, s_lo = st[g]["s_hi"], st[g]["s_lo"]
        bs = jnp.concatenate(
            [st[g]["q4"], blk_rows, extra_rows(64, [L_SLC * s_hi, L_SLC * s_lo, s_hi, s_lo])], axis=0)
        m_tile0 = stage_a(g, bs, n_full, 0, True)
        gt = gT_ref[0, g * 16:(g + 1) * 16, :]
        gates = [_head_lanes(gt[k * HEADS_PER_KV:(k + 1) * HEADS_PER_KV, :]) for k in range(N_GATES)]
        return dict(bs=bs, base=gates[0] * cmp_out[g][0] + gates[2] * o_win[g], g_s=gates[1], m_tile0=m_tile0)

    grp = [select(g) for g in groups]

    blocks_per_tile = KT_SLC // L_SLC
    n_tiles = n_slc // blocks_per_tile
    assert n_tiles <= 32
    valid2 = jnp.concatenate([valid] * N_KV, axis=1)
    sel_all = jnp.where(valid2 & (score < -1.5), 1.0, 0.0).astype(BF16)
    blk_cnt = lax.dot_general(jnp.ones((8, N_KV * TQ), BF16), sel_all, (((1,), (1,)), ((), ())),
                              preferred_element_type=F32)
    memb = jnp.where(lax.broadcasted_iota(I32, (n_slc, V7X_LANES), 0) // blocks_per_tile
                     == lax.broadcasted_iota(I32, (n_slc, V7X_LANES), 1), 1.0, 0.0).astype(BF16)
    tile_cnt = jnp.dot(jnp.where(blk_cnt > 0.5, 1.0, 0.0).astype(BF16), memb,
                       preferred_element_type=F32)[0:1]
    lane_t = lax.broadcasted_iota(I32, (1, V7X_LANES), 1)
    need = (tile_cnt > 0.5) & (lane_t < n_full)
    bit = lax.bitcast_convert_type(((lane_t & 15) + 127) << 23, F32)
    lo_bits = jnp.sum(jnp.where(need & (lane_t < 16), bit, 0.0)).astype(I32)
    hi_bits = jnp.sum(jnp.where(need & (lane_t >= 16), bit, 0.0)).astype(I32)
    n_sel = jnp.int32(0)
    for t in range(n_tiles - 1):
        tile_list[n_sel] = t
        n_sel = n_sel + (((lo_bits if t < 16 else hi_bits) >> (t % 16)) & 1)

    order = lambda k: jnp.where(k == 0, n_full, tile_list[jnp.maximum(k - 1, 0)])

    def half(carry, tile_b, slot_b, tile_a):
        m_next = [c[2] if tile_a is None else stage_a(g, grp[g]["bs"], tile_a, 1 - slot_b, False)
                  for g, c in enumerate(carry)]
        out = []
        for g in range(N_KV):
            m, acc, m_tile = carry[g]
            out.append(stage_b(g, tile_b, slot_b, m, acc, m_tile) + (m_next[g],))
        return tuple(out)

    def body(p, carry):
        k = 2 * p
        return half(half(carry, order(k), 0, order(k + 1)), order(k + 1), 1, order(k + 2))

    init = tuple((jnp.full((1, nl), MASK_NEG, F32), jnp.zeros((V_ROWS, nl), F32), grp[g]["m_tile0"])
                 for g in range(N_KV))
    carry = lax.fori_loop(0, n_sel // 2, body, init)
    k_tail = 2 * (n_sel // 2)
    carry = lax.cond(n_sel % 2 == 1,
                     lambda c: half(half(c, order(k_tail), 0, order(k_tail + 1)), order(k_tail + 1), 1, None),
                     lambda c: half(c, order(k_tail), 0, None), carry)
    for g in range(N_KV):
        _, acc, _ = carry[g]
        o_s = acc[0:HEAD_DIM] * (1.0 / acc[HEAD_DIM:HEAD_DIM + 1])
        out = grp[g]["base"] + grp[g]["g_s"] * o_s
        for h in range(HEADS_PER_KV):
            r0 = g * hd_rows + h * HEAD_DIM
            o_ref[0, r0:r0 + HEAD_DIM, :] = out[:, h * TQ:(h + 1) * TQ]


def _attention(qT, gT, kcmp, ks_aug, vsT4, kw_aug, vwT4):
    b_, _, t_ = qT.shape
    n_cmp = t_ // STRIDE
    n_slc = t_ // L_SLC
    c = jnp.arange(n_cmp, dtype=I32)
    cmp_cols = jnp.concatenate([v.astype(BF16)[:, None] for v in (c // 16, c // 16, c % 16, c % 16)]
                               + [jnp.zeros((n_cmp, HEAD_DIM - 4), BF16)], axis=1)
    kc4 = kcmp[..., 0:KV_W].reshape(b_, n_cmp, N_KV, HEAD_DIM).transpose(0, 2, 1, 3).astype(BF16)
    kc_aug = jnp.concatenate(
        [kc4, jnp.broadcast_to(cmp_cols, (b_, N_KV) + cmp_cols.shape)], axis=-1)
    vcT = kcmp[..., KV_W:2 * KV_W].reshape(b_, n_cmp, N_KV, HEAD_DIM).transpose(0, 2, 3, 1).astype(BF16)
    s = jnp.arange(n_slc, dtype=I32)[:, None]
    cc = c[None, :]
    r_slc = L_SLC // STRIDE
    mt = (((cc >= r_slc * s) & (cc < r_slc * s + r_slc)).astype(F32)
          + ((cc + 1 >= r_slc * s) & (cc + 1 < r_slc * s + r_slc)).astype(F32))
    mt = jnp.where(cc < n_cmp - 1, mt, 0.0).astype(BF16)
    kv_spec = lambda a: pl.BlockSpec((1,) + a.shape[1:], lambda b, i: (b, 0, 0, 0))
    return pl.pallas_call(
        _attn_kernel,
        name="nsa_attention",
        grid=(b_, t_ // TQ),
        in_specs=[pl.BlockSpec((1, D_ATTN, TQ), lambda b, i: (b, 0, i)),
                  pl.BlockSpec((1, 32, TQ), lambda b, i: (b, 0, i)),
                  kv_spec(kc_aug), kv_spec(vcT), kv_spec(ks_aug), kv_spec(vsT4),
                  kv_spec(kw_aug), kv_spec(vwT4),
                  pl.BlockSpec(mt.shape, lambda b, i: (0, 0))],
        out_specs=pl.BlockSpec((1, D_ATTN, TQ), lambda b, i: (b, 0, i)),
        out_shape=jax.ShapeDtypeStruct((b_, D_ATTN, t_), F32),
        scratch_shapes=[pltpu.VMEM((N_KV, 2, KT_SLC, HEADS_PER_KV * TQ), F32), pltpu.SMEM((32,), I32)],
        compiler_params=_cparams(("parallel", "arbitrary"), V7X_VMEM_LIMIT_BYTES),
    )(qT, gT, kc_aug, vcT, ks_aug, vsT4, kw_aug, vwT4, mt)


def _rglru_seg_pitch(tc):
    return tc // 8 + 8


def _rglru_kernel(xr_ref, yg_ref, cw_ref, cb_ref, wa_ref, ba_ref, wx_ref, bx_ref, lam_ref,
                  o_ref, tail_ref, h_ref, xs_scr, ys_scr, os_scr):
    tc = xr_ref.shape[1]

    @pl.when(pl.program_id(1) == 0)
    def _():
        tail_ref[...] = jnp.zeros_like(tail_ref)
        h_ref[...] = jnp.zeros_like(h_ref)

    n_v = tc // 8
    n_slab = xr_ref.shape[2] // V7X_LANES
    pitch = xs_scr.shape[1] // 8
    for c in range(n_slab):
        for s in range(8):
            lanes = slice(c * V7X_LANES, (c + 1) * V7X_LANES)
            xs_scr[c, s * pitch:s * pitch + n_v, :] = xr_ref[0, s * n_v:(s + 1) * n_v, lanes]
            ys_scr[c, s * pitch:s * pitch + n_v, :] = yg_ref[0, s * n_v:(s + 1) * n_v, lanes]

    def piece(scr, v):
        return jnp.concatenate([scr[c, pl.ds(v, 8, stride=pitch), :] for c in range(n_slab)], axis=1)

    x = [piece(xs_scr, v) for v in range(n_v)]
    row8 = lax.broadcasted_iota(I32, (8, x[0].shape[1]), 0)
    tail = tail_ref[...]
    before = {d: jnp.where(row8 == 0, tail[CONV_W - 1 - d:CONV_W - d, :], pltpu.roll(x[n_v - d], 1, axis=0))
              for d in range(1, CONV_W)}
    tail_ref[0:CONV_W - 1, :] = jnp.concatenate([x[n_v - d][7:8, :] for d in range(CONV_W - 1, 0, -1)], axis=0)
    cw = cw_ref[...]
    cb = cb_ref[...]
    xc = []
    for v in range(n_v):
        acc = x[v] * cw[CONV_W - 1:CONV_W, :] + cb
        for d in range(1, CONV_W):
            acc = acc + (x[v - d] if v >= d else before[d - v]) * cw[CONV_W - 1 - d:CONV_W - d, :]
        xc.append(acc)

    xb = jnp.concatenate(xc, axis=0).astype(BF16)
    r = _sigmoid(jnp.dot(xb, wa_ref[...], preferred_element_type=F32) + ba_ref[...])
    gi = _sigmoid(jnp.dot(xb, wx_ref[...], preferred_element_type=F32) + bx_ref[...])
    z = -lam_ref[...]
    softplus = jnp.maximum(z, 0.0) + jnp.log1p(jnp.exp(-jnp.abs(z)))
    log_a = (-C_LRU * r) * softplus
    a = jnp.exp(log_a)
    th = jnp.tanh(log_a)
    neg_expm1 = 2.0 * th / (th - 1.0)
    bb = jnp.sqrt(neg_expm1) * gi * jnp.concatenate(xc, axis=0)

    a_cum, b_cum = [a[0:8]], [bb[0:8]]
    for v in range(1, n_v):
        a_v = a[v * 8:(v + 1) * 8]
        b_cum.append(a_v * b_cum[-1] + bb[v * 8:(v + 1) * 8])
        a_cum.append(a_v * a_cum[-1])
    h_in = [h_ref[0:1, :]]
    for s in range(7):
        h_in.append(a_cum[-1][s:s + 1, :] * h_in[-1] + b_cum[-1][s:s + 1, :])
    h_ref[0:1, :] = a_cum[-1][7:8, :] * h_in[-1] + b_cum[-1][7:8, :]
    h_in = jnp.concatenate(h_in, axis=0)
    for v in range(n_v):
        out = (a_cum[v] * h_in + b_cum[v]) * _gelu_tanh(piece(ys_scr, v))
        for c in range(n_slab):
            os_scr[c, pl.ds(v, 8, stride=pitch), :] = out[:, c * V7X_LANES:(c + 1) * V7X_LANES]
    for s in range(8):
        o_ref[0, s * n_v:(s + 1) * n_v, :] = jnp.concatenate(
            [os_scr[c, s * pitch:s * pitch + n_v, :] for c in range(n_slab)], axis=1)


def _block_diag(w):
    n, c, d = w.shape
    return jnp.einsum('ncd,nm->ncmd', w, jnp.eye(n, dtype=w.dtype)).reshape(n * c, n * d)


def _rglru(xr, yg, conv_w, conv_b, wa, ba, wx, bx, lam):
    b_, t_, c_ = xr.shape
    tc = min(RGLRU_CHUNK, t_)
    wa_bd = _block_diag(wa).astype(BF16)
    wx_bd = _block_diag(wx).astype(BF16)
    vec = lambda v: v.reshape(1, c_)
    full = lambda a: pl.BlockSpec(a.shape, lambda b, i: (0,) * a.ndim)
    blk = pl.BlockSpec((1, tc, c_), lambda b, i: (b, i, 0))
    args = (xr, yg, conv_w, vec(conv_b), wa_bd, vec(ba), wx_bd, vec(bx), vec(lam))
    return pl.pallas_call(
        _rglru_kernel,
        name="rglru",
        grid=(b_, t_ // tc),
        in_specs=[blk, blk] + [full(a) for a in args[2:]],
        out_specs=blk,
        out_shape=jax.ShapeDtypeStruct((b_, t_, c_), F32),
        scratch_shapes=[pltpu.VMEM((8, c_), F32), pltpu.VMEM((8, c_), F32)]
                       + [pltpu.VMEM((c_ // V7X_LANES, 8 * _rglru_seg_pitch(tc), V7X_LANES), F32)] * 3,
        compiler_params=_cparams(("parallel", "arbitrary"), V7X_VMEM_LIMIT_BYTES),
    )(*args)


def _out_proj_kernel(oaT_ref, orec_ref, x_ref, ga_ref, gr_ref, wa_ref, wr_ref, o_ref):
    oaT = oaT_ref[0]
    ms = jnp.mean(oaT * oaT, axis=0, keepdims=True)
    na = (oaT * lax.rsqrt(ms + RMS_EPS)).T * ga_ref[...]
    nr = _rms(orec_ref[0], gr_ref[...])
    y = (jnp.dot(na.astype(BF16), wa_ref[...], preferred_element_type=F32)
         + jnp.dot(nr.astype(BF16), wr_ref[...], preferred_element_type=F32))
    o_ref[0] = x_ref[0] + y


def _out_proj(oaT, orec, x, g_attn, g_rec, w_out):
    b_, t_, d_ = x.shape
    tm = min(256, t_)
    wa = w_out[:D_ATTN].astype(BF16)
    wr = w_out[D_ATTN:].astype(BF16)
    ga = g_attn.reshape(1, -1)
    gr = g_rec.reshape(1, -1)
    full = lambda a: pl.BlockSpec(a.shape, lambda b, i: (0,) * a.ndim)
    return pl.pallas_call(
        _out_proj_kernel,
        name="out_proj",
        grid=(b_, t_ // tm),
        in_specs=[pl.BlockSpec((1, D_ATTN, tm), lambda b, i: (b, 0, i)),
                  pl.BlockSpec((1, tm, orec.shape[2]), lambda b, i: (b, i, 0)),
                  pl.BlockSpec((1, tm, d_), lambda b, i: (b, i, 0)),
                  full(ga), full(gr), full(wa), full(wr)],
        out_specs=pl.BlockSpec((1, tm, d_), lambda b, i: (b, i, 0)),
        out_shape=jax.ShapeDtypeStruct((b_, t_, d_), F32),
        compiler_params=_cparams(("parallel", "parallel"), V7X_VMEM_LIMIT_BYTES),
    )(oaT, orec, x, ga, gr, wa, wr)


def _ple_apply(x, p_ref, g_ref, wg_ref, wp_ref, fg_ref, *, final):
    gate = _sigmoid(jnp.dot(_rms(x, g_ref[...]).astype(BF16), wg_ref[...], preferred_element_type=F32))
    proj = jnp.dot(p_ref[...].astype(BF16), wp_ref[...], preferred_element_type=F32)
    y = x + gate * proj
    return _rms(y, fg_ref[...]) if final else y


def _ple_operands(p, norm_g, w_gate, w_proj, final_g, rows, row_map, const_map):
    d_ = w_gate.shape[0]
    arrays = (p, norm_g.reshape(1, d_), w_gate.astype(BF16), w_proj.astype(BF16), final_g.reshape(1, d_))
    once = dict(pipeline_mode=pl.Buffered(1))
    specs = [pl.BlockSpec((rows, p.shape[1]), row_map)] + [pl.BlockSpec(a.shape, const_map, **once)
                                                            for a in arrays[1:]]
    return arrays, specs


def _ffn_kernel(be_ref, x_ref, g_ref, wg_ref, wu_ref, wd_ref, *rest, dense, final):
    ple_refs, (o_ref, hn_ref, acc_ref) = rest[:-3], rest[-3:]
    f = pl.program_id(1)
    used = True if dense else pl.program_id(0) < be_ref[pl.num_programs(0)]

    @pl.when(f == 0)
    def _():
        if dense:
            x = x_ref[...]
            hn_ref[...] = _rms(x, g_ref[...]).astype(BF16)
            acc_ref[...] = x
        else:
            hn_ref[...] = _unpack_bf16_pairs(x_ref[...])
            acc_ref[...] = jnp.zeros_like(acc_ref)

    def accumulate():
        hn = hn_ref[...]
        gate = jnp.dot(hn, wg_ref[0], preferred_element_type=F32)
        up = jnp.dot(hn, wu_ref[0], preferred_element_type=F32)
        act = (_silu(gate) * up).astype(BF16)
        acc_ref[...] += jnp.dot(act, wd_ref[0], preferred_element_type=F32)

    if dense:
        accumulate()
    else:
        pl.when(used)(accumulate)

    @pl.when(f == pl.num_programs(1) - 1)
    def _():
        if dense:
            o_ref[...] = _ple_apply(acc_ref[...], *ple_refs, final=final)
        else:
            o_ref[...] = acc_ref[...].astype(o_ref.dtype)


def _ffn(x, norm_g, wg, wu, wd, block_e, n_used=None, *, dense, tf, ple=None, final=False):
    n = x.shape[0]
    d_ = wg.shape[1]
    tm = min(ROW_TILE if dense else MOE_ROWS, n)
    ff = wg.shape[2]
    assert ff % tf == 0 and n % tm == 0
    if block_e is None:
        block_e = jnp.zeros((n // tm,), I32)
    if n_used is None:
        n_used = jnp.int32(n // tm)
    n_blk, nf = n // tm, ff // tf
    be_all = jnp.concatenate([block_e, jnp.reshape(n_used, (1,)).astype(I32)])
    wmode = dict(pipeline_mode=pl.Buffered(1)) if (wg.shape[0] == 1 and ff == tf) else {}
    ftile = lambda i, f, be: jnp.where(i < be[n_blk], f, nf - 1)
    ple_arrays, ple_specs = ((), [])
    if dense:
        ple_arrays, ple_specs = _ple_operands(*ple, tm, lambda i, f, be: (i, 0), lambda i, f, be: (0, 0))
    grid_spec = pltpu.PrefetchScalarGridSpec(
        num_scalar_prefetch=1,
        grid=(n_blk, nf),
        in_specs=[pl.BlockSpec((tm, x.shape[1]), lambda i, f, be: (i, 0)),
                  pl.BlockSpec((1, d_), lambda i, f, be: (0, 0)),
                  pl.BlockSpec((1, d_, tf), lambda i, f, be: (be[i], 0, ftile(i, f, be)), **wmode),
                  pl.BlockSpec((1, d_, tf), lambda i, f, be: (be[i], 0, ftile(i, f, be)), **wmode),
                  pl.BlockSpec((1, tf, d_), lambda i, f, be: (be[i], ftile(i, f, be), 0), **wmode)] + ple_specs,
        out_specs=pl.BlockSpec((tm, d_), lambda i, f, be: (i, 0)),
        scratch_shapes=[pltpu.VMEM((tm, d_), BF16), pltpu.VMEM((tm, d_), F32)],
    )
    return pl.pallas_call(
        functools.partial(_ffn_kernel, dense=dense, final=final),
        name="ffn_dense" if dense else "ffn_expert",
        grid_spec=grid_spec,
        out_shape=jax.ShapeDtypeStruct((n, d_), F32 if dense else BF16),
        compiler_params=_cparams(("parallel", "arbitrary"), V7X_VMEM_LIMIT_BYTES),
    )(be_all, x, norm_g.reshape(1, d_), wg, wu, wd, *ple_arrays)


def _pack_bf16_pairs(x):
    w = x.shape[1] // 2
    bits = lax.bitcast_convert_type(x.astype(BF16).astype(F32), jnp.uint32)
    return (bits[:, :w] >> 16) | bits[:, w:]


def _unpack_bf16_pairs(words):
    lo = lax.bitcast_convert_type(words << 16, F32)
    hi = lax.bitcast_convert_type(words & jnp.uint32(0xFFFF0000), F32)
    return jnp.concatenate([lo, hi], axis=1).astype(BF16)


def _router_kernel(x_ref, g_ref, wr_ref, tri_ref, h_ref, e_ref, p_ref, r_ref, cb_ref, tot_ref, cnt_ref):
    @pl.when(pl.program_id(0) == 0)
    def _():
        cnt_ref[...] = jnp.zeros_like(cnt_ref)

    hn = _rms(x_ref[...], g_ref[...])
    h_ref[...] = _pack_bf16_pairs(hn)
    logits = lax.dot_general(wr_ref[...], hn, (((1,), (1,)), ((), ())),
                             precision=lax.Precision.HIGHEST, preferred_element_type=F32)
    eidx = lax.broadcasted_iota(I32, logits.shape, 0)
    m1 = jnp.max(logits, axis=0, keepdims=True)
    i1 = jnp.min(jnp.where(logits == m1, eidx, N_EXPERTS), axis=0, keepdims=True)
    rest = jnp.where(eidx == i1, -jnp.inf, logits)
    m2 = jnp.max(rest, axis=0, keepdims=True)
    i2 = jnp.min(jnp.where(rest == m2, eidx, N_EXPERTS), axis=0, keepdims=True)
    e2 = jnp.exp(m2 - m1)
    inv = 1.0 / (1.0 + e2)
    e_ref[...] = jnp.concatenate([i1, i2], axis=0)
    p_ref[...] = jnp.concatenate([inv, e2 * inv], axis=0)

    tm = logits.shape[1]
    chosen = jnp.where((eidx == i1) | (eidx == i2), 1.0, 0.0)
    incl = jnp.dot(chosen.astype(BF16), tri_ref[...], preferred_element_type=F32)
    before = incl - chosen + cnt_ref[:, 0:1]
    r_ref[...] = jnp.concatenate(
        [jnp.sum(jnp.where(eidx == i1, before, 0.0), axis=0, keepdims=True),
         jnp.sum(jnp.where(eidx == i2, before, 0.0), axis=0, keepdims=True)], axis=0).astype(I32)
    lane = lax.broadcasted_iota(I32, cb_ref.shape, 1)
    cb = jnp.zeros(cb_ref.shape, F32)
    for c in range(tm // CMB_ROWS):
        cb = jnp.where(lane == c, before[:, c * CMB_ROWS:c * CMB_ROWS + 1], cb)
    cb_ref[...] = cb.astype(I32)
    total = cnt_ref[...] + incl[:, tm - 1:tm]
    cnt_ref[...] = total
    tot_ref[...] = total.astype(I32)


def _router(x, norm_g, router_w):
    n, d_ = x.shape
    tm = min(ROW_TILE, n)
    assert tm % CMB_ROWS == 0
    wr = router_w.T
    tri = (jnp.arange(tm)[:, None] <= jnp.arange(tm)[None, :]).astype(BF16)
    h, top_e, top_p, rank, cb, tot = pl.pallas_call(
        _router_kernel,
        name="router",
        grid=(n // tm,),
        in_specs=[pl.BlockSpec((tm, d_), lambda i: (i, 0)),
                  pl.BlockSpec((1, d_), lambda i: (0, 0)),
                  pl.BlockSpec(wr.shape, lambda i: (0, 0)),
                  pl.BlockSpec(tri.shape, lambda i: (0, 0))],
        out_specs=[pl.BlockSpec((tm, d_ // 2), lambda i: (i, 0)),
                   pl.BlockSpec((2, tm), lambda i: (0, i)),
                   pl.BlockSpec((2, tm), lambda i: (0, i)),
                   pl.BlockSpec((2, tm), lambda i: (0, i)),
                   pl.BlockSpec((N_EXPERTS, V7X_LANES), lambda i: (0, i)),
                   pl.BlockSpec((N_EXPERTS, V7X_LANES), lambda i: (0, 0))],
        out_shape=[jax.ShapeDtypeStruct((n, d_ // 2), jnp.uint32),
                   jax.ShapeDtypeStruct((2, n), I32),
                   jax.ShapeDtypeStruct((2, n), F32),
                   jax.ShapeDtypeStruct((2, n), I32),
                   jax.ShapeDtypeStruct((N_EXPERTS, n // tm * V7X_LANES), I32),
                   jax.ShapeDtypeStruct((N_EXPERTS, V7X_LANES), I32)],
        scratch_shapes=[pltpu.VMEM((N_EXPERTS, V7X_LANES), F32)],
        compiler_params=_cparams(("arbitrary",), V7X_VMEM_LIMIT_BYTES),
    )(x, norm_g.reshape(1, d_), wr, tri)
    cb = cb.reshape(N_EXPERTS, n // tm, V7X_LANES)[:, :, :tm // CMB_ROWS].reshape(N_EXPERTS, n // CMB_ROWS).T
    return h, top_e, top_p, rank, cb, tot[:, 0]


SCATTER_UNROLL = 8


def _scatter_kernel(dest_ref, h_ref, zeros_ref, xs_ref, sem):
    del zeros_ref
    n_tok = h_ref.shape[0]

    def row_copy(t, k):
        return pltpu.make_async_copy(h_ref.at[pl.ds(t, 1), :], xs_ref.at[pl.ds(dest_ref[k, t], 1), :], sem)

    def issue(u, carry):
        for r in range(SCATTER_UNROLL):
            t = u * SCATTER_UNROLL + r
            row_copy(t, 0).start()
            row_copy(t, 1).start()
        return carry

    lax.fori_loop(0, n_tok // SCATTER_UNROLL, issue, 0)
    for _ in range(2):
        pltpu.make_async_copy(h_ref, xs_ref.at[pl.ds(0, n_tok), :], sem).wait()


def _scatter(h, dest, n_slots):
    n, w = h.shape
    tc = min(MOE_TOK_CHUNK, n)
    zeros = jnp.zeros((n_slots, w), h.dtype)
    return pl.pallas_call(
        _scatter_kernel,
        name="moe_scatter",
        grid=(n // tc,),
        in_specs=[pl.BlockSpec((2, tc), lambda c: (0, c), memory_space=pltpu.SMEM),
                  pl.BlockSpec((tc, w), lambda c: (c, 0)),
                  pl.BlockSpec(memory_space=pl.ANY)],
        out_specs=pl.BlockSpec(memory_space=pl.ANY),
        out_shape=jax.ShapeDtypeStruct((n_slots, w), h.dtype),
        scratch_shapes=[pltpu.SemaphoreType.DMA(())],
        input_output_aliases={2: 0},
        compiler_params=_cparams(("arbitrary",), V7X_VMEM_LIMIT_BYTES),
    )(dest, h, zeros)


N_CMB_OPS = 2 * N_EXPERTS


def _combine_kernel(jt_ref, vt_ref, x_ref, dest_ref, gate_ref, *rest, final):
    yb_refs = rest[:N_CMB_OPS]
    ple_refs = rest[N_CMB_OPS:-1]
    o_ref = rest[-1]
    c = pl.program_id(0)
    dest = dest_ref[...]
    gate = gate_ref[...]
    lane = lax.broadcasted_iota(I32, (dest.shape[0], CMB_ROWS), 1)
    acc = x_ref[...]
    for k in range(N_CMB_OPS):
        base = jnp.where(vt_ref[c * N_CMB_OPS + k] == 1, jt_ref[c * N_CMB_OPS + k] * CMB_ROWS, -2 * CMB_ROWS)
        rel = dest - base
        w = (jnp.where(lane == rel[:, 0:1], gate[:, 0:1], 0.0)
             + jnp.where(lane == rel[:, 1:2], gate[:, 1:2], 0.0)).astype(BF16)
        acc = acc + jnp.dot(w, yb_refs[k][...], preferred_element_type=F32)
    o_ref[...] = _ple_apply(acc, *ple_refs, final=final)


def _combine(x, yb, dest_t, gate_t, jt, vt, ple, final):
    n, d_ = x.shape
    ple_arrays, ple_specs = _ple_operands(*ple, CMB_ROWS, lambda c, jt, vt: (c, 0), lambda c, jt, vt: (0, 0))
    yb_spec = lambda k: pl.BlockSpec((CMB_ROWS, d_), lambda c, jt, vt, k=k: (jt[c * N_CMB_OPS + k], 0))
    grid_spec = pltpu.PrefetchScalarGridSpec(
        num_scalar_prefetch=2,
        grid=(n // CMB_ROWS,),
        in_specs=[pl.BlockSpec((CMB_ROWS, d_), lambda c, jt, vt: (c, 0)),
                  pl.BlockSpec((CMB_ROWS, 2), lambda c, jt, vt: (c, 0)),
                  pl.BlockSpec((CMB_ROWS, 2), lambda c, jt, vt: (c, 0))]
                 + [yb_spec(k) for k in range(N_CMB_OPS)] + ple_specs,
        out_specs=pl.BlockSpec((CMB_ROWS, d_), lambda c, jt, vt: (c, 0)),
    )
    return pl.pallas_call(
        functools.partial(_combine_kernel, final=final),
        name="moe_combine",
        grid_spec=grid_spec,
        out_shape=jax.ShapeDtypeStruct((n, d_), F32),
        compiler_params=_cparams(("parallel",), V7X_VMEM_LIMIT_BYTES),
    )(jt, vt, x, dest_t, gate_t, *([yb] * N_CMB_OPS), *ple_arrays)


def _moe_plan(top_e, rank, cb, counts, n_slots):
    padded = (counts + MOE_ROWS - 1) // MOE_ROWS * MOE_ROWS
    pad_ends = jnp.cumsum(padded)
    pad_starts = pad_ends - padded
    start_of = sum(jnp.where(top_e == e, pad_starts[e], 0) for e in range(N_EXPERTS))
    dest = (start_of + rank).astype(I32)
    n_blk = n_slots // MOE_ROWS
    blk_start = jnp.arange(n_blk, dtype=I32) * MOE_ROWS
    block_e = jnp.minimum(jnp.sum((pad_ends[None, :] <= blk_start[:, None]).astype(I32), axis=1),
                          N_EXPERTS - 1).astype(I32)

    cb = jnp.concatenate([cb, counts[None]], axis=0)
    lo = pad_starts[None] + cb[:-1]
    hi = pad_starts[None] + cb[1:]
    ja = lo // CMB_ROWS
    jb = (hi - 1) // CMB_ROWS
    va = hi > lo
    vb = va & (jb > ja)
    jt = jnp.stack([ja, jb], axis=-1).reshape(-1)
    vt = jnp.stack([va, vb], axis=-1).reshape(-1)
    jt = jnp.where(vt, jt, 0).astype(I32)
    return dest, block_e, pad_ends[-1] // MOE_ROWS, jt, vt.astype(I32)


def _moe(x, norm_g, router_w, wg, wu, wd, ple, final):
    n, d_ = x.shape
    n_slots = (2 * n // MOE_ROWS + N_EXPERTS) * MOE_ROWS
    h, top_e, top_p, rank, cb, counts = _router(x, norm_g, router_w)
    dest, block_e, n_used, jt, vt = _moe_plan(top_e, rank, cb, counts, n_slots)
    xs = _scatter(h, dest, n_slots)
    yb = _ffn(xs, norm_g, wg, wu, wd, block_e, n_used, dense=False, tf=wg.shape[2] // 2)
    return _combine(x, yb, dest.T, top_p.T, jt, vt, ple, final)


def kernel(x, p, attn_norm, w_in, cmp_pos, cmp_w1, cmp_w2, conv_w, conv_b, lru_wa, lru_ba, lru_wx, lru_bx,
           lru_lambda, out_norm_attn, out_norm_rec, w_out, ffn_norm, dense_w_gate, dense_w_up, dense_w_down,
           router_w, moe_w_gate, moe_w_up, moe_w_down, ple_norm, ple_w_gate, ple_w_proj, final_norm):
    b_, t_, d_ = x.shape
    depth = w_in.shape[0]
    n = b_ * t_
    slc_cols, win_cols = _key_position_columns(t_)
    for i in range(depth):
        kcvc, ks, kw, xr, yg, qT, vsT, vwT, gT = _proj_in(x, attn_norm[i], w_in[i], slc_cols, win_cols)
        kcmp = _compress(kcvc, cmp_pos[i], cmp_w1[i], cmp_w2[i])
        oaT = _attention(qT, gT, kcmp, ks, vsT, kw, vwT)
        orec = _rglru(xr, yg, conv_w[i], conv_b[i], lru_wa[i], lru_ba[i], lru_wx[i], lru_bx[i], lru_lambda[i])
        x = _out_proj(oaT, orec, x, out_norm_attn[i], out_norm_rec[i], w_out[i])
        x2 = x.reshape(n, d_)
        j = i // 2
        ple = (p[i].reshape(n, -1), ple_norm[i], ple_w_gate[i], ple_w_proj[i], final_norm)
        final = i == depth - 1
        if i % 2 == 0:
            x2 = _ffn(x2, ffn_norm[i], dense_w_gate[j][None].astype(BF16), dense_w_up[j][None].astype(BF16),
                      dense_w_down[j][None].astype(BF16), None, dense=True, tf=dense_w_gate.shape[2],
                      ple=ple, final=final)
        else:
            x2 = _moe(x2, ffn_norm[i], router_w[j], moe_w_gate[j].astype(BF16), moe_w_up[j].astype(BF16),
                      moe_w_down[j].astype(BF16), ple, final)
        x = x2.reshape(b_, t_, d_)
    return x
```

```python
import functools

import jax
import jax.numpy as jnp
from jax import lax
from jax.experimental import pallas as pl
from jax.experimental.pallas import tpu as pltpu

F32 = jnp.float32
BF16 = jnp.bfloat16
I32 = jnp.int32

N_ATTN_HEADS = 8
HEAD_DIM = 64
N_KV = 2
HEADS_PER_KV = N_ATTN_HEADS // N_KV
D_ATTN = N_ATTN_HEADS * HEAD_DIM
KV_W = N_KV * HEAD_DIM
N_GATES = 3
L_CMP = 32
STRIDE = 16
L_SLC = 64
N_SEL = 16
N_LOCAL = 2
W_WIN = 512
CMP_HID = 2 * HEAD_DIM
CONV_W = 4
C_LRU = 8.0
N_EXPERTS = 8
RMS_EPS = 1e-6
ATTN_SCALE = HEAD_DIM ** -0.5
LOG2E = 1.4426950408889634
LOG2E_HI = 1.4453125
LOG2E_LO = LOG2E - LOG2E_HI

V7X_LANES = 128
V7X_VMEM_LIMIT_BYTES = 56 * 1024 * 1024

TQ = 128
KT_SLC = 256
KT_WIN = 128
N_WIN_TILES = W_WIN // KT_WIN + 1
V_ROWS = HEAD_DIM + 16
MASK_NEG = -1e30
SEL_NEG = -float(2 ** 20)
ROW_TILE = 512
RGLRU_CHUNK = 256
MOE_ROWS = 512
MOE_TOK_CHUNK = 512
CMB_ROWS = 256


def _cparams(semantics, vmem=None):
    return pltpu.CompilerParams(dimension_semantics=semantics, vmem_limit_bytes=vmem)


def _rms(x, g):
    ms = jnp.mean(x * x, axis=-1, keepdims=True)
    return x * lax.rsqrt(ms + RMS_EPS) * g


def _gelu_tanh(x):
    c = 0.7978845608028654
    return x * (0.5 * (1.0 + jnp.tanh(c * (x + 0.044715 * (x * x * x)))))


def _sigmoid(x):
    return 1.0 / (1.0 + jnp.exp(-x))


def _silu(x):
    return x * _sigmoid(x)


N_TOK_COLS = 2 * KV_W + 2 * KV_W + 2 * 512
N_TR_ROWS = D_ATTN + 2 * KV_W + 32


def _proj_in_kernel(x_ref, g_ref, wtok_ref, wtr_ref,
                    slc_ref, win_ref, kcvc_ref, ks_ref, kw_ref, xr_ref, yg_ref, qT_ref, vsT_ref, vwT_ref, gT_ref):
    hn = _rms(x_ref[0], g_ref[...]).astype(BF16)
    tok = jnp.dot(hn, wtok_ref[...], preferred_element_type=F32)
    kcvc_ref[0] = tok[:, 0:256]
    xr_ref[0] = tok[:, 512:1024]
    yg_ref[0] = tok[:, 1024:1536]
    lane = lax.broadcasted_iota(I32, (tok.shape[0], V7X_LANES), 1)
    slc_c = slc_ref[...].astype(F32)
    win_c = win_ref[...].astype(F32)
    for g in range(N_KV):
        in_g = lambda k2: jnp.where(lane < HEAD_DIM, k2 if g == 0 else pltpu.roll(k2, HEAD_DIM, axis=1), 0.0)
        ks_ref[0, g] = jnp.concatenate([in_g(tok[:, 256:384]) + slc_c[:, :V7X_LANES], slc_c[:, V7X_LANES:]],
                                       axis=1).astype(BF16)
        kw_ref[0, g] = (in_g(tok[:, 384:512]) + win_c).astype(BF16)
    tr = lax.dot_general(wtr_ref[...], hn, (((1,), (1,)), ((), ())),
                         preferred_element_type=F32)
    qT_ref[0] = (tr[0:512] * (ATTN_SCALE * LOG2E)).astype(BF16)
    ones = jnp.ones((V_ROWS - HEAD_DIM, tr.shape[1]), BF16)
    for g in range(N_KV):
        for v_ref, r0 in ((vsT_ref, 512), (vwT_ref, 640)):
            v_ref[0, g, 0:HEAD_DIM, :] = tr[r0 + g * HEAD_DIM:r0 + (g + 1) * HEAD_DIM].astype(BF16)
            v_ref[0, g, HEAD_DIM:V_ROWS, :] = ones
    gT_ref[0] = _sigmoid(tr[768:800])


def _key_position_columns(t_):
    pos = jnp.arange(t_, dtype=I32)
    dup = lambda vals: [v.astype(BF16)[:, None] for v in vals for _ in range(2)]
    pad_to = lambda cols, width: jnp.concatenate(
        cols + [jnp.zeros((t_, width - sum(c.shape[1] for c in cols)), BF16)], axis=1)
    lead = jnp.zeros((t_, HEAD_DIM), BF16)
    slc = pad_to([lead, jax.nn.one_hot(pos // L_SLC, t_ // L_SLC, dtype=BF16)]
                 + dup([pos // L_SLC, pos % L_SLC]), 2 * HEAD_DIM + t_ // L_SLC)
    win = pad_to([lead] + dup([pos % KT_WIN]), 2 * HEAD_DIM)
    return slc, win


def _proj_in(x, norm_g, w_in, slc_cols, win_cols):
    b_, t_, d_ = x.shape
    tm = min(ROW_TILE, t_)
    q, kc, vc, ks, vs, kw, vw, g, xr, yg = jnp.split(
        w_in, [512, 640, 768, 896, 1024, 1152, 1280, 1304, 1816], axis=1)
    wtok = jnp.concatenate([kc, vc, ks, kw, xr, yg], axis=1).astype(BF16)
    g4 = g.reshape(d_, N_KV, HEADS_PER_KV, N_GATES).transpose(0, 1, 3, 2)
    g4 = jnp.pad(g4.reshape(d_, N_KV, 12), ((0, 0), (0, 0), (0, 4))).reshape(d_, 32)
    wtr = jnp.concatenate([q, vs, vw, g4], axis=1).T.astype(BF16)
    nt = t_ // tm
    row = lambda shape: pl.BlockSpec((1, tm, shape), lambda b, i: (b, i, 0))
    col = lambda shape: pl.BlockSpec((1, shape, tm), lambda b, i: (b, 0, i))
    full = lambda a: pl.BlockSpec(a.shape, lambda b, i: (0,) * a.ndim)
    g2 = norm_g.reshape(1, d_)
    ws, ww = slc_cols.shape[1], win_cols.shape[1]
    grow = lambda width: pl.BlockSpec((1, N_KV, tm, width), lambda b, i: (b, 0, i, 0))
    gcol = pl.BlockSpec((1, N_KV, V_ROWS, tm), lambda b, i: (b, 0, 0, i))
    outs = pl.pallas_call(
        _proj_in_kernel,
        name="proj_in",
        grid=(b_, nt),
        in_specs=[row(d_), full(g2), full(wtok), full(wtr),
                  pl.BlockSpec((tm, ws), lambda b, i: (i, 0)), pl.BlockSpec((tm, ww), lambda b, i: (i, 0))],
        out_specs=[row(256), grow(ws), grow(ww), row(512), row(512), col(512), gcol, gcol, col(32)],
        out_shape=[
            jax.ShapeDtypeStruct((b_, t_, 256), F32),
            jax.ShapeDtypeStruct((b_, N_KV, t_, ws), BF16),
            jax.ShapeDtypeStruct((b_, N_KV, t_, ww), BF16),
            jax.ShapeDtypeStruct((b_, t_, 512), F32),
            jax.ShapeDtypeStruct((b_, t_, 512), F32),
            jax.ShapeDtypeStruct((b_, 512, t_), BF16),
            jax.ShapeDtypeStruct((b_, N_KV, V_ROWS, t_), BF16),
            jax.ShapeDtypeStruct((b_, N_KV, V_ROWS, t_), BF16),
            jax.ShapeDtypeStruct((b_, 32, t_), F32),
        ],
        compiler_params=_cparams(("parallel", "parallel"), V7X_VMEM_LIMIT_BYTES),
    )(x, g2, wtok, wtr, slc_cols, win_cols)
    return outs


def _compress_kernel(xc_ref, pa_ref, pb_ref, w1a_ref, w1b_ref, w2_ref, out_ref):
    xc = xc_ref[0].astype(BF16)
    n_chunk = xc.shape[0]
    a = jnp.dot(xc, w1a_ref[...], preferred_element_type=F32)
    bm = jnp.dot(xc, w1b_ref[...], preferred_element_type=F32)
    posc = (jnp.dot(pa_ref[...], w1a_ref[...], preferred_element_type=F32)
            + jnp.dot(pb_ref[...], w1b_ref[...], preferred_element_type=F32))[0:1]
    row = lax.broadcasted_iota(I32, bm.shape, 0)
    bm_up = jnp.where(row < n_chunk - 1, pltpu.roll(bm, n_chunk - 1, axis=0), 0.0)
    hid = _gelu_tanh(a + bm_up + posc).astype(BF16)
    out_ref[0] = jnp.dot(hid, w2_ref[...], preferred_element_type=F32)


def _compress(kcvc, cmp_pos, cmp_w1, cmp_w2):
    b_, t_, _ = kcvc.shape
    n_chunk = t_ // STRIDE
    xc = kcvc.reshape(b_, n_chunk, STRIDE * 256)
    eye = jnp.eye(2 * N_KV, dtype=F32)
    w1 = cmp_w1.reshape(2, 2, STRIDE, HEAD_DIM, CMP_HID)

    def expand(half):
        w = w1[:, half]
        full = jnp.einsum('wldo,wx,gy->lxgdwyo', w, jnp.eye(2, dtype=F32), jnp.eye(2, dtype=F32))
        return full.reshape(STRIDE * 256, 4 * CMP_HID).astype(BF16)

    del eye
    w1a, w1b = expand(0), expand(1)
    pos = cmp_pos.reshape(2, 2, STRIDE, HEAD_DIM)

    def posrow(half):
        p = jnp.broadcast_to(pos[:, half][:, None], (2, N_KV, STRIDE, HEAD_DIM))
        p = p.transpose(2, 0, 1, 3).reshape(1, STRIDE * 256)
        return jnp.pad(p, ((0, 7), (0, 0))).astype(BF16)

    pa, pb = posrow(0), posrow(1)
    w2 = jnp.einsum('whd,wx,gy->wghxyd', cmp_w2, jnp.eye(2, dtype=F32), jnp.eye(2, dtype=F32))
    w2 = w2.reshape(4 * CMP_HID, 4 * HEAD_DIM).astype(BF16)
    full = lambda a: pl.BlockSpec(a.shape, lambda b: (0,) * a.ndim)
    return pl.pallas_call(
        _compress_kernel,
        name="compress_kv",
        grid=(b_,),
        in_specs=[pl.BlockSpec((1, n_chunk, STRIDE * 256), lambda b: (b, 0, 0)),
                  full(pa), full(pb), full(w1a), full(w1b), full(w2)],
        out_specs=pl.BlockSpec((1, n_chunk, 256), lambda b: (b, 0, 0)),
        out_shape=jax.ShapeDtypeStruct((b_, n_chunk, 256), F32),
        compiler_params=_cparams(("parallel",), V7X_VMEM_LIMIT_BYTES),
    )(xc, pa, pb, w1a, w1b, w2)


def _head_lanes(rows):
    r = rows.shape[0] // HEADS_PER_KV
    return jnp.concatenate([rows[h * r:(h + 1) * r, :] for h in range(HEADS_PER_KV)], axis=1)


def _attn_kernel(qT_ref, gT_ref, kc_ref, vcT_ref, ks_ref, vsT_ref, kw_ref, vwT_ref, mt_ref, o_ref, s_scr,
                 tile_list):
    i = pl.program_id(1)
    n_cmp = kc_ref.shape[2]
    n_slc = mt_ref.shape[0]
    nl = HEADS_PER_KV * TQ
    hd_rows = HEADS_PER_KV * HEAD_DIM

    lane = lax.broadcasted_iota(I32, (1, nl), 1)
    t_loc = lane % TQ
    t_row = i * TQ + t_loc

    def extra_rows(n_rows, rows):
        ridx = lax.broadcasted_iota(I32, (16, nl), 0)
        out = jnp.zeros((16, nl), F32)
        for k, r in enumerate(rows):
            out = jnp.where(ridx == k, r, out)
        return jnp.concatenate([out.astype(BF16), jnp.zeros((n_rows - 16, nl), BF16)], axis=0)

    ok_c = (lax.broadcasted_iota(I32, (n_cmp, nl), 0) * STRIDE + (L_CMP - 1)) <= t_row
    r_idx = lax.broadcasted_iota(I32, (KT_WIN, nl), 0)
    s_idx = lax.broadcasted_iota(I32, (n_slc, TQ), 0)
    cur = (i * TQ + lax.broadcasted_iota(I32, (1, TQ), 1)) // L_SLC
    valid = s_idx <= cur
    forced = valid & ((s_idx == 0) | (s_idx > cur - N_LOCAL))
    s_idx_f = s_idx.astype(F32)
    n_full = (i * TQ) // KT_SLC
    mt = mt_ref[...]

    def pick(_, score):
        m = jnp.max(score, axis=0, keepdims=True)
        idx = jnp.min(jnp.where(score == m, s_idx_f, 1.0e9), axis=0, keepdims=True)
        return jnp.where(s_idx_f == idx, -2.0, score)

    def stage_a(g, bs, tile, slot, causal):
        k0 = pl.multiple_of(tile * KT_SLC, KT_SLC)
        s = jnp.dot(ks_ref[0, g, pl.ds(k0, KT_SLC), :], bs, preferred_element_type=F32)
        if causal:
            pos = k0 + lax.broadcasted_iota(I32, (KT_SLC, nl), 0)
            s = jnp.where(pos <= t_row, s, MASK_NEG)
        s_scr[g, slot] = s
        return jnp.max(s, axis=0, keepdims=True)

    def stage_b(g, tile, slot, m, acc, m_tile):
        k0 = pl.multiple_of(tile * KT_SLC, KT_SLC)
        m_new = jnp.maximum(m, m_tile)
        p = jnp.exp2(s_scr[g, slot] - m_new)
        acc = jnp.exp2(m - m_new) * acc + jnp.dot(vsT_ref[0, g, :, pl.ds(k0, KT_SLC)], p.astype(BF16),
                                                  preferred_element_type=F32)
        return m_new, acc

    groups = range(N_KV)

    def setup(g):
        q4 = _head_lanes(qT_ref[0, g * hd_rows:(g + 1) * hd_rows, :])
        head = g * HEADS_PER_KV + lane // TQ + 1
        slope = lax.bitcast_convert_type((127 - head) << 23, F32)
        return dict(q4=q4, slope=slope, s_hi=slope * LOG2E_HI, s_lo=slope * LOG2E_LO)

    st = [setup(g) for g in groups]

    def cmp_scores(g):
        s_hi, s_lo = st[g]["s_hi"], st[g]["s_lo"]
        bc = jnp.concatenate(
            [st[g]["q4"], extra_rows(64, [256.0 * s_hi, 256.0 * s_lo, 16.0 * s_hi, 16.0 * s_lo])], axis=0)
        return jnp.dot(kc_ref[0, g], bc, preferred_element_type=F32)

    def cmp_softmax(g, s_c):
        s_c = jnp.where(ok_c, s_c, MASK_NEG)
        m_c = jnp.max(s_c, axis=0, keepdims=True)
        e_c = jnp.exp2(s_c - m_c)
        l_c = jnp.sum(e_c, axis=0, keepdims=True)
        inv_c = jnp.where(m_c > 0.5 * MASK_NEG, 1.0 / jnp.maximum(l_c, 1e-30), 0.0)
        p_c = e_c * inv_c
        o_c = jnp.dot(vcT_ref[0, g], p_c.astype(BF16), preferred_element_type=F32)
        p_grp = (p_c[:, 0:TQ] + p_c[:, TQ:2 * TQ]) + (p_c[:, 2 * TQ:3 * TQ] + p_c[:, 3 * TQ:4 * TQ])
        p1 = p_grp.astype(BF16)
        r1 = p_grp - p1.astype(F32)
        p2 = r1.astype(BF16)
        p3 = (r1 - p2.astype(F32)).astype(BF16)
        p_slc = (jnp.dot(mt, p1, preferred_element_type=F32) + jnp.dot(mt, p2, preferred_element_type=F32)
                 + jnp.dot(mt, p3, preferred_element_type=F32))
        return o_c, p_slc

    def win_scores(g):
        slope = st[g]["slope"]
        bw = jnp.concatenate([st[g]["q4"], extra_rows(64, [st[g]["s_hi"], st[g]["s_lo"]])], axis=0)
        s_tiles, c_tiles = [], []
        for a in range(N_WIN_TILES):
            tile = i - (N_WIN_TILES - 1) + a
            k0 = pl.multiple_of(jnp.maximum(tile, 0) * KT_WIN, KT_WIN)
            s = jnp.dot(kw_ref[0, g, pl.ds(k0, KT_WIN), :], bw, preferred_element_type=F32)
            if a == 0:
                s = jnp.where(r_idx > t_loc, s, MASK_NEG)
            elif a == N_WIN_TILES - 1:
                s = jnp.where(r_idx <= t_loc, s, MASK_NEG)
            s_tiles.append(s)
            c_a = slope * (LOG2E * KT_WIN * (a - (N_WIN_TILES - 1)))
            c_tiles.append(jnp.where(tile >= 0, c_a, MASK_NEG))
        return s_tiles, c_tiles

    def win_softmax(g, s_tiles, c_tiles):
        m_w = s_tiles[0].max(axis=0, keepdims=True) + c_tiles[0]
        for s, c_a in zip(s_tiles[1:], c_tiles[1:]):
            m_w = jnp.maximum(m_w, s.max(axis=0, keepdims=True) + c_a)
        acc_w = jnp.zeros((V_ROWS, nl), F32)
        for a, (s, c_a) in enumerate(zip(s_tiles, c_tiles)):
            tile = i - (N_WIN_TILES - 1) + a
            k0 = pl.multiple_of(jnp.maximum(tile, 0) * KT_WIN, KT_WIN)
            p = jnp.exp2(s - (m_w - c_a))
            acc_w = acc_w + jnp.dot(vwT_ref[0, g, :, pl.ds(k0, KT_WIN)], p.astype(BF16),
                                    preferred_element_type=F32)
        return acc_w[0:HEAD_DIM] * (1.0 / acc_w[HEAD_DIM:HEAD_DIM + 1])

    s_cmp = [cmp_scores(g) for g in groups]
    s_win = [win_scores(g) for g in groups]
    cmp_out, scores = [], []
    for g in groups:
        cmp_out.append(cmp_softmax(g, s_cmp[g]))
        score0 = jnp.where(forced, -2.0, jnp.where(valid, cmp_out[g][1], -1.0))
        scores.append(lax.fori_loop(0, N_SEL - (N_LOCAL + 1), pick, score0, unroll=True))
    o_win = [win_softmax(g, *s_win[g]) for g in groups]
    score = jnp.concatenate(scores, axis=1)

    def select(g):
        sel = valid & (score[:, g * TQ:(g + 1) * TQ] < -1.5)
        sel4 = jnp.concatenate([sel] * HEADS_PER_KV, axis=1)
        blk_rows = jnp.where(sel4, 0.0, SEL_NEG).astype(BF16)
        s_hi, s_lo = st[g]["s_hi"], st[g]["s_lo"]
        bs = jnp.concatenate(
            [st[g]["q4"], blk_rows, extra_rows(64, [L_SLC * s_hi, L_SLC * s_lo, s_hi, s_lo])], axis=0)
        m_tile0 = stage_a(g, bs, n_full, 0, True)
        gt = gT_ref[0, g * 16:(g + 1) * 16, :]
        gates = [_head_lanes(gt[k * HEADS_PER_KV:(k + 1) * HEADS_PER_KV, :]) for k in range(N_GATES)]
        return dict(bs=bs, base=gates[0] * cmp_out[g][0] + gates[2] * o_win[g], g_s=gates[1], m_tile0=m_tile0)

    grp = [select(g) for g in groups]

    blocks_per_tile = KT_SLC // L_SLC
    n_tiles = n_slc // blocks_per_tile
    assert n_tiles <= 32
    valid2 = jnp.concatenate([valid] * N_KV, axis=1)
    sel_all = jnp.where(valid2 & (score < -1.5), 1.0, 0.0).astype(BF16)
    blk_cnt = lax.dot_general(jnp.ones((8, N_KV * TQ), BF16), sel_all, (((1,), (1,)), ((), ())),
                              preferred_element_type=F32)
    memb = jnp.where(lax.broadcasted_iota(I32, (n_slc, V7X_LANES), 0) // blocks_per_tile
                     == lax.broadcasted_iota(I32, (n_slc, V7X_LANES), 1), 1.0, 0.0).astype(BF16)
    tile_cnt = jnp.dot(jnp.where(blk_cnt > 0.5, 1.0, 0.0).astype(BF16), memb,
                       preferred_element_type=F32)[0:1]
    lane_t = lax.broadcasted_iota(I32, (1, V7X_LANES), 1)
    need = (tile_cnt > 0.5) & (lane_t < n_full)
    bit = lax.bitcast_convert_type(((lane_t & 15) + 127) << 23, F32)
    lo_bits = jnp.sum(jnp.where(need & (lane_t < 16), bit, 0.0)).astype(I32)
    hi_bits = jnp.sum(jnp.where(need & (lane_t >= 16), bit, 0.0)).astype(I32)
    n_sel = jnp.int32(0)
    for t in range(n_tiles - 1):
        tile_list[n_sel] = t
        n_sel = n_sel + (((lo_bits if t < 16 else hi_bits) >> (t % 16)) & 1)

    order = lambda k: jnp.where(k == 0, n_full, tile_list[jnp.maximum(k - 1, 0)])

    def half(carry, tile_b, slot_b, tile_a):
        m_next = [c[2] if tile_a is None else stage_a(g, grp[g]["bs"], tile_a, 1 - slot_b, False)
                  for g, c in enumerate(carry)]
        out = []
        for g in range(N_KV):
            m, acc, m_tile = carry[g]
            out.append(stage_b(g, tile_b, slot_b, m, acc, m_tile) + (m_next[g],))
        return tuple(out)

    def body(p, carry):
        k = 2 * p
        return half(half(carry, order(k), 0, order(k + 1)), order(k + 1), 1, order(k + 2))

    init = tuple((jnp.full((1, nl), MASK_NEG, F32), jnp.zeros((V_ROWS, nl), F32), grp[g]["m_tile0"])
                 for g in range(N_KV))
    carry = lax.fori_loop(0, n_sel // 2, body, init)
    k_tail = 2 * (n_sel // 2)
    carry = lax.cond(n_sel % 2 == 1,
                     lambda c: half(half(c, order(k_tail), 0, order(k_tail + 1)), order(k_tail + 1), 1, None),
                     lambda c: half(c, order(k_tail), 0, None), carry)
    for g in range(N_KV):
        _, acc, _ = carry[g]
        o_s = acc[0:HEAD_DIM] * (1.0 / acc[HEAD_DIM:HEAD_DIM + 1])
        out = grp[g]["base"] + grp[g]["g_s"] * o_s
        for h in range(HEADS_PER_KV):
            r0 = g * hd_rows + h * HEAD_DIM
            o_ref[0, r0:r0 + HEAD_DIM, :] = out[:, h * TQ:(h + 1) * TQ].astype(o_ref.dtype)


def _attention(qT, gT, kcmp, ks_aug, vsT4, kw_aug, vwT4):
    b_, _, t_ = qT.shape
    n_cmp = t_ // STRIDE
    n_slc = t_ // L_SLC
    c = jnp.arange(n_cmp, dtype=I32)
    cmp_cols = jnp.concatenate([v.astype(BF16)[:, None] for v in (c // 16, c // 16, c % 16, c % 16)]
                               + [jnp.zeros((n_cmp, HEAD_DIM - 4), BF16)], axis=1)
    kc4 = kcmp[..., 0:KV_W].reshape(b_, n_cmp, N_KV, HEAD_DIM).transpose(0, 2, 1, 3).astype(BF16)
    kc_aug = jnp.concatenate(
        [kc4, jnp.broadcast_to(cmp_cols, (b_, N_KV) + cmp_cols.shape)], axis=-1)
    vcT = kcmp[..., KV_W:2 * KV_W].reshape(b_, n_cmp, N_KV, HEAD_DIM).transpose(0, 2, 3, 1).astype(BF16)
    s = jnp.arange(n_slc, dtype=I32)[:, None]
    cc = c[None, :]
    r_slc = L_SLC // STRIDE
    mt = (((cc >= r_slc * s) & (cc < r_slc * s + r_slc)).astype(F32)
          + ((cc + 1 >= r_slc * s) & (cc + 1 < r_slc * s + r_slc)).astype(F32))
    mt = jnp.where(cc < n_cmp - 1, mt, 0.0).astype(BF16)
    kv_spec = lambda a: pl.BlockSpec((1,) + a.shape[1:], lambda b, i: (b, 0, 0, 0))
    return pl.pallas_call(
        _attn_kernel,
        name="nsa_attention",
        grid=(b_, t_ // TQ),
        in_specs=[pl.BlockSpec((1, D_ATTN, TQ), lambda b, i: (b, 0, i)),
                  pl.BlockSpec((1, 32, TQ), lambda b, i: (b, 0, i)),
                  kv_spec(kc_aug), kv_spec(vcT), kv_spec(ks_aug), kv_spec(vsT4),
                  kv_spec(kw_aug), kv_spec(vwT4),
                  pl.BlockSpec(mt.shape, lambda b, i: (0, 0))],
        out_specs=pl.BlockSpec((1, D_ATTN, TQ), lambda b, i: (b, 0, i)),
        out_shape=jax.ShapeDtypeStruct((b_, D_ATTN, t_), BF16),
        scratch_shapes=[pltpu.VMEM((N_KV, 2, KT_SLC, HEADS_PER_KV * TQ), F32), pltpu.SMEM((32,), I32)],
        compiler_params=_cparams(("parallel", "arbitrary"), V7X_VMEM_LIMIT_BYTES),
    )(qT, gT, kc_aug, vcT, ks_aug, vsT4, kw_aug, vwT4, mt)


def _rglru_seg_pitch(tc):
    return tc // 8 + 8


def _rglru_kernel(xr_ref, yg_ref, cw_ref, cb_ref, wa_ref, ba_ref, wx_ref, bx_ref, lam_ref,
                  o_ref, tail_ref, h_ref, xs_scr, ys_scr, os_scr):
    tc = xr_ref.shape[1]

    @pl.when(pl.program_id(1) == 0)
    def _():
        tail_ref[...] = jnp.zeros_like(tail_ref)
        h_ref[...] = jnp.zeros_like(h_ref)

    n_v = tc // 8
    n_slab = xr_ref.shape[2] // V7X_LANES
    pitch = xs_scr.shape[1] // 8
    for c in range(n_slab):
        for s in range(8):
            lanes = slice(c * V7X_LANES, (c + 1) * V7X_LANES)
            xs_scr[c, s * pitch:s * pitch + n_v, :] = xr_ref[0, s * n_v:(s + 1) * n_v, lanes]
            ys_scr[c, s * pitch:s * pitch + n_v, :] = yg_ref[0, s * n_v:(s + 1) * n_v, lanes]

    def piece(scr, v):
        return jnp.concatenate([scr[c, pl.ds(v, 8, stride=pitch), :] for c in range(n_slab)], axis=1)

    x = [piece(xs_scr, v) for v in range(n_v)]
    row8 = lax.broadcasted_iota(I32, (8, x[0].shape[1]), 0)
    tail = tail_ref[...]
    before = {d: jnp.where(row8 == 0, tail[CONV_W - 1 - d:CONV_W - d, :], pltpu.roll(x[n_v - d], 1, axis=0))
              for d in range(1, CONV_W)}
    tail_ref[0:CONV_W - 1, :] = jnp.concatenate([x[n_v - d][7:8, :] for d in range(CONV_W - 1, 0, -1)], axis=0)
    cw = cw_ref[...]
    cb = cb_ref[...]
    xc = []
    for v in range(n_v):
        acc = x[v] * cw[CONV_W - 1:CONV_W, :] + cb
        for d in range(1, CONV_W):
            acc = acc + (x[v - d] if v >= d else before[d - v]) * cw[CONV_W - 1 - d:CONV_W - d, :]
        xc.append(acc)

    xb = jnp.concatenate(xc, axis=0).astype(BF16)
    r = _sigmoid(jnp.dot(xb, wa_ref[...], preferred_element_type=F32) + ba_ref[...])
    gi = _sigmoid(jnp.dot(xb, wx_ref[...], preferred_element_type=F32) + bx_ref[...])
    z = -lam_ref[...]
    softplus = jnp.maximum(z, 0.0) + jnp.log1p(jnp.exp(-jnp.abs(z)))
    log_a = (-C_LRU * r) * softplus
    a = jnp.exp(log_a)
    th = jnp.tanh(log_a)
    neg_expm1 = 2.0 * th / (th - 1.0)
    bb = jnp.sqrt(neg_expm1) * gi * jnp.concatenate(xc, axis=0)

    a_cum, b_cum = [a[0:8]], [bb[0:8]]
    for v in range(1, n_v):
        a_v = a[v * 8:(v + 1) * 8]
        b_cum.append(a_v * b_cum[-1] + bb[v * 8:(v + 1) * 8])
        a_cum.append(a_v * a_cum[-1])
    h_in = [h_ref[0:1, :]]
    for s in range(7):
        h_in.append(a_cum[-1][s:s + 1, :] * h_in[-1] + b_cum[-1][s:s + 1, :])
    h_ref[0:1, :] = a_cum[-1][7:8, :] * h_in[-1] + b_cum[-1][7:8, :]
    h_in = jnp.concatenate(h_in, axis=0)
    for v in range(n_v):
        out = (a_cum[v] * h_in + b_cum[v]) * _gelu_tanh(piece(ys_scr, v))
        for c in range(n_slab):
            os_scr[c, pl.ds(v, 8, stride=pitch), :] = out[:, c * V7X_LANES:(c + 1) * V7X_LANES]
    for s in range(8):
        o_ref[0, s * n_v:(s + 1) * n_v, :] = jnp.concatenate(
            [os_scr[c, s * pitch:s * pitch + n_v, :] for c in range(n_slab)], axis=1).astype(o_ref.dtype)


def _block_diag(w):
    n, c, d = w.shape
    return jnp.einsum('ncd,nm->ncmd', w, jnp.eye(n, dtype=w.dtype)).reshape(n * c, n * d)


def _rglru(xr, yg, conv_w, conv_b, wa, ba, wx, bx, lam):
    b_, t_, c_ = xr.shape
    tc = min(RGLRU_CHUNK, t_)
    wa_bd = _block_diag(wa).astype(BF16)
    wx_bd = _block_diag(wx).astype(BF16)
    vec = lambda v: v.reshape(1, c_)
    full = lambda a: pl.BlockSpec(a.shape, lambda b, i: (0,) * a.ndim)
    blk = pl.BlockSpec((1, tc, c_), lambda b, i: (b, i, 0))
    args = (xr, yg, conv_w, vec(conv_b), wa_bd, vec(ba), wx_bd, vec(bx), vec(lam))
    return pl.pallas_call(
        _rglru_kernel,
        name="rglru",
        grid=(b_, t_ // tc),
        in_specs=[blk, blk] + [full(a) for a in args[2:]],
        out_specs=blk,
        out_shape=jax.ShapeDtypeStruct((b_, t_, c_), BF16),
        scratch_shapes=[pltpu.VMEM((8, c_), F32), pltpu.VMEM((8, c_), F32)]
                       + [pltpu.VMEM((c_ // V7X_LANES, 8 * _rglru_seg_pitch(tc), V7X_LANES), F32)] * 3,
        compiler_params=_cparams(("parallel", "arbitrary"), V7X_VMEM_LIMIT_BYTES),
    )(*args)


def _out_proj_kernel(oaT_ref, orec_ref, x_ref, ga_ref, gr_ref, wa_ref, wr_ref, o_ref):
    oaT = oaT_ref[0].astype(F32)
    ms = jnp.mean(oaT * oaT, axis=0, keepdims=True)
    na = (oaT * lax.rsqrt(ms + RMS_EPS)).T * ga_ref[...]
    nr = _rms(orec_ref[0].astype(F32), gr_ref[...])
    y = (jnp.dot(na.astype(BF16), wa_ref[...], preferred_element_type=F32)
         + jnp.dot(nr.astype(BF16), wr_ref[...], preferred_element_type=F32))
    o_ref[0] = x_ref[0] + y


def _out_proj(oaT, orec, x, g_attn, g_rec, w_out):
    b_, t_, d_ = x.shape
    tm = min(ROW_TILE, t_)
    wa = w_out[:D_ATTN].astype(BF16)
    wr = w_out[D_ATTN:].astype(BF16)
    ga = g_attn.reshape(1, -1)
    gr = g_rec.reshape(1, -1)
    full = lambda a: pl.BlockSpec(a.shape, lambda b, i: (0,) * a.ndim)
    return pl.pallas_call(
        _out_proj_kernel,
        name="out_proj",
        grid=(b_, t_ // tm),
        in_specs=[pl.BlockSpec((1, D_ATTN, tm), lambda b, i: (b, 0, i)),
                  pl.BlockSpec((1, tm, orec.shape[2]), lambda b, i: (b, i, 0)),
                  pl.BlockSpec((1, tm, d_), lambda b, i: (b, i, 0)),
                  full(ga), full(gr), full(wa), full(wr)],
        out_specs=pl.BlockSpec((1, tm, d_), lambda b, i: (b, i, 0)),
        out_shape=jax.ShapeDtypeStruct((b_, t_, d_), F32),
        compiler_params=_cparams(("parallel", "parallel"), V7X_VMEM_LIMIT_BYTES),
    )(oaT, orec, x, ga, gr, wa, wr)


def _ple_apply(x, p_ref, g_ref, wg_ref, wp_ref, fg_ref, *, final):
    gate = _sigmoid(jnp.dot(_rms(x, g_ref[...]).astype(BF16), wg_ref[...], preferred_element_type=F32))
    proj = jnp.dot(p_ref[...].astype(BF16), wp_ref[...], preferred_element_type=F32)
    y = x + gate * proj
    return _rms(y, fg_ref[...]) if final else y


def _ple_operands(p, norm_g, w_gate, w_proj, final_g, rows, row_map, const_map):
    d_ = w_gate.shape[0]
    arrays = (p, norm_g.reshape(1, d_), w_gate.astype(BF16), w_proj.astype(BF16), final_g.reshape(1, d_))
    once = dict(pipeline_mode=pl.Buffered(1))
    specs = [pl.BlockSpec((rows, p.shape[1]), row_map)] + [pl.BlockSpec(a.shape, const_map, **once)
                                                            for a in arrays[1:]]
    return arrays, specs


def _ffn_kernel(be_ref, x_ref, g_ref, wg_ref, wu_ref, wd_ref, *rest, dense, final):
    ple_refs, (o_ref, hn_ref, acc_ref) = rest[:-3], rest[-3:]
    f = pl.program_id(1)
    used = True if dense else pl.program_id(0) < be_ref[pl.num_programs(0)]

    @pl.when(f == 0)
    def _():
        if dense:
            x = x_ref[...]
            hn_ref[...] = _rms(x, g_ref[...]).astype(BF16)
            acc_ref[...] = x
        else:
            hn_ref[...] = _unpack_bf16_pairs(x_ref[...])
            acc_ref[...] = jnp.zeros_like(acc_ref)

    def accumulate():
        hn = hn_ref[...]
        gate = jnp.dot(hn, wg_ref[0], preferred_element_type=F32)
        up = jnp.dot(hn, wu_ref[0], preferred_element_type=F32)
        act = (_silu(gate) * up).astype(BF16)
        acc_ref[...] += jnp.dot(act, wd_ref[0], preferred_element_type=F32)

    if dense:
        accumulate()
    else:
        pl.when(used)(accumulate)

    @pl.when(f == pl.num_programs(1) - 1)
    def _():
        if dense:
            o_ref[...] = _ple_apply(acc_ref[...], *ple_refs, final=final)
        else:
            o_ref[...] = acc_ref[...].astype(o_ref.dtype)


def _ffn(x, norm_g, wg, wu, wd, block_e, n_used=None, *, dense, tf, ple=None, final=False):
    n = x.shape[0]
    d_ = wg.shape[1]
    tm = min(ROW_TILE if dense else MOE_ROWS, n)
    ff = wg.shape[2]
    assert ff % tf == 0 and n % tm == 0
    if block_e is None:
        block_e = jnp.zeros((n // tm,), I32)
    if n_used is None:
        n_used = jnp.int32(n // tm)
    n_blk, nf = n // tm, ff // tf
    be_all = jnp.concatenate([block_e, jnp.reshape(n_used, (1,)).astype(I32)])
    wmode = dict(pipeline_mode=pl.Buffered(1)) if (wg.shape[0] == 1 and ff == tf) else {}
    ftile = lambda i, f, be: jnp.where(i < be[n_blk], f, nf - 1)
    ple_arrays, ple_specs = ((), [])
    if dense:
        ple_arrays, ple_specs = _ple_operands(*ple, tm, lambda i, f, be: (i, 0), lambda i, f, be: (0, 0))
    grid_spec = pltpu.PrefetchScalarGridSpec(
        num_scalar_prefetch=1,
        grid=(n_blk, nf),
        in_specs=[pl.BlockSpec((tm, x.shape[1]), lambda i, f, be: (i, 0)),
                  pl.BlockSpec((1, d_), lambda i, f, be: (0, 0)),
                  pl.BlockSpec((1, d_, tf), lambda i, f, be: (be[i], 0, ftile(i, f, be)), **wmode),
                  pl.BlockSpec((1, d_, tf), lambda i, f, be: (be[i], 0, ftile(i, f, be)), **wmode),
                  pl.BlockSpec((1, tf, d_), lambda i, f, be: (be[i], ftile(i, f, be), 0), **wmode)] + ple_specs,
        out_specs=pl.BlockSpec((tm, d_), lambda i, f, be: (i, 0)),
        scratch_shapes=[pltpu.VMEM((tm, d_), BF16), pltpu.VMEM((tm, d_), F32)],
    )
    return pl.pallas_call(
        functools.partial(_ffn_kernel, dense=dense, final=final),
        name="ffn_dense" if dense else "ffn_expert",
        grid_spec=grid_spec,
        out_shape=jax.ShapeDtypeStruct((n, d_), F32 if dense else BF16),
        compiler_params=_cparams(("parallel", "arbitrary"), V7X_VMEM_LIMIT_BYTES),
    )(be_all, x, norm_g.reshape(1, d_), wg, wu, wd, *ple_arrays)


def _pack_bf16_pairs(x):
    w = x.shape[1] // 2
    bits = lax.bitcast_convert_type(x.astype(BF16).astype(F32), jnp.uint32)
    return (bits[:, :w] >> 16) | bits[:, w:]


def _unpack_bf16_pairs(words):
    lo = lax.bitcast_convert_type(words << 16, F32)
    hi = lax.bitcast_convert_type(words & jnp.uint32(0xFFFF0000), F32)
    return jnp.concatenate([lo, hi], axis=1).astype(BF16)


def _router_kernel(x_ref, g_ref, wr_ref, tri_ref, h_ref, e_ref, p_ref, r_ref, cb_ref, tot_ref, cnt_ref):
    @pl.when(pl.program_id(0) == 0)
    def _():
        cnt_ref[...] = jnp.zeros_like(cnt_ref)

    hn = _rms(x_ref[...], g_ref[...])
    h_ref[...] = _pack_bf16_pairs(hn)
    logits = lax.dot_general(wr_ref[...], hn, (((1,), (1,)), ((), ())),
                             precision=lax.Precision.HIGHEST, preferred_element_type=F32)
    eidx = lax.broadcasted_iota(I32, logits.shape, 0)
    m1 = jnp.max(logits, axis=0, keepdims=True)
    i1 = jnp.min(jnp.where(logits == m1, eidx, N_EXPERTS), axis=0, keepdims=True)
    rest = jnp.where(eidx == i1, -jnp.inf, logits)
    m2 = jnp.max(rest, axis=0, keepdims=True)
    i2 = jnp.min(jnp.where(rest == m2, eidx, N_EXPERTS), axis=0, keepdims=True)
    e2 = jnp.exp(m2 - m1)
    inv = 1.0 / (1.0 + e2)
    e_ref[...] = jnp.concatenate([i1, i2], axis=0)
    p_ref[...] = jnp.concatenate([inv, e2 * inv], axis=0)

    tm = logits.shape[1]
    chosen = jnp.where((eidx == i1) | (eidx == i2), 1.0, 0.0)
    incl = jnp.dot(chosen.astype(BF16), tri_ref[...], preferred_element_type=F32)
    before = incl - chosen + cnt_ref[:, 0:1]
    r_ref[...] = jnp.concatenate(
        [jnp.sum(jnp.where(eidx == i1, before, 0.0), axis=0, keepdims=True),
         jnp.sum(jnp.where(eidx == i2, before, 0.0), axis=0, keepdims=True)], axis=0).astype(I32)
    lane = lax.broadcasted_iota(I32, cb_ref.shape, 1)
    cb = jnp.zeros(cb_ref.shape, F32)
    for c in range(tm // CMB_ROWS):
        cb = jnp.where(lane == c, before[:, c * CMB_ROWS:c * CMB_ROWS + 1], cb)
    cb_ref[...] = cb.astype(I32)
    total = cnt_ref[...] + incl[:, tm - 1:tm]
    cnt_ref[...] = total
    tot_ref[...] = total.astype(I32)


def _router(x, norm_g, router_w):
    n, d_ = x.shape
    tm = min(ROW_TILE, n)
    assert tm % CMB_ROWS == 0
    wr = router_w.T
    tri = (jnp.arange(tm)[:, None] <= jnp.arange(tm)[None, :]).astype(BF16)
    h, top_e, top_p, rank, cb, tot = pl.pallas_call(
        _router_kernel,
        name="router",
        grid=(n // tm,),
        in_specs=[pl.BlockSpec((tm, d_), lambda i: (i, 0)),
                  pl.BlockSpec((1, d_), lambda i: (0, 0)),
                  pl.BlockSpec(wr.shape, lambda i: (0, 0)),
                  pl.BlockSpec(tri.shape, lambda i: (0, 0))],
        out_specs=[pl.BlockSpec((tm, d_ // 2), lambda i: (i, 0)),
                   pl.BlockSpec((2, tm), lambda i: (0, i)),
                   pl.BlockSpec((2, tm), lambda i: (0, i)),
                   pl.BlockSpec((2, tm), lambda i: (0, i)),
                   pl.BlockSpec((N_EXPERTS, V7X_LANES), lambda i: (0, i)),
                   pl.BlockSpec((N_EXPERTS, V7X_LANES), lambda i: (0, 0))],
        out_shape=[jax.ShapeDtypeStruct((n, d_ // 2), jnp.uint32),
                   jax.ShapeDtypeStruct((2, n), I32),
                   jax.ShapeDtypeStruct((2, n), F32),
                   jax.ShapeDtypeStruct((2, n), I32),
                   jax.ShapeDtypeStruct((N_EXPERTS, n // tm * V7X_LANES), I32),
                   jax.ShapeDtypeStruct((N_EXPERTS, V7X_LANES), I32)],
        scratch_shapes=[pltpu.VMEM((N_EXPERTS, V7X_LANES), F32)],
        compiler_params=_cparams(("arbitrary",), V7X_VMEM_LIMIT_BYTES),
    )(x, norm_g.reshape(1, d_), wr, tri)
    cb = cb.reshape(N_EXPERTS, n // tm, V7X_LANES)[:, :, :tm // CMB_ROWS].reshape(N_EXPERTS, n // CMB_ROWS).T
    return h, top_e, top_p, rank, cb, tot[:, 0]


SCATTER_UNROLL = 8


def _scatter_kernel(dest_ref, h_ref, zeros_ref, xs_ref, sem):
    del zeros_ref
    n_tok = h_ref.shape[0]

    def row_copy(t, k):
        return pltpu.make_async_copy(h_ref.at[pl.ds(t, 1), :], xs_ref.at[pl.ds(dest_ref[k, t], 1), :], sem)

    def issue(u, carry):
        for r in range(SCATTER_UNROLL):
            t = u * SCATTER_UNROLL + r
            row_copy(t, 0).start()
            row_copy(t, 1).start()
        return carry

    lax.fori_loop(0, n_tok // SCATTER_UNROLL, issue, 0)
    for _ in range(2):
        pltpu.make_async_copy(h_ref, xs_ref.at[pl.ds(0, n_tok), :], sem).wait()


def _scatter(h, dest, n_slots):
    n, w = h.shape
    tc = min(MOE_TOK_CHUNK, n)
    zeros = jnp.zeros((n_slots, w), h.dtype)
    return pl.pallas_call(
        _scatter_kernel,
        name="moe_scatter",
        grid=(n // tc,),
        in_specs=[pl.BlockSpec((2, tc), lambda c: (0, c), memory_space=pltpu.SMEM),
                  pl.BlockSpec((tc, w), lambda c: (c, 0)),
                  pl.BlockSpec(memory_space=pl.ANY)],
        out_specs=pl.BlockSpec(memory_space=pl.ANY),
        out_shape=jax.ShapeDtypeStruct((n_slots, w), h.dtype),
        scratch_shapes=[pltpu.SemaphoreType.DMA(())],
        input_output_aliases={2: 0},
        compiler_params=_cparams(("arbitrary",), V7X_VMEM_LIMIT_BYTES),
    )(dest, h, zeros)


N_CMB_OPS = 2 * N_EXPERTS


def _combine_kernel(jt_ref, vt_ref, x_ref, dest_ref, gate_ref, *rest, final):
    yb_refs = rest[:N_CMB_OPS]
    ple_refs = rest[N_CMB_OPS:-1]
    o_ref = rest[-1]
    c = pl.program_id(0)
    dest = dest_ref[...]
    gate = gate_ref[...]
    lane = lax.broadcasted_iota(I32, (dest.shape[0], CMB_ROWS), 1)
    acc = x_ref[...]
    for k in range(N_CMB_OPS):
        base = jnp.where(vt_ref[c * N_CMB_OPS + k] == 1, jt_ref[c * N_CMB_OPS + k] * CMB_ROWS, -2 * CMB_ROWS)
        rel = dest - base
        w = (jnp.where(lane == rel[:, 0:1], gate[:, 0:1], 0.0)
             + jnp.where(lane == rel[:, 1:2], gate[:, 1:2], 0.0)).astype(BF16)
        acc = acc + jnp.dot(w, yb_refs[k][...], preferred_element_type=F32)
    o_ref[...] = _ple_apply(acc, *ple_refs, final=final)


def _combine(x, yb, dest_t, gate_t, jt, vt, ple, final):
    n, d_ = x.shape
    ple_arrays, ple_specs = _ple_operands(*ple, CMB_ROWS, lambda c, jt, vt: (c, 0), lambda c, jt, vt: (0, 0))
    yb_spec = lambda k: pl.BlockSpec((CMB_ROWS, d_), lambda c, jt, vt, k=k: (jt[c * N_CMB_OPS + k], 0))
    grid_spec = pltpu.PrefetchScalarGridSpec(
        num_scalar_prefetch=2,
        grid=(n // CMB_ROWS,),
        in_specs=[pl.BlockSpec((CMB_ROWS, d_), lambda c, jt, vt: (c, 0)),
                  pl.BlockSpec((CMB_ROWS, 2), lambda c, jt, vt: (c, 0)),
                  pl.BlockSpec((CMB_ROWS, 2), lambda c, jt, vt: (c, 0))]
                 + [yb_spec(k) for k in range(N_CMB_OPS)] + ple_specs,
        out_specs=pl.BlockSpec((CMB_ROWS, d_), lambda c, jt, vt: (c, 0)),
    )
    return pl.pallas_call(
        functools.partial(_combine_kernel, final=final),
        name="moe_combine",
        grid_spec=grid_spec,
        out_shape=jax.ShapeDtypeStruct((n, d_), F32),
        compiler_params=_cparams(("parallel",), V7X_VMEM_LIMIT_BYTES),
    )(jt, vt, x, dest_t, gate_t, *([yb] * N_CMB_OPS), *ple_arrays)


def _moe_plan(top_e, rank, cb, counts, n_slots):
    padded = (counts + MOE_ROWS - 1) // MOE_ROWS * MOE_ROWS
    pad_ends = jnp.cumsum(padded)
    pad_starts = pad_ends - padded
    start_of = sum(jnp.where(top_e == e, pad_starts[e], 0) for e in range(N_EXPERTS))
    dest = (start_of + rank).astype(I32)
    n_blk = n_slots // MOE_ROWS
    blk_start = jnp.arange(n_blk, dtype=I32) * MOE_ROWS
    block_e = jnp.minimum(jnp.sum((pad_ends[None, :] <= blk_start[:, None]).astype(I32), axis=1),
                          N_EXPERTS - 1).astype(I32)

    cb = jnp.concatenate([cb, counts[None]], axis=0)
    lo = pad_starts[None] + cb[:-1]
    hi = pad_starts[None] + cb[1:]
    ja = lo // CMB_ROWS
    jb = (hi - 1) // CMB_ROWS
    va = hi > lo
    vb = va & (jb > ja)
    jt = jnp.stack([ja, jb], axis=-1).reshape(-1)
    vt = jnp.stack([va, vb], axis=-1).reshape(-1)
    jt = jnp.where(vt, jt, 0).astype(I32)
    return dest, block_e, pad_ends[-1] // MOE_ROWS, jt, vt.astype(I32)


def _moe(x, norm_g, router_w, wg, wu, wd, ple, final):
    n, d_ = x.shape
    n_slots = (2 * n // MOE_ROWS + N_EXPERTS) * MOE_ROWS
    h, top_e, top_p, rank, cb, counts = _router(x, norm_g, router_w)
    dest, block_e, n_used, jt, vt = _moe_plan(top_e, rank, cb, counts, n_slots)
    xs = _scatter(h, dest, n_slots)
    yb = _ffn(xs, norm_g, wg, wu, wd, block_e, n_used, dense=False, tf=wg.shape[2] // 2)
    return _combine(x, yb, dest.T, top_p.T, jt, vt, ple, final)


def kernel(x, p, attn_norm, w_in, cmp_pos, cmp_w1, cmp_w2, conv_w, conv_b, lru_wa, lru_ba, lru_wx, lru_bx,
           lru_lambda, out_norm_attn, out_norm_rec, w_out, ffn_norm, dense_w_gate, dense_w_up, dense_w_down,
           router_w, moe_w_gate, moe_w_up, moe_w_down, ple_norm, ple_w_gate, ple_w_proj, final_norm):
    b_, t_, d_ = x.shape
    depth = w_in.shape[0]
    n = b_ * t_
    slc_cols, win_cols = _key_position_columns(t_)
    for i in range(depth):
        kcvc, ks, kw, xr, yg, qT, vsT, vwT, gT = _proj_in(x, attn_norm[i], w_in[i], slc_cols, win_cols)
        kcmp = _compress(kcvc, cmp_pos[i], cmp_w1[i], cmp_w2[i])
        oaT = _attention(qT, gT, kcmp, ks, vsT, kw, vwT)
        orec = _rglru(xr, yg, conv_w[i], conv_b[i], lru_wa[i], lru_ba[i], lru_wx[i], lru_bx[i], lru_lambda[i])
        x = _out_proj(oaT, orec, x, out_norm_attn[i], out_norm_rec[i], w_out[i])
        x2 = x.reshape(n, d_)
        j = i // 2
        ple = (p[i].reshape(n, -1), ple_norm[i], ple_w_gate[i], ple_w_proj[i], final_norm)
        final = i == depth - 1
        if i % 2 == 0:
            x2 = _ffn(x2, ffn_norm[i], dense_w_gate[j][None].astype(BF16), dense_w_up[j][None].astype(BF16),
                      dense_w_down[j][None].astype(BF16), None, dense=True, tf=dense_w_gate.shape[2],
                      ple=ple, final=final)
        else:
            x2 = _moe(x2, ffn_norm[i], router_w[j], moe_w_gate[j].astype(BF16), moe_w_up[j].astype(BF16),
                      moe_w_down[j].astype(BF16), ple, final)
        x = x2.reshape(b_, t_, d_)
    return x
```

```python
import functools

import jax
import jax.numpy as jnp
from jax import lax
from jax.experimental import pallas as pl
from jax.experimental.pallas import tpu as pltpu

F32 = jnp.float32
BF16 = jnp.bfloat16
I32 = jnp.int32

N_ATTN_HEADS = 8
HEAD_DIM = 64
N_KV = 2
HEADS_PER_KV = N_ATTN_HEADS // N_KV
D_ATTN = N_ATTN_HEADS * HEAD_DIM
KV_W = N_KV * HEAD_DIM
N_GATES = 3
L_CMP = 32
STRIDE = 16
L_SLC = 64
N_SEL = 16
N_LOCAL = 2
W_WIN = 512
CMP_HID = 2 * HEAD_DIM
CONV_W = 4
C_LRU = 8.0
N_EXPERTS = 8
RMS_EPS = 1e-6
ATTN_SCALE = HEAD_DIM ** -0.5
LOG2E = 1.4426950408889634
LOG2E_HI = 1.4453125
LOG2E_LO = LOG2E - LOG2E_HI

V7X_LANES = 128
V7X_VMEM_LIMIT_BYTES = 56 * 1024 * 1024

TQ = 128
KT_SLC = 256
SLC_UNROLL = 4
KT_WIN = 128
N_WIN_TILES = W_WIN // KT_WIN + 1
V_ROWS = HEAD_DIM + 16
MASK_NEG = -1e30
SEL_NEG = -float(2 ** 20)
ROW_TILE = 512
RGLRU_CHUNK = 256
MOE_ROWS = 512
MOE_TOK_CHUNK = 512
CMB_ROWS = 256


def _cparams(semantics, vmem=None):
    return pltpu.CompilerParams(dimension_semantics=semantics, vmem_limit_bytes=vmem)


def _rms(x, g):
    ms = jnp.mean(x * x, axis=-1, keepdims=True)
    return x * lax.rsqrt(ms + RMS_EPS) * g


def _gelu_tanh(x):
    c = 0.7978845608028654
    return x * (0.5 * (1.0 + jnp.tanh(c * (x + 0.044715 * (x * x * x)))))


def _sigmoid(x):
    return 1.0 / (1.0 + jnp.exp(-x))


def _silu(x):
    return x * _sigmoid(x)


N_TOK_COLS = 2 * KV_W + 2 * KV_W + 2 * 512
N_TR_ROWS = D_ATTN + 2 * KV_W + 32


def _proj_in_kernel(x_ref, g_ref, wtok_ref, wtr_ref,
                    slc_ref, win_ref, kcvc_ref, ks_ref, kw_ref, xr_ref, yg_ref, qT_ref, vsT_ref, vwT_ref, gT_ref):
    hn = _rms(x_ref[0], g_ref[...]).astype(BF16)
    tok = jnp.dot(hn, wtok_ref[...], preferred_element_type=F32)
    kcvc_ref[0] = tok[:, 0:256]
    xr_ref[0] = tok[:, 512:1024].astype(BF16)
    yg_ref[0] = tok[:, 1024:1536].astype(BF16)
    lane = lax.broadcasted_iota(I32, (tok.shape[0], V7X_LANES), 1)
    slc_c = slc_ref[...].astype(F32)
    win_c = win_ref[...].astype(F32)
    for g in range(N_KV):
        in_g = lambda k2: jnp.where(lane < HEAD_DIM, k2 if g == 0 else pltpu.roll(k2, HEAD_DIM, axis=1), 0.0)
        ks_ref[0, g] = jnp.concatenate([in_g(tok[:, 256:384]) + slc_c[:, :V7X_LANES], slc_c[:, V7X_LANES:]],
                                       axis=1).astype(BF16)
        kw_ref[0, g] = (in_g(tok[:, 384:512]) + win_c).astype(BF16)
    tr = lax.dot_general(wtr_ref[...], hn, (((1,), (1,)), ((), ())),
                         preferred_element_type=F32)
    qT_ref[0] = (tr[0:512] * (ATTN_SCALE * LOG2E)).astype(BF16)
    ones = jnp.ones((V_ROWS - HEAD_DIM, tr.shape[1]), BF16)
    for g in range(N_KV):
        for v_ref, r0 in ((vsT_ref, 512), (vwT_ref, 640)):
            v_ref[0, g, 0:HEAD_DIM, :] = tr[r0 + g * HEAD_DIM:r0 + (g + 1) * HEAD_DIM].astype(BF16)
            v_ref[0, g, HEAD_DIM:V_ROWS, :] = ones
    gT_ref[0] = _sigmoid(tr[768:800])


def _key_position_columns(t_):
    pos = jnp.arange(t_, dtype=I32)
    dup = lambda vals: [v.astype(BF16)[:, None] for v in vals for _ in range(2)]
    pad_to = lambda cols, width: jnp.concatenate(
        cols + [jnp.zeros((t_, width - sum(c.shape[1] for c in cols)), BF16)], axis=1)
    lead = jnp.zeros((t_, HEAD_DIM), BF16)
    slc = pad_to([lead, jax.nn.one_hot(pos // L_SLC, t_ // L_SLC, dtype=BF16)]
                 + dup([pos // L_SLC, pos % L_SLC]), 2 * HEAD_DIM + t_ // L_SLC)
    win = pad_to([lead] + dup([pos % KT_WIN]), 2 * HEAD_DIM)
    return slc, win


def _proj_in(x, norm_g, w_in, slc_cols, win_cols):
    b_, t_, d_ = x.shape
    tm = min(ROW_TILE, t_)
    q, kc, vc, ks, vs, kw, vw, g, xr, yg = jnp.split(
        w_in, [512, 640, 768, 896, 1024, 1152, 1280, 1304, 1816], axis=1)
    wtok = jnp.concatenate([kc, vc, ks, kw, xr, yg], axis=1).astype(BF16)
    g4 = g.reshape(d_, N_KV, HEADS_PER_KV, N_GATES).transpose(0, 1, 3, 2)
    g4 = jnp.pad(g4.reshape(d_, N_KV, 12), ((0, 0), (0, 0), (0, 4))).reshape(d_, 32)
    wtr = jnp.concatenate([q, vs, vw, g4], axis=1).T.astype(BF16)
    nt = t_ // tm
    row = lambda shape: pl.BlockSpec((1, tm, shape), lambda b, i: (b, i, 0))
    col = lambda shape: pl.BlockSpec((1, shape, tm), lambda b, i: (b, 0, i))
    full = lambda a: pl.BlockSpec(a.shape, lambda b, i: (0,) * a.ndim)
    g2 = norm_g.reshape(1, d_)
    ws, ww = slc_cols.shape[1], win_cols.shape[1]
    grow = lambda width: pl.BlockSpec((1, N_KV, tm, width), lambda b, i: (b, 0, i, 0))
    gcol = pl.BlockSpec((1, N_KV, V_ROWS, tm), lambda b, i: (b, 0, 0, i))
    outs = pl.pallas_call(
        _proj_in_kernel,
        name="proj_in",
        grid=(b_, nt),
        in_specs=[row(d_), full(g2), full(wtok), full(wtr),
                  pl.BlockSpec((tm, ws), lambda b, i: (i, 0)), pl.BlockSpec((tm, ww), lambda b, i: (i, 0))],
        out_specs=[row(256), grow(ws), grow(ww), row(512), row(512), col(512), gcol, gcol, col(32)],
        out_shape=[
            jax.ShapeDtypeStruct((b_, t_, 256), F32),
            jax.ShapeDtypeStruct((b_, N_KV, t_, ws), BF16),
            jax.ShapeDtypeStruct((b_, N_KV, t_, ww), BF16),
            jax.ShapeDtypeStruct((b_, t_, 512), BF16),
            jax.ShapeDtypeStruct((b_, t_, 512), BF16),
            jax.ShapeDtypeStruct((b_, 512, t_), BF16),
            jax.ShapeDtypeStruct((b_, N_KV, V_ROWS, t_), BF16),
            jax.ShapeDtypeStruct((b_, N_KV, V_ROWS, t_), BF16),
            jax.ShapeDtypeStruct((b_, 32, t_), F32),
        ],
        compiler_params=_cparams(("parallel", "parallel"), V7X_VMEM_LIMIT_BYTES),
    )(x, g2, wtok, wtr, slc_cols, win_cols)
    return outs


def _compress_kernel(xc_ref, pa_ref, pb_ref, w1a_ref, w1b_ref, w2_ref, out_ref):
    xc = xc_ref[0].astype(BF16)
    n_chunk = xc.shape[0]
    a = jnp.dot(xc, w1a_ref[...], preferred_element_type=F32)
    bm = jnp.dot(xc, w1b_ref[...], preferred_element_type=F32)
    posc = (jnp.dot(pa_ref[...], w1a_ref[...], preferred_element_type=F32)
            + jnp.dot(pb_ref[...], w1b_ref[...], preferred_element_type=F32))[0:1]
    row = lax.broadcasted_iota(I32, bm.shape, 0)
    bm_up = jnp.where(row < n_chunk - 1, pltpu.roll(bm, n_chunk - 1, axis=0), 0.0)
    hid = _gelu_tanh(a + bm_up + posc).astype(BF16)
    out_ref[0] = jnp.dot(hid, w2_ref[...], preferred_element_type=F32)


def _compress(kcvc, cmp_pos, cmp_w1, cmp_w2):
    b_, t_, _ = kcvc.shape
    n_chunk = t_ // STRIDE
    xc = kcvc.reshape(b_, n_chunk, STRIDE * 256)
    eye = jnp.eye(2 * N_KV, dtype=F32)
    w1 = cmp_w1.reshape(2, 2, STRIDE, HEAD_DIM, CMP_HID)

    def expand(half):
        w = w1[:, half]
        full = jnp.einsum('wldo,wx,gy->lxgdwyo', w, jnp.eye(2, dtype=F32), jnp.eye(2, dtype=F32))
        return full.reshape(STRIDE * 256, 4 * CMP_HID).astype(BF16)

    del eye
    w1a, w1b = expand(0), expand(1)
    pos = cmp_pos.reshape(2, 2, STRIDE, HEAD_DIM)

    def posrow(half):
        p = jnp.broadcast_to(pos[:, half][:, None], (2, N_KV, STRIDE, HEAD_DIM))
        p = p.transpose(2, 0, 1, 3).reshape(1, STRIDE * 256)
        return jnp.pad(p, ((0, 7), (0, 0))).astype(BF16)

    pa, pb = posrow(0), posrow(1)
    w2 = jnp.einsum('whd,wx,gy->wghxyd', cmp_w2, jnp.eye(2, dtype=F32), jnp.eye(2, dtype=F32))
    w2 = w2.reshape(4 * CMP_HID, 4 * HEAD_DIM).astype(BF16)
    full = lambda a: pl.BlockSpec(a.shape, lambda b: (0,) * a.ndim)
    return pl.pallas_call(
        _compress_kernel,
        name="compress_kv",
        grid=(b_,),
        in_specs=[pl.BlockSpec((1, n_chunk, STRIDE * 256), lambda b: (b, 0, 0)),
                  full(pa), full(pb), full(w1a), full(w1b), full(w2)],
        out_specs=pl.BlockSpec((1, n_chunk, 256), lambda b: (b, 0, 0)),
        out_shape=jax.ShapeDtypeStruct((b_, n_chunk, 256), F32),
        compiler_params=_cparams(("parallel",), V7X_VMEM_LIMIT_BYTES),
    )(xc, pa, pb, w1a, w1b, w2)


def _head_lanes(rows):
    r = rows.shape[0] // HEADS_PER_KV
    return jnp.concatenate([rows[h * r:(h + 1) * r, :] for h in range(HEADS_PER_KV)], axis=1)


def _attn_kernel(qT_ref, gT_ref, kc_ref, vcT_ref, ks_ref, vsT_ref, kw_ref, vwT_ref, mt_ref, o_ref, s_scr,
                 tile_list):
    i = pl.program_id(1)
    n_cmp = kc_ref.shape[2]
    n_slc = mt_ref.shape[0]
    nl = HEADS_PER_KV * TQ
    hd_rows = HEADS_PER_KV * HEAD_DIM

    lane = lax.broadcasted_iota(I32, (1, nl), 1)
    t_loc = lane % TQ
    t_row = i * TQ + t_loc

    def extra_rows(n_rows, rows):
        ridx = lax.broadcasted_iota(I32, (16, nl), 0)
        out = jnp.zeros((16, nl), F32)
        for k, r in enumerate(rows):
            out = jnp.where(ridx == k, r, out)
        return jnp.concatenate([out.astype(BF16), jnp.zeros((n_rows - 16, nl), BF16)], axis=0)

    ok_c = (lax.broadcasted_iota(I32, (n_cmp, nl), 0) * STRIDE + (L_CMP - 1)) <= t_row
    r_idx = lax.broadcasted_iota(I32, (KT_WIN, nl), 0)
    s_idx = lax.broadcasted_iota(I32, (n_slc, TQ), 0)
    cur = (i * TQ + lax.broadcasted_iota(I32, (1, TQ), 1)) // L_SLC
    valid = s_idx <= cur
    forced = valid & ((s_idx == 0) | (s_idx > cur - N_LOCAL))
    s_idx_f = s_idx.astype(F32)
    n_full = (i * TQ) // KT_SLC
    mt = mt_ref[...]

    def pick(_, score):
        m = jnp.max(score, axis=0, keepdims=True)
        idx = jnp.min(jnp.where(score == m, s_idx_f, 1.0e9), axis=0, keepdims=True)
        return jnp.where(s_idx_f == idx, -2.0, score)

    def stage_a(g, bs, tile, slot, causal):
        k0 = pl.multiple_of(tile * KT_SLC, KT_SLC)
        s = jnp.dot(ks_ref[0, g, pl.ds(k0, KT_SLC), :], bs, preferred_element_type=F32)
        if causal:
            pos = k0 + lax.broadcasted_iota(I32, (KT_SLC, nl), 0)
            s = jnp.where(pos <= t_row, s, MASK_NEG)
        s_scr[g, slot] = s
        return jnp.max(s, axis=0, keepdims=True)

    def stage_b(g, tile, slot, m, acc, m_tile):
        k0 = pl.multiple_of(tile * KT_SLC, KT_SLC)
        m_new = jnp.maximum(m, m_tile)
        p = jnp.exp2(s_scr[g, slot] - m_new)
        acc = jnp.exp2(m - m_new) * acc + jnp.dot(vsT_ref[0, g, :, pl.ds(k0, KT_SLC)], p.astype(BF16),
                                                  preferred_element_type=F32)
        return m_new, acc

    groups = range(N_KV)

    def setup(g):
        q4 = _head_lanes(qT_ref[0, g * hd_rows:(g + 1) * hd_rows, :])
        head = g * HEADS_PER_KV + lane // TQ + 1
        slope = lax.bitcast_convert_type((127 - head) << 23, F32)
        return dict(q4=q4, slope=slope, s_hi=slope * LOG2E_HI, s_lo=slope * LOG2E_LO)

    st = [setup(g) for g in groups]

    def cmp_scores(g):
        s_hi, s_lo = st[g]["s_hi"], st[g]["s_lo"]
        bc = jnp.concatenate(
            [st[g]["q4"], extra_rows(64, [256.0 * s_hi, 256.0 * s_lo, 16.0 * s_hi, 16.0 * s_lo])], axis=0)
        return jnp.dot(kc_ref[0, g], bc, preferred_element_type=F32)

    def cmp_softmax(g, s_c):
        s_c = jnp.where(ok_c, s_c, MASK_NEG)
        m_c = jnp.max(s_c, axis=0, keepdims=True)
        e_c = jnp.exp2(s_c - m_c)
        l_c = jnp.sum(e_c, axis=0, keepdims=True)
        inv_c = jnp.where(m_c > 0.5 * MASK_NEG, 1.0 / jnp.maximum(l_c, 1e-30), 0.0)
        p_c = e_c * inv_c
        o_c = jnp.dot(vcT_ref[0, g], p_c.astype(BF16), preferred_element_type=F32)
        p_grp = (p_c[:, 0:TQ] + p_c[:, TQ:2 * TQ]) + (p_c[:, 2 * TQ:3 * TQ] + p_c[:, 3 * TQ:4 * TQ])
        p1 = p_grp.astype(BF16)
        r1 = p_grp - p1.astype(F32)
        p2 = r1.astype(BF16)
        p3 = (r1 - p2.astype(F32)).astype(BF16)
        p_slc = (jnp.dot(mt, p1, preferred_element_type=F32) + jnp.dot(mt, p2, preferred_element_type=F32)
                 + jnp.dot(mt, p3, preferred_element_type=F32))
        return o_c, p_slc

    def win_scores(g):
        slope = st[g]["slope"]
        bw = jnp.concatenate([st[g]["q4"], extra_rows(64, [st[g]["s_hi"], st[g]["s_lo"]])], axis=0)
        s_tiles, c_tiles = [], []
        for a in range(N_WIN_TILES):
            tile = i - (N_WIN_TILES - 1) + a
            k0 = pl.multiple_of(jnp.maximum(tile, 0) * KT_WIN, KT_WIN)
            s = jnp.dot(kw_ref[0, g, pl.ds(k0, KT_WIN), :], bw, preferred_element_type=F32)
            if a == 0:
                s = jnp.where(r_idx > t_loc, s, MASK_NEG)
            elif a == N_WIN_TILES - 1:
                s = jnp.where(r_idx <= t_loc, s, MASK_NEG)
            s_tiles.append(s)
            c_a = slope * (LOG2E * KT_WIN * (a - (N_WIN_TILES - 1)))
            c_tiles.append(jnp.where(tile >= 0, c_a, MASK_NEG))
        return s_tiles, c_tiles

    def win_softmax(g, s_tiles, c_tiles):
        m_w = s_tiles[0].max(axis=0, keepdims=True) + c_tiles[0]
        for s, c_a in zip(s_tiles[1:], c_tiles[1:]):
            m_w = jnp.maximum(m_w, s.max(axis=0, keepdims=True) + c_a)
        acc_w = jnp.zeros((V_ROWS, nl), F32)
        for a, (s, c_a) in enumerate(zip(s_tiles, c_tiles)):
            tile = i - (N_WIN_TILES - 1) + a
            k0 = pl.multiple_of(jnp.maximum(tile, 0) * KT_WIN, KT_WIN)
            p = jnp.exp2(s - (m_w - c_a))
            acc_w = acc_w + jnp.dot(vwT_ref[0, g, :, pl.ds(k0, KT_WIN)], p.astype(BF16),
                                    preferred_element_type=F32)
        return acc_w[0:HEAD_DIM] * (1.0 / acc_w[HEAD_DIM:HEAD_DIM + 1])

    s_cmp = [cmp_scores(g) for g in groups]
    s_win = [win_scores(g) for g in groups]
    cmp_out, scores = [], []
    for g in groups:
        cmp_out.append(cmp_softmax(g, s_cmp[g]))
        score0 = jnp.where(forced, -2.0, jnp.where(valid, cmp_out[g][1], -1.0))
        scores.append(lax.fori_loop(0, N_SEL - (N_LOCAL + 1), pick, score0, unroll=True))
    o_win = [win_softmax(g, *s_win[g]) for g in groups]
    score = jnp.concatenate(scores, axis=1)

    def select(g):
        sel = valid & (score[:, g * TQ:(g + 1) * TQ] < -1.5)
        sel4 = jnp.concatenate([sel] * HEADS_PER_KV, axis=1)
        blk_rows = jnp.where(sel4, 0.0, SEL_NEG).astype(BF16)
        s_hi, s_lo = st[g]["s_hi"], st[g]["s_lo"]
        bs = jnp.concatenate(
            [st[g]["q4"], blk_rows, extra_rows(64, [L_SLC * s_hi, L_SLC * s_lo, s_hi, s_lo])], axis=0)
        m_tile0 = stage_a(g, bs, n_full, 0, True)
        gt = gT_ref[0, g * 16:(g + 1) * 16, :]
        gates = [_head_lanes(gt[k * HEADS_PER_KV:(k + 1) * HEADS_PER_KV, :]) for k in range(N_GATES)]
        return dict(bs=bs, base=gates[0] * cmp_out[g][0] + gates[2] * o_win[g], g_s=gates[1], m_tile0=m_tile0)

    grp = [select(g) for g in groups]

    blocks_per_tile = KT_SLC // L_SLC
    n_tiles = n_slc // blocks_per_tile
    assert n_tiles <= 32
    valid2 = jnp.concatenate([valid] * N_KV, axis=1)
    sel_all = jnp.where(valid2 & (score < -1.5), 1.0, 0.0).astype(BF16)
    blk_cnt = lax.dot_general(jnp.ones((8, N_KV * TQ), BF16), sel_all, (((1,), (1,)), ((), ())),
                              preferred_element_type=F32)
    memb = jnp.where(lax.broadcasted_iota(I32, (n_slc, V7X_LANES), 0) // blocks_per_tile
                     == lax.broadcasted_iota(I32, (n_slc, V7X_LANES), 1), 1.0, 0.0).astype(BF16)
    tile_cnt = jnp.dot(jnp.where(blk_cnt > 0.5, 1.0, 0.0).astype(BF16), memb,
                       preferred_element_type=F32)[0:1]
    lane_t = lax.broadcasted_iota(I32, (1, V7X_LANES), 1)
    need = (tile_cnt > 0.5) & (lane_t < n_full)
    bit = lax.bitcast_convert_type(((lane_t & 15) + 127) << 23, F32)
    lo_bits = jnp.sum(jnp.where(need & (lane_t < 16), bit, 0.0)).astype(I32)
    hi_bits = jnp.sum(jnp.where(need & (lane_t >= 16), bit, 0.0)).astype(I32)
    n_sel = jnp.int32(0)
    for t in range(n_tiles - 1):
        tile_list[n_sel] = t
        n_sel = n_sel + (((lo_bits if t < 16 else hi_bits) >> (t % 16)) & 1)

    order = lambda k: jnp.where(k == 0, n_full, tile_list[jnp.maximum(k - 1, 0)])

    def half(carry, tile_b, slot_b, tile_a):
        m_next = [c[2] if tile_a is None else stage_a(g, grp[g]["bs"], tile_a, 1 - slot_b, False)
                  for g, c in enumerate(carry)]
        out = []
        for g in range(N_KV):
            m, acc, m_tile = carry[g]
            out.append(stage_b(g, tile_b, slot_b, m, acc, m_tile) + (m_next[g],))
        return tuple(out)

    def run(carry, k0, n, last):
        for j in range(n):
            nxt = None if (last and j == n - 1) else order(k0 + j + 1)
            carry = half(carry, order(k0 + j), j % 2, nxt)
        return carry

    init = tuple((jnp.full((1, nl), MASK_NEG, F32), jnp.zeros((V_ROWS, nl), F32), grp[g]["m_tile0"])
                 for g in range(N_KV))
    carry = lax.fori_loop(0, n_sel // SLC_UNROLL, lambda p, c: run(c, SLC_UNROLL * p, SLC_UNROLL, False), init)
    k_tail = SLC_UNROLL * (n_sel // SLC_UNROLL)
    carry = lax.switch(n_sel % SLC_UNROLL, [functools.partial(run, k0=k_tail, n=r + 1, last=True)
                                            for r in range(SLC_UNROLL)], carry)
    for g in range(N_KV):
        _, acc, _ = carry[g]
        o_s = acc[0:HEAD_DIM] * (1.0 / acc[HEAD_DIM:HEAD_DIM + 1])
        out = grp[g]["base"] + grp[g]["g_s"] * o_s
        for h in range(HEADS_PER_KV):
            r0 = g * hd_rows + h * HEAD_DIM
            o_ref[0, r0:r0 + HEAD_DIM, :] = out[:, h * TQ:(h + 1) * TQ].astype(o_ref.dtype)


def _attention(qT, gT, kcmp, ks_aug, vsT4, kw_aug, vwT4):
    b_, _, t_ = qT.shape
    n_cmp = t_ // STRIDE
    n_slc = t_ // L_SLC
    c = jnp.arange(n_cmp, dtype=I32)
    cmp_cols = jnp.concatenate([v.astype(BF16)[:, None] for v in (c // 16, c // 16, c % 16, c % 16)]
                               + [jnp.zeros((n_cmp, HEAD_DIM - 4), BF16)], axis=1)
    kc4 = kcmp[..., 0:KV_W].reshape(b_, n_cmp, N_KV, HEAD_DIM).transpose(0, 2, 1, 3).astype(BF16)
    kc_aug = jnp.concatenate(
        [kc4, jnp.broadcast_to(cmp_cols, (b_, N_KV) + cmp_cols.shape)], axis=-1)
    vcT = kcmp[..., KV_W:2 * KV_W].reshape(b_, n_cmp, N_KV, HEAD_DIM).transpose(0, 2, 3, 1).astype(BF16)
    s = jnp.arange(n_slc, dtype=I32)[:, None]
    cc = c[None, :]
    r_slc = L_SLC // STRIDE
    mt = (((cc >= r_slc * s) & (cc < r_slc * s + r_slc)).astype(F32)
          + ((cc + 1 >= r_slc * s) & (cc + 1 < r_slc * s + r_slc)).astype(F32))
    mt = jnp.where(cc < n_cmp - 1, mt, 0.0).astype(BF16)
    kv_spec = lambda a: pl.BlockSpec((1,) + a.shape[1:], lambda b, i: (b, 0, 0, 0))
    return pl.pallas_call(
        _attn_kernel,
        name="nsa_attention",
        grid=(b_, t_ // TQ),
        in_specs=[pl.BlockSpec((1, D_ATTN, TQ), lambda b, i: (b, 0, i)),
                  pl.BlockSpec((1, 32, TQ), lambda b, i: (b, 0, i)),
                  kv_spec(kc_aug), kv_spec(vcT), kv_spec(ks_aug), kv_spec(vsT4),
                  kv_spec(kw_aug), kv_spec(vwT4),
                  pl.BlockSpec(mt.shape, lambda b, i: (0, 0))],
        out_specs=pl.BlockSpec((1, D_ATTN, TQ), lambda b, i: (b, 0, i)),
        out_shape=jax.ShapeDtypeStruct((b_, D_ATTN, t_), BF16),
        scratch_shapes=[pltpu.VMEM((N_KV, 2, KT_SLC, HEADS_PER_KV * TQ), F32), pltpu.SMEM((32,), I32)],
        compiler_params=_cparams(("parallel", "arbitrary"), V7X_VMEM_LIMIT_BYTES),
    )(qT, gT, kc_aug, vcT, ks_aug, vsT4, kw_aug, vwT4, mt)


def _rglru_seg_pitch(tc):
    return tc // 8 + 8


def _rglru_kernel(xr_ref, yg_ref, cw_ref, cb_ref, wa_ref, ba_ref, wx_ref, bx_ref, lam_ref,
                  o_ref, tail_ref, h_ref, xs_scr, ys_scr, os_scr):
    tc = xr_ref.shape[1]

    @pl.when(pl.program_id(1) == 0)
    def _():
        tail_ref[...] = jnp.zeros_like(tail_ref)
        h_ref[...] = jnp.zeros_like(h_ref)

    n_v = tc // 8
    n_slab = xr_ref.shape[2] // V7X_LANES
    pitch = xs_scr.shape[1] // 8
    for c in range(n_slab):
        for s in range(8):
            lanes = slice(c * V7X_LANES, (c + 1) * V7X_LANES)
            xs_scr[c, s * pitch:s * pitch + n_v, :] = xr_ref[0, s * n_v:(s + 1) * n_v, lanes].astype(F32)
            ys_scr[c, s * pitch:s * pitch + n_v, :] = yg_ref[0, s * n_v:(s + 1) * n_v, lanes].astype(F32)

    def piece(scr, v):
        return jnp.concatenate([scr[c, pl.ds(v, 8, stride=pitch), :] for c in range(n_slab)], axis=1)

    x = [piece(xs_scr, v) for v in range(n_v)]
    row8 = lax.broadcasted_iota(I32, (8, x[0].shape[1]), 0)
    tail = tail_ref[...]
    before = {d: jnp.where(row8 == 0, tail[CONV_W - 1 - d:CONV_W - d, :], pltpu.roll(x[n_v - d], 1, axis=0))
              for d in range(1, CONV_W)}
    tail_ref[0:CONV_W - 1, :] = jnp.concatenate([x[n_v - d][7:8, :] for d in range(CONV_W - 1, 0, -1)], axis=0)
    cw = cw_ref[...]
    cb = cb_ref[...]
    xc = []
    for v in range(n_v):
        acc = x[v] * cw[CONV_W - 1:CONV_W, :] + cb
        for d in range(1, CONV_W):
            acc = acc + (x[v - d] if v >= d else before[d - v]) * cw[CONV_W - 1 - d:CONV_W - d, :]
        xc.append(acc)

    xb = jnp.concatenate(xc, axis=0).astype(BF16)
    r = _sigmoid(jnp.dot(xb, wa_ref[...], preferred_element_type=F32) + ba_ref[...])
    gi = _sigmoid(jnp.dot(xb, wx_ref[...], preferred_element_type=F32) + bx_ref[...])
    z = -lam_ref[...]
    softplus = jnp.maximum(z, 0.0) + jnp.log1p(jnp.exp(-jnp.abs(z)))
    log_a = (-C_LRU * r) * softplus
    a = jnp.exp(log_a)
    th = jnp.tanh(log_a)
    neg_expm1 = 2.0 * th / (th - 1.0)
    bb = jnp.sqrt(neg_expm1) * gi * jnp.concatenate(xc, axis=0)

    a_cum, b_cum = [a[0:8]], [bb[0:8]]
    for v in range(1, n_v):
        a_v = a[v * 8:(v + 1) * 8]
        b_cum.append(a_v * b_cum[-1] + bb[v * 8:(v + 1) * 8])
        a_cum.append(a_v * a_cum[-1])
    h_in = [h_ref[0:1, :]]
    for s in range(7):
        h_in.append(a_cum[-1][s:s + 1, :] * h_in[-1] + b_cum[-1][s:s + 1, :])
    h_ref[0:1, :] = a_cum[-1][7:8, :] * h_in[-1] + b_cum[-1][7:8, :]
    h_in = jnp.concatenate(h_in, axis=0)
    for v in range(n_v):
        out = (a_cum[v] * h_in + b_cum[v]) * _gelu_tanh(piece(ys_scr, v))
        for c in range(n_slab):
            os_scr[c, pl.ds(v, 8, stride=pitch), :] = out[:, c * V7X_LANES:(c + 1) * V7X_LANES]
    for s in range(8):
        o_ref[0, s * n_v:(s + 1) * n_v, :] = jnp.concatenate(
            [os_scr[c, s * pitch:s * pitch + n_v, :] for c in range(n_slab)], axis=1).astype(o_ref.dtype)


def _block_diag(w):
    n, c, d = w.shape
    return jnp.einsum('ncd,nm->ncmd', w, jnp.eye(n, dtype=w.dtype)).reshape(n * c, n * d)


def _rglru(xr, yg, conv_w, conv_b, wa, ba, wx, bx, lam):
    b_, t_, c_ = xr.shape
    tc = min(RGLRU_CHUNK, t_)
    wa_bd = _block_diag(wa).astype(BF16)
    wx_bd = _block_diag(wx).astype(BF16)
    vec = lambda v: v.reshape(1, c_)
    full = lambda a: pl.BlockSpec(a.shape, lambda b, i: (0,) * a.ndim)
    blk = pl.BlockSpec((1, tc, c_), lambda b, i: (b, i, 0))
    args = (xr, yg, conv_w, vec(conv_b), wa_bd, vec(ba), wx_bd, vec(bx), vec(lam))
    return pl.pallas_call(
        _rglru_kernel,
        name="rglru",
        grid=(b_, t_ // tc),
        in_specs=[blk, blk] + [full(a) for a in args[2:]],
        out_specs=blk,
        out_shape=jax.ShapeDtypeStruct((b_, t_, c_), BF16),
        scratch_shapes=[pltpu.VMEM((8, c_), F32), pltpu.VMEM((8, c_), F32)]
                       + [pltpu.VMEM((c_ // V7X_LANES, 8 * _rglru_seg_pitch(tc), V7X_LANES), F32)] * 3,
        compiler_params=_cparams(("parallel", "arbitrary"), V7X_VMEM_LIMIT_BYTES),
    )(*args)


def _out_proj_kernel(oaT_ref, orec_ref, x_ref, ga_ref, gr_ref, wa_ref, wr_ref, o_ref):
    oaT = oaT_ref[0].astype(F32)
    ms = jnp.mean(oaT * oaT, axis=0, keepdims=True)
    na = (oaT * lax.rsqrt(ms + RMS_EPS)).T * ga_ref[...]
    nr = _rms(orec_ref[0].astype(F32), gr_ref[...])
    y = (jnp.dot(na.astype(BF16), wa_ref[...], preferred_element_type=F32)
         + jnp.dot(nr.astype(BF16), wr_ref[...], preferred_element_type=F32))
    o_ref[0] = x_ref[0] + y


def _out_proj(oaT, orec, x, g_attn, g_rec, w_out):
    b_, t_, d_ = x.shape
    tm = min(ROW_TILE, t_)
    wa = w_out[:D_ATTN].astype(BF16)
    wr = w_out[D_ATTN:].astype(BF16)
    ga = g_attn.reshape(1, -1)
    gr = g_rec.reshape(1, -1)
    full = lambda a: pl.BlockSpec(a.shape, lambda b, i: (0,) * a.ndim)
    return pl.pallas_call(
        _out_proj_kernel,
        name="out_proj",
        grid=(b_, t_ // tm),
        in_specs=[pl.BlockSpec((1, D_ATTN, tm), lambda b, i: (b, 0, i)),
                  pl.BlockSpec((1, tm, orec.shape[2]), lambda b, i: (b, i, 0)),
                  pl.BlockSpec((1, tm, d_), lambda b, i: (b, i, 0)),
                  full(ga), full(gr), full(wa), full(wr)],
        out_specs=pl.BlockSpec((1, tm, d_), lambda b, i: (b, i, 0)),
        out_shape=jax.ShapeDtypeStruct((b_, t_, d_), F32),
        compiler_params=_cparams(("parallel", "parallel"), V7X_VMEM_LIMIT_BYTES),
    )(oaT, orec, x, ga, gr, wa, wr)


def _ple_apply(x, p_ref, g_ref, wg_ref, wp_ref, fg_ref, *, final):
    gate = _sigmoid(jnp.dot(_rms(x, g_ref[...]).astype(BF16), wg_ref[...], preferred_element_type=F32))
    proj = jnp.dot(p_ref[...].astype(BF16), wp_ref[...], preferred_element_type=F32)
    y = x + gate * proj
    return _rms(y, fg_ref[...]) if final else y


def _ple_operands(p, norm_g, w_gate, w_proj, final_g, rows, row_map, const_map):
    d_ = w_gate.shape[0]
    arrays = (p, norm_g.reshape(1, d_), w_gate.astype(BF16), w_proj.astype(BF16), final_g.reshape(1, d_))
    once = dict(pipeline_mode=pl.Buffered(1))
    specs = [pl.BlockSpec((rows, p.shape[1]), row_map)] + [pl.BlockSpec(a.shape, const_map, **once)
                                                            for a in arrays[1:]]
    return arrays, specs


def _ffn_kernel(be_ref, x_ref, g_ref, wg_ref, wu_ref, wd_ref, *rest, dense, final):
    ple_refs, (o_ref, hn_ref, acc_ref) = rest[:-3], rest[-3:]
    f = pl.program_id(1)
    used = True if dense else pl.program_id(0) < be_ref[pl.num_programs(0)]

    @pl.when(f == 0)
    def _():
        if dense:
            x = x_ref[...]
            hn_ref[...] = _rms(x, g_ref[...]).astype(BF16)
            acc_ref[...] = x
        else:
            hn_ref[...] = _unpack_bf16_pairs(x_ref[...])
            acc_ref[...] = jnp.zeros_like(acc_ref)

    def accumulate():
        hn = hn_ref[...]
        gate = jnp.dot(hn, wg_ref[0], preferred_element_type=F32)
        up = jnp.dot(hn, wu_ref[0], preferred_element_type=F32)
        act = (_silu(gate) * up).astype(BF16)
        acc_ref[...] += jnp.dot(act, wd_ref[0], preferred_element_type=F32)

    if dense:
        accumulate()
    else:
        pl.when(used)(accumulate)

    @pl.when(f == pl.num_programs(1) - 1)
    def _():
        if dense:
            o_ref[...] = _ple_apply(acc_ref[...], *ple_refs, final=final)
        else:
            o_ref[...] = acc_ref[...].astype(o_ref.dtype)


def _ffn(x, norm_g, wg, wu, wd, block_e, n_used=None, *, dense, tf, ple=None, final=False):
    n = x.shape[0]
    d_ = wg.shape[1]
    tm = min(ROW_TILE if dense else MOE_ROWS, n)
    ff = wg.shape[2]
    assert ff % tf == 0 and n % tm == 0
    if block_e is None:
        block_e = jnp.zeros((n // tm,), I32)
    if n_used is None:
        n_used = jnp.int32(n // tm)
    n_blk, nf = n // tm, ff // tf
    be_all = jnp.concatenate([block_e, jnp.reshape(n_used, (1,)).astype(I32)])
    wmode = dict(pipeline_mode=pl.Buffered(1)) if (wg.shape[0] == 1 and ff == tf) else {}
    ftile = lambda i, f, be: jnp.where(i < be[n_blk], f, nf - 1)
    ple_arrays, ple_specs = ((), [])
    if dense:
        ple_arrays, ple_specs = _ple_operands(*ple, tm, lambda i, f, be: (i, 0), lambda i, f, be: (0, 0))
    grid_spec = pltpu.PrefetchScalarGridSpec(
        num_scalar_prefetch=1,
        grid=(n_blk, nf),
        in_specs=[pl.BlockSpec((tm, x.shape[1]), lambda i, f, be: (i, 0)),
                  pl.BlockSpec((1, d_), lambda i, f, be: (0, 0)),
                  pl.BlockSpec((1, d_, tf), lambda i, f, be: (be[i], 0, ftile(i, f, be)), **wmode),
                  pl.BlockSpec((1, d_, tf), lambda i, f, be: (be[i], 0, ftile(i, f, be)), **wmode),
                  pl.BlockSpec((1, tf, d_), lambda i, f, be: (be[i], ftile(i, f, be), 0), **wmode)] + ple_specs,
        out_specs=pl.BlockSpec((tm, d_), lambda i, f, be: (i, 0)),
        scratch_shapes=[pltpu.VMEM((tm, d_), BF16), pltpu.VMEM((tm, d_), F32)],
    )
    return pl.pallas_call(
        functools.partial(_ffn_kernel, dense=dense, final=final),
        name="ffn_dense" if dense else "ffn_expert",
        grid_spec=grid_spec,
        out_shape=jax.ShapeDtypeStruct((n, d_), F32 if dense else BF16),
        compiler_params=_cparams(("parallel", "arbitrary"), V7X_VMEM_LIMIT_BYTES),
    )(be_all, x, norm_g.reshape(1, d_), wg, wu, wd, *ple_arrays)


def _pack_bf16_pairs(x):
    w = x.shape[1] // 2
    bits = lax.bitcast_convert_type(x.astype(BF16).astype(F32), jnp.uint32)
    return (bits[:, :w] >> 16) | bits[:, w:]


def _unpack_bf16_pairs(words):
    lo = lax.bitcast_convert_type(words << 16, F32)
    hi = lax.bitcast_convert_type(words & jnp.uint32(0xFFFF0000), F32)
    return jnp.concatenate([lo, hi], axis=1).astype(BF16)


def _router_kernel(x_ref, g_ref, wr_ref, tri_ref, h_ref, e_ref, p_ref, r_ref, cb_ref, tot_ref, cnt_ref):
    @pl.when(pl.program_id(0) == 0)
    def _():
        cnt_ref[...] = jnp.zeros_like(cnt_ref)

    hn = _rms(x_ref[...], g_ref[...])
    h_ref[...] = _pack_bf16_pairs(hn)
    logits = lax.dot_general(wr_ref[...], hn, (((1,), (1,)), ((), ())),
                             precision=lax.Precision.HIGHEST, preferred_element_type=F32)
    eidx = lax.broadcasted_iota(I32, logits.shape, 0)
    m1 = jnp.max(logits, axis=0, keepdims=True)
    i1 = jnp.min(jnp.where(logits == m1, eidx, N_EXPERTS), axis=0, keepdims=True)
    rest = jnp.where(eidx == i1, -jnp.inf, logits)
    m2 = jnp.max(rest, axis=0, keepdims=True)
    i2 = jnp.min(jnp.where(rest == m2, eidx, N_EXPERTS), axis=0, keepdims=True)
    e2 = jnp.exp(m2 - m1)
    inv = 1.0 / (1.0 + e2)
    e_ref[...] = jnp.concatenate([i1, i2], axis=0)
    p_ref[...] = jnp.concatenate([inv, e2 * inv], axis=0)

    tm = logits.shape[1]
    chosen = jnp.where((eidx == i1) | (eidx == i2), 1.0, 0.0)
    incl = jnp.dot(chosen.astype(BF16), tri_ref[...], preferred_element_type=F32)
    before = incl - chosen + cnt_ref[:, 0:1]
    r_ref[...] = jnp.concatenate(
        [jnp.sum(jnp.where(eidx == i1, before, 0.0), axis=0, keepdims=True),
         jnp.sum(jnp.where(eidx == i2, before, 0.0), axis=0, keepdims=True)], axis=0).astype(I32)
    lane = lax.broadcasted_iota(I32, cb_ref.shape, 1)
    cb = jnp.zeros(cb_ref.shape, F32)
    for c in range(tm // CMB_ROWS):
        cb = jnp.where(lane == c, before[:, c * CMB_ROWS:c * CMB_ROWS + 1], cb)
    cb_ref[...] = cb.astype(I32)
    total = cnt_ref[...] + incl[:, tm - 1:tm]
    cnt_ref[...] = total
    tot_ref[...] = total.astype(I32)


def _router(x, norm_g, router_w):
    n, d_ = x.shape
    tm = min(ROW_TILE, n)
    assert tm % CMB_ROWS == 0
    wr = router_w.T
    tri = (jnp.arange(tm)[:, None] <= jnp.arange(tm)[None, :]).astype(BF16)
    h, top_e, top_p, rank, cb, tot = pl.pallas_call(
        _router_kernel,
        name="router",
        grid=(n // tm,),
        in_specs=[pl.BlockSpec((tm, d_), lambda i: (i, 0)),
                  pl.BlockSpec((1, d_), lambda i: (0, 0)),
                  pl.BlockSpec(wr.shape, lambda i: (0, 0)),
                  pl.BlockSpec(tri.shape, lambda i: (0, 0))],
        out_specs=[pl.BlockSpec((tm, d_ // 2), lambda i: (i, 0)),
                   pl.BlockSpec((2, tm), lambda i: (0, i)),
                   pl.BlockSpec((2, tm), lambda i: (0, i)),
                   pl.BlockSpec((2, tm), lambda i: (0, i)),
                   pl.BlockSpec((N_EXPERTS, V7X_LANES), lambda i: (0, i)),
                   pl.BlockSpec((N_EXPERTS, V7X_LANES), lambda i: (0, 0))],
        out_shape=[jax.ShapeDtypeStruct((n, d_ // 2), jnp.uint32),
                   jax.ShapeDtypeStruct((2, n), I32),
                   jax.ShapeDtypeStruct((2, n), F32),
                   jax.ShapeDtypeStruct((2, n), I32),
                   jax.ShapeDtypeStruct((N_EXPERTS, n // tm * V7X_LANES), I32),
                   jax.ShapeDtypeStruct((N_EXPERTS, V7X_LANES), I32)],
        scratch_shapes=[pltpu.VMEM((N_EXPERTS, V7X_LANES), F32)],
        compiler_params=_cparams(("arbitrary",), V7X_VMEM_LIMIT_BYTES),
    )(x, norm_g.reshape(1, d_), wr, tri)
    cb = cb.reshape(N_EXPERTS, n // tm, V7X_LANES)[:, :, :tm // CMB_ROWS].reshape(N_EXPERTS, n // CMB_ROWS).T
    return h, top_e, top_p, rank, cb, tot[:, 0]


SCATTER_UNROLL = 8


def _scatter_kernel(dest_ref, h_ref, zeros_ref, xs_ref, sem):
    del zeros_ref
    n_tok = h_ref.shape[0]

    def row_copy(t, k):
        return pltpu.make_async_copy(h_ref.at[pl.ds(t, 1), :], xs_ref.at[pl.ds(dest_ref[k, t], 1), :], sem)

    def issue(u, carry):
        for r in range(SCATTER_UNROLL):
            t = u * SCATTER_UNROLL + r
            row_copy(t, 0).start()
            row_copy(t, 1).start()
        return carry

    lax.fori_loop(0, n_tok // SCATTER_UNROLL, issue, 0)
    for _ in range(2):
        pltpu.make_async_copy(h_ref, xs_ref.at[pl.ds(0, n_tok), :], sem).wait()


def _scatter(h, dest, n_slots):
    n, w = h.shape
    tc = min(MOE_TOK_CHUNK, n)
    zeros = jnp.zeros((n_slots, w), h.dtype)
    return pl.pallas_call(
        _scatter_kernel,
        name="moe_scatter",
        grid=(n // tc,),
        in_specs=[pl.BlockSpec((2, tc), lambda c: (0, c), memory_space=pltpu.SMEM),
                  pl.BlockSpec((tc, w), lambda c: (c, 0)),
                  pl.BlockSpec(memory_space=pl.ANY)],
        out_specs=pl.BlockSpec(memory_space=pl.ANY),
        out_shape=jax.ShapeDtypeStruct((n_slots, w), h.dtype),
        scratch_shapes=[pltpu.SemaphoreType.DMA(())],
        input_output_aliases={2: 0},
        compiler_params=_cparams(("arbitrary",), V7X_VMEM_LIMIT_BYTES),
    )(dest, h, zeros)


N_CMB_OPS = 2 * N_EXPERTS


def _combine_kernel(jt_ref, vt_ref, x_ref, dest_ref, gate_ref, *rest, final):
    yb_refs = rest[:N_CMB_OPS]
    ple_refs = rest[N_CMB_OPS:-1]
    o_ref = rest[-1]
    c = pl.program_id(0)
    dest = dest_ref[...]
    gate = gate_ref[...]
    lane = lax.broadcasted_iota(I32, (dest.shape[0], CMB_ROWS), 1)
    acc = x_ref[...]
    for k in range(N_CMB_OPS):
        base = jnp.where(vt_ref[c * N_CMB_OPS + k] == 1, jt_ref[c * N_CMB_OPS + k] * CMB_ROWS, -2 * CMB_ROWS)
        rel = dest - base
        w = (jnp.where(lane == rel[:, 0:1], gate[:, 0:1], 0.0)
             + jnp.where(lane == rel[:, 1:2], gate[:, 1:2], 0.0)).astype(BF16)
        acc = acc + jnp.dot(w, yb_refs[k][...], preferred_element_type=F32)
    o_ref[...] = _ple_apply(acc, *ple_refs, final=final)


def _combine(x, yb, dest_t, gate_t, jt, vt, ple, final):
    n, d_ = x.shape
    ple_arrays, ple_specs = _ple_operands(*ple, CMB_ROWS, lambda c, jt, vt: (c, 0), lambda c, jt, vt: (0, 0))
    yb_spec = lambda k: pl.BlockSpec((CMB_ROWS, d_), lambda c, jt, vt, k=k: (jt[c * N_CMB_OPS + k], 0))
    grid_spec = pltpu.PrefetchScalarGridSpec(
        num_scalar_prefetch=2,
        grid=(n // CMB_ROWS,),
        in_specs=[pl.BlockSpec((CMB_ROWS, d_), lambda c, jt, vt: (c, 0)),
                  pl.BlockSpec((CMB_ROWS, 2), lambda c, jt, vt: (c, 0)),
                  pl.BlockSpec((CMB_ROWS, 2), lambda c, jt, vt: (c, 0))]
                 + [yb_spec(k) for k in range(N_CMB_OPS)] + ple_specs,
        out_specs=pl.BlockSpec((CMB_ROWS, d_), lambda c, jt, vt: (c, 0)),
    )
    return pl.pallas_call(
        functools.partial(_combine_kernel, final=final),
        name="moe_combine",
        grid_spec=grid_spec,
        out_shape=jax.ShapeDtypeStruct((n, d_), F32),
        compiler_params=_cparams(("parallel",), V7X_VMEM_LIMIT_BYTES),
    )(jt, vt, x, dest_t, gate_t, *([yb] * N_CMB_OPS), *ple_arrays)


def _moe_plan(top_e, rank, cb, counts, n_slots):
    padded = (counts + MOE_ROWS - 1) // MOE_ROWS * MOE_ROWS
    pad_ends = jnp.cumsum(padded)
    pad_starts = pad_ends - padded
    start_of = sum(jnp.where(top_e == e, pad_starts[e], 0) for e in range(N_EXPERTS))
    dest = (start_of + rank).astype(I32)
    n_blk = n_slots // MOE_ROWS
    blk_start = jnp.arange(n_blk, dtype=I32) * MOE_ROWS
    block_e = jnp.minimum(jnp.sum((pad_ends[None, :] <= blk_start[:, None]).astype(I32), axis=1),
                          N_EXPERTS - 1).astype(I32)

    cb = jnp.concatenate([cb, counts[None]], axis=0)
    lo = pad_starts[None] + cb[:-1]
    hi = pad_starts[None] + cb[1:]
    ja = lo // CMB_ROWS
    jb = (hi - 1) // CMB_ROWS
    va = hi > lo
    vb = va & (jb > ja)
    jt = jnp.stack([ja, jb], axis=-1).reshape(-1)
    vt = jnp.stack([va, vb], axis=-1).reshape(-1)
    jt = jnp.where(vt, jt, 0).astype(I32)
    return dest, block_e, pad_ends[-1] // MOE_ROWS, jt, vt.astype(I32)


def _moe(x, norm_g, router_w, wg, wu, wd, ple, final):
    n, d_ = x.shape
    n_slots = (2 * n // MOE_ROWS + N_EXPERTS) * MOE_ROWS
    h, top_e, top_p, rank, cb, counts = _router(x, norm_g, router_w)
    dest, block_e, n_used, jt, vt = _moe_plan(top_e, rank, cb, counts, n_slots)
    xs = _scatter(h, dest, n_slots)
    yb = _ffn(xs, norm_g, wg, wu, wd, block_e, n_used, dense=False, tf=wg.shape[2] // 2)
    return _combine(x, yb, dest.T, top_p.T, jt, vt, ple, final)


def kernel(x, p, attn_norm, w_in, cmp_pos, cmp_w1, cmp_w2, conv_w, conv_b, lru_wa, lru_ba, lru_wx, lru_bx,
           lru_lambda, out_norm_attn, out_norm_rec, w_out, ffn_norm, dense_w_gate, dense_w_up, dense_w_down,
           router_w, moe_w_gate, moe_w_up, moe_w_down, ple_norm, ple_w_gate, ple_w_proj, final_norm):
    b_, t_, d_ = x.shape
    depth = w_in.shape[0]
    n = b_ * t_
    slc_cols, win_cols = _key_position_columns(t_)
    for i in range(depth):
        kcvc, ks, kw, xr, yg, qT, vsT, vwT, gT = _proj_in(x, attn_norm[i], w_in[i], slc_cols, win_cols)
        kcmp = _compress(kcvc, cmp_pos[i], cmp_w1[i], cmp_w2[i])
        oaT = _attention(qT, gT, kcmp, ks, vsT, kw, vwT)
        orec = _rglru(xr, yg, conv_w[i], conv_b[i], lru_wa[i], lru_ba[i], lru_wx[i], lru_bx[i], lru_lambda[i])
        x = _out_proj(oaT, orec, x, out_norm_attn[i], out_norm_rec[i], w_out[i])
        x2 = x.reshape(n, d_)
        j = i // 2
        ple = (p[i].reshape(n, -1), ple_norm[i], ple_w_gate[i], ple_w_proj[i], final_norm)
        final = i == depth - 1
        if i % 2 == 0:
            x2 = _ffn(x2, ffn_norm[i], dense_w_gate[j][None].astype(BF16), dense_w_up[j][None].astype(BF16),
                      dense_w_down[j][None].astype(BF16), None, dense=True, tf=dense_w_gate.shape[2],
                      ple=ple, final=final)
        else:
            x2 = _moe(x2, ffn_norm[i], router_w[j], moe_w_gate[j].astype(BF16), moe_w_up[j].astype(BF16),
                      moe_w_down[j].astype(BF16), ple, final)
        x = x2.reshape(b_, t_, d_)
    return x
```

```python
import functools

import jax
import jax.numpy as jnp
from jax import lax
from jax.experimental import pallas as pl
from jax.experimental.pallas import tpu as pltpu

F32 = jnp.float32
BF16 = jnp.bfloat16
I32 = jnp.int32

N_ATTN_HEADS = 8
HEAD_DIM = 64
N_KV = 2
HEADS_PER_KV = N_ATTN_HEADS // N_KV
D_ATTN = N_ATTN_HEADS * HEAD_DIM
KV_W = N_KV * HEAD_DIM
N_GATES = 3
L_CMP = 32
STRIDE = 16
L_SLC = 64
N_SEL = 16
N_LOCAL = 2
W_WIN = 512
CMP_HID = 2 * HEAD_DIM
CONV_W = 4
C_LRU = 8.0
N_EXPERTS = 8
RMS_EPS = 1e-6
ATTN_SCALE = HEAD_DIM ** -0.5
LOG2E = 1.4426950408889634
LOG2E_HI = 1.4453125
LOG2E_LO = LOG2E - LOG2E_HI

V7X_LANES = 128
V7X_VMEM_LIMIT_BYTES = 56 * 1024 * 1024

TQ = 128
KT_SLC = 256
SLC_UNROLL = 4
KT_WIN = 128
N_WIN_TILES = W_WIN // KT_WIN + 1
V_ROWS = HEAD_DIM + 16
MASK_NEG = -1e30
SEL_NEG = -float(2 ** 20)
ROW_TILE = 512
RGLRU_CHUNK = 256
MOE_ROWS = 512
MOE_TOK_CHUNK = 512
CMB_ROWS = 256


def _cparams(semantics, vmem=None):
    return pltpu.CompilerParams(dimension_semantics=semantics, vmem_limit_bytes=vmem)


def _rms(x, g):
    ms = jnp.mean(x * x, axis=-1, keepdims=True)
    return x * lax.rsqrt(ms + RMS_EPS) * g


def _gelu_tanh(x):
    c = 0.7978845608028654
    return x * (0.5 * (1.0 + jnp.tanh(c * (x + 0.044715 * (x * x * x)))))


def _sigmoid(x):
    return 1.0 / (1.0 + jnp.exp(-x))


def _silu(x):
    return x * _sigmoid(x)


def _proj_in_kernel(x_ref, g_ref, wtok_ref, wtr_ref,
                    slc_ref, win_ref, kcvc_ref, ks_ref, kw_ref, xr_ref, yg_ref, qT_ref, vsT_ref, vwT_ref, gT_ref):
    hn = _rms(x_ref[0], g_ref[...]).astype(BF16)
    tok = jnp.dot(hn, wtok_ref[...], preferred_element_type=F32)
    kcvc_ref[0] = tok[:, 0:256]
    xr_ref[0] = tok[:, 512:1024].astype(BF16)
    yg_ref[0] = tok[:, 1024:1536].astype(BF16)
    lane = lax.broadcasted_iota(I32, (tok.shape[0], V7X_LANES), 1)
    slc_c = slc_ref[...].astype(F32)
    win_c = win_ref[...].astype(F32)
    for g in range(N_KV):
        in_g = lambda k2: jnp.where(lane < HEAD_DIM, k2 if g == 0 else pltpu.roll(k2, HEAD_DIM, axis=1), 0.0)
        ks_ref[0, g] = jnp.concatenate([in_g(tok[:, 256:384]) + slc_c[:, :V7X_LANES], slc_c[:, V7X_LANES:]],
                                       axis=1).astype(BF16)
        kw_ref[0, g] = (in_g(tok[:, 384:512]) + win_c).astype(BF16)
    tr = lax.dot_general(wtr_ref[...], hn, (((1,), (1,)), ((), ())),
                         preferred_element_type=F32)
    qT_ref[0] = (tr[0:512] * (ATTN_SCALE * LOG2E)).astype(BF16)
    ones = jnp.ones((V_ROWS - HEAD_DIM, tr.shape[1]), BF16)
    for g in range(N_KV):
        for v_ref, r0 in ((vsT_ref, 512), (vwT_ref, 640)):
            v_ref[0, g, 0:HEAD_DIM, :] = tr[r0 + g * HEAD_DIM:r0 + (g + 1) * HEAD_DIM].astype(BF16)
            v_ref[0, g, HEAD_DIM:V_ROWS, :] = ones
    gT_ref[0] = _sigmoid(tr[768:800])


def _key_position_columns(t_):
    pos = jnp.arange(t_, dtype=I32)
    dup = lambda vals: [v.astype(BF16)[:, None] for v in vals for _ in range(2)]
    pad_to = lambda cols, width: jnp.concatenate(
        cols + [jnp.zeros((t_, width - sum(c.shape[1] for c in cols)), BF16)], axis=1)
    lead = jnp.zeros((t_, HEAD_DIM), BF16)
    slc = pad_to([lead, jax.nn.one_hot(pos // L_SLC, t_ // L_SLC, dtype=BF16)]
                 + dup([pos // L_SLC, pos % L_SLC]), 2 * HEAD_DIM + t_ // L_SLC)
    win = pad_to([lead] + dup([pos % KT_WIN]), 2 * HEAD_DIM)
    return slc, win


def _proj_in(x, norm_g, w_in, slc_cols, win_cols):
    b_, t_, d_ = x.shape
    tm = min(ROW_TILE, t_)
    q, kc, vc, ks, vs, kw, vw, g, xr, yg = jnp.split(
        w_in, [512, 640, 768, 896, 1024, 1152, 1280, 1304, 1816], axis=1)
    wtok = jnp.concatenate([kc, vc, ks, kw, xr, yg], axis=1).astype(BF16)
    g4 = g.reshape(d_, N_KV, HEADS_PER_KV, N_GATES).transpose(0, 1, 3, 2)
    g4 = jnp.pad(g4.reshape(d_, N_KV, 12), ((0, 0), (0, 0), (0, 4))).reshape(d_, 32)
    wtr = jnp.concatenate([q, vs, vw, g4], axis=1).T.astype(BF16)
    nt = t_ // tm
    row = lambda shape: pl.BlockSpec((1, tm, shape), lambda b, i: (b, i, 0))
    col = lambda shape: pl.BlockSpec((1, shape, tm), lambda b, i: (b, 0, i))
    full = lambda a: pl.BlockSpec(a.shape, lambda b, i: (0,) * a.ndim)
    g2 = norm_g.reshape(1, d_)
    ws, ww = slc_cols.shape[1], win_cols.shape[1]
    grow = lambda width: pl.BlockSpec((1, N_KV, tm, width), lambda b, i: (b, 0, i, 0))
    gcol = pl.BlockSpec((1, N_KV, V_ROWS, tm), lambda b, i: (b, 0, 0, i))
    outs = pl.pallas_call(
        _proj_in_kernel,
        name="proj_in",
        grid=(b_, nt),
        in_specs=[row(d_), full(g2), full(wtok), full(wtr),
                  pl.BlockSpec((tm, ws), lambda b, i: (i, 0)), pl.BlockSpec((tm, ww), lambda b, i: (i, 0))],
        out_specs=[row(256), grow(ws), grow(ww), row(512), row(512), col(512), gcol, gcol, col(32)],
        out_shape=[
            jax.ShapeDtypeStruct((b_, t_, 256), F32),
            jax.ShapeDtypeStruct((b_, N_KV, t_, ws), BF16),
            jax.ShapeDtypeStruct((b_, N_KV, t_, ww), BF16),
            jax.ShapeDtypeStruct((b_, t_, 512), BF16),
            jax.ShapeDtypeStruct((b_, t_, 512), BF16),
            jax.ShapeDtypeStruct((b_, 512, t_), BF16),
            jax.ShapeDtypeStruct((b_, N_KV, V_ROWS, t_), BF16),
            jax.ShapeDtypeStruct((b_, N_KV, V_ROWS, t_), BF16),
            jax.ShapeDtypeStruct((b_, 32, t_), F32),
        ],
        compiler_params=_cparams(("parallel", "parallel"), V7X_VMEM_LIMIT_BYTES),
    )(x, g2, wtok, wtr, slc_cols, win_cols)
    return outs


def _compress_kernel(xc_ref, pa_ref, pb_ref, w1a_ref, w1b_ref, w2_ref, out_ref):
    xc = xc_ref[0].astype(BF16)
    n_chunk = xc.shape[0]
    a = jnp.dot(xc, w1a_ref[...], preferred_element_type=F32)
    bm = jnp.dot(xc, w1b_ref[...], preferred_element_type=F32)
    posc = (jnp.dot(pa_ref[...], w1a_ref[...], preferred_element_type=F32)
            + jnp.dot(pb_ref[...], w1b_ref[...], preferred_element_type=F32))[0:1]
    row = lax.broadcasted_iota(I32, bm.shape, 0)
    bm_up = jnp.where(row < n_chunk - 1, pltpu.roll(bm, n_chunk - 1, axis=0), 0.0)
    hid = _gelu_tanh(a + bm_up + posc).astype(BF16)
    out_ref[0] = jnp.dot(hid, w2_ref[...], preferred_element_type=F32)


def _compress(kcvc, cmp_pos, cmp_w1, cmp_w2):
    b_, t_, _ = kcvc.shape
    n_chunk = t_ // STRIDE
    xc = kcvc.reshape(b_, n_chunk, STRIDE * 256)
    w1 = cmp_w1.reshape(2, 2, STRIDE, HEAD_DIM, CMP_HID)

    def expand(half):
        w = w1[:, half]
        full = jnp.einsum('wldo,wx,gy->lxgdwyo', w, jnp.eye(2, dtype=F32), jnp.eye(2, dtype=F32))
        return full.reshape(STRIDE * 256, 4 * CMP_HID).astype(BF16)

    w1a, w1b = expand(0), expand(1)
    pos = cmp_pos.reshape(2, 2, STRIDE, HEAD_DIM)

    def posrow(half):
        p = jnp.broadcast_to(pos[:, half][:, None], (2, N_KV, STRIDE, HEAD_DIM))
        p = p.transpose(2, 0, 1, 3).reshape(1, STRIDE * 256)
        return jnp.pad(p, ((0, 7), (0, 0))).astype(BF16)

    pa, pb = posrow(0), posrow(1)
    w2 = jnp.einsum('whd,wx,gy->wghxyd', cmp_w2, jnp.eye(2, dtype=F32), jnp.eye(2, dtype=F32))
    w2 = w2.reshape(4 * CMP_HID, 4 * HEAD_DIM).astype(BF16)
    full = lambda a: pl.BlockSpec(a.shape, lambda b: (0,) * a.ndim)
    return pl.pallas_call(
        _compress_kernel,
        name="compress_kv",
        grid=(b_,),
        in_specs=[pl.BlockSpec((1, n_chunk, STRIDE * 256), lambda b: (b, 0, 0)),
                  full(pa), full(pb), full(w1a), full(w1b), full(w2)],
        out_specs=pl.BlockSpec((1, n_chunk, 256), lambda b: (b, 0, 0)),
        out_shape=jax.ShapeDtypeStruct((b_, n_chunk, 256), F32),
        compiler_params=_cparams(("parallel",), V7X_VMEM_LIMIT_BYTES),
    )(xc, pa, pb, w1a, w1b, w2)


def _head_lanes(rows):
    r = rows.shape[0] // HEADS_PER_KV
    return jnp.concatenate([rows[h * r:(h + 1) * r, :] for h in range(HEADS_PER_KV)], axis=1)


def _attn_kernel(qT_ref, gT_ref, kc_ref, vcT_ref, ks_ref, vsT_ref, kw_ref, vwT_ref, mt_ref, o_ref, s_scr,
                 tile_list):
    i = pl.program_id(1)
    n_cmp = kc_ref.shape[2]
    n_slc = mt_ref.shape[0]
    nl = HEADS_PER_KV * TQ
    hd_rows = HEADS_PER_KV * HEAD_DIM

    lane = lax.broadcasted_iota(I32, (1, nl), 1)
    t_loc = lane % TQ
    t_row = i * TQ + t_loc

    def extra_rows(n_rows, rows):
        ridx = lax.broadcasted_iota(I32, (16, nl), 0)
        out = jnp.zeros((16, nl), F32)
        for k, r in enumerate(rows):
            out = jnp.where(ridx == k, r, out)
        return jnp.concatenate([out.astype(BF16), jnp.zeros((n_rows - 16, nl), BF16)], axis=0)

    ok_c = (lax.broadcasted_iota(I32, (n_cmp, nl), 0) * STRIDE + (L_CMP - 1)) <= t_row
    r_idx = lax.broadcasted_iota(I32, (KT_WIN, nl), 0)
    s_idx = lax.broadcasted_iota(I32, (n_slc, TQ), 0)
    cur = (i * TQ + lax.broadcasted_iota(I32, (1, TQ), 1)) // L_SLC
    valid = s_idx <= cur
    forced = valid & ((s_idx == 0) | (s_idx > cur - N_LOCAL))
    s_idx_f = s_idx.astype(F32)
    n_full = (i * TQ) // KT_SLC
    mt = mt_ref[...]

    def pick(_, score):
        m = jnp.max(score, axis=0, keepdims=True)
        idx = jnp.min(jnp.where(score == m, s_idx_f, 1.0e9), axis=0, keepdims=True)
        return jnp.where(s_idx_f == idx, -2.0, score)

    def stage_a(g, bs, tile, slot, causal):
        k0 = pl.multiple_of(tile * KT_SLC, KT_SLC)
        s = jnp.dot(ks_ref[0, g, pl.ds(k0, KT_SLC), :], bs, preferred_element_type=F32)
        if causal:
            pos = k0 + lax.broadcasted_iota(I32, (KT_SLC, nl), 0)
            s = jnp.where(pos <= t_row, s, MASK_NEG)
        s_scr[g, slot] = s
        return jnp.max(s, axis=0, keepdims=True)

    def stage_b(g, tile, slot, m, acc, m_tile):
        k0 = pl.multiple_of(tile * KT_SLC, KT_SLC)
        m_new = jnp.maximum(m, m_tile)
        p = jnp.exp2(s_scr[g, slot] - m_new)
        acc = jnp.exp2(m - m_new) * acc + jnp.dot(vsT_ref[0, g, :, pl.ds(k0, KT_SLC)], p.astype(BF16),
                                                  preferred_element_type=F32)
        return m_new, acc

    groups = range(N_KV)

    def setup(g):
        q4 = _head_lanes(qT_ref[0, g * hd_rows:(g + 1) * hd_rows, :])
        head = g * HEADS_PER_KV + lane // TQ + 1
        slope = lax.bitcast_convert_type((127 - head) << 23, F32)
        return dict(q4=q4, slope=slope, s_hi=slope * LOG2E_HI, s_lo=slope * LOG2E_LO)

    st = [setup(g) for g in groups]

    def cmp_scores(g):
        s_hi, s_lo = st[g]["s_hi"], st[g]["s_lo"]
        bc = jnp.concatenate(
            [st[g]["q4"], extra_rows(64, [256.0 * s_hi, 256.0 * s_lo, 16.0 * s_hi, 16.0 * s_lo])], axis=0)
        return jnp.dot(kc_ref[0, g], bc, preferred_element_type=F32)

    def cmp_softmax(g, s_c):
        s_c = jnp.where(ok_c, s_c, MASK_NEG)
        m_c = jnp.max(s_c, axis=0, keepdims=True)
        e_c = jnp.exp2(s_c - m_c)
        l_c = jnp.sum(e_c, axis=0, keepdims=True)
        inv_c = jnp.where(m_c > 0.5 * MASK_NEG, 1.0 / jnp.maximum(l_c, 1e-30), 0.0)
        p_c = e_c * inv_c
        o_c = jnp.dot(vcT_ref[0, g], p_c.astype(BF16), preferred_element_type=F32)
        p_grp = (p_c[:, 0:TQ] + p_c[:, TQ:2 * TQ]) + (p_c[:, 2 * TQ:3 * TQ] + p_c[:, 3 * TQ:4 * TQ])
        p1 = p_grp.astype(BF16)
        r1 = p_grp - p1.astype(F32)
        p2 = r1.astype(BF16)
        p3 = (r1 - p2.astype(F32)).astype(BF16)
        p_slc = (jnp.dot(mt, p1, preferred_element_type=F32) + jnp.dot(mt, p2, preferred_element_type=F32)
                 + jnp.dot(mt, p3, preferred_element_type=F32))
        return o_c, p_slc

    def win_scores(g):
        slope = st[g]["slope"]
        bw = jnp.concatenate([st[g]["q4"], extra_rows(64, [st[g]["s_hi"], st[g]["s_lo"]])], axis=0)
        s_tiles, c_tiles = [], []
        for a in range(N_WIN_TILES):
            tile = i - (N_WIN_TILES - 1) + a
            k0 = pl.multiple_of(jnp.maximum(tile, 0) * KT_WIN, KT_WIN)
            s = jnp.dot(kw_ref[0, g, pl.ds(k0, KT_WIN), :], bw, preferred_element_type=F32)
            if a == 0:
                s = jnp.where(r_idx > t_loc, s, MASK_NEG)
            elif a == N_WIN_TILES - 1:
                s = jnp.where(r_idx <= t_loc, s, MASK_NEG)
            s_tiles.append(s)
            c_a = slope * (LOG2E * KT_WIN * (a - (N_WIN_TILES - 1)))
            c_tiles.append(jnp.where(tile >= 0, c_a, MASK_NEG))
        return s_tiles, c_tiles

    def win_softmax(g, s_tiles, c_tiles):
        m_w = s_tiles[0].max(axis=0, keepdims=True) + c_tiles[0]
        for s, c_a in zip(s_tiles[1:], c_tiles[1:]):
            m_w = jnp.maximum(m_w, s.max(axis=0, keepdims=True) + c_a)
        acc_w = jnp.zeros((V_ROWS, nl), F32)
        for a, (s, c_a) in enumerate(zip(s_tiles, c_tiles)):
            tile = i - (N_WIN_TILES - 1) + a
            k0 = pl.multiple_of(jnp.maximum(tile, 0) * KT_WIN, KT_WIN)
            p = jnp.exp2(s - (m_w - c_a))
            acc_w = acc_w + jnp.dot(vwT_ref[0, g, :, pl.ds(k0, KT_WIN)], p.astype(BF16),
                                    preferred_element_type=F32)
        return acc_w[0:HEAD_DIM] * (1.0 / acc_w[HEAD_DIM:HEAD_DIM + 1])

    s_cmp = [cmp_scores(g) for g in groups]
    s_win = [win_scores(g) for g in groups]
    cmp_out, scores = [], []
    for g in groups:
        cmp_out.append(cmp_softmax(g, s_cmp[g]))
        score0 = jnp.where(forced, -2.0, jnp.where(valid, cmp_out[g][1], -1.0))
        scores.append(lax.fori_loop(0, N_SEL - (N_LOCAL + 1), pick, score0, unroll=True))
    o_win = [win_softmax(g, *s_win[g]) for g in groups]
    score = jnp.concatenate(scores, axis=1)

    def select(g):
        sel = valid & (score[:, g * TQ:(g + 1) * TQ] < -1.5)
        sel4 = jnp.concatenate([sel] * HEADS_PER_KV, axis=1)
        blk_rows = jnp.where(sel4, 0.0, SEL_NEG).astype(BF16)
        s_hi, s_lo = st[g]["s_hi"], st[g]["s_lo"]
        bs = jnp.concatenate(
            [st[g]["q4"], blk_rows, extra_rows(64, [L_SLC * s_hi, L_SLC * s_lo, s_hi, s_lo])], axis=0)
        m_tile0 = stage_a(g, bs, n_full, 0, True)
        gt = gT_ref[0, g * 16:(g + 1) * 16, :]
        gates = [_head_lanes(gt[k * HEADS_PER_KV:(k + 1) * HEADS_PER_KV, :]) for k in range(N_GATES)]
        return dict(bs=bs, base=gates[0] * cmp_out[g][0] + gates[2] * o_win[g], g_s=gates[1], m_tile0=m_tile0)

    grp = [select(g) for g in groups]

    blocks_per_tile = KT_SLC // L_SLC
    n_tiles = n_slc // blocks_per_tile
    assert n_tiles <= 32
    valid2 = jnp.concatenate([valid] * N_KV, axis=1)
    sel_all = jnp.where(valid2 & (score < -1.5), 1.0, 0.0).astype(BF16)
    blk_cnt = lax.dot_general(jnp.ones((8, N_KV * TQ), BF16), sel_all, (((1,), (1,)), ((), ())),
                              preferred_element_type=F32)
    memb = jnp.where(lax.broadcasted_iota(I32, (n_slc, V7X_LANES), 0) // blocks_per_tile
                     == lax.broadcasted_iota(I32, (n_slc, V7X_LANES), 1), 1.0, 0.0).astype(BF16)
    tile_cnt = jnp.dot(jnp.where(blk_cnt > 0.5, 1.0, 0.0).astype(BF16), memb,
                       preferred_element_type=F32)[0:1]
    lane_t = lax.broadcasted_iota(I32, (1, V7X_LANES), 1)
    need = (tile_cnt > 0.5) & (lane_t < n_full)
    bit = lax.bitcast_convert_type(((lane_t & 15) + 127) << 23, F32)
    lo_bits = jnp.sum(jnp.where(need & (lane_t < 16), bit, 0.0)).astype(I32)
    hi_bits = jnp.sum(jnp.where(need & (lane_t >= 16), bit, 0.0)).astype(I32)
    n_sel = jnp.int32(0)
    for t in range(n_tiles - 1):
        tile_list[n_sel] = t
        n_sel = n_sel + (((lo_bits if t < 16 else hi_bits) >> (t % 16)) & 1)

    order = lambda k: jnp.where(k == 0, n_full, tile_list[jnp.maximum(k - 1, 0)])

    def half(carry, tile_b, slot_b, tile_a):
        m_next = [c[2] if tile_a is None else stage_a(g, grp[g]["bs"], tile_a, 1 - slot_b, False)
                  for g, c in enumerate(carry)]
        out = []
        for g in range(N_KV):
            m, acc, m_tile = carry[g]
            out.append(stage_b(g, tile_b, slot_b, m, acc, m_tile) + (m_next[g],))
        return tuple(out)

    def run(carry, k0, n, last):
        for j in range(n):
            nxt = None if (last and j == n - 1) else order(k0 + j + 1)
            carry = half(carry, order(k0 + j), j % 2, nxt)
        return carry

    init = tuple((jnp.full((1, nl), MASK_NEG, F32), jnp.zeros((V_ROWS, nl), F32), grp[g]["m_tile0"])
                 for g in range(N_KV))
    carry = lax.fori_loop(0, n_sel // SLC_UNROLL, lambda p, c: run(c, SLC_UNROLL * p, SLC_UNROLL, False), init)
    k_tail = SLC_UNROLL * (n_sel // SLC_UNROLL)
    carry = lax.switch(n_sel % SLC_UNROLL, [functools.partial(run, k0=k_tail, n=r + 1, last=True)
                                            for r in range(SLC_UNROLL)], carry)
    for g in range(N_KV):
        _, acc, _ = carry[g]
        o_s = acc[0:HEAD_DIM] * (1.0 / acc[HEAD_DIM:HEAD_DIM + 1])
        out = grp[g]["base"] + grp[g]["g_s"] * o_s
        for h in range(HEADS_PER_KV):
            r0 = g * hd_rows + h * HEAD_DIM
            o_ref[0, r0:r0 + HEAD_DIM, :] = out[:, h * TQ:(h + 1) * TQ].astype(o_ref.dtype)


def _attention(qT, gT, kcmp, ks_aug, vsT4, kw_aug, vwT4):
    b_, _, t_ = qT.shape
    n_cmp = t_ // STRIDE
    n_slc = t_ // L_SLC
    c = jnp.arange(n_cmp, dtype=I32)
    cmp_cols = jnp.concatenate([v.astype(BF16)[:, None] for v in (c // 16, c // 16, c % 16, c % 16)]
                               + [jnp.zeros((n_cmp, HEAD_DIM - 4), BF16)], axis=1)
    kc4 = kcmp[..., 0:KV_W].reshape(b_, n_cmp, N_KV, HEAD_DIM).transpose(0, 2, 1, 3).astype(BF16)
    kc_aug = jnp.concatenate(
        [kc4, jnp.broadcast_to(cmp_cols, (b_, N_KV) + cmp_cols.shape)], axis=-1)
    vcT = kcmp[..., KV_W:2 * KV_W].reshape(b_, n_cmp, N_KV, HEAD_DIM).transpose(0, 2, 3, 1).astype(BF16)
    s = jnp.arange(n_slc, dtype=I32)[:, None]
    cc = c[None, :]
    r_slc = L_SLC // STRIDE
    mt = (((cc >= r_slc * s) & (cc < r_slc * s + r_slc)).astype(F32)
          + ((cc + 1 >= r_slc * s) & (cc + 1 < r_slc * s + r_slc)).astype(F32))
    mt = jnp.where(cc < n_cmp - 1, mt, 0.0).astype(BF16)
    kv_spec = lambda a: pl.BlockSpec((1,) + a.shape[1:], lambda b, i: (b, 0, 0, 0))
    return pl.pallas_call(
        _attn_kernel,
        name="nsa_attention",
        grid=(b_, t_ // TQ),
        in_specs=[pl.BlockSpec((1, D_ATTN, TQ), lambda b, i: (b, 0, i)),
                  pl.BlockSpec((1, 32, TQ), lambda b, i: (b, 0, i)),
                  kv_spec(kc_aug), kv_spec(vcT), kv_spec(ks_aug), kv_spec(vsT4),
                  kv_spec(kw_aug), kv_spec(vwT4),
                  pl.BlockSpec(mt.shape, lambda b, i: (0, 0))],
        out_specs=pl.BlockSpec((1, D_ATTN, TQ), lambda b, i: (b, 0, i)),
        out_shape=jax.ShapeDtypeStruct((b_, D_ATTN, t_), BF16),
        scratch_shapes=[pltpu.VMEM((N_KV, 2, KT_SLC, HEADS_PER_KV * TQ), F32), pltpu.SMEM((32,), I32)],
        compiler_params=_cparams(("parallel", "arbitrary"), V7X_VMEM_LIMIT_BYTES),
    )(qT, gT, kc_aug, vcT, ks_aug, vsT4, kw_aug, vwT4, mt)


def _rglru_seg_pitch(tc):
    return tc // 8 + 8


def _rglru_kernel(xr_ref, yg_ref, cw_ref, cb_ref, wa_ref, ba_ref, wx_ref, bx_ref, lam_ref,
                  o_ref, tail_ref, h_ref, xs_scr, ys_scr, os_scr):
    tc = xr_ref.shape[1]

    @pl.when(pl.program_id(1) == 0)
    def _():
        tail_ref[...] = jnp.zeros_like(tail_ref)
        h_ref[...] = jnp.zeros_like(h_ref)

    n_v = tc // 8
    n_slab = xr_ref.shape[2] // V7X_LANES
    pitch = xs_scr.shape[1] // 8
    for c in range(n_slab):
        for s in range(8):
            lanes = slice(c * V7X_LANES, (c + 1) * V7X_LANES)
            xs_scr[c, s * pitch:s * pitch + n_v, :] = xr_ref[0, s * n_v:(s + 1) * n_v, lanes].astype(F32)
            ys_scr[c, s * pitch:s * pitch + n_v, :] = yg_ref[0, s * n_v:(s + 1) * n_v, lanes].astype(F32)

    def piece(scr, v):
        return jnp.concatenate([scr[c, pl.ds(v, 8, stride=pitch), :] for c in range(n_slab)], axis=1)

    x = [piece(xs_scr, v) for v in range(n_v)]
    row8 = lax.broadcasted_iota(I32, (8, x[0].shape[1]), 0)
    tail = tail_ref[...]
    before = {d: jnp.where(row8 == 0, tail[CONV_W - 1 - d:CONV_W - d, :], pltpu.roll(x[n_v - d], 1, axis=0))
              for d in range(1, CONV_W)}
    tail_ref[0:CONV_W - 1, :] = jnp.concatenate([x[n_v - d][7:8, :] for d in range(CONV_W - 1, 0, -1)], axis=0)
    cw = cw_ref[...]
    cb = cb_ref[...]
    xc = []
    for v in range(n_v):
        acc = x[v] * cw[CONV_W - 1:CONV_W, :] + cb
        for d in range(1, CONV_W):
            acc = acc + (x[v - d] if v >= d else before[d - v]) * cw[CONV_W - 1 - d:CONV_W - d, :]
        xc.append(acc)

    xb = jnp.concatenate(xc, axis=0).astype(BF16)
    r = _sigmoid(jnp.dot(xb, wa_ref[...], preferred_element_type=F32) + ba_ref[...])
    gi = _sigmoid(jnp.dot(xb, wx_ref[...], preferred_element_type=F32) + bx_ref[...])
    z = -lam_ref[...]
    softplus = jnp.maximum(z, 0.0) + jnp.log1p(jnp.exp(-jnp.abs(z)))
    log_a = (-C_LRU * r) * softplus
    a = jnp.exp(log_a)
    th = jnp.tanh(log_a)
    neg_expm1 = 2.0 * th / (th - 1.0)
    bb = jnp.sqrt(neg_expm1) * gi * jnp.concatenate(xc, axis=0)

    a_cum, b_cum = [a[0:8]], [bb[0:8]]
    for v in range(1, n_v):
        a_v = a[v * 8:(v + 1) * 8]
        b_cum.append(a_v * b_cum[-1] + bb[v * 8:(v + 1) * 8])
        a_cum.append(a_v * a_cum[-1])
    h_in = [h_ref[0:1, :]]
    for s in range(7):
        h_in.append(a_cum[-1][s:s + 1, :] * h_in[-1] + b_cum[-1][s:s + 1, :])
    h_ref[0:1, :] = a_cum[-1][7:8, :] * h_in[-1] + b_cum[-1][7:8, :]
    h_in = jnp.concatenate(h_in, axis=0)
    for v in range(n_v):
        out = (a_cum[v] * h_in + b_cum[v]) * _gelu_tanh(piece(ys_scr, v))
        for c in range(n_slab):
            os_scr[c, pl.ds(v, 8, stride=pitch), :] = out[:, c * V7X_LANES:(c + 1) * V7X_LANES]
    for s in range(8):
        o_ref[0, s * n_v:(s + 1) * n_v, :] = jnp.concatenate(
            [os_scr[c, s * pitch:s * pitch + n_v, :] for c in range(n_slab)], axis=1).astype(o_ref.dtype)


def _block_diag(w):
    n, c, d = w.shape
    return jnp.einsum('ncd,nm->ncmd', w, jnp.eye(n, dtype=w.dtype)).reshape(n * c, n * d)


def _rglru(xr, yg, conv_w, conv_b, wa, ba, wx, bx, lam):
    b_, t_, c_ = xr.shape
    tc = min(RGLRU_CHUNK, t_)
    wa_bd = _block_diag(wa).astype(BF16)
    wx_bd = _block_diag(wx).astype(BF16)
    vec = lambda v: v.reshape(1, c_)
    full = lambda a: pl.BlockSpec(a.shape, lambda b, i: (0,) * a.ndim)
    blk = pl.BlockSpec((1, tc, c_), lambda b, i: (b, i, 0))
    args = (xr, yg, conv_w, vec(conv_b), wa_bd, vec(ba), wx_bd, vec(bx), vec(lam))
    return pl.pallas_call(
        _rglru_kernel,
        name="rglru",
        grid=(b_, t_ // tc),
        in_specs=[blk, blk] + [full(a) for a in args[2:]],
        out_specs=blk,
        out_shape=jax.ShapeDtypeStruct((b_, t_, c_), BF16),
        scratch_shapes=[pltpu.VMEM((8, c_), F32), pltpu.VMEM((8, c_), F32)]
                       + [pltpu.VMEM((c_ // V7X_LANES, 8 * _rglru_seg_pitch(tc), V7X_LANES), F32)] * 3,
        compiler_params=_cparams(("parallel", "arbitrary"), V7X_VMEM_LIMIT_BYTES),
    )(*args)


def _out_proj_kernel(oaT_ref, orec_ref, x_ref, ga_ref, gr_ref, wa_ref, wr_ref, o_ref):
    oaT = oaT_ref[0].astype(F32)
    ms = jnp.mean(oaT * oaT, axis=0, keepdims=True)
    na = (oaT * lax.rsqrt(ms + RMS_EPS)).T * ga_ref[...]
    nr = _rms(orec_ref[0].astype(F32), gr_ref[...])
    y = (jnp.dot(na.astype(BF16), wa_ref[...], preferred_element_type=F32)
         + jnp.dot(nr.astype(BF16), wr_ref[...], preferred_element_type=F32))
    o_ref[0] = x_ref[0] + y


def _out_proj(oaT, orec, x, g_attn, g_rec, w_out):
    b_, t_, d_ = x.shape
    tm = min(ROW_TILE, t_)
    wa = w_out[:D_ATTN].astype(BF16)
    wr = w_out[D_ATTN:].astype(BF16)
    ga = g_attn.reshape(1, -1)
    gr = g_rec.reshape(1, -1)
    full = lambda a: pl.BlockSpec(a.shape, lambda b, i: (0,) * a.ndim)
    return pl.pallas_call(
        _out_proj_kernel,
        name="out_proj",
        grid=(b_, t_ // tm),
        in_specs=[pl.BlockSpec((1, D_ATTN, tm), lambda b, i: (b, 0, i)),
                  pl.BlockSpec((1, tm, orec.shape[2]), lambda b, i: (b, i, 0)),
                  pl.BlockSpec((1, tm, d_), lambda b, i: (b, i, 0)),
                  full(ga), full(gr), full(wa), full(wr)],
        out_specs=pl.BlockSpec((1, tm, d_), lambda b, i: (b, i, 0)),
        out_shape=jax.ShapeDtypeStruct((b_, t_, d_), F32),
        compiler_params=_cparams(("parallel", "parallel"), V7X_VMEM_LIMIT_BYTES),
    )(oaT, orec, x, ga, gr, wa, wr)


def _ple_apply(x, p_ref, g_ref, wg_ref, wp_ref, fg_ref, *, final):
    gate = _sigmoid(jnp.dot(_rms(x, g_ref[...]).astype(BF16), wg_ref[...], preferred_element_type=F32))
    proj = jnp.dot(p_ref[...].astype(BF16), wp_ref[...], preferred_element_type=F32)
    y = x + gate * proj
    return _rms(y, fg_ref[...]) if final else y


def _ple_operands(p, norm_g, w_gate, w_proj, final_g, rows, row_map, const_map):
    d_ = w_gate.shape[0]
    arrays = (p, norm_g.reshape(1, d_), w_gate.astype(BF16), w_proj.astype(BF16), final_g.reshape(1, d_))
    once = dict(pipeline_mode=pl.Buffered(1))
    specs = [pl.BlockSpec((rows, p.shape[1]), row_map)] + [pl.BlockSpec(a.shape, const_map, **once)
                                                            for a in arrays[1:]]
    return arrays, specs


def _ffn_kernel(be_ref, x_ref, g_ref, wg_ref, wu_ref, wd_ref, *rest, dense, final):
    ple_refs, (o_ref, hn_ref, acc_ref) = rest[:-3], rest[-3:]
    f = pl.program_id(1)
    used = True if dense else pl.program_id(0) < be_ref[pl.num_programs(0)]

    @pl.when(f == 0)
    def _():
        if dense:
            x = x_ref[...]
            hn_ref[...] = _rms(x, g_ref[...]).astype(BF16)
            acc_ref[...] = x
        else:
            hn_ref[...] = _unpack_bf16_pairs(x_ref[...])
            acc_ref[...] = jnp.zeros_like(acc_ref)

    def accumulate():
        hn = hn_ref[...]
        gate = jnp.dot(hn, wg_ref[0], preferred_element_type=F32)
        up = jnp.dot(hn, wu_ref[0], preferred_element_type=F32)
        act = (_silu(gate) * up).astype(BF16)
        acc_ref[...] += jnp.dot(act, wd_ref[0], preferred_element_type=F32)

    if dense:
        accumulate()
    else:
        pl.when(used)(accumulate)

    @pl.when(f == pl.num_programs(1) - 1)
    def _():
        if dense:
            o_ref[...] = _ple_apply(acc_ref[...], *ple_refs, final=final)
        else:
            o_ref[...] = acc_ref[...].astype(o_ref.dtype)


def _ffn(x, norm_g, wg, wu, wd, block_e, n_used=None, *, dense, tf, ple=None, final=False):
    n = x.shape[0]
    d_ = wg.shape[1]
    tm = min(ROW_TILE if dense else MOE_ROWS, n)
    ff = wg.shape[2]
    assert ff % tf == 0 and n % tm == 0
    if block_e is None:
        block_e = jnp.zeros((n // tm,), I32)
    if n_used is None:
        n_used = jnp.int32(n // tm)
    n_blk, nf = n // tm, ff // tf
    be_all = jnp.concatenate([block_e, jnp.reshape(n_used, (1,)).astype(I32)])
    wmode = dict(pipeline_mode=pl.Buffered(1)) if (wg.shape[0] == 1 and ff == tf) else {}
    ftile = lambda i, f, be: jnp.where(i < be[n_blk], f, nf - 1)
    ple_arrays, ple_specs = ((), [])
    if dense:
        ple_arrays, ple_specs = _ple_operands(*ple, tm, lambda i, f, be: (i, 0), lambda i, f, be: (0, 0))
    grid_spec = pltpu.PrefetchScalarGridSpec(
        num_scalar_prefetch=1,
        grid=(n_blk, nf),
        in_specs=[pl.BlockSpec((tm, x.shape[1]), lambda i, f, be: (i, 0)),
                  pl.BlockSpec((1, d_), lambda i, f, be: (0, 0)),
                  pl.BlockSpec((1, d_, tf), lambda i, f, be: (be[i], 0, ftile(i, f, be)), **wmode),
                  pl.BlockSpec((1, d_, tf), lambda i, f, be: (be[i], 0, ftile(i, f, be)), **wmode),
                  pl.BlockSpec((1, tf, d_), lambda i, f, be: (be[i], ftile(i, f, be), 0), **wmode)] + ple_specs,
        out_specs=pl.BlockSpec((tm, d_), lambda i, f, be: (i, 0)),
        scratch_shapes=[pltpu.VMEM((tm, d_), BF16), pltpu.VMEM((tm, d_), F32)],
    )
    return pl.pallas_call(
        functools.partial(_ffn_kernel, dense=dense, final=final),
        name="ffn_dense" if dense else "ffn_expert",
        grid_spec=grid_spec,
        out_shape=jax.ShapeDtypeStruct((n, d_), F32 if dense else BF16),
        compiler_params=_cparams(("parallel", "arbitrary"), V7X_VMEM_LIMIT_BYTES),
    )(be_all, x, norm_g.reshape(1, d_), wg, wu, wd, *ple_arrays)


def _pack_bf16_pairs(x):
    w = x.shape[1] // 2
    bits = lax.bitcast_convert_type(x.astype(BF16).astype(F32), jnp.uint32)
    return (bits[:, :w] >> 16) | bits[:, w:]


def _unpack_bf16_pairs(words):
    lo = lax.bitcast_convert_type(words << 16, F32)
    hi = lax.bitcast_convert_type(words & jnp.uint32(0xFFFF0000), F32)
    return jnp.concatenate([lo, hi], axis=1).astype(BF16)


def _router_kernel(x_ref, g_ref, wr_ref, tri_ref, h_ref, e_ref, p_ref, r_ref, cb_ref, tot_ref, cnt_ref):
    @pl.when(pl.program_id(0) == 0)
    def _():
        cnt_ref[...] = jnp.zeros_like(cnt_ref)

    hn = _rms(x_ref[...], g_ref[...])
    h_ref[...] = _pack_bf16_pairs(hn)
    logits = lax.dot_general(wr_ref[...], hn, (((1,), (1,)), ((), ())),
                             precision=lax.Precision.HIGHEST, preferred_element_type=F32)
    eidx = lax.broadcasted_iota(I32, logits.shape, 0)
    m1 = jnp.max(logits, axis=0, keepdims=True)
    i1 = jnp.min(jnp.where(logits == m1, eidx, N_EXPERTS), axis=0, keepdims=True)
    rest = jnp.where(eidx == i1, -jnp.inf, logits)
    m2 = jnp.max(rest, axis=0, keepdims=True)
    i2 = jnp.min(jnp.where(rest == m2, eidx, N_EXPERTS), axis=0, keepdims=True)
    e2 = jnp.exp(m2 - m1)
    inv = 1.0 / (1.0 + e2)
    e_ref[...] = jnp.concatenate([i1, i2], axis=0)
    p_ref[...] = jnp.concatenate([inv, e2 * inv], axis=0)

    tm = logits.shape[1]
    chosen = jnp.where((eidx == i1) | (eidx == i2), 1.0, 0.0)
    incl = jnp.dot(chosen.astype(BF16), tri_ref[...], preferred_element_type=F32)
    before = incl - chosen + cnt_ref[:, 0:1]
    r_ref[...] = jnp.concatenate(
        [jnp.sum(jnp.where(eidx == i1, before, 0.0), axis=0, keepdims=True),
         jnp.sum(jnp.where(eidx == i2, before, 0.0), axis=0, keepdims=True)], axis=0).astype(I32)
    lane = lax.broadcasted_iota(I32, cb_ref.shape, 1)
    cb = jnp.zeros(cb_ref.shape, F32)
    for c in range(tm // CMB_ROWS):
        cb = jnp.where(lane == c, before[:, c * CMB_ROWS:c * CMB_ROWS + 1], cb)
    cb_ref[...] = cb.astype(I32)
    total = cnt_ref[...] + incl[:, tm - 1:tm]
    cnt_ref[...] = total
    tot_ref[...] = total.astype(I32)


def _router(x, norm_g, router_w):
    n, d_ = x.shape
    tm = min(ROW_TILE, n)
    assert tm % CMB_ROWS == 0
    wr = router_w.T
    tri = (jnp.arange(tm)[:, None] <= jnp.arange(tm)[None, :]).astype(BF16)
    h, top_e, top_p, rank, cb, tot = pl.pallas_call(
        _router_kernel,
        name="router",
        grid=(n // tm,),
        in_specs=[pl.BlockSpec((tm, d_), lambda i: (i, 0)),
                  pl.BlockSpec((1, d_), lambda i: (0, 0)),
                  pl.BlockSpec(wr.shape, lambda i: (0, 0)),
                  pl.BlockSpec(tri.shape, lambda i: (0, 0))],
        out_specs=[pl.BlockSpec((tm, d_ // 2), lambda i: (i, 0)),
                   pl.BlockSpec((2, tm), lambda i: (0, i)),
                   pl.BlockSpec((2, tm), lambda i: (0, i)),
                   pl.BlockSpec((2, tm), lambda i: (0, i)),
                   pl.BlockSpec((N_EXPERTS, V7X_LANES), lambda i: (0, i)),
                   pl.BlockSpec((N_EXPERTS, V7X_LANES), lambda i: (0, 0))],
        out_shape=[jax.ShapeDtypeStruct((n, d_ // 2), jnp.uint32),
                   jax.ShapeDtypeStruct((2, n), I32),
                   jax.ShapeDtypeStruct((2, n), F32),
                   jax.ShapeDtypeStruct((2, n), I32),
                   jax.ShapeDtypeStruct((N_EXPERTS, n // tm * V7X_LANES), I32),
                   jax.ShapeDtypeStruct((N_EXPERTS, V7X_LANES), I32)],
        scratch_shapes=[pltpu.VMEM((N_EXPERTS, V7X_LANES), F32)],
        compiler_params=_cparams(("arbitrary",), V7X_VMEM_LIMIT_BYTES),
    )(x, norm_g.reshape(1, d_), wr, tri)
    cb = cb.reshape(N_EXPERTS, n // tm, V7X_LANES)[:, :, :tm // CMB_ROWS].reshape(N_EXPERTS, n // CMB_ROWS).T
    return h, top_e, top_p, rank, cb, tot[:, 0]


SCATTER_UNROLL = 16


def _scatter_kernel(dest_ref, h_ref, zeros_ref, xs_ref, sem):
    del zeros_ref
    n_tok = h_ref.shape[0]

    def row_copy(t, k):
        return pltpu.make_async_copy(h_ref.at[pl.ds(t, 1), :], xs_ref.at[pl.ds(dest_ref[k, t], 1), :], sem)

    def issue(u, carry):
        for r in range(SCATTER_UNROLL):
            t = u * SCATTER_UNROLL + r
            row_copy(t, 0).start()
            row_copy(t, 1).start()
        return carry

    lax.fori_loop(0, n_tok // SCATTER_UNROLL, issue, 0)
    for _ in range(2):
        pltpu.make_async_copy(h_ref, xs_ref.at[pl.ds(0, n_tok), :], sem).wait()


def _scatter(h, dest, n_slots):
    n, w = h.shape
    tc = min(MOE_TOK_CHUNK, n)
    zeros = jnp.zeros((n_slots, w), h.dtype)
    return pl.pallas_call(
        _scatter_kernel,
        name="moe_scatter",
        grid=(n // tc,),
        in_specs=[pl.BlockSpec((2, tc), lambda c: (0, c), memory_space=pltpu.SMEM),
                  pl.BlockSpec((tc, w), lambda c: (c, 0)),
                  pl.BlockSpec(memory_space=pl.ANY)],
        out_specs=pl.BlockSpec(memory_space=pl.ANY),
        out_shape=jax.ShapeDtypeStruct((n_slots, w), h.dtype),
        scratch_shapes=[pltpu.SemaphoreType.DMA(())],
        input_output_aliases={2: 0},
        compiler_params=_cparams(("arbitrary",), V7X_VMEM_LIMIT_BYTES),
    )(dest, h, zeros)


N_CMB_OPS = 2 * N_EXPERTS


def _combine_kernel(jt_ref, vt_ref, x_ref, dest_ref, gate_ref, *rest, final):
    yb_refs = rest[:N_CMB_OPS]
    ple_refs = rest[N_CMB_OPS:-1]
    o_ref = rest[-1]
    c = pl.program_id(0)
    dest = dest_ref[...]
    gate = gate_ref[...]
    lane = lax.broadcasted_iota(I32, (dest.shape[0], CMB_ROWS), 1)
    acc = x_ref[...]
    for k in range(N_CMB_OPS):
        base = jnp.where(vt_ref[c * N_CMB_OPS + k] == 1, jt_ref[c * N_CMB_OPS + k] * CMB_ROWS, -2 * CMB_ROWS)
        rel = dest - base
        w = (jnp.where(lane == rel[:, 0:1], gate[:, 0:1], 0.0)
             + jnp.where(lane == rel[:, 1:2], gate[:, 1:2], 0.0)).astype(BF16)
        acc = acc + jnp.dot(w, yb_refs[k][...], preferred_element_type=F32)
    o_ref[...] = _ple_apply(acc, *ple_refs, final=final)


def _combine(x, yb, dest_t, gate_t, jt, vt, ple, final):
    n, d_ = x.shape
    ple_arrays, ple_specs = _ple_operands(*ple, CMB_ROWS, lambda c, jt, vt: (c, 0), lambda c, jt, vt: (0, 0))
    yb_spec = lambda k: pl.BlockSpec((CMB_ROWS, d_), lambda c, jt, vt, k=k: (jt[c * N_CMB_OPS + k], 0))
    grid_spec = pltpu.PrefetchScalarGridSpec(
        num_scalar_prefetch=2,
        grid=(n // CMB_ROWS,),
        in_specs=[pl.BlockSpec((CMB_ROWS, d_), lambda c, jt, vt: (c, 0)),
                  pl.BlockSpec((CMB_ROWS, 2), lambda c, jt, vt: (c, 0)),
                  pl.BlockSpec((CMB_ROWS, 2), lambda c, jt, vt: (c, 0))]
                 + [yb_spec(k) for k in range(N_CMB_OPS)] + ple_specs,
        out_specs=pl.BlockSpec((CMB_ROWS, d_), lambda c, jt, vt: (c, 0)),
    )
    return pl.pallas_call(
        functools.partial(_combine_kernel, final=final),
        name="moe_combine",
        grid_spec=grid_spec,
        out_shape=jax.ShapeDtypeStruct((n, d_), F32),
        compiler_params=_cparams(("parallel",), V7X_VMEM_LIMIT_BYTES),
    )(jt, vt, x, dest_t, gate_t, *([yb] * N_CMB_OPS), *ple_arrays)


def _moe_plan(top_e, rank, cb, counts, n_slots):
    padded = (counts + MOE_ROWS - 1) // MOE_ROWS * MOE_ROWS
    pad_ends = jnp.cumsum(padded)
    pad_starts = pad_ends - padded
    start_of = sum(jnp.where(top_e == e, pad_starts[e], 0) for e in range(N_EXPERTS))
    dest = (start_of + rank).astype(I32)
    n_blk = n_slots // MOE_ROWS
    blk_start = jnp.arange(n_blk, dtype=I32) * MOE_ROWS
    block_e = jnp.minimum(jnp.sum((pad_ends[None, :] <= blk_start[:, None]).astype(I32), axis=1),
                          N_EXPERTS - 1).astype(I32)

    cb = jnp.concatenate([cb, counts[None]], axis=0)
    lo = pad_starts[None] + cb[:-1]
    hi = pad_starts[None] + cb[1:]
    ja = lo // CMB_ROWS
    jb = (hi - 1) // CMB_ROWS
    va = hi > lo
    vb = va & (jb > ja)
    jt = jnp.stack([ja, jb], axis=-1).reshape(-1)
    vt = jnp.stack([va, vb], axis=-1).reshape(-1)
    jt = jnp.where(vt, jt, 0).astype(I32)
    return dest, block_e, pad_ends[-1] // MOE_ROWS, jt, vt.astype(I32)


def _moe(x, norm_g, router_w, wg, wu, wd, ple, final):
    n, d_ = x.shape
    n_slots = (2 * n // MOE_ROWS + N_EXPERTS) * MOE_ROWS
    h, top_e, top_p, rank, cb, counts = _router(x, norm_g, router_w)
    dest, block_e, n_used, jt, vt = _moe_plan(top_e, rank, cb, counts, n_slots)
    xs = _scatter(h, dest, n_slots)
    yb = _ffn(xs, norm_g, wg, wu, wd, block_e, n_used, dense=False, tf=wg.shape[2] // 2)
    return _combine(x, yb, dest.T, top_p.T, jt, vt, ple, final)


def kernel(x, p, attn_norm, w_in, cmp_pos, cmp_w1, cmp_w2, conv_w, conv_b, lru_wa, lru_ba, lru_wx, lru_bx,
           lru_lambda, out_norm_attn, out_norm_rec, w_out, ffn_norm, dense_w_gate, dense_w_up, dense_w_down,
           router_w, moe_w_gate, moe_w_up, moe_w_down, ple_norm, ple_w_gate, ple_w_proj, final_norm):
    b_, t_, d_ = x.shape
    depth = w_in.shape[0]
    n = b_ * t_
    slc_cols, win_cols = _key_position_columns(t_)
    for i in range(depth):
        kcvc, ks, kw, xr, yg, qT, vsT, vwT, gT = _proj_in(x, attn_norm[i], w_in[i], slc_cols, win_cols)
        kcmp = _compress(kcvc, cmp_pos[i], cmp_w1[i], cmp_w2[i])
        oaT = _attention(qT, gT, kcmp, ks, vsT, kw, vwT)
        orec = _rglru(xr, yg, conv_w[i], conv_b[i], lru_wa[i], lru_ba[i], lru_wx[i], lru_bx[i], lru_lambda[i])
        x = _out_proj(oaT, orec, x, out_norm_attn[i], out_norm_rec[i], w_out[i])
        x2 = x.reshape(n, d_)
        j = i // 2
        ple = (p[i].reshape(n, -1), ple_norm[i], ple_w_gate[i], ple_w_proj[i], final_norm)
        final = i == depth - 1
        if i % 2 == 0:
            x2 = _ffn(x2, ffn_norm[i], dense_w_gate[j][None].astype(BF16), dense_w_up[j][None].astype(BF16),
                      dense_w_down[j][None].astype(BF16), None, dense=True, tf=dense_w_gate.shape[2],
                      ple=ple, final=final)
        else:
            x2 = _moe(x2, ffn_norm[i], router_w[j], moe_w_gate[j].astype(BF16), moe_w_up[j].astype(BF16),
                      moe_w_down[j].astype(BF16), ple, final)
        x = x2.reshape(b_, t_, d_)
    return x
```

```python
import functools

import jax
import jax.numpy as jnp
from jax import lax
from jax.experimental import pallas as pl
from jax.experimental.pallas import tpu as pltpu

F32 = jnp.float32
BF16 = jnp.bfloat16
I32 = jnp.int32

N_ATTN_HEADS = 8
HEAD_DIM = 64
N_KV = 2
HEADS_PER_KV = N_ATTN_HEADS // N_KV
D_ATTN = N_ATTN_HEADS * HEAD_DIM
KV_W = N_KV * HEAD_DIM
N_GATES = 3
L_CMP = 32
STRIDE = 16
L_SLC = 64
N_SEL = 16
N_LOCAL = 2
W_WIN = 512
CMP_HID = 2 * HEAD_DIM
CONV_W = 4
C_LRU = 8.0
N_EXPERTS = 8
RMS_EPS = 1e-6
ATTN_SCALE = HEAD_DIM ** -0.5
LOG2E = 1.4426950408889634
LOG2E_HI = 1.4453125
LOG2E_LO = LOG2E - LOG2E_HI

V7X_LANES = 128
V7X_VMEM_LIMIT_BYTES = 56 * 1024 * 1024

TQ = 128
KT_SLC = 256
SLC_UNROLL = 4
KT_WIN = 128
N_WIN_TILES = W_WIN // KT_WIN + 1
V_ROWS = HEAD_DIM + 16
MASK_NEG = -1e30
SEL_NEG = -float(2 ** 20)
ROW_TILE = 512
RGLRU_CHUNK = 256
MOE_ROWS = 512
MOE_TOK_CHUNK = 512
CMB_ROWS = 256


def _cparams(semantics, vmem=None):
    return pltpu.CompilerParams(dimension_semantics=semantics, vmem_limit_bytes=vmem)


def _rms(x, g):
    ms = jnp.mean(x * x, axis=-1, keepdims=True)
    return x * lax.rsqrt(ms + RMS_EPS) * g


def _gelu_tanh(x):
    c = 0.7978845608028654
    return x * (0.5 * (1.0 + jnp.tanh(c * (x + 0.044715 * (x * x * x)))))


def _sigmoid(x):
    return 1.0 / (1.0 + jnp.exp(-x))


def _silu(x):
    return x * _sigmoid(x)


def _proj_in_kernel(x_ref, g_ref, wtok_ref, wtr_ref,
                    slc_ref, win_ref, kcvc_ref, ks_ref, kw_ref, xr_ref, yg_ref, qT_ref, vsT_ref, vwT_ref, gT_ref):
    hn = _rms(x_ref[0], g_ref[...]).astype(BF16)
    tok = jnp.dot(hn, wtok_ref[...], preferred_element_type=F32)
    kcvc_ref[0] = tok[:, 0:256]
    xr_ref[0] = tok[:, 512:1024].astype(BF16)
    yg_ref[0] = tok[:, 1024:1536].astype(BF16)
    lane = lax.broadcasted_iota(I32, (tok.shape[0], V7X_LANES), 1)
    slc_c = slc_ref[...].astype(F32)
    win_c = win_ref[...].astype(F32)
    for g in range(N_KV):
        in_g = lambda k2: jnp.where(lane < HEAD_DIM, k2 if g == 0 else pltpu.roll(k2, HEAD_DIM, axis=1), 0.0)
        ks_ref[0, g] = jnp.concatenate([in_g(tok[:, 256:384]) + slc_c[:, :V7X_LANES], slc_c[:, V7X_LANES:]],
                                       axis=1).astype(BF16)
        kw_ref[0, g] = (in_g(tok[:, 384:512]) + win_c).astype(BF16)
    tr = lax.dot_general(wtr_ref[...], hn, (((1,), (1,)), ((), ())),
                         preferred_element_type=F32)
    qT_ref[0] = (tr[0:512] * (ATTN_SCALE * LOG2E)).astype(BF16)
    ones = jnp.ones((V_ROWS - HEAD_DIM, tr.shape[1]), BF16)
    for g in range(N_KV):
        for v_ref, r0 in ((vsT_ref, 512), (vwT_ref, 640)):
            v_ref[0, g, 0:HEAD_DIM, :] = tr[r0 + g * HEAD_DIM:r0 + (g + 1) * HEAD_DIM].astype(BF16)
            v_ref[0, g, HEAD_DIM:V_ROWS, :] = ones
    gT_ref[0] = _sigmoid(tr[768:800])


def _key_position_columns(t_):
    pos = jnp.arange(t_, dtype=I32)
    dup = lambda vals: [v.astype(BF16)[:, None] for v in vals for _ in range(2)]
    pad_to = lambda cols, width: jnp.concatenate(
        cols + [jnp.zeros((t_, width - sum(c.shape[1] for c in cols)), BF16)], axis=1)
    lead = jnp.zeros((t_, HEAD_DIM), BF16)
    slc = pad_to([lead, jax.nn.one_hot(pos // L_SLC, t_ // L_SLC, dtype=BF16)]
                 + dup([pos // L_SLC, pos % L_SLC]), 2 * HEAD_DIM + t_ // L_SLC)
    win = pad_to([lead] + dup([pos % KT_WIN]), 2 * HEAD_DIM)
    return slc, win


def _proj_in(x, norm_g, w_in, slc_cols, win_cols):
    b_, t_, d_ = x.shape
    tm = min(ROW_TILE, t_)
    q, kc, vc, ks, vs, kw, vw, g, xr, yg = jnp.split(
        w_in, [512, 640, 768, 896, 1024, 1152, 1280, 1304, 1816], axis=1)
    wtok = jnp.concatenate([kc, vc, ks, kw, xr, yg], axis=1).astype(BF16)
    g4 = g.reshape(d_, N_KV, HEADS_PER_KV, N_GATES).transpose(0, 1, 3, 2)
    g4 = jnp.pad(g4.reshape(d_, N_KV, 12), ((0, 0), (0, 0), (0, 4))).reshape(d_, 32)
    wtr = jnp.concatenate([q, vs, vw, g4], axis=1).T.astype(BF16)
    nt = t_ // tm
    row = lambda shape: pl.BlockSpec((1, tm, shape), lambda b, i: (b, i, 0))
    col = lambda shape: pl.BlockSpec((1, shape, tm), lambda b, i: (b, 0, i))
    full = lambda a: pl.BlockSpec(a.shape, lambda b, i: (0,) * a.ndim)
    g2 = norm_g.reshape(1, d_)
    ws, ww = slc_cols.shape[1], win_cols.shape[1]
    grow = lambda width: pl.BlockSpec((1, N_KV, tm, width), lambda b, i: (b, 0, i, 0))
    gcol = pl.BlockSpec((1, N_KV, V_ROWS, tm), lambda b, i: (b, 0, 0, i))
    outs = pl.pallas_call(
        _proj_in_kernel,
        name="proj_in",
        grid=(b_, nt),
        in_specs=[row(d_), full(g2), full(wtok), full(wtr),
                  pl.BlockSpec((tm, ws), lambda b, i: (i, 0)), pl.BlockSpec((tm, ww), lambda b, i: (i, 0))],
        out_specs=[row(256), grow(ws), grow(ww), row(512), row(512), col(512), gcol, gcol, col(32)],
        out_shape=[
            jax.ShapeDtypeStruct((b_, t_, 256), F32),
            jax.ShapeDtypeStruct((b_, N_KV, t_, ws), BF16),
            jax.ShapeDtypeStruct((b_, N_KV, t_, ww), BF16),
            jax.ShapeDtypeStruct((b_, t_, 512), BF16),
            jax.ShapeDtypeStruct((b_, t_, 512), BF16),
            jax.ShapeDtypeStruct((b_, 512, t_), BF16),
            jax.ShapeDtypeStruct((b_, N_KV, V_ROWS, t_), BF16),
            jax.ShapeDtypeStruct((b_, N_KV, V_ROWS, t_), BF16),
            jax.ShapeDtypeStruct((b_, 32, t_), F32),
        ],
        compiler_params=_cparams(("parallel", "parallel"), V7X_VMEM_LIMIT_BYTES),
    )(x, g2, wtok, wtr, slc_cols, win_cols)
    return outs


def _compress_kernel(xc_ref, pa_ref, pb_ref, w1a_ref, w1b_ref, w2_ref, out_ref):
    xc = xc_ref[0].astype(BF16)
    n_chunk = xc.shape[0]
    a = jnp.dot(xc, w1a_ref[...], preferred_element_type=F32)
    bm = jnp.dot(xc, w1b_ref[...], preferred_element_type=F32)
    posc = (jnp.dot(pa_ref[...], w1a_ref[...], preferred_element_type=F32)
            + jnp.dot(pb_ref[...], w1b_ref[...], preferred_element_type=F32))[0:1]
    row = lax.broadcasted_iota(I32, bm.shape, 0)
    bm_up = jnp.where(row < n_chunk - 1, pltpu.roll(bm, n_chunk - 1, axis=0), 0.0)
    hid = _gelu_tanh(a + bm_up + posc).astype(BF16)
    out_ref[0] = jnp.dot(hid, w2_ref[...], preferred_element_type=F32)


def _compress(kcvc, cmp_pos, cmp_w1, cmp_w2):
    b_, t_, _ = kcvc.shape
    n_chunk = t_ // STRIDE
    xc = kcvc.reshape(b_, n_chunk, STRIDE * 256)
    w1 = cmp_w1.reshape(2, 2, STRIDE, HEAD_DIM, CMP_HID)

    def expand(half):
        w = w1[:, half]
        full = jnp.einsum('wldo,wx,gy->lxgdwyo', w, jnp.eye(2, dtype=F32), jnp.eye(2, dtype=F32))
        return full.reshape(STRIDE * 256, 4 * CMP_HID).astype(BF16)

    w1a, w1b = expand(0), expand(1)
    pos = cmp_pos.reshape(2, 2, STRIDE, HEAD_DIM)

    def posrow(half):
        p = jnp.broadcast_to(pos[:, half][:, None], (2, N_KV, STRIDE, HEAD_DIM))
        p = p.transpose(2, 0, 1, 3).reshape(1, STRIDE * 256)
        return jnp.pad(p, ((0, 7), (0, 0))).astype(BF16)

    pa, pb = posrow(0), posrow(1)
    w2 = jnp.einsum('whd,wx,gy->wghxyd', cmp_w2, jnp.eye(2, dtype=F32), jnp.eye(2, dtype=F32))
    w2 = w2.reshape(4 * CMP_HID, 4 * HEAD_DIM).astype(BF16)
    full = lambda a: pl.BlockSpec(a.shape, lambda b: (0,) * a.ndim)
    return pl.pallas_call(
        _compress_kernel,
        name="compress_kv",
        grid=(b_,),
        in_specs=[pl.BlockSpec((1, n_chunk, STRIDE * 256), lambda b: (b, 0, 0)),
                  full(pa), full(pb), full(w1a), full(w1b), full(w2)],
        out_specs=pl.BlockSpec((1, n_chunk, 256), lambda b: (b, 0, 0)),
        out_shape=jax.ShapeDtypeStruct((b_, n_chunk, 256), F32),
        compiler_params=_cparams(("parallel",), V7X_VMEM_LIMIT_BYTES),
    )(xc, pa, pb, w1a, w1b, w2)


def _head_lanes(rows):
    r = rows.shape[0] // HEADS_PER_KV
    return jnp.concatenate([rows[h * r:(h + 1) * r, :] for h in range(HEADS_PER_KV)], axis=1)


def _attn_kernel(qT_ref, gT_ref, kc_ref, vcT_ref, ks_ref, vsT_ref, kw_ref, vwT_ref, mt_ref, o_ref, s_scr,
                 tile_list):
    i = pl.program_id(1)
    n_cmp = kc_ref.shape[2]
    n_slc = mt_ref.shape[0]
    nl = HEADS_PER_KV * TQ
    hd_rows = HEADS_PER_KV * HEAD_DIM

    lane = lax.broadcasted_iota(I32, (1, nl), 1)
    t_loc = lane % TQ
    t_row = i * TQ + t_loc

    def extra_rows(n_rows, rows):
        ridx = lax.broadcasted_iota(I32, (16, nl), 0)
        out = jnp.zeros((16, nl), F32)
        for k, r in enumerate(rows):
            out = jnp.where(ridx == k, r, out)
        return jnp.concatenate([out.astype(BF16), jnp.zeros((n_rows - 16, nl), BF16)], axis=0)

    ok_c = (lax.broadcasted_iota(I32, (n_cmp, nl), 0) * STRIDE + (L_CMP - 1)) <= t_row
    r_idx = lax.broadcasted_iota(I32, (KT_WIN, nl), 0)
    s_idx = lax.broadcasted_iota(I32, (n_slc, TQ), 0)
    cur = (i * TQ + lax.broadcasted_iota(I32, (1, TQ), 1)) // L_SLC
    valid = s_idx <= cur
    forced = valid & ((s_idx == 0) | (s_idx > cur - N_LOCAL))
    s_idx_f = s_idx.astype(F32)
    n_full = (i * TQ) // KT_SLC
    mt = mt_ref[...]

    def pick(_, score):
        m = jnp.max(score, axis=0, keepdims=True)
        idx = jnp.min(jnp.where(score == m, s_idx_f, 1.0e9), axis=0, keepdims=True)
        return jnp.where(s_idx_f == idx, -2.0, score)

    def stage_a(g, bs, tile, slot, causal):
        k0 = pl.multiple_of(tile * KT_SLC, KT_SLC)
        s = jnp.dot(ks_ref[0, g, pl.ds(k0, KT_SLC), :], bs, preferred_element_type=F32)
        if causal:
            pos = k0 + lax.broadcasted_iota(I32, (KT_SLC, nl), 0)
            s = jnp.where(pos <= t_row, s, MASK_NEG)
        s_scr[g, slot] = s
        return jnp.max(s, axis=0, keepdims=True)

    def stage_b(g, tile, slot, m, acc, m_tile):
        k0 = pl.multiple_of(tile * KT_SLC, KT_SLC)
        m_new = jnp.maximum(m, m_tile)
        p = jnp.exp2(s_scr[g, slot] - m_new)
        acc = jnp.exp2(m - m_new) * acc + jnp.dot(vsT_ref[0, g, :, pl.ds(k0, KT_SLC)], p.astype(BF16),
                                                  preferred_element_type=F32)
        return m_new, acc

    groups = range(N_KV)

    def setup(g):
        q4 = _head_lanes(qT_ref[0, g * hd_rows:(g + 1) * hd_rows, :])
        head = g * HEADS_PER_KV + lane // TQ + 1
        slope = lax.bitcast_convert_type((127 - head) << 23, F32)
        return dict(q4=q4, slope=slope, s_hi=slope * LOG2E_HI, s_lo=slope * LOG2E_LO)

    st = [setup(g) for g in groups]

    def cmp_scores(g):
        s_hi, s_lo = st[g]["s_hi"], st[g]["s_lo"]
        bc = jnp.concatenate(
            [st[g]["q4"], extra_rows(64, [256.0 * s_hi, 256.0 * s_lo, 16.0 * s_hi, 16.0 * s_lo])], axis=0)
        return jnp.dot(kc_ref[0, g], bc, preferred_element_type=F32)

    def cmp_softmax(g, s_c):
        s_c = jnp.where(ok_c, s_c, MASK_NEG)
        m_c = jnp.max(s_c, axis=0, keepdims=True)
        e_c = jnp.exp2(s_c - m_c)
        l_c = jnp.sum(e_c, axis=0, keepdims=True)
        inv_c = jnp.where(m_c > 0.5 * MASK_NEG, 1.0 / jnp.maximum(l_c, 1e-30), 0.0)
        p_c = e_c * inv_c
        o_c = jnp.dot(vcT_ref[0, g], p_c.astype(BF16), preferred_element_type=F32)
        p_grp = (p_c[:, 0:TQ] + p_c[:, TQ:2 * TQ]) + (p_c[:, 2 * TQ:3 * TQ] + p_c[:, 3 * TQ:4 * TQ])
        p1 = p_grp.astype(BF16)
        r1 = p_grp - p1.astype(F32)
        p2 = r1.astype(BF16)
        p3 = (r1 - p2.astype(F32)).astype(BF16)
        p_slc = (jnp.dot(mt, p1, preferred_element_type=F32) + jnp.dot(mt, p2, preferred_element_type=F32)
                 + jnp.dot(mt, p3, preferred_element_type=F32))
        return o_c, p_slc

    def win_scores(g):
        slope = st[g]["slope"]
        bw = jnp.concatenate([st[g]["q4"], extra_rows(64, [st[g]["s_hi"], st[g]["s_lo"]])], axis=0)
        s_tiles, c_tiles = [], []
        for a in range(N_WIN_TILES):
            tile = i - (N_WIN_TILES - 1) + a
            k0 = pl.multiple_of(jnp.maximum(tile, 0) * KT_WIN, KT_WIN)
            s = jnp.dot(kw_ref[0, g, pl.ds(k0, KT_WIN), :], bw, preferred_element_type=F32)
            if a == 0:
                s = jnp.where(r_idx > t_loc, s, MASK_NEG)
            elif a == N_WIN_TILES - 1:
                s = jnp.where(r_idx <= t_loc, s, MASK_NEG)
            s_tiles.append(s)
            c_a = slope * (LOG2E * KT_WIN * (a - (N_WIN_TILES - 1)))
            c_tiles.append(jnp.where(tile >= 0, c_a, MASK_NEG))
        return s_tiles, c_tiles

    def win_softmax(g, s_tiles, c_tiles):
        m_w = s_tiles[0].max(axis=0, keepdims=True) + c_tiles[0]
        for s, c_a in zip(s_tiles[1:], c_tiles[1:]):
            m_w = jnp.maximum(m_w, s.max(axis=0, keepdims=True) + c_a)
        acc_w = jnp.zeros((V_ROWS, nl), F32)
        for a, (s, c_a) in enumerate(zip(s_tiles, c_tiles)):
            tile = i - (N_WIN_TILES - 1) + a
            k0 = pl.multiple_of(jnp.maximum(tile, 0) * KT_WIN, KT_WIN)
            p = jnp.exp2(s - (m_w - c_a))
            acc_w = acc_w + jnp.dot(vwT_ref[0, g, :, pl.ds(k0, KT_WIN)], p.astype(BF16),
                                    preferred_element_type=F32)
        return acc_w[0:HEAD_DIM] * (1.0 / acc_w[HEAD_DIM:HEAD_DIM + 1])

    s_cmp = [cmp_scores(g) for g in groups]
    s_win = [win_scores(g) for g in groups]
    cmp_out, scores = [], []
    for g in groups:
        cmp_out.append(cmp_softmax(g, s_cmp[g]))
        score0 = jnp.where(forced, -2.0, jnp.where(valid, cmp_out[g][1], -1.0))
        scores.append(lax.fori_loop(0, N_SEL - (N_LOCAL + 1), pick, score0, unroll=True))
    o_win = [win_softmax(g, *s_win[g]) for g in groups]
    score = jnp.concatenate(scores, axis=1)

    def select(g):
        sel = valid & (score[:, g * TQ:(g + 1) * TQ] < -1.5)
        sel4 = jnp.concatenate([sel] * HEADS_PER_KV, axis=1)
        blk_rows = jnp.where(sel4, 0.0, SEL_NEG).astype(BF16)
        s_hi, s_lo = st[g]["s_hi"], st[g]["s_lo"]
        bs = jnp.concatenate(
            [st[g]["q4"], blk_rows, extra_rows(64, [L_SLC * s_hi, L_SLC * s_lo, s_hi, s_lo])], axis=0)
        m_tile0 = stage_a(g, bs, n_full, 0, True)
        gt = gT_ref[0, g * 16:(g + 1) * 16, :]
        gates = [_head_lanes(gt[k * HEADS_PER_KV:(k + 1) * HEADS_PER_KV, :]) for k in range(N_GATES)]
        return dict(bs=bs, base=gates[0] * cmp_out[g][0] + gates[2] * o_win[g], g_s=gates[1], m_tile0=m_tile0)

    grp = [select(g) for g in groups]

    blocks_per_tile = KT_SLC // L_SLC
    n_tiles = n_slc // blocks_per_tile
    assert n_tiles <= 32
    valid2 = jnp.concatenate([valid] * N_KV, axis=1)
    sel_all = jnp.where(valid2 & (score < -1.5), 1.0, 0.0).astype(BF16)
    blk_cnt = lax.dot_general(jnp.ones((8, N_KV * TQ), BF16), sel_all, (((1,), (1,)), ((), ())),
                              preferred_element_type=F32)
    memb = jnp.where(lax.broadcasted_iota(I32, (n_slc, V7X_LANES), 0) // blocks_per_tile
                     == lax.broadcasted_iota(I32, (n_slc, V7X_LANES), 1), 1.0, 0.0).astype(BF16)
    tile_cnt = jnp.dot(jnp.where(blk_cnt > 0.5, 1.0, 0.0).astype(BF16), memb,
                       preferred_element_type=F32)[0:1]
    lane_t = lax.broadcasted_iota(I32, (1, V7X_LANES), 1)
    need = (tile_cnt > 0.5) & (lane_t < n_full)
    bit = lax.bitcast_convert_type(((lane_t & 15) + 127) << 23, F32)
    lo_bits = jnp.sum(jnp.where(need & (lane_t < 16), bit, 0.0)).astype(I32)
    hi_bits = jnp.sum(jnp.where(need & (lane_t >= 16), bit, 0.0)).astype(I32)
    n_sel = jnp.int32(0)
    for t in range(n_tiles - 1):
        tile_list[n_sel] = t
        n_sel = n_sel + (((lo_bits if t < 16 else hi_bits) >> (t % 16)) & 1)

    order = lambda k: jnp.where(k == 0, n_full, tile_list[jnp.maximum(k - 1, 0)])

    def half(carry, tile_b, slot_b, tile_a):
        m_next = [c[2] if tile_a is None else stage_a(g, grp[g]["bs"], tile_a, 1 - slot_b, False)
                  for g, c in enumerate(carry)]
        out = []
        for g in range(N_KV):
            m, acc, m_tile = carry[g]
            out.append(stage_b(g, tile_b, slot_b, m, acc, m_tile) + (m_next[g],))
        return tuple(out)

    def run(carry, k0, n, last):
        for j in range(n):
            nxt = None if (last and j == n - 1) else order(k0 + j + 1)
            carry = half(carry, order(k0 + j), j % 2, nxt)
        return carry

    init = tuple((jnp.full((1, nl), MASK_NEG, F32), jnp.zeros((V_ROWS, nl), F32), grp[g]["m_tile0"])
                 for g in range(N_KV))
    carry = lax.fori_loop(0, n_sel // SLC_UNROLL, lambda p, c: run(c, SLC_UNROLL * p, SLC_UNROLL, False), init)
    k_tail = SLC_UNROLL * (n_sel // SLC_UNROLL)
    carry = lax.switch(n_sel % SLC_UNROLL, [functools.partial(run, k0=k_tail, n=r + 1, last=True)
                                            for r in range(SLC_UNROLL)], carry)
    for g in range(N_KV):
        _, acc, _ = carry[g]
        o_s = acc[0:HEAD_DIM] * (1.0 / acc[HEAD_DIM:HEAD_DIM + 1])
        out = grp[g]["base"] + grp[g]["g_s"] * o_s
        for h in range(HEADS_PER_KV):
            r0 = g * hd_rows + h * HEAD_DIM
            o_ref[0, r0:r0 + HEAD_DIM, :] = out[:, h * TQ:(h + 1) * TQ].astype(o_ref.dtype)


def _attention(qT, gT, kcmp, ks_aug, vsT4, kw_aug, vwT4):
    b_, _, t_ = qT.shape
    n_cmp = t_ // STRIDE
    n_slc = t_ // L_SLC
    c = jnp.arange(n_cmp, dtype=I32)
    cmp_cols = jnp.concatenate([v.astype(BF16)[:, None] for v in (c // 16, c // 16, c % 16, c % 16)]
                               + [jnp.zeros((n_cmp, HEAD_DIM - 4), BF16)], axis=1)
    kc4 = kcmp[..., 0:KV_W].reshape(b_, n_cmp, N_KV, HEAD_DIM).transpose(0, 2, 1, 3).astype(BF16)
    kc_aug = jnp.concatenate(
        [kc4, jnp.broadcast_to(cmp_cols, (b_, N_KV) + cmp_cols.shape)], axis=-1)
    vcT = kcmp[..., KV_W:2 * KV_W].reshape(b_, n_cmp, N_KV, HEAD_DIM).transpose(0, 2, 3, 1).astype(BF16)
    s = jnp.arange(n_slc, dtype=I32)[:, None]
    cc = c[None, :]
    r_slc = L_SLC // STRIDE
    mt = (((cc >= r_slc * s) & (cc < r_slc * s + r_slc)).astype(F32)
          + ((cc + 1 >= r_slc * s) & (cc + 1 < r_slc * s + r_slc)).astype(F32))
    mt = jnp.where(cc < n_cmp - 1, mt, 0.0).astype(BF16)
    kv_spec = lambda a: pl.BlockSpec((1,) + a.shape[1:], lambda b, i: (b, 0, 0, 0))
    return pl.pallas_call(
        _attn_kernel,
        name="nsa_attention",
        grid=(b_, t_ // TQ),
        in_specs=[pl.BlockSpec((1, D_ATTN, TQ), lambda b, i: (b, 0, i)),
                  pl.BlockSpec((1, 32, TQ), lambda b, i: (b, 0, i)),
                  kv_spec(kc_aug), kv_spec(vcT), kv_spec(ks_aug), kv_spec(vsT4),
                  kv_spec(kw_aug), kv_spec(vwT4),
                  pl.BlockSpec(mt.shape, lambda b, i: (0, 0))],
        out_specs=pl.BlockSpec((1, D_ATTN, TQ), lambda b, i: (b, 0, i)),
        out_shape=jax.ShapeDtypeStruct((b_, D_ATTN, t_), BF16),
        scratch_shapes=[pltpu.VMEM((N_KV, 2, KT_SLC, HEADS_PER_KV * TQ), F32), pltpu.SMEM((32,), I32)],
        compiler_params=_cparams(("parallel", "arbitrary"), V7X_VMEM_LIMIT_BYTES),
    )(qT, gT, kc_aug, vcT, ks_aug, vsT4, kw_aug, vwT4, mt)


def _rglru_seg_pitch(tc):
    return tc // 8 + 8


def _rglru_kernel(xr_ref, yg_ref, cw_ref, cb_ref, wa_ref, ba_ref, wx_ref, bx_ref, lam_ref,
                  o_ref, tail_ref, h_ref, xs_scr, ys_scr, os_scr):
    tc = xr_ref.shape[1]

    @pl.when(pl.program_id(1) == 0)
    def _():
        tail_ref[...] = jnp.zeros_like(tail_ref)
        h_ref[...] = jnp.zeros_like(h_ref)

    n_v = tc // 8
    n_slab = xr_ref.shape[2] // V7X_LANES
    pitch = xs_scr.shape[1] // 8
    for c in range(n_slab):
        for s in range(8):
            lanes = slice(c * V7X_LANES, (c + 1) * V7X_LANES)
            xs_scr[c, s * pitch:s * pitch + n_v, :] = xr_ref[0, s * n_v:(s + 1) * n_v, lanes].astype(F32)
            ys_scr[c, s * pitch:s * pitch + n_v, :] = yg_ref[0, s * n_v:(s + 1) * n_v, lanes].astype(F32)

    def piece(scr, v):
        return jnp.concatenate([scr[c, pl.ds(v, 8, stride=pitch), :] for c in range(n_slab)], axis=1)

    x = [piece(xs_scr, v) for v in range(n_v)]
    row8 = lax.broadcasted_iota(I32, (8, x[0].shape[1]), 0)
    tail = tail_ref[...]
    before = {d: jnp.where(row8 == 0, tail[CONV_W - 1 - d:CONV_W - d, :], pltpu.roll(x[n_v - d], 1, axis=0))
              for d in range(1, CONV_W)}
    tail_ref[0:CONV_W - 1, :] = jnp.concatenate([x[n_v - d][7:8, :] for d in range(CONV_W - 1, 0, -1)], axis=0)
    cw = cw_ref[...]
    cb = cb_ref[...]
    xc = []
    for v in range(n_v):
        acc = x[v] * cw[CONV_W - 1:CONV_W, :] + cb
        for d in range(1, CONV_W):
            acc = acc + (x[v - d] if v >= d else before[d - v]) * cw[CONV_W - 1 - d:CONV_W - d, :]
        xc.append(acc)

    xb = jnp.concatenate(xc, axis=0).astype(BF16)
    r = _sigmoid(jnp.dot(xb, wa_ref[...], preferred_element_type=F32) + ba_ref[...])
    gi = _sigmoid(jnp.dot(xb, wx_ref[...], preferred_element_type=F32) + bx_ref[...])
    z = -lam_ref[...]
    softplus = jnp.maximum(z, 0.0) + jnp.log1p(jnp.exp(-jnp.abs(z)))
    log_a = (-C_LRU * r) * softplus
    a = jnp.exp(log_a)
    th = jnp.tanh(log_a)
    neg_expm1 = 2.0 * th / (th - 1.0)
    bb = jnp.sqrt(neg_expm1) * gi * jnp.concatenate(xc, axis=0)

    a_cum, b_cum = [a[0:8]], [bb[0:8]]
    for v in range(1, n_v):
        a_v = a[v * 8:(v + 1) * 8]
        b_cum.append(a_v * b_cum[-1] + bb[v * 8:(v + 1) * 8])
        a_cum.append(a_v * a_cum[-1])
    h_in = [h_ref[0:1, :]]
    for s in range(7):
        h_in.append(a_cum[-1][s:s + 1, :] * h_in[-1] + b_cum[-1][s:s + 1, :])
    h_ref[0:1, :] = a_cum[-1][7:8, :] * h_in[-1] + b_cum[-1][7:8, :]
    h_in = jnp.concatenate(h_in, axis=0)
    for v in range(n_v):
        out = (a_cum[v] * h_in + b_cum[v]) * _gelu_tanh(piece(ys_scr, v))
        for c in range(n_slab):
            os_scr[c, pl.ds(v, 8, stride=pitch), :] = out[:, c * V7X_LANES:(c + 1) * V7X_LANES]
    for s in range(8):
        o_ref[0, s * n_v:(s + 1) * n_v, :] = jnp.concatenate(
            [os_scr[c, s * pitch:s * pitch + n_v, :] for c in range(n_slab)], axis=1).astype(o_ref.dtype)


def _block_diag(w):
    n, c, d = w.shape
    return jnp.einsum('ncd,nm->ncmd', w, jnp.eye(n, dtype=w.dtype)).reshape(n * c, n * d)


def _rglru(xr, yg, conv_w, conv_b, wa, ba, wx, bx, lam):
    b_, t_, c_ = xr.shape
    tc = min(RGLRU_CHUNK, t_)
    wa_bd = _block_diag(wa).astype(BF16)
    wx_bd = _block_diag(wx).astype(BF16)
    vec = lambda v: v.reshape(1, c_)
    full = lambda a: pl.BlockSpec(a.shape, lambda b, i: (0,) * a.ndim)
    blk = pl.BlockSpec((1, tc, c_), lambda b, i: (b, i, 0))
    args = (xr, yg, conv_w, vec(conv_b), wa_bd, vec(ba), wx_bd, vec(bx), vec(lam))
    return pl.pallas_call(
        _rglru_kernel,
        name="rglru",
        grid=(b_, t_ // tc),
        in_specs=[blk, blk] + [full(a) for a in args[2:]],
        out_specs=blk,
        out_shape=jax.ShapeDtypeStruct((b_, t_, c_), BF16),
        scratch_shapes=[pltpu.VMEM((8, c_), F32), pltpu.VMEM((8, c_), F32)]
                       + [pltpu.VMEM((c_ // V7X_LANES, 8 * _rglru_seg_pitch(tc), V7X_LANES), F32)] * 3,
        compiler_params=_cparams(("parallel", "arbitrary"), V7X_VMEM_LIMIT_BYTES),
    )(*args)


def _out_proj_kernel(oaT_ref, orec_ref, x_ref, ga_ref, gr_ref, wa_ref, wr_ref, o_ref):
    oaT = oaT_ref[0].astype(F32)
    ms = jnp.mean(oaT * oaT, axis=0, keepdims=True)
    na = (oaT * lax.rsqrt(ms + RMS_EPS)).T * ga_ref[...]
    nr = _rms(orec_ref[0].astype(F32), gr_ref[...])
    y = (jnp.dot(na.astype(BF16), wa_ref[...], preferred_element_type=F32)
         + jnp.dot(nr.astype(BF16), wr_ref[...], preferred_element_type=F32))
    o_ref[0] = x_ref[0] + y


def _out_proj(oaT, orec, x, g_attn, g_rec, w_out):
    b_, t_, d_ = x.shape
    tm = min(ROW_TILE, t_)
    wa = w_out[:D_ATTN].astype(BF16)
    wr = w_out[D_ATTN:].astype(BF16)
    ga = g_attn.reshape(1, -1)
    gr = g_rec.reshape(1, -1)
    full = lambda a: pl.BlockSpec(a.shape, lambda b, i: (0,) * a.ndim)
    return pl.pallas_call(
        _out_proj_kernel,
        name="out_proj",
        grid=(b_, t_ // tm),
        in_specs=[pl.BlockSpec((1, D_ATTN, tm), lambda b, i: (b, 0, i)),
                  pl.BlockSpec((1, tm, orec.shape[2]), lambda b, i: (b, i, 0)),
                  pl.BlockSpec((1, tm, d_), lambda b, i: (b, i, 0)),
                  full(ga), full(gr), full(wa), full(wr)],
        out_specs=pl.BlockSpec((1, tm, d_), lambda b, i: (b, i, 0)),
        out_shape=jax.ShapeDtypeStruct((b_, t_, d_), F32),
        compiler_params=_cparams(("parallel", "parallel"), V7X_VMEM_LIMIT_BYTES),
    )(oaT, orec, x, ga, gr, wa, wr)


def _ple_apply(x, p_ref, g_ref, wg_ref, wp_ref, fg_ref, *, final):
    gate = _sigmoid(jnp.dot(_rms(x, g_ref[...]).astype(BF16), wg_ref[...], preferred_element_type=F32))
    proj = jnp.dot(p_ref[...].astype(BF16), wp_ref[...], preferred_element_type=F32)
    y = x + gate * proj
    return _rms(y, fg_ref[...]) if final else y


def _ple_operands(p, norm_g, w_gate, w_proj, final_g, rows, row_map, const_map):
    d_ = w_gate.shape[0]
    arrays = (p, norm_g.reshape(1, d_), w_gate.astype(BF16), w_proj.astype(BF16), final_g.reshape(1, d_))
    once = dict(pipeline_mode=pl.Buffered(1))
    specs = [pl.BlockSpec((rows, p.shape[1]), row_map)] + [pl.BlockSpec(a.shape, const_map, **once)
                                                            for a in arrays[1:]]
    return arrays, specs


def _ffn_kernel(be_ref, x_ref, g_ref, wg_ref, wu_ref, wd_ref, *rest, dense, final):
    ple_refs, (o_ref, hn_ref, acc_ref) = rest[:-3], rest[-3:]
    f = pl.program_id(1)
    used = True if dense else pl.program_id(0) < be_ref[pl.num_programs(0)]

    @pl.when(f == 0)
    def _():
        if dense:
            x = x_ref[...]
            hn_ref[...] = _rms(x, g_ref[...]).astype(BF16)
            acc_ref[...] = x
        else:
            hn_ref[...] = _unpack_bf16_pairs(x_ref[...])
            acc_ref[...] = jnp.zeros_like(acc_ref)

    def accumulate():
        hn = hn_ref[...]
        gate = jnp.dot(hn, wg_ref[0], preferred_element_type=F32)
        up = jnp.dot(hn, wu_ref[0], preferred_element_type=F32)
        act = (_silu(gate) * up).astype(BF16)
        acc_ref[...] += jnp.dot(act, wd_ref[0], preferred_element_type=F32)

    if dense:
        accumulate()
    else:
        pl.when(used)(accumulate)

    @pl.when(f == pl.num_programs(1) - 1)
    def _():
        if dense:
            o_ref[...] = _ple_apply(acc_ref[...], *ple_refs, final=final)
        else:
            o_ref[...] = acc_ref[...].astype(o_ref.dtype)


def _ffn(x, norm_g, wg, wu, wd, block_e, n_used=None, *, dense, tf, ple=None, final=False):
    n = x.shape[0]
    d_ = wg.shape[1]
    tm = min(ROW_TILE if dense else MOE_ROWS, n)
    ff = wg.shape[2]
    assert ff % tf == 0 and n % tm == 0
    if block_e is None:
        block_e = jnp.zeros((n // tm,), I32)
    if n_used is None:
        n_used = jnp.int32(n // tm)
    n_blk, nf = n // tm, ff // tf
    be_all = jnp.concatenate([block_e, jnp.reshape(n_used, (1,)).astype(I32)])
    wmode = dict(pipeline_mode=pl.Buffered(1)) if (wg.shape[0] == 1 and ff == tf) else {}
    ftile = lambda i, f, be: jnp.where(i < be[n_blk], f, nf - 1)
    ple_arrays, ple_specs = ((), [])
    if dense:
        ple_arrays, ple_specs = _ple_operands(*ple, tm, lambda i, f, be: (i, 0), lambda i, f, be: (0, 0))
    grid_spec = pltpu.PrefetchScalarGridSpec(
        num_scalar_prefetch=1,
        grid=(n_blk, nf),
        in_specs=[pl.BlockSpec((tm, x.shape[1]), lambda i, f, be: (i, 0)),
                  pl.BlockSpec((1, d_), lambda i, f, be: (0, 0)),
                  pl.BlockSpec((1, d_, tf), lambda i, f, be: (be[i], 0, ftile(i, f, be)), **wmode),
                  pl.BlockSpec((1, d_, tf), lambda i, f, be: (be[i], 0, ftile(i, f, be)), **wmode),
                  pl.BlockSpec((1, tf, d_), lambda i, f, be: (be[i], ftile(i, f, be), 0), **wmode)] + ple_specs,
        out_specs=pl.BlockSpec((tm, d_), lambda i, f, be: (i, 0)),
        scratch_shapes=[pltpu.VMEM((tm, d_), BF16), pltpu.VMEM((tm, d_), F32)],
    )
    return pl.pallas_call(
        functools.partial(_ffn_kernel, dense=dense, final=final),
        name="ffn_dense" if dense else "ffn_expert",
        grid_spec=grid_spec,
        out_shape=jax.ShapeDtypeStruct((n, d_), F32 if dense else BF16),
        compiler_params=_cparams(("parallel", "arbitrary"), V7X_VMEM_LIMIT_BYTES),
    )(be_all, x, norm_g.reshape(1, d_), wg, wu, wd, *ple_arrays)


def _pack_bf16_pairs(x):
    w = x.shape[1] // 2
    bits = lax.bitcast_convert_type(x.astype(BF16).astype(F32), jnp.uint32)
    return (bits[:, :w] >> 16) | bits[:, w:]


def _unpack_bf16_pairs(words):
    lo = lax.bitcast_convert_type(words << 16, F32)
    hi = lax.bitcast_convert_type(words & jnp.uint32(0xFFFF0000), F32)
    return jnp.concatenate([lo, hi], axis=1).astype(BF16)


def _router_kernel(x_ref, g_ref, wr_ref, tri_ref, h_ref, e_ref, p_ref, r_ref, cb_ref, tot_ref, cnt_ref):
    @pl.when(pl.program_id(0) == 0)
    def _():
        cnt_ref[...] = jnp.zeros_like(cnt_ref)

    hn = _rms(x_ref[...], g_ref[...])
    h_ref[...] = _pack_bf16_pairs(hn)
    logits = lax.dot_general(wr_ref[...], hn, (((1,), (1,)), ((), ())),
                             precision=lax.Precision.HIGHEST, preferred_element_type=F32)
    eidx = lax.broadcasted_iota(I32, logits.shape, 0)
    m1 = jnp.max(logits, axis=0, keepdims=True)
    i1 = jnp.min(jnp.where(logits == m1, eidx, N_EXPERTS), axis=0, keepdims=True)
    rest = jnp.where(eidx == i1, -jnp.inf, logits)
    m2 = jnp.max(rest, axis=0, keepdims=True)
    i2 = jnp.min(jnp.where(rest == m2, eidx, N_EXPERTS), axis=0, keepdims=True)
    e2 = jnp.exp(m2 - m1)
    inv = 1.0 / (1.0 + e2)
    e_ref[...] = jnp.concatenate([i1, i2], axis=0)
    p_ref[...] = jnp.concatenate([inv, e2 * inv], axis=0)

    tm = logits.shape[1]
    chosen = jnp.where((eidx == i1) | (eidx == i2), 1.0, 0.0)
    incl = jnp.dot(chosen.astype(BF16), tri_ref[...], preferred_element_type=F32)
    before = incl - chosen + cnt_ref[:, 0:1]
    r_ref[...] = jnp.concatenate(
        [jnp.sum(jnp.where(eidx == i1, before, 0.0), axis=0, keepdims=True),
         jnp.sum(jnp.where(eidx == i2, before, 0.0), axis=0, keepdims=True)], axis=0).astype(I32)
    lane = lax.broadcasted_iota(I32, cb_ref.shape, 1)
    cb = jnp.zeros(cb_ref.shape, F32)
    for c in range(tm // CMB_ROWS):
        cb = jnp.where(lane == c, before[:, c * CMB_ROWS:c * CMB_ROWS + 1], cb)
    cb_ref[...] = cb.astype(I32)
    total = cnt_ref[...] + incl[:, tm - 1:tm]
    cnt_ref[...] = total
    tot_ref[...] = total.astype(I32)


def _router(x, norm_g, router_w):
    n, d_ = x.shape
    tm = min(ROW_TILE, n)
    assert tm % CMB_ROWS == 0
    wr = router_w.T
    tri = (jnp.arange(tm)[:, None] <= jnp.arange(tm)[None, :]).astype(BF16)
    h, top_e, top_p, rank, cb, tot = pl.pallas_call(
        _router_kernel,
        name="router",
        grid=(n // tm,),
        in_specs=[pl.BlockSpec((tm, d_), lambda i: (i, 0)),
                  pl.BlockSpec((1, d_), lambda i: (0, 0)),
                  pl.BlockSpec(wr.shape, lambda i: (0, 0)),
                  pl.BlockSpec(tri.shape, lambda i: (0, 0))],
        out_specs=[pl.BlockSpec((tm, d_ // 2), lambda i: (i, 0)),
                   pl.BlockSpec((2, tm), lambda i: (0, i)),
                   pl.BlockSpec((2, tm), lambda i: (0, i)),
                   pl.BlockSpec((2, tm), lambda i: (0, i)),
                   pl.BlockSpec((N_EXPERTS, V7X_LANES), lambda i: (0, i)),
                   pl.BlockSpec((N_EXPERTS, V7X_LANES), lambda i: (0, 0))],
        out_shape=[jax.ShapeDtypeStruct((n, d_ // 2), jnp.uint32),
                   jax.ShapeDtypeStruct((2, n), I32),
                   jax.ShapeDtypeStruct((2, n), F32),
                   jax.ShapeDtypeStruct((2, n), I32),
                   jax.ShapeDtypeStruct((N_EXPERTS, n // tm * V7X_LANES), I32),
                   jax.ShapeDtypeStruct((N_EXPERTS, V7X_LANES), I32)],
        scratch_shapes=[pltpu.VMEM((N_EXPERTS, V7X_LANES), F32)],
        compiler_params=_cparams(("arbitrary",), V7X_VMEM_LIMIT_BYTES),
    )(x, norm_g.reshape(1, d_), wr, tri)
    cb = cb.reshape(N_EXPERTS, n // tm, V7X_LANES)[:, :, :tm // CMB_ROWS].reshape(N_EXPERTS, n // CMB_ROWS).T
    return h, top_e, top_p, rank, cb, tot[:, 0]


SCATTER_UNROLL = 16


def _scatter_kernel(dest_ref, h_ref, zeros_ref, xs_ref, sem):
    del zeros_ref
    n_tok = h_ref.shape[0]

    def row_copy(t, k):
        return pltpu.make_async_copy(h_ref.at[pl.ds(t, 1), :], xs_ref.at[pl.ds(dest_ref[k, t], 1), :], sem)

    def issue(u, carry):
        for r in range(SCATTER_UNROLL):
            t = u * SCATTER_UNROLL + r
            row_copy(t, 0).start(priority=0)
            row_copy(t, 1).start(priority=1)
        return carry

    lax.fori_loop(0, n_tok // SCATTER_UNROLL, issue, 0)
    for _ in range(2):
        pltpu.make_async_copy(h_ref, xs_ref.at[pl.ds(0, n_tok), :], sem).wait()


def _scatter(h, dest, n_slots):
    n, w = h.shape
    tc = min(MOE_TOK_CHUNK, n)
    zeros = jnp.zeros((n_slots, w), h.dtype)
    return pl.pallas_call(
        _scatter_kernel,
        name="moe_scatter",
        grid=(n // tc,),
        in_specs=[pl.BlockSpec((2, tc), lambda c: (0, c), memory_space=pltpu.SMEM),
                  pl.BlockSpec((tc, w), lambda c: (c, 0)),
                  pl.BlockSpec(memory_space=pl.ANY)],
        out_specs=pl.BlockSpec(memory_space=pl.ANY),
        out_shape=jax.ShapeDtypeStruct((n_slots, w), h.dtype),
        scratch_shapes=[pltpu.SemaphoreType.DMA(())],
        input_output_aliases={2: 0},
        compiler_params=_cparams(("arbitrary",), V7X_VMEM_LIMIT_BYTES),
    )(dest, h, zeros)


N_CMB_OPS = 2 * N_EXPERTS


def _combine_kernel(jt_ref, vt_ref, x_ref, dest_ref, gate_ref, *rest, final):
    yb_refs = rest[:N_CMB_OPS]
    ple_refs = rest[N_CMB_OPS:-1]
    o_ref = rest[-1]
    c = pl.program_id(0)
    dest = dest_ref[...]
    gate = gate_ref[...]
    lane = lax.broadcasted_iota(I32, (dest.shape[0], CMB_ROWS), 1)
    acc = x_ref[...]
    for k in range(N_CMB_OPS):
        base = jnp.where(vt_ref[c * N_CMB_OPS + k] == 1, jt_ref[c * N_CMB_OPS + k] * CMB_ROWS, -2 * CMB_ROWS)
        rel = dest - base
        w = (jnp.where(lane == rel[:, 0:1], gate[:, 0:1], 0.0)
             + jnp.where(lane == rel[:, 1:2], gate[:, 1:2], 0.0)).astype(BF16)
        acc = acc + jnp.dot(w, yb_refs[k][...], preferred_element_type=F32)
    o_ref[...] = _ple_apply(acc, *ple_refs, final=final)


def _combine(x, yb, dest_t, gate_t, jt, vt, ple, final):
    n, d_ = x.shape
    ple_arrays, ple_specs = _ple_operands(*ple, CMB_ROWS, lambda c, jt, vt: (c, 0), lambda c, jt, vt: (0, 0))
    yb_spec = lambda k: pl.BlockSpec((CMB_ROWS, d_), lambda c, jt, vt, k=k: (jt[c * N_CMB_OPS + k], 0))
    grid_spec = pltpu.PrefetchScalarGridSpec(
        num_scalar_prefetch=2,
        grid=(n // CMB_ROWS,),
        in_specs=[pl.BlockSpec((CMB_ROWS, d_), lambda c, jt, vt: (c, 0)),
                  pl.BlockSpec((CMB_ROWS, 2), lambda c, jt, vt: (c, 0)),
                  pl.BlockSpec((CMB_ROWS, 2), lambda c, jt, vt: (c, 0))]
                 + [yb_spec(k) for k in range(N_CMB_OPS)] + ple_specs,
        out_specs=pl.BlockSpec((CMB_ROWS, d_), lambda c, jt, vt: (c, 0)),
    )
    return pl.pallas_call(
        functools.partial(_combine_kernel, final=final),
        name="moe_combine",
        grid_spec=grid_spec,
        out_shape=jax.ShapeDtypeStruct((n, d_), F32),
        compiler_params=_cparams(("parallel",), V7X_VMEM_LIMIT_BYTES),
    )(jt, vt, x, dest_t, gate_t, *([yb] * N_CMB_OPS), *ple_arrays)


def _moe_plan(top_e, rank, cb, counts, n_slots):
    padded = (counts + MOE_ROWS - 1) // MOE_ROWS * MOE_ROWS
    pad_ends = jnp.cumsum(padded)
    pad_starts = pad_ends - padded
    start_of = sum(jnp.where(top_e == e, pad_starts[e], 0) for e in range(N_EXPERTS))
    dest = (start_of + rank).astype(I32)
    n_blk = n_slots // MOE_ROWS
    blk_start = jnp.arange(n_blk, dtype=I32) * MOE_ROWS
    block_e = jnp.minimum(jnp.sum((pad_ends[None, :] <= blk_start[:, None]).astype(I32), axis=1),
                          N_EXPERTS - 1).astype(I32)

    cb = jnp.concatenate([cb, counts[None]], axis=0)
    lo = pad_starts[None] + cb[:-1]
    hi = pad_starts[None] + cb[1:]
    ja = lo // CMB_ROWS
    jb = (hi - 1) // CMB_ROWS
    va = hi > lo
    vb = va & (jb > ja)
    jt = jnp.stack([ja, jb], axis=-1).reshape(-1)
    vt = jnp.stack([va, vb], axis=-1).reshape(-1)
    jt = jnp.where(vt, jt, 0).astype(I32)
    return dest, block_e, pad_ends[-1] // MOE_ROWS, jt, vt.astype(I32)


def _moe(x, norm_g, router_w, wg, wu, wd, ple, final):
    n, d_ = x.shape
    n_slots = (2 * n // MOE_ROWS + N_EXPERTS) * MOE_ROWS
    h, top_e, top_p, rank, cb, counts = _router(x, norm_g, router_w)
    dest, block_e, n_used, jt, vt = _moe_plan(top_e, rank, cb, counts, n_slots)
    xs = _scatter(h, dest, n_slots)
    yb = _ffn(xs, norm_g, wg, wu, wd, block_e, n_used, dense=False, tf=wg.shape[2] // 2)
    return _combine(x, yb, dest.T, top_p.T, jt, vt, ple, final)


def kernel(x, p, attn_norm, w_in, cmp_pos, cmp_w1, cmp_w2, conv_w, conv_b, lru_wa, lru_ba, lru_wx, lru_bx,
           lru_lambda, out_norm_attn, out_norm_rec, w_out, ffn_norm, dense_w_gate, dense_w_up, dense_w_down,
           router_w, moe_w_gate, moe_w_up, moe_w_down, ple_norm, ple_w_gate, ple_w_proj, final_norm):
    b_, t_, d_ = x.shape
    depth = w_in.shape[0]
    n = b_ * t_
    slc_cols, win_cols = _key_position_columns(t_)
    for i in range(depth):
        kcvc, ks, kw, xr, yg, qT, vsT, vwT, gT = _proj_in(x, attn_norm[i], w_in[i], slc_cols, win_cols)
        kcmp = _compress(kcvc, cmp_pos[i], cmp_w1[i], cmp_w2[i])
        oaT = _attention(qT, gT, kcmp, ks, vsT, kw, vwT)
        orec = _rglru(xr, yg, conv_w[i], conv_b[i], lru_wa[i], lru_ba[i], lru_wx[i], lru_bx[i], lru_lambda[i])
        x = _out_proj(oaT, orec, x, out_norm_attn[i], out_norm_rec[i], w_out[i])
        x2 = x.reshape(n, d_)
        j = i // 2
        ple = (p[i].reshape(n, -1), ple_norm[i], ple_w_gate[i], ple_w_proj[i], final_norm)
        final = i == depth - 1
        if i % 2 == 0:
            x2 = _ffn(x2, ffn_norm[i], dense_w_gate[j][None].astype(BF16), dense_w_up[j][None].astype(BF16),
                      dense_w_down[j][None].astype(BF16), None, dense=True, tf=dense_w_gate.shape[2],
                      ple=ple, final=final)
        else:
            x2 = _moe(x2, ffn_norm[i], router_w[j], moe_w_gate[j].astype(BF16), moe_w_up[j].astype(BF16),
                      moe_w_down[j].astype(BF16), ple, final)
        x = x2.reshape(b_, t_, d_)
    return x
```

```python
import functools

import jax
import jax.numpy as jnp
from jax import lax
from jax.experimental import pallas as pl
from jax.experimental.pallas import tpu as pltpu

F32 = jnp.float32
BF16 = jnp.bfloat16
I32 = jnp.int32

N_ATTN_HEADS = 8
HEAD_DIM = 64
N_KV = 2
HEADS_PER_KV = N_ATTN_HEADS // N_KV
D_ATTN = N_ATTN_HEADS * HEAD_DIM
KV_W = N_KV * HEAD_DIM
N_GATES = 3
L_CMP = 32
STRIDE = 16
L_SLC = 64
N_SEL = 16
N_LOCAL = 2
W_WIN = 512
CMP_HID = 2 * HEAD_DIM
CONV_W = 4
C_LRU = 8.0
N_EXPERTS = 8
RMS_EPS = 1e-6
ATTN_SCALE = HEAD_DIM ** -0.5
LOG2E = 1.4426950408889634
LOG2E_HI = 1.4453125
LOG2E_LO = LOG2E - LOG2E_HI

V7X_LANES = 128
V7X_VMEM_LIMIT_BYTES = 56 * 1024 * 1024

TQ = 128
KT_SLC = 256
SLC_UNROLL = 4
KT_WIN = 128
N_WIN_TILES = W_WIN // KT_WIN + 1
V_ROWS = HEAD_DIM + 16
MASK_NEG = -1e30
SEL_NEG = -float(2 ** 20)
ROW_TILE = 512
PROJ_TILE = 1024
RGLRU_CHUNK = 256
MOE_ROWS = 512
MOE_TOK_CHUNK = 512
CMB_ROWS = 256


def _cparams(semantics, vmem=None):
    return pltpu.CompilerParams(dimension_semantics=semantics, vmem_limit_bytes=vmem)


def _rms(x, g):
    ms = jnp.mean(x * x, axis=-1, keepdims=True)
    return x * lax.rsqrt(ms + RMS_EPS) * g


def _gelu_tanh(x):
    c = 0.7978845608028654
    return x * (0.5 * (1.0 + jnp.tanh(c * (x + 0.044715 * (x * x * x)))))


def _sigmoid(x):
    return 1.0 / (1.0 + jnp.exp(-x))


def _silu(x):
    return x * _sigmoid(x)


def _proj_in_kernel(x_ref, g_ref, wtok_ref, wtr_ref,
                    slc_ref, win_ref, kcvc_ref, ks_ref, kw_ref, xr_ref, yg_ref, qT_ref, vsT_ref, vwT_ref, gT_ref):
    hn = _rms(x_ref[0], g_ref[...]).astype(BF16)
    tok = jnp.dot(hn, wtok_ref[...], preferred_element_type=F32)
    kcvc_ref[0] = tok[:, 0:256]
    xr_ref[0] = tok[:, 512:1024].astype(BF16)
    yg_ref[0] = tok[:, 1024:1536].astype(BF16)
    lane = lax.broadcasted_iota(I32, (tok.shape[0], V7X_LANES), 1)
    slc_c = slc_ref[...].astype(F32)
    win_c = win_ref[...].astype(F32)
    for g in range(N_KV):
        in_g = lambda k2: jnp.where(lane < HEAD_DIM, k2 if g == 0 else pltpu.roll(k2, HEAD_DIM, axis=1), 0.0)
        ks_ref[0, g] = jnp.concatenate([in_g(tok[:, 256:384]) + slc_c[:, :V7X_LANES], slc_c[:, V7X_LANES:]],
                                       axis=1).astype(BF16)
        kw_ref[0, g] = (in_g(tok[:, 384:512]) + win_c).astype(BF16)
    tr = lax.dot_general(wtr_ref[...], hn, (((1,), (1,)), ((), ())),
                         preferred_element_type=F32)
    qT_ref[0] = (tr[0:512] * (ATTN_SCALE * LOG2E)).astype(BF16)
    ones = jnp.ones((V_ROWS - HEAD_DIM, tr.shape[1]), BF16)
    for g in range(N_KV):
        for v_ref, r0 in ((vsT_ref, 512), (vwT_ref, 640)):
            v_ref[0, g, 0:HEAD_DIM, :] = tr[r0 + g * HEAD_DIM:r0 + (g + 1) * HEAD_DIM].astype(BF16)
            v_ref[0, g, HEAD_DIM:V_ROWS, :] = ones
    gT_ref[0] = _sigmoid(tr[768:800])


def _key_position_columns(t_):
    pos = jnp.arange(t_, dtype=I32)
    dup = lambda vals: [v.astype(BF16)[:, None] for v in vals for _ in range(2)]
    pad_to = lambda cols, width: jnp.concatenate(
        cols + [jnp.zeros((t_, width - sum(c.shape[1] for c in cols)), BF16)], axis=1)
    lead = jnp.zeros((t_, HEAD_DIM), BF16)
    slc = pad_to([lead, jax.nn.one_hot(pos // L_SLC, t_ // L_SLC, dtype=BF16)]
                 + dup([pos // L_SLC, pos % L_SLC]), 2 * HEAD_DIM + t_ // L_SLC)
    win = pad_to([lead] + dup([pos % KT_WIN]), 2 * HEAD_DIM)
    return slc, win


def _proj_in(x, norm_g, w_in, slc_cols, win_cols):
    b_, t_, d_ = x.shape
    tm = min(PROJ_TILE, t_)
    q, kc, vc, ks, vs, kw, vw, g, xr, yg = jnp.split(
        w_in, [512, 640, 768, 896, 1024, 1152, 1280, 1304, 1816], axis=1)
    wtok = jnp.concatenate([kc, vc, ks, kw, xr, yg], axis=1).astype(BF16)
    g4 = g.reshape(d_, N_KV, HEADS_PER_KV, N_GATES).transpose(0, 1, 3, 2)
    g4 = jnp.pad(g4.reshape(d_, N_KV, 12), ((0, 0), (0, 0), (0, 4))).reshape(d_, 32)
    wtr = jnp.concatenate([q, vs, vw, g4], axis=1).T.astype(BF16)
    nt = t_ // tm
    row = lambda shape: pl.BlockSpec((1, tm, shape), lambda b, i: (b, i, 0))
    col = lambda shape: pl.BlockSpec((1, shape, tm), lambda b, i: (b, 0, i))
    full = lambda a: pl.BlockSpec(a.shape, lambda b, i: (0,) * a.ndim)
    g2 = norm_g.reshape(1, d_)
    ws, ww = slc_cols.shape[1], win_cols.shape[1]
    grow = lambda width: pl.BlockSpec((1, N_KV, tm, width), lambda b, i: (b, 0, i, 0))
    gcol = pl.BlockSpec((1, N_KV, V_ROWS, tm), lambda b, i: (b, 0, 0, i))
    outs = pl.pallas_call(
        _proj_in_kernel,
        name="proj_in",
        grid=(b_, nt),
        in_specs=[row(d_), full(g2), full(wtok), full(wtr),
                  pl.BlockSpec((tm, ws), lambda b, i: (i, 0)), pl.BlockSpec((tm, ww), lambda b, i: (i, 0))],
        out_specs=[row(256), grow(ws), grow(ww), row(512), row(512), col(512), gcol, gcol, col(32)],
        out_shape=[
            jax.ShapeDtypeStruct((b_, t_, 256), F32),
            jax.ShapeDtypeStruct((b_, N_KV, t_, ws), BF16),
            jax.ShapeDtypeStruct((b_, N_KV, t_, ww), BF16),
            jax.ShapeDtypeStruct((b_, t_, 512), BF16),
            jax.ShapeDtypeStruct((b_, t_, 512), BF16),
            jax.ShapeDtypeStruct((b_, 512, t_), BF16),
            jax.ShapeDtypeStruct((b_, N_KV, V_ROWS, t_), BF16),
            jax.ShapeDtypeStruct((b_, N_KV, V_ROWS, t_), BF16),
            jax.ShapeDtypeStruct((b_, 32, t_), F32),
        ],
        compiler_params=_cparams(("parallel", "parallel"), V7X_VMEM_LIMIT_BYTES),
    )(x, g2, wtok, wtr, slc_cols, win_cols)
    return outs


def _compress_kernel(xc_ref, pa_ref, pb_ref, w1a_ref, w1b_ref, w2_ref, out_ref):
    xc = xc_ref[0].astype(BF16)
    n_chunk = xc.shape[0]
    a = jnp.dot(xc, w1a_ref[...], preferred_element_type=F32)
    bm = jnp.dot(xc, w1b_ref[...], preferred_element_type=F32)
    posc = (jnp.dot(pa_ref[...], w1a_ref[...], preferred_element_type=F32)
            + jnp.dot(pb_ref[...], w1b_ref[...], preferred_element_type=F32))[0:1]
    row = lax.broadcasted_iota(I32, bm.shape, 0)
    bm_up = jnp.where(row < n_chunk - 1, pltpu.roll(bm, n_chunk - 1, axis=0), 0.0)
    hid = _gelu_tanh(a + bm_up + posc).astype(BF16)
    out_ref[0] = jnp.dot(hid, w2_ref[...], preferred_element_type=F32)


def _compress(kcvc, cmp_pos, cmp_w1, cmp_w2):
    b_, t_, _ = kcvc.shape
    n_chunk = t_ // STRIDE
    xc = kcvc.reshape(b_, n_chunk, STRIDE * 256)
    w1 = cmp_w1.reshape(2, 2, STRIDE, HEAD_DIM, CMP_HID)

    def expand(half):
        w = w1[:, half]
        full = jnp.einsum('wldo,wx,gy->lxgdwyo', w, jnp.eye(2, dtype=F32), jnp.eye(2, dtype=F32))
        return full.reshape(STRIDE * 256, 4 * CMP_HID).astype(BF16)

    w1a, w1b = expand(0), expand(1)
    pos = cmp_pos.reshape(2, 2, STRIDE, HEAD_DIM)

    def posrow(half):
        p = jnp.broadcast_to(pos[:, half][:, None], (2, N_KV, STRIDE, HEAD_DIM))
        p = p.transpose(2, 0, 1, 3).reshape(1, STRIDE * 256)
        return jnp.pad(p, ((0, 7), (0, 0))).astype(BF16)

    pa, pb = posrow(0), posrow(1)
    w2 = jnp.einsum('whd,wx,gy->wghxyd', cmp_w2, jnp.eye(2, dtype=F32), jnp.eye(2, dtype=F32))
    w2 = w2.reshape(4 * CMP_HID, 4 * HEAD_DIM).astype(BF16)
    full = lambda a: pl.BlockSpec(a.shape, lambda b: (0,) * a.ndim)
    return pl.pallas_call(
        _compress_kernel,
        name="compress_kv",
        grid=(b_,),
        in_specs=[pl.BlockSpec((1, n_chunk, STRIDE * 256), lambda b: (b, 0, 0)),
                  full(pa), full(pb), full(w1a), full(w1b), full(w2)],
        out_specs=pl.BlockSpec((1, n_chunk, 256), lambda b: (b, 0, 0)),
        out_shape=jax.ShapeDtypeStruct((b_, n_chunk, 256), F32),
        compiler_params=_cparams(("parallel",), V7X_VMEM_LIMIT_BYTES),
    )(xc, pa, pb, w1a, w1b, w2)


def _head_lanes(rows):
    r = rows.shape[0] // HEADS_PER_KV
    return jnp.concatenate([rows[h * r:(h + 1) * r, :] for h in range(HEADS_PER_KV)], axis=1)


def _attn_kernel(qT_ref, gT_ref, kc_ref, vcT_ref, ks_ref, vsT_ref, kw_ref, vwT_ref, mt_ref, o_ref, s_scr,
                 tile_list):
    i = pl.program_id(1)
    n_cmp = kc_ref.shape[2]
    n_slc = mt_ref.shape[0]
    nl = HEADS_PER_KV * TQ
    hd_rows = HEADS_PER_KV * HEAD_DIM

    lane = lax.broadcasted_iota(I32, (1, nl), 1)
    t_loc = lane % TQ
    t_row = i * TQ + t_loc

    def extra_rows(n_rows, rows):
        ridx = lax.broadcasted_iota(I32, (16, nl), 0)
        out = jnp.zeros((16, nl), F32)
        for k, r in enumerate(rows):
            out = jnp.where(ridx == k, r, out)
        return jnp.concatenate([out.astype(BF16), jnp.zeros((n_rows - 16, nl), BF16)], axis=0)

    ok_c = (lax.broadcasted_iota(I32, (n_cmp, nl), 0) * STRIDE + (L_CMP - 1)) <= t_row
    r_idx = lax.broadcasted_iota(I32, (KT_WIN, nl), 0)
    s_idx = lax.broadcasted_iota(I32, (n_slc, TQ), 0)
    cur = (i * TQ + lax.broadcasted_iota(I32, (1, TQ), 1)) // L_SLC
    valid = s_idx <= cur
    forced = valid & ((s_idx == 0) | (s_idx > cur - N_LOCAL))
    s_idx_f = s_idx.astype(F32)
    n_full = (i * TQ) // KT_SLC
    mt = mt_ref[...]

    def pick(_, score):
        m = jnp.max(score, axis=0, keepdims=True)
        idx = jnp.min(jnp.where(score == m, s_idx_f, 1.0e9), axis=0, keepdims=True)
        return jnp.where(s_idx_f == idx, -2.0, score)

    def stage_a(g, bs, tile, slot, causal):
        k0 = pl.multiple_of(tile * KT_SLC, KT_SLC)
        s = jnp.dot(ks_ref[0, g, pl.ds(k0, KT_SLC), :], bs, preferred_element_type=F32)
        if causal:
            pos = k0 + lax.broadcasted_iota(I32, (KT_SLC, nl), 0)
            s = jnp.where(pos <= t_row, s, MASK_NEG)
        s_scr[g, slot] = s
        return jnp.max(s, axis=0, keepdims=True)

    def stage_b(g, tile, slot, m, acc, m_tile):
        k0 = pl.multiple_of(tile * KT_SLC, KT_SLC)
        m_new = jnp.maximum(m, m_tile)
        p = jnp.exp2(s_scr[g, slot] - m_new)
        acc = jnp.exp2(m - m_new) * acc + jnp.dot(vsT_ref[0, g, :, pl.ds(k0, KT_SLC)], p.astype(BF16),
                                                  preferred_element_type=F32)
        return m_new, acc

    groups = range(N_KV)

    def setup(g):
        q4 = _head_lanes(qT_ref[0, g * hd_rows:(g + 1) * hd_rows, :])
        head = g * HEADS_PER_KV + lane // TQ + 1
        slope = lax.bitcast_convert_type((127 - head) << 23, F32)
        return dict(q4=q4, slope=slope, s_hi=slope * LOG2E_HI, s_lo=slope * LOG2E_LO)

    st = [setup(g) for g in groups]

    def cmp_scores(g):
        s_hi, s_lo = st[g]["s_hi"], st[g]["s_lo"]
        bc = jnp.concatenate(
            [st[g]["q4"], extra_rows(64, [256.0 * s_hi, 256.0 * s_lo, 16.0 * s_hi, 16.0 * s_lo])], axis=0)
        return jnp.dot(kc_ref[0, g], bc, preferred_element_type=F32)

    def cmp_softmax(g, s_c):
        s_c = jnp.where(ok_c, s_c, MASK_NEG)
        m_c = jnp.max(s_c, axis=0, keepdims=True)
        e_c = jnp.exp2(s_c - m_c)
        l_c = jnp.sum(e_c, axis=0, keepdims=True)
        inv_c = jnp.where(m_c > 0.5 * MASK_NEG, 1.0 / jnp.maximum(l_c, 1e-30), 0.0)
        p_c = e_c * inv_c
        o_c = jnp.dot(vcT_ref[0, g], p_c.astype(BF16), preferred_element_type=F32)
        p_grp = (p_c[:, 0:TQ] + p_c[:, TQ:2 * TQ]) + (p_c[:, 2 * TQ:3 * TQ] + p_c[:, 3 * TQ:4 * TQ])
        p1 = p_grp.astype(BF16)
        r1 = p_grp - p1.astype(F32)
        p2 = r1.astype(BF16)
        p3 = (r1 - p2.astype(F32)).astype(BF16)
        p_slc = (jnp.dot(mt, p1, preferred_element_type=F32) + jnp.dot(mt, p2, preferred_element_type=F32)
                 + jnp.dot(mt, p3, preferred_element_type=F32))
        return o_c, p_slc

    def win_scores(g):
        slope = st[g]["slope"]
        bw = jnp.concatenate([st[g]["q4"], extra_rows(64, [st[g]["s_hi"], st[g]["s_lo"]])], axis=0)
        s_tiles, c_tiles = [], []
        for a in range(N_WIN_TILES):
            tile = i - (N_WIN_TILES - 1) + a
            k0 = pl.multiple_of(jnp.maximum(tile, 0) * KT_WIN, KT_WIN)
            s = jnp.dot(kw_ref[0, g, pl.ds(k0, KT_WIN), :], bw, preferred_element_type=F32)
            if a == 0:
                s = jnp.where(r_idx > t_loc, s, MASK_NEG)
            elif a == N_WIN_TILES - 1:
                s = jnp.where(r_idx <= t_loc, s, MASK_NEG)
            s_tiles.append(s)
            c_a = slope * (LOG2E * KT_WIN * (a - (N_WIN_TILES - 1)))
            c_tiles.append(jnp.where(tile >= 0, c_a, MASK_NEG))
        return s_tiles, c_tiles

    def win_softmax(g, s_tiles, c_tiles):
        m_w = s_tiles[0].max(axis=0, keepdims=True) + c_tiles[0]
        for s, c_a in zip(s_tiles[1:], c_tiles[1:]):
            m_w = jnp.maximum(m_w, s.max(axis=0, keepdims=True) + c_a)
        acc_w = jnp.zeros((V_ROWS, nl), F32)
        for a, (s, c_a) in enumerate(zip(s_tiles, c_tiles)):
            tile = i - (N_WIN_TILES - 1) + a
            k0 = pl.multiple_of(jnp.maximum(tile, 0) * KT_WIN, KT_WIN)
            p = jnp.exp2(s - (m_w - c_a))
            acc_w = acc_w + jnp.dot(vwT_ref[0, g, :, pl.ds(k0, KT_WIN)], p.astype(BF16),
                                    preferred_element_type=F32)
        return acc_w[0:HEAD_DIM] * (1.0 / acc_w[HEAD_DIM:HEAD_DIM + 1])

    s_cmp = [cmp_scores(g) for g in groups]
    s_win = [win_scores(g) for g in groups]
    cmp_out, scores = [], []
    for g in groups:
        cmp_out.append(cmp_softmax(g, s_cmp[g]))
        score0 = jnp.where(forced, -2.0, jnp.where(valid, cmp_out[g][1], -1.0))
        scores.append(lax.fori_loop(0, N_SEL - (N_LOCAL + 1), pick, score0, unroll=True))
    o_win = [win_softmax(g, *s_win[g]) for g in groups]
    score = jnp.concatenate(scores, axis=1)

    def select(g):
        sel = valid & (score[:, g * TQ:(g + 1) * TQ] < -1.5)
        sel4 = jnp.concatenate([sel] * HEADS_PER_KV, axis=1)
        blk_rows = jnp.where(sel4, 0.0, SEL_NEG).astype(BF16)
        s_hi, s_lo = st[g]["s_hi"], st[g]["s_lo"]
        bs = jnp.concatenate(
            [st[g]["q4"], blk_rows, extra_rows(64, [L_SLC * s_hi, L_SLC * s_lo, s_hi, s_lo])], axis=0)
        m_tile0 = stage_a(g, bs, n_full, 0, True)
        gt = gT_ref[0, g * 16:(g + 1) * 16, :]
        gates = [_head_lanes(gt[k * HEADS_PER_KV:(k + 1) * HEADS_PER_KV, :]) for k in range(N_GATES)]
        return dict(bs=bs, base=gates[0] * cmp_out[g][0] + gates[2] * o_win[g], g_s=gates[1], m_tile0=m_tile0)

    grp = [select(g) for g in groups]

    blocks_per_tile = KT_SLC // L_SLC
    n_tiles = n_slc // blocks_per_tile
    assert n_tiles <= 32
    valid2 = jnp.concatenate([valid] * N_KV, axis=1)
    sel_all = jnp.where(valid2 & (score < -1.5), 1.0, 0.0).astype(BF16)
    blk_cnt = lax.dot_general(jnp.ones((8, N_KV * TQ), BF16), sel_all, (((1,), (1,)), ((), ())),
                              preferred_element_type=F32)
    memb = jnp.where(lax.broadcasted_iota(I32, (n_slc, V7X_LANES), 0) // blocks_per_tile
                     == lax.broadcasted_iota(I32, (n_slc, V7X_LANES), 1), 1.0, 0.0).astype(BF16)
    tile_cnt = jnp.dot(jnp.where(blk_cnt > 0.5, 1.0, 0.0).astype(BF16), memb,
                       preferred_element_type=F32)[0:1]
    lane_t = lax.broadcasted_iota(I32, (1, V7X_LANES), 1)
    need = (tile_cnt > 0.5) & (lane_t < n_full)
    bit = lax.bitcast_convert_type(((lane_t & 15) + 127) << 23, F32)
    lo_bits = jnp.sum(jnp.where(need & (lane_t < 16), bit, 0.0)).astype(I32)
    hi_bits = jnp.sum(jnp.where(need & (lane_t >= 16), bit, 0.0)).astype(I32)
    n_sel = jnp.int32(0)
    for t in range(n_tiles - 1):
        tile_list[n_sel] = t
        n_sel = n_sel + (((lo_bits if t < 16 else hi_bits) >> (t % 16)) & 1)

    order = lambda k: jnp.where(k == 0, n_full, tile_list[jnp.maximum(k - 1, 0)])

    def half(carry, tile_b, slot_b, tile_a):
        m_next = [c[2] if tile_a is None else stage_a(g, grp[g]["bs"], tile_a, 1 - slot_b, False)
                  for g, c in enumerate(carry)]
        out = []
        for g in range(N_KV):
            m, acc, m_tile = carry[g]
            out.append(stage_b(g, tile_b, slot_b, m, acc, m_tile) + (m_next[g],))
        return tuple(out)

    def run(carry, k0, n, last):
        for j in range(n):
            nxt = None if (last and j == n - 1) else order(k0 + j + 1)
            carry = half(carry, order(k0 + j), j % 2, nxt)
        return carry

    init = tuple((jnp.full((1, nl), MASK_NEG, F32), jnp.zeros((V_ROWS, nl), F32), grp[g]["m_tile0"])
                 for g in range(N_KV))
    carry = lax.fori_loop(0, n_sel // SLC_UNROLL, lambda p, c: run(c, SLC_UNROLL * p, SLC_UNROLL, False), init)
    k_tail = SLC_UNROLL * (n_sel // SLC_UNROLL)
    carry = lax.switch(n_sel % SLC_UNROLL, [functools.partial(run, k0=k_tail, n=r + 1, last=True)
                                            for r in range(SLC_UNROLL)], carry)
    for g in range(N_KV):
        _, acc, _ = carry[g]
        o_s = acc[0:HEAD_DIM] * (1.0 / acc[HEAD_DIM:HEAD_DIM + 1])
        out = grp[g]["base"] + grp[g]["g_s"] * o_s
        for h in range(HEADS_PER_KV):
            r0 = g * hd_rows + h * HEAD_DIM
            o_ref[0, r0:r0 + HEAD_DIM, :] = out[:, h * TQ:(h + 1) * TQ].astype(o_ref.dtype)


def _attention(qT, gT, kcmp, ks_aug, vsT4, kw_aug, vwT4):
    b_, _, t_ = qT.shape
    n_cmp = t_ // STRIDE
    n_slc = t_ // L_SLC
    c = jnp.arange(n_cmp, dtype=I32)
    cmp_cols = jnp.concatenate([v.astype(BF16)[:, None] for v in (c // 16, c // 16, c % 16, c % 16)]
                               + [jnp.zeros((n_cmp, HEAD_DIM - 4), BF16)], axis=1)
    kc4 = kcmp[..., 0:KV_W].reshape(b_, n_cmp, N_KV, HEAD_DIM).transpose(0, 2, 1, 3).astype(BF16)
    kc_aug = jnp.concatenate(
        [kc4, jnp.broadcast_to(cmp_cols, (b_, N_KV) + cmp_cols.shape)], axis=-1)
    vcT = kcmp[..., KV_W:2 * KV_W].reshape(b_, n_cmp, N_KV, HEAD_DIM).transpose(0, 2, 3, 1).astype(BF16)
    s = jnp.arange(n_slc, dtype=I32)[:, None]
    cc = c[None, :]
    r_slc = L_SLC // STRIDE
    mt = (((cc >= r_slc * s) & (cc < r_slc * s + r_slc)).astype(F32)
          + ((cc + 1 >= r_slc * s) & (cc + 1 < r_slc * s + r_slc)).astype(F32))
    mt = jnp.where(cc < n_cmp - 1, mt, 0.0).astype(BF16)
    kv_spec = lambda a: pl.BlockSpec((1,) + a.shape[1:], lambda b, i: (b, 0, 0, 0))
    return pl.pallas_call(
        _attn_kernel,
        name="nsa_attention",
        grid=(b_, t_ // TQ),
        in_specs=[pl.BlockSpec((1, D_ATTN, TQ), lambda b, i: (b, 0, i)),
                  pl.BlockSpec((1, 32, TQ), lambda b, i: (b, 0, i)),
                  kv_spec(kc_aug), kv_spec(vcT), kv_spec(ks_aug), kv_spec(vsT4),
                  kv_spec(kw_aug), kv_spec(vwT4),
                  pl.BlockSpec(mt.shape, lambda b, i: (0, 0))],
        out_specs=pl.BlockSpec((1, D_ATTN, TQ), lambda b, i: (b, 0, i)),
        out_shape=jax.ShapeDtypeStruct((b_, D_ATTN, t_), BF16),
        scratch_shapes=[pltpu.VMEM((N_KV, 2, KT_SLC, HEADS_PER_KV * TQ), F32), pltpu.SMEM((32,), I32)],
        compiler_params=_cparams(("parallel", "arbitrary"), V7X_VMEM_LIMIT_BYTES),
    )(qT, gT, kc_aug, vcT, ks_aug, vsT4, kw_aug, vwT4, mt)


def _rglru_seg_pitch(tc):
    return tc // 8 + 8


def _rglru_kernel(xr_ref, yg_ref, cw_ref, cb_ref, wa_ref, ba_ref, wx_ref, bx_ref, lam_ref,
                  o_ref, tail_ref, h_ref, xs_scr, ys_scr, os_scr):
    tc = xr_ref.shape[1]

    @pl.when(pl.program_id(1) == 0)
    def _():
        tail_ref[...] = jnp.zeros_like(tail_ref)
        h_ref[...] = jnp.zeros_like(h_ref)

    n_v = tc // 8
    n_slab = xr_ref.shape[2] // V7X_LANES
    pitch = xs_scr.shape[1] // 8
    for c in range(n_slab):
        for s in range(8):
            lanes = slice(c * V7X_LANES, (c + 1) * V7X_LANES)
            xs_scr[c, s * pitch:s * pitch + n_v, :] = xr_ref[0, s * n_v:(s + 1) * n_v, lanes].astype(F32)
            ys_scr[c, s * pitch:s * pitch + n_v, :] = yg_ref[0, s * n_v:(s + 1) * n_v, lanes].astype(F32)

    def piece(scr, v):
        return jnp.concatenate([scr[c, pl.ds(v, 8, stride=pitch), :] for c in range(n_slab)], axis=1)

    x = [piece(xs_scr, v) for v in range(n_v)]
    row8 = lax.broadcasted_iota(I32, (8, x[0].shape[1]), 0)
    tail = tail_ref[...]
    before = {d: jnp.where(row8 == 0, tail[CONV_W - 1 - d:CONV_W - d, :], pltpu.roll(x[n_v - d], 1, axis=0))
              for d in range(1, CONV_W)}
    tail_ref[0:CONV_W - 1, :] = jnp.concatenate([x[n_v - d][7:8, :] for d in range(CONV_W - 1, 0, -1)], axis=0)
    cw = cw_ref[...]
    cb = cb_ref[...]
    xc = []
    for v in range(n_v):
        acc = x[v] * cw[CONV_W - 1:CONV_W, :] + cb
        for d in range(1, CONV_W):
            acc = acc + (x[v - d] if v >= d else before[d - v]) * cw[CONV_W - 1 - d:CONV_W - d, :]
        xc.append(acc)

    xb = jnp.concatenate(xc, axis=0).astype(BF16)
    r = _sigmoid(jnp.dot(xb, wa_ref[...], preferred_element_type=F32) + ba_ref[...])
    gi = _sigmoid(jnp.dot(xb, wx_ref[...], preferred_element_type=F32) + bx_ref[...])
    z = -lam_ref[...]
    softplus = jnp.maximum(z, 0.0) + jnp.log1p(jnp.exp(-jnp.abs(z)))
    log_a = (-C_LRU * r) * softplus
    a = jnp.exp(log_a)
    th = jnp.tanh(log_a)
    neg_expm1 = 2.0 * th / (th - 1.0)
    bb = jnp.sqrt(neg_expm1) * gi * jnp.concatenate(xc, axis=0)

    a_cum, b_cum = [a[0:8]], [bb[0:8]]
    for v in range(1, n_v):
        a_v = a[v * 8:(v + 1) * 8]
        b_cum.append(a_v * b_cum[-1] + bb[v * 8:(v + 1) * 8])
        a_cum.append(a_v * a_cum[-1])
    h_in = [h_ref[0:1, :]]
    for s in range(7):
        h_in.append(a_cum[-1][s:s + 1, :] * h_in[-1] + b_cum[-1][s:s + 1, :])
    h_ref[0:1, :] = a_cum[-1][7:8, :] * h_in[-1] + b_cum[-1][7:8, :]
    h_in = jnp.concatenate(h_in, axis=0)
    for v in range(n_v):
        out = (a_cum[v] * h_in + b_cum[v]) * _gelu_tanh(piece(ys_scr, v))
        for c in range(n_slab):
            os_scr[c, pl.ds(v, 8, stride=pitch), :] = out[:, c * V7X_LANES:(c + 1) * V7X_LANES]
    for s in range(8):
        o_ref[0, s * n_v:(s + 1) * n_v, :] = jnp.concatenate(
            [os_scr[c, s * pitch:s * pitch + n_v, :] for c in range(n_slab)], axis=1).astype(o_ref.dtype)


def _block_diag(w):
    n, c, d = w.shape
    return jnp.einsum('ncd,nm->ncmd', w, jnp.eye(n, dtype=w.dtype)).reshape(n * c, n * d)


def _rglru(xr, yg, conv_w, conv_b, wa, ba, wx, bx, lam):
    b_, t_, c_ = xr.shape
    tc = min(RGLRU_CHUNK, t_)
    wa_bd = _block_diag(wa).astype(BF16)
    wx_bd = _block_diag(wx).astype(BF16)
    vec = lambda v: v.reshape(1, c_)
    full = lambda a: pl.BlockSpec(a.shape, lambda b, i: (0,) * a.ndim)
    blk = pl.BlockSpec((1, tc, c_), lambda b, i: (b, i, 0))
    args = (xr, yg, conv_w, vec(conv_b), wa_bd, vec(ba), wx_bd, vec(bx), vec(lam))
    return pl.pallas_call(
        _rglru_kernel,
        name="rglru",
        grid=(b_, t_ // tc),
        in_specs=[blk, blk] + [full(a) for a in args[2:]],
        out_specs=blk,
        out_shape=jax.ShapeDtypeStruct((b_, t_, c_), BF16),
        scratch_shapes=[pltpu.VMEM((8, c_), F32), pltpu.VMEM((8, c_), F32)]
                       + [pltpu.VMEM((c_ // V7X_LANES, 8 * _rglru_seg_pitch(tc), V7X_LANES), F32)] * 3,
        compiler_params=_cparams(("parallel", "arbitrary"), V7X_VMEM_LIMIT_BYTES),
    )(*args)


def _out_proj_kernel(oaT_ref, orec_ref, x_ref, ga_ref, gr_ref, wa_ref, wr_ref, o_ref):
    oaT = oaT_ref[0].astype(F32)
    ms = jnp.mean(oaT * oaT, axis=0, keepdims=True)
    na = (oaT * lax.rsqrt(ms + RMS_EPS)).T * ga_ref[...]
    nr = _rms(orec_ref[0].astype(F32), gr_ref[...])
    y = (jnp.dot(na.astype(BF16), wa_ref[...], preferred_element_type=F32)
         + jnp.dot(nr.astype(BF16), wr_ref[...], preferred_element_type=F32))
    o_ref[0] = x_ref[0] + y


def _out_proj(oaT, orec, x, g_attn, g_rec, w_out):
    b_, t_, d_ = x.shape
    tm = min(PROJ_TILE, t_)
    wa = w_out[:D_ATTN].astype(BF16)
    wr = w_out[D_ATTN:].astype(BF16)
    ga = g_attn.reshape(1, -1)
    gr = g_rec.reshape(1, -1)
    full = lambda a: pl.BlockSpec(a.shape, lambda b, i: (0,) * a.ndim)
    return pl.pallas_call(
        _out_proj_kernel,
        name="out_proj",
        grid=(b_, t_ // tm),
        in_specs=[pl.BlockSpec((1, D_ATTN, tm), lambda b, i: (b, 0, i)),
                  pl.BlockSpec((1, tm, orec.shape[2]), lambda b, i: (b, i, 0)),
                  pl.BlockSpec((1, tm, d_), lambda b, i: (b, i, 0)),
                  full(ga), full(gr), full(wa), full(wr)],
        out_specs=pl.BlockSpec((1, tm, d_), lambda b, i: (b, i, 0)),
        out_shape=jax.ShapeDtypeStruct((b_, t_, d_), F32),
        compiler_params=_cparams(("parallel", "parallel"), V7X_VMEM_LIMIT_BYTES),
    )(oaT, orec, x, ga, gr, wa, wr)


def _ple_apply(x, p_ref, g_ref, wg_ref, wp_ref, fg_ref, *, final):
    gate = _sigmoid(jnp.dot(_rms(x, g_ref[...]).astype(BF16), wg_ref[...], preferred_element_type=F32))
    proj = jnp.dot(p_ref[...].astype(BF16), wp_ref[...], preferred_element_type=F32)
    y = x + gate * proj
    return _rms(y, fg_ref[...]) if final else y


def _ple_operands(p, norm_g, w_gate, w_proj, final_g, rows, row_map, const_map):
    d_ = w_gate.shape[0]
    arrays = (p, norm_g.reshape(1, d_), w_gate.astype(BF16), w_proj.astype(BF16), final_g.reshape(1, d_))
    once = dict(pipeline_mode=pl.Buffered(1))
    specs = [pl.BlockSpec((rows, p.shape[1]), row_map)] + [pl.BlockSpec(a.shape, const_map, **once)
                                                            for a in arrays[1:]]
    return arrays, specs


def _ffn_kernel(be_ref, x_ref, g_ref, wg_ref, wu_ref, wd_ref, *rest, dense, final):
    ple_refs, (o_ref, hn_ref, acc_ref) = rest[:-3], rest[-3:]
    f = pl.program_id(1)
    used = True if dense else pl.program_id(0) < be_ref[pl.num_programs(0)]

    @pl.when(f == 0)
    def _():
        if dense:
            x = x_ref[...]
            hn_ref[...] = _rms(x, g_ref[...]).astype(BF16)
            acc_ref[...] = x
        else:
            hn_ref[...] = _unpack_bf16_pairs(x_ref[...])
            acc_ref[...] = jnp.zeros_like(acc_ref)

    def accumulate():
        hn = hn_ref[...]
        gate = jnp.dot(hn, wg_ref[0], preferred_element_type=F32)
        up = jnp.dot(hn, wu_ref[0], preferred_element_type=F32)
        act = (_silu(gate) * up).astype(BF16)
        acc_ref[...] += jnp.dot(act, wd_ref[0], preferred_element_type=F32)

    if dense:
        accumulate()
    else:
        pl.when(used)(accumulate)

    @pl.when(f == pl.num_programs(1) - 1)
    def _():
        if dense:
            o_ref[...] = _ple_apply(acc_ref[...], *ple_refs, final=final)
        else:
            o_ref[...] = acc_ref[...].astype(o_ref.dtype)


def _ffn(x, norm_g, wg, wu, wd, block_e, n_used=None, *, dense, tf, ple=None, final=False):
    n = x.shape[0]
    d_ = wg.shape[1]
    tm = min(ROW_TILE if dense else MOE_ROWS, n)
    ff = wg.shape[2]
    assert ff % tf == 0 and n % tm == 0
    if block_e is None:
        block_e = jnp.zeros((n // tm,), I32)
    if n_used is None:
        n_used = jnp.int32(n // tm)
    n_blk, nf = n // tm, ff // tf
    be_all = jnp.concatenate([block_e, jnp.reshape(n_used, (1,)).astype(I32)])
    wmode = dict(pipeline_mode=pl.Buffered(1)) if (wg.shape[0] == 1 and ff == tf) else {}
    ftile = lambda i, f, be: jnp.where(i < be[n_blk], f, nf - 1)
    ple_arrays, ple_specs = ((), [])
    if dense:
        ple_arrays, ple_specs = _ple_operands(*ple, tm, lambda i, f, be: (i, 0), lambda i, f, be: (0, 0))
    grid_spec = pltpu.PrefetchScalarGridSpec(
        num_scalar_prefetch=1,
        grid=(n_blk, nf),
        in_specs=[pl.BlockSpec((tm, x.shape[1]), lambda i, f, be: (i, 0)),
                  pl.BlockSpec((1, d_), lambda i, f, be: (0, 0)),
                  pl.BlockSpec((1, d_, tf), lambda i, f, be: (be[i], 0, ftile(i, f, be)), **wmode),
                  pl.BlockSpec((1, d_, tf), lambda i, f, be: (be[i], 0, ftile(i, f, be)), **wmode),
                  pl.BlockSpec((1, tf, d_), lambda i, f, be: (be[i], ftile(i, f, be), 0), **wmode)] + ple_specs,
        out_specs=pl.BlockSpec((tm, d_), lambda i, f, be: (i, 0)),
        scratch_shapes=[pltpu.VMEM((tm, d_), BF16), pltpu.VMEM((tm, d_), F32)],
    )
    return pl.pallas_call(
        functools.partial(_ffn_kernel, dense=dense, final=final),
        name="ffn_dense" if dense else "ffn_expert",
        grid_spec=grid_spec,
        out_shape=jax.ShapeDtypeStruct((n, d_), F32 if dense else BF16),
        compiler_params=_cparams(("parallel", "arbitrary"), V7X_VMEM_LIMIT_BYTES),
    )(be_all, x, norm_g.reshape(1, d_), wg, wu, wd, *ple_arrays)


def _pack_bf16_pairs(x):
    w = x.shape[1] // 2
    bits = lax.bitcast_convert_type(x.astype(BF16).astype(F32), jnp.uint32)
    return (bits[:, :w] >> 16) | bits[:, w:]


def _unpack_bf16_pairs(words):
    lo = lax.bitcast_convert_type(words << 16, F32)
    hi = lax.bitcast_convert_type(words & jnp.uint32(0xFFFF0000), F32)
    return jnp.concatenate([lo, hi], axis=1).astype(BF16)


def _router_kernel(x_ref, g_ref, wr_ref, tri_ref, h_ref, e_ref, p_ref, r_ref, cb_ref, tot_ref, cnt_ref):
    @pl.when(pl.program_id(0) == 0)
    def _():
        cnt_ref[...] = jnp.zeros_like(cnt_ref)

    hn = _rms(x_ref[...], g_ref[...])
    h_ref[...] = _pack_bf16_pairs(hn)
    logits = lax.dot_general(wr_ref[...], hn, (((1,), (1,)), ((), ())),
                             precision=lax.Precision.HIGHEST, preferred_element_type=F32)
    eidx = lax.broadcasted_iota(I32, logits.shape, 0)
    m1 = jnp.max(logits, axis=0, keepdims=True)
    i1 = jnp.min(jnp.where(logits == m1, eidx, N_EXPERTS), axis=0, keepdims=True)
    rest = jnp.where(eidx == i1, -jnp.inf, logits)
    m2 = jnp.max(rest, axis=0, keepdims=True)
    i2 = jnp.min(jnp.where(rest == m2, eidx, N_EXPERTS), axis=0, keepdims=True)
    e2 = jnp.exp(m2 - m1)
    inv = 1.0 / (1.0 + e2)
    e_ref[...] = jnp.concatenate([i1, i2], axis=0)
    p_ref[...] = jnp.concatenate([inv, e2 * inv], axis=0)

    tm = logits.shape[1]
    chosen = jnp.where((eidx == i1) | (eidx == i2), 1.0, 0.0)
    incl = jnp.dot(chosen.astype(BF16), tri_ref[...], preferred_element_type=F32)
    before = incl - chosen + cnt_ref[:, 0:1]
    r_ref[...] = jnp.concatenate(
        [jnp.sum(jnp.where(eidx == i1, before, 0.0), axis=0, keepdims=True),
         jnp.sum(jnp.where(eidx == i2, before, 0.0), axis=0, keepdims=True)], axis=0).astype(I32)
    lane = lax.broadcasted_iota(I32, cb_ref.shape, 1)
    cb = jnp.zeros(cb_ref.shape, F32)
    for c in range(tm // CMB_ROWS):
        cb = jnp.where(lane == c, before[:, c * CMB_ROWS:c * CMB_ROWS + 1], cb)
    cb_ref[...] = cb.astype(I32)
    total = cnt_ref[...] + incl[:, tm - 1:tm]
    cnt_ref[...] = total
    tot_ref[...] = total.astype(I32)


def _router(x, norm_g, router_w):
    n, d_ = x.shape
    tm = min(ROW_TILE, n)
    assert tm % CMB_ROWS == 0
    wr = router_w.T
    tri = (jnp.arange(tm)[:, None] <= jnp.arange(tm)[None, :]).astype(BF16)
    h, top_e, top_p, rank, cb, tot = pl.pallas_call(
        _router_kernel,
        name="router",
        grid=(n // tm,),
        in_specs=[pl.BlockSpec((tm, d_), lambda i: (i, 0)),
                  pl.BlockSpec((1, d_), lambda i: (0, 0)),
                  pl.BlockSpec(wr.shape, lambda i: (0, 0)),
                  pl.BlockSpec(tri.shape, lambda i: (0, 0))],
        out_specs=[pl.BlockSpec((tm, d_ // 2), lambda i: (i, 0)),
                   pl.BlockSpec((2, tm), lambda i: (0, i)),
                   pl.BlockSpec((2, tm), lambda i: (0, i)),
                   pl.BlockSpec((2, tm), lambda i: (0, i)),
                   pl.BlockSpec((N_EXPERTS, V7X_LANES), lambda i: (0, i)),
                   pl.BlockSpec((N_EXPERTS, V7X_LANES), lambda i: (0, 0))],
        out_shape=[jax.ShapeDtypeStruct((n, d_ // 2), jnp.uint32),
                   jax.ShapeDtypeStruct((2, n), I32),
                   jax.ShapeDtypeStruct((2, n), F32),
                   jax.ShapeDtypeStruct((2, n), I32),
                   jax.ShapeDtypeStruct((N_EXPERTS, n // tm * V7X_LANES), I32),
                   jax.ShapeDtypeStruct((N_EXPERTS, V7X_LANES), I32)],
        scratch_shapes=[pltpu.VMEM((N_EXPERTS, V7X_LANES), F32)],
        compiler_params=_cparams(("arbitrary",), V7X_VMEM_LIMIT_BYTES),
    )(x, norm_g.reshape(1, d_), wr, tri)
    cb = cb.reshape(N_EXPERTS, n // tm, V7X_LANES)[:, :, :tm // CMB_ROWS].reshape(N_EXPERTS, n // CMB_ROWS).T
    return h, top_e, top_p, rank, cb, tot[:, 0]


SCATTER_UNROLL = 16


def _scatter_kernel(dest_ref, h_ref, zeros_ref, xs_ref, sem):
    del zeros_ref
    n_tok = h_ref.shape[0]

    def row_copy(t, k):
        return pltpu.make_async_copy(h_ref.at[pl.ds(t, 1), :], xs_ref.at[pl.ds(dest_ref[k, t], 1), :], sem)

    def issue(u, carry):
        for r in range(SCATTER_UNROLL):
            t = u * SCATTER_UNROLL + r
            row_copy(t, 0).start(priority=0)
            row_copy(t, 1).start(priority=1)
        return carry

    lax.fori_loop(0, n_tok // SCATTER_UNROLL, issue, 0)
    for _ in range(2):
        pltpu.make_async_copy(h_ref, xs_ref.at[pl.ds(0, n_tok), :], sem).wait()


def _scatter(h, dest, n_slots):
    n, w = h.shape
    tc = min(MOE_TOK_CHUNK, n)
    zeros = jnp.zeros((n_slots, w), h.dtype)
    return pl.pallas_call(
        _scatter_kernel,
        name="moe_scatter",
        grid=(n // tc,),
        in_specs=[pl.BlockSpec((2, tc), lambda c: (0, c), memory_space=pltpu.SMEM),
                  pl.BlockSpec((tc, w), lambda c: (c, 0)),
                  pl.BlockSpec(memory_space=pl.ANY)],
        out_specs=pl.BlockSpec(memory_space=pl.ANY),
        out_shape=jax.ShapeDtypeStruct((n_slots, w), h.dtype),
        scratch_shapes=[pltpu.SemaphoreType.DMA(())],
        input_output_aliases={2: 0},
        compiler_params=_cparams(("arbitrary",), V7X_VMEM_LIMIT_BYTES),
    )(dest, h, zeros)


N_CMB_OPS = 2 * N_EXPERTS


def _combine_kernel(jt_ref, vt_ref, x_ref, dest_ref, gate_ref, *rest, final):
    yb_refs = rest[:N_CMB_OPS]
    ple_refs = rest[N_CMB_OPS:-1]
    o_ref = rest[-1]
    c = pl.program_id(0)
    dest = dest_ref[...]
    gate = gate_ref[...]
    lane = lax.broadcasted_iota(I32, (dest.shape[0], CMB_ROWS), 1)
    acc = x_ref[...]
    for k in range(N_CMB_OPS):
        base = jnp.where(vt_ref[c * N_CMB_OPS + k] == 1, jt_ref[c * N_CMB_OPS + k] * CMB_ROWS, -2 * CMB_ROWS)
        rel = dest - base
        w = (jnp.where(lane == rel[:, 0:1], gate[:, 0:1], 0.0)
             + jnp.where(lane == rel[:, 1:2], gate[:, 1:2], 0.0)).astype(BF16)
        acc = acc + jnp.dot(w, yb_refs[k][...], preferred_element_type=F32)
    o_ref[...] = _ple_apply(acc, *ple_refs, final=final)


def _combine(x, yb, dest_t, gate_t, jt, vt, ple, final):
    n, d_ = x.shape
    ple_arrays, ple_specs = _ple_operands(*ple, CMB_ROWS, lambda c, jt, vt: (c, 0), lambda c, jt, vt: (0, 0))
    yb_spec = lambda k: pl.BlockSpec((CMB_ROWS, d_), lambda c, jt, vt, k=k: (jt[c * N_CMB_OPS + k], 0))
    grid_spec = pltpu.PrefetchScalarGridSpec(
        num_scalar_prefetch=2,
        grid=(n // CMB_ROWS,),
        in_specs=[pl.BlockSpec((CMB_ROWS, d_), lambda c, jt, vt: (c, 0)),
                  pl.BlockSpec((CMB_ROWS, 2), lambda c, jt, vt: (c, 0)),
                  pl.BlockSpec((CMB_ROWS, 2), lambda c, jt, vt: (c, 0))]
                 + [yb_spec(k) for k in range(N_CMB_OPS)] + ple_specs,
        out_specs=pl.BlockSpec((CMB_ROWS, d_), lambda c, jt, vt: (c, 0)),
    )
    return pl.pallas_call(
        functools.partial(_combine_kernel, final=final),
        name="moe_combine",
        grid_spec=grid_spec,
        out_shape=jax.ShapeDtypeStruct((n, d_), F32),
        compiler_params=_cparams(("parallel",), V7X_VMEM_LIMIT_BYTES),
    )(jt, vt, x, dest_t, gate_t, *([yb] * N_CMB_OPS), *ple_arrays)


def _moe_plan(top_e, rank, cb, counts, n_slots):
    padded = (counts + MOE_ROWS - 1) // MOE_ROWS * MOE_ROWS
    pad_ends = jnp.cumsum(padded)
    pad_starts = pad_ends - padded
    start_of = sum(jnp.where(top_e == e, pad_starts[e], 0) for e in range(N_EXPERTS))
    dest = (start_of + rank).astype(I32)
    n_blk = n_slots // MOE_ROWS
    blk_start = jnp.arange(n_blk, dtype=I32) * MOE_ROWS
    block_e = jnp.minimum(jnp.sum((pad_ends[None, :] <= blk_start[:, None]).astype(I32), axis=1),
                          N_EXPERTS - 1).astype(I32)

    cb = jnp.concatenate([cb, counts[None]], axis=0)
    lo = pad_starts[None] + cb[:-1]
    hi = pad_starts[None] + cb[1:]
    ja = lo // CMB_ROWS
    jb = (hi - 1) // CMB_ROWS
    va = hi > lo
    vb = va & (jb > ja)
    jt = jnp.stack([ja, jb], axis=-1).reshape(-1)
    vt = jnp.stack([va, vb], axis=-1).reshape(-1)
    jt = jnp.where(vt, jt, 0).astype(I32)
    return dest, block_e, pad_ends[-1] // MOE_ROWS, jt, vt.astype(I32)


def _moe(x, norm_g, router_w, wg, wu, wd, ple, final):
    n, d_ = x.shape
    n_slots = (2 * n // MOE_ROWS + N_EXPERTS) * MOE_ROWS
    h, top_e, top_p, rank, cb, counts = _router(x, norm_g, router_w)
    dest, block_e, n_used, jt, vt = _moe_plan(top_e, rank, cb, counts, n_slots)
    xs = _scatter(h, dest, n_slots)
    yb = _ffn(xs, norm_g, wg, wu, wd, block_e, n_used, dense=False, tf=wg.shape[2] // 2)
    return _combine(x, yb, dest.T, top_p.T, jt, vt, ple, final)


def kernel(x, p, attn_norm, w_in, cmp_pos, cmp_w1, cmp_w2, conv_w, conv_b, lru_wa, lru_ba, lru_wx, lru_bx,
           lru_lambda, out_norm_attn, out_norm_rec, w_out, ffn_norm, dense_w_gate, dense_w_up, dense_w_down,
           router_w, moe_w_gate, moe_w_up, moe_w_down, ple_norm, ple_w_gate, ple_w_proj, final_norm):
    b_, t_, d_ = x.shape
    depth = w_in.shape[0]
    n = b_ * t_
    slc_cols, win_cols = _key_position_columns(t_)
    for i in range(depth):
        kcvc, ks, kw, xr, yg, qT, vsT, vwT, gT = _proj_in(x, attn_norm[i], w_in[i], slc_cols, win_cols)
        kcmp = _compress(kcvc, cmp_pos[i], cmp_w1[i], cmp_w2[i])
        oaT = _attention(qT, gT, kcmp, ks, vsT, kw, vwT)
        orec = _rglru(xr, yg, conv_w[i], conv_b[i], lru_wa[i], lru_ba[i], lru_wx[i], lru_bx[i], lru_lambda[i])
        x = _out_proj(oaT, orec, x, out_norm_attn[i], out_norm_rec[i], w_out[i])
        x2 = x.reshape(n, d_)
        j = i // 2
        ple = (p[i].reshape(n, -1), ple_norm[i], ple_w_gate[i], ple_w_proj[i], final_norm)
        final = i == depth - 1
        if i % 2 == 0:
            x2 = _ffn(x2, ffn_norm[i], dense_w_gate[j][None].astype(BF16), dense_w_up[j][None].astype(BF16),
                      dense_w_down[j][None].astype(BF16), None, dense=True, tf=dense_w_gate.shape[2],
                      ple=ple, final=final)
        else:
            x2 = _moe(x2, ffn_norm[i], router_w[j], moe_w_gate[j].astype(BF16), moe_w_up[j].astype(BF16),
                      moe_w_down[j].astype(BF16), ple, final)
        x = x2.reshape(b_, t_, d_)
    return x
```

```python
import functools

import jax
import jax.numpy as jnp
from jax import lax
from jax.experimental import pallas as pl
from jax.experimental.pallas import tpu as pltpu

F32 = jnp.float32
BF16 = jnp.bfloat16
I32 = jnp.int32

N_ATTN_HEADS = 8
HEAD_DIM = 64
N_KV = 2
HEADS_PER_KV = N_ATTN_HEADS // N_KV
D_ATTN = N_ATTN_HEADS * HEAD_DIM
KV_W = N_KV * HEAD_DIM
N_GATES = 3
L_CMP = 32
STRIDE = 16
L_SLC = 64
N_SEL = 16
N_LOCAL = 2
W_WIN = 512
CMP_HID = 2 * HEAD_DIM
CONV_W = 4
C_LRU = 8.0
N_EXPERTS = 8
RMS_EPS = 1e-6
ATTN_SCALE = HEAD_DIM ** -0.5
LOG2E = 1.4426950408889634
LOG2E_HI = 1.4453125
LOG2E_LO = LOG2E - LOG2E_HI

V7X_LANES = 128
V7X_VMEM_LIMIT_BYTES = 56 * 1024 * 1024

TQ = 128
KT_SLC = 256
SLC_UNROLL = 4
KT_WIN = 128
N_WIN_TILES = W_WIN // KT_WIN + 1
V_ROWS = HEAD_DIM + 16
MASK_NEG = -1e30
SEL_NEG = -float(2 ** 20)
ROW_TILE = 512
PROJ_TILE = 1024
RGLRU_CHUNK = 512
MOE_ROWS = 512
MOE_TOK_CHUNK = 512
CMB_ROWS = 256


def _cparams(semantics, vmem=None):
    return pltpu.CompilerParams(dimension_semantics=semantics, vmem_limit_bytes=vmem)


def _rms(x, g):
    ms = jnp.mean(x * x, axis=-1, keepdims=True)
    return x * lax.rsqrt(ms + RMS_EPS) * g


def _gelu_tanh(x):
    c = 0.7978845608028654
    return x * (0.5 * (1.0 + jnp.tanh(c * (x + 0.044715 * (x * x * x)))))


def _sigmoid(x):
    return 1.0 / (1.0 + jnp.exp(-x))


def _silu(x):
    return x * _sigmoid(x)


def _proj_in_kernel(x_ref, g_ref, wtok_ref, wtr_ref,
                    slc_ref, win_ref, kcvc_ref, ks_ref, kw_ref, xr_ref, yg_ref, qT_ref, vsT_ref, vwT_ref, gT_ref):
    hn = _rms(x_ref[0], g_ref[...]).astype(BF16)
    tok = jnp.dot(hn, wtok_ref[...], preferred_element_type=F32)
    kcvc_ref[0] = tok[:, 0:256]
    xr_ref[0] = tok[:, 512:1024].astype(BF16)
    yg_ref[0] = tok[:, 1024:1536].astype(BF16)
    lane = lax.broadcasted_iota(I32, (tok.shape[0], V7X_LANES), 1)
    slc_c = slc_ref[...].astype(F32)
    win_c = win_ref[...].astype(F32)
    for g in range(N_KV):
        in_g = lambda k2: jnp.where(lane < HEAD_DIM, k2 if g == 0 else pltpu.roll(k2, HEAD_DIM, axis=1), 0.0)
        ks_ref[0, g] = jnp.concatenate([in_g(tok[:, 256:384]) + slc_c[:, :V7X_LANES], slc_c[:, V7X_LANES:]],
                                       axis=1).astype(BF16)
        kw_ref[0, g] = (in_g(tok[:, 384:512]) + win_c).astype(BF16)
    tr = lax.dot_general(wtr_ref[...], hn, (((1,), (1,)), ((), ())),
                         preferred_element_type=F32)
    qT_ref[0] = (tr[0:512] * (ATTN_SCALE * LOG2E)).astype(BF16)
    ones = jnp.ones((V_ROWS - HEAD_DIM, tr.shape[1]), BF16)
    for g in range(N_KV):
        for v_ref, r0 in ((vsT_ref, 512), (vwT_ref, 640)):
            v_ref[0, g, 0:HEAD_DIM, :] = tr[r0 + g * HEAD_DIM:r0 + (g + 1) * HEAD_DIM].astype(BF16)
            v_ref[0, g, HEAD_DIM:V_ROWS, :] = ones
    gT_ref[0] = _sigmoid(tr[768:800])


def _key_position_columns(t_):
    pos = jnp.arange(t_, dtype=I32)
    dup = lambda vals: [v.astype(BF16)[:, None] for v in vals for _ in range(2)]
    pad_to = lambda cols, width: jnp.concatenate(
        cols + [jnp.zeros((t_, width - sum(c.shape[1] for c in cols)), BF16)], axis=1)
    lead = jnp.zeros((t_, HEAD_DIM), BF16)
    slc = pad_to([lead, jax.nn.one_hot(pos // L_SLC, t_ // L_SLC, dtype=BF16)]
                 + dup([pos // L_SLC, pos % L_SLC]), 2 * HEAD_DIM + t_ // L_SLC)
    win = pad_to([lead] + dup([pos % KT_WIN]), 2 * HEAD_DIM)
    return slc, win


def _proj_in(x, norm_g, w_in, slc_cols, win_cols):
    b_, t_, d_ = x.shape
    tm = min(PROJ_TILE, t_)
    q, kc, vc, ks, vs, kw, vw, g, xr, yg = jnp.split(
        w_in, [512, 640, 768, 896, 1024, 1152, 1280, 1304, 1816], axis=1)
    wtok = jnp.concatenate([kc, vc, ks, kw, xr, yg], axis=1).astype(BF16)
    g4 = g.reshape(d_, N_KV, HEADS_PER_KV, N_GATES).transpose(0, 1, 3, 2)
    g4 = jnp.pad(g4.reshape(d_, N_KV, 12), ((0, 0), (0, 0), (0, 4))).reshape(d_, 32)
    wtr = jnp.concatenate([q, vs, vw, g4], axis=1).T.astype(BF16)
    nt = t_ // tm
    row = lambda shape: pl.BlockSpec((1, tm, shape), lambda b, i: (b, i, 0))
    col = lambda shape: pl.BlockSpec((1, shape, tm), lambda b, i: (b, 0, i))
    full = lambda a: pl.BlockSpec(a.shape, lambda b, i: (0,) * a.ndim)
    g2 = norm_g.reshape(1, d_)
    ws, ww = slc_cols.shape[1], win_cols.shape[1]
    grow = lambda width: pl.BlockSpec((1, N_KV, tm, width), lambda b, i: (b, 0, i, 0))
    gcol = pl.BlockSpec((1, N_KV, V_ROWS, tm), lambda b, i: (b, 0, 0, i))
    outs = pl.pallas_call(
        _proj_in_kernel,
        name="proj_in",
        grid=(b_, nt),
        in_specs=[row(d_), full(g2), full(wtok), full(wtr),
                  pl.BlockSpec((tm, ws), lambda b, i: (i, 0)), pl.BlockSpec((tm, ww), lambda b, i: (i, 0))],
        out_specs=[row(256), grow(ws), grow(ww), row(512), row(512), col(512), gcol, gcol, col(32)],
        out_shape=[
            jax.ShapeDtypeStruct((b_, t_, 256), F32),
            jax.ShapeDtypeStruct((b_, N_KV, t_, ws), BF16),
            jax.ShapeDtypeStruct((b_, N_KV, t_, ww), BF16),
            jax.ShapeDtypeStruct((b_, t_, 512), BF16),
            jax.ShapeDtypeStruct((b_, t_, 512), BF16),
            jax.ShapeDtypeStruct((b_, 512, t_), BF16),
            jax.ShapeDtypeStruct((b_, N_KV, V_ROWS, t_), BF16),
            jax.ShapeDtypeStruct((b_, N_KV, V_ROWS, t_), BF16),
            jax.ShapeDtypeStruct((b_, 32, t_), F32),
        ],
        compiler_params=_cparams(("parallel", "parallel"), V7X_VMEM_LIMIT_BYTES),
    )(x, g2, wtok, wtr, slc_cols, win_cols)
    return outs


def _compress_kernel(xc_ref, pa_ref, pb_ref, w1a_ref, w1b_ref, w2_ref, out_ref):
    xc = xc_ref[0].astype(BF16)
    n_chunk = xc.shape[0]
    a = jnp.dot(xc, w1a_ref[...], preferred_element_type=F32)
    bm = jnp.dot(xc, w1b_ref[...], preferred_element_type=F32)
    posc = (jnp.dot(pa_ref[...], w1a_ref[...], preferred_element_type=F32)
            + jnp.dot(pb_ref[...], w1b_ref[...], preferred_element_type=F32))[0:1]
    row = lax.broadcasted_iota(I32, bm.shape, 0)
    bm_up = jnp.where(row < n_chunk - 1, pltpu.roll(bm, n_chunk - 1, axis=0), 0.0)
    hid = _gelu_tanh(a + bm_up + posc).astype(BF16)
    out_ref[0] = jnp.dot(hid, w2_ref[...], preferred_element_type=F32)


def _compress(kcvc, cmp_pos, cmp_w1, cmp_w2):
    b_, t_, _ = kcvc.shape
    n_chunk = t_ // STRIDE
    xc = kcvc.reshape(b_, n_chunk, STRIDE * 256)
    w1 = cmp_w1.reshape(2, 2, STRIDE, HEAD_DIM, CMP_HID)

    def expand(half):
        w = w1[:, half]
        full = jnp.einsum('wldo,wx,gy->lxgdwyo', w, jnp.eye(2, dtype=F32), jnp.eye(2, dtype=F32))
        return full.reshape(STRIDE * 256, 4 * CMP_HID).astype(BF16)

    w1a, w1b = expand(0), expand(1)
    pos = cmp_pos.reshape(2, 2, STRIDE, HEAD_DIM)

    def posrow(half):
        p = jnp.broadcast_to(pos[:, half][:, None], (2, N_KV, STRIDE, HEAD_DIM))
        p = p.transpose(2, 0, 1, 3).reshape(1, STRIDE * 256)
        return jnp.pad(p, ((0, 7), (0, 0))).astype(BF16)

    pa, pb = posrow(0), posrow(1)
    w2 = jnp.einsum('whd,wx,gy->wghxyd', cmp_w2, jnp.eye(2, dtype=F32), jnp.eye(2, dtype=F32))
    w2 = w2.reshape(4 * CMP_HID, 4 * HEAD_DIM).astype(BF16)
    full = lambda a: pl.BlockSpec(a.shape, lambda b: (0,) * a.ndim)
    return pl.pallas_call(
        _compress_kernel,
        name="compress_kv",
        grid=(b_,),
        in_specs=[pl.BlockSpec((1, n_chunk, STRIDE * 256), lambda b: (b, 0, 0)),
                  full(pa), full(pb), full(w1a), full(w1b), full(w2)],
        out_specs=pl.BlockSpec((1, n_chunk, 256), lambda b: (b, 0, 0)),
        out_shape=jax.ShapeDtypeStruct((b_, n_chunk, 256), F32),
        compiler_params=_cparams(("parallel",), V7X_VMEM_LIMIT_BYTES),
    )(xc, pa, pb, w1a, w1b, w2)


def _head_lanes(rows):
    r = rows.shape[0] // HEADS_PER_KV
    return jnp.concatenate([rows[h * r:(h + 1) * r, :] for h in range(HEADS_PER_KV)], axis=1)


def _attn_kernel(qT_ref, gT_ref, kc_ref, vcT_ref, ks_ref, vsT_ref, kw_ref, vwT_ref, mt_ref, o_ref, s_scr,
                 tile_list):
    i = pl.program_id(1)
    n_cmp = kc_ref.shape[2]
    n_slc = mt_ref.shape[0]
    nl = HEADS_PER_KV * TQ
    hd_rows = HEADS_PER_KV * HEAD_DIM

    lane = lax.broadcasted_iota(I32, (1, nl), 1)
    t_loc = lane % TQ
    t_row = i * TQ + t_loc

    def extra_rows(n_rows, rows):
        ridx = lax.broadcasted_iota(I32, (16, nl), 0)
        out = jnp.zeros((16, nl), F32)
        for k, r in enumerate(rows):
            out = jnp.where(ridx == k, r, out)
        return jnp.concatenate([out.astype(BF16), jnp.zeros((n_rows - 16, nl), BF16)], axis=0)

    ok_c = (lax.broadcasted_iota(I32, (n_cmp, nl), 0) * STRIDE + (L_CMP - 1)) <= t_row
    r_idx = lax.broadcasted_iota(I32, (KT_WIN, nl), 0)
    s_idx = lax.broadcasted_iota(I32, (n_slc, TQ), 0)
    cur = (i * TQ + lax.broadcasted_iota(I32, (1, TQ), 1)) // L_SLC
    valid = s_idx <= cur
    forced = valid & ((s_idx == 0) | (s_idx > cur - N_LOCAL))
    s_idx_f = s_idx.astype(F32)
    n_full = (i * TQ) // KT_SLC
    mt = mt_ref[...]

    def pick(_, score):
        m = jnp.max(score, axis=0, keepdims=True)
        idx = jnp.min(jnp.where(score == m, s_idx_f, 1.0e9), axis=0, keepdims=True)
        return jnp.where(s_idx_f == idx, -2.0, score)

    def stage_a(g, bs, tile, slot, causal):
        k0 = pl.multiple_of(tile * KT_SLC, KT_SLC)
        s = jnp.dot(ks_ref[0, g, pl.ds(k0, KT_SLC), :], bs, preferred_element_type=F32)
        if causal:
            pos = k0 + lax.broadcasted_iota(I32, (KT_SLC, nl), 0)
            s = jnp.where(pos <= t_row, s, MASK_NEG)
        s_scr[g, slot] = s
        return jnp.max(s, axis=0, keepdims=True)

    def stage_b(g, tile, slot, m, acc, m_tile):
        k0 = pl.multiple_of(tile * KT_SLC, KT_SLC)
        m_new = jnp.maximum(m, m_tile)
        p = jnp.exp2(s_scr[g, slot] - m_new)
        acc = jnp.exp2(m - m_new) * acc + jnp.dot(vsT_ref[0, g, :, pl.ds(k0, KT_SLC)], p.astype(BF16),
                                                  preferred_element_type=F32)
        return m_new, acc

    groups = range(N_KV)

    def setup(g):
        q4 = _head_lanes(qT_ref[0, g * hd_rows:(g + 1) * hd_rows, :])
        head = g * HEADS_PER_KV + lane // TQ + 1
        slope = lax.bitcast_convert_type((127 - head) << 23, F32)
        return dict(q4=q4, slope=slope, s_hi=slope * LOG2E_HI, s_lo=slope * LOG2E_LO)

    st = [setup(g) for g in groups]

    def cmp_scores(g):
        s_hi, s_lo = st[g]["s_hi"], st[g]["s_lo"]
        bc = jnp.concatenate(
            [st[g]["q4"], extra_rows(64, [256.0 * s_hi, 256.0 * s_lo, 16.0 * s_hi, 16.0 * s_lo])], axis=0)
        return jnp.dot(kc_ref[0, g], bc, preferred_element_type=F32)

    def cmp_softmax(g, s_c):
        s_c = jnp.where(ok_c, s_c, MASK_NEG)
        m_c = jnp.max(s_c, axis=0, keepdims=True)
        e_c = jnp.exp2(s_c - m_c)
        l_c = jnp.sum(e_c, axis=0, keepdims=True)
        inv_c = jnp.where(m_c > 0.5 * MASK_NEG, 1.0 / jnp.maximum(l_c, 1e-30), 0.0)
        p_c = e_c * inv_c
        o_c = jnp.dot(vcT_ref[0, g], p_c.astype(BF16), preferred_element_type=F32)
        p_grp = (p_c[:, 0:TQ] + p_c[:, TQ:2 * TQ]) + (p_c[:, 2 * TQ:3 * TQ] + p_c[:, 3 * TQ:4 * TQ])
        p1 = p_grp.astype(BF16)
        r1 = p_grp - p1.astype(F32)
        p2 = r1.astype(BF16)
        p3 = (r1 - p2.astype(F32)).astype(BF16)
        p_slc = (jnp.dot(mt, p1, preferred_element_type=F32) + jnp.dot(mt, p2, preferred_element_type=F32)
                 + jnp.dot(mt, p3, preferred_element_type=F32))
        return o_c, p_slc

    def win_scores(g):
        slope = st[g]["slope"]
        bw = jnp.concatenate([st[g]["q4"], extra_rows(64, [st[g]["s_hi"], st[g]["s_lo"]])], axis=0)
        s_tiles, c_tiles = [], []
        for a in range(N_WIN_TILES):
            tile = i - (N_WIN_TILES - 1) + a
            k0 = pl.multiple_of(jnp.maximum(tile, 0) * KT_WIN, KT_WIN)
            s = jnp.dot(kw_ref[0, g, pl.ds(k0, KT_WIN), :], bw, preferred_element_type=F32)
            if a == 0:
                s = jnp.where(r_idx > t_loc, s, MASK_NEG)
            elif a == N_WIN_TILES - 1:
                s = jnp.where(r_idx <= t_loc, s, MASK_NEG)
            s_tiles.append(s)
            c_a = slope * (LOG2E * KT_WIN * (a - (N_WIN_TILES - 1)))
            c_tiles.append(jnp.where(tile >= 0, c_a, MASK_NEG))
        return s_tiles, c_tiles

    def win_softmax(g, s_tiles, c_tiles):
        m_w = s_tiles[0].max(axis=0, keepdims=True) + c_tiles[0]
        for s, c_a in zip(s_tiles[1:], c_tiles[1:]):
            m_w = jnp.maximum(m_w, s.max(axis=0, keepdims=True) + c_a)
        acc_w = jnp.zeros((V_ROWS, nl), F32)
        for a, (s, c_a) in enumerate(zip(s_tiles, c_tiles)):
            tile = i - (N_WIN_TILES - 1) + a
            k0 = pl.multiple_of(jnp.maximum(tile, 0) * KT_WIN, KT_WIN)
            p = jnp.exp2(s - (m_w - c_a))
            acc_w = acc_w + jnp.dot(vwT_ref[0, g, :, pl.ds(k0, KT_WIN)], p.astype(BF16),
                                    preferred_element_type=F32)
        return acc_w[0:HEAD_DIM] * (1.0 / acc_w[HEAD_DIM:HEAD_DIM + 1])

    s_cmp = [cmp_scores(g) for g in groups]
    s_win = [win_scores(g) for g in groups]
    cmp_out, scores = [], []
    for g in groups:
        cmp_out.append(cmp_softmax(g, s_cmp[g]))
        score0 = jnp.where(forced, -2.0, jnp.where(valid, cmp_out[g][1], -1.0))
        scores.append(lax.fori_loop(0, N_SEL - (N_LOCAL + 1), pick, score0, unroll=True))
    o_win = [win_softmax(g, *s_win[g]) for g in groups]
    score = jnp.concatenate(scores, axis=1)

    def select(g):
        sel = valid & (score[:, g * TQ:(g + 1) * TQ] < -1.5)
        sel4 = jnp.concatenate([sel] * HEADS_PER_KV, axis=1)
        blk_rows = jnp.where(sel4, 0.0, SEL_NEG).astype(BF16)
        s_hi, s_lo = st[g]["s_hi"], st[g]["s_lo"]
        bs = jnp.concatenate(
            [st[g]["q4"], blk_rows, extra_rows(64, [L_SLC * s_hi, L_SLC * s_lo, s_hi, s_lo])], axis=0)
        m_tile0 = stage_a(g, bs, n_full, 0, True)
        gt = gT_ref[0, g * 16:(g + 1) * 16, :]
        gates = [_head_lanes(gt[k * HEADS_PER_KV:(k + 1) * HEADS_PER_KV, :]) for k in range(N_GATES)]
        return dict(bs=bs, base=gates[0] * cmp_out[g][0] + gates[2] * o_win[g], g_s=gates[1], m_tile0=m_tile0)

    grp = [select(g) for g in groups]

    blocks_per_tile = KT_SLC // L_SLC
    n_tiles = n_slc // blocks_per_tile
    assert n_tiles <= 32
    valid2 = jnp.concatenate([valid] * N_KV, axis=1)
    sel_all = jnp.where(valid2 & (score < -1.5), 1.0, 0.0).astype(BF16)
    blk_cnt = lax.dot_general(jnp.ones((8, N_KV * TQ), BF16), sel_all, (((1,), (1,)), ((), ())),
                              preferred_element_type=F32)
    memb = jnp.where(lax.broadcasted_iota(I32, (n_slc, V7X_LANES), 0) // blocks_per_tile
                     == lax.broadcasted_iota(I32, (n_slc, V7X_LANES), 1), 1.0, 0.0).astype(BF16)
    tile_cnt = jnp.dot(jnp.where(blk_cnt > 0.5, 1.0, 0.0).astype(BF16), memb,
                       preferred_element_type=F32)[0:1]
    lane_t = lax.broadcasted_iota(I32, (1, V7X_LANES), 1)
    need = (tile_cnt > 0.5) & (lane_t < n_full)
    bit = lax.bitcast_convert_type(((lane_t & 15) + 127) << 23, F32)
    lo_bits = jnp.sum(jnp.where(need & (lane_t < 16), bit, 0.0)).astype(I32)
    hi_bits = jnp.sum(jnp.where(need & (lane_t >= 16), bit, 0.0)).astype(I32)
    n_sel = jnp.int32(0)
    for t in range(n_tiles - 1):
        tile_list[n_sel] = t
        n_sel = n_sel + (((lo_bits if t < 16 else hi_bits) >> (t % 16)) & 1)

    order = lambda k: jnp.where(k == 0, n_full, tile_list[jnp.maximum(k - 1, 0)])

    def half(carry, tile_b, slot_b, tile_a):
        m_next = [c[2] if tile_a is None else stage_a(g, grp[g]["bs"], tile_a, 1 - slot_b, False)
                  for g, c in enumerate(carry)]
        out = []
        for g in range(N_KV):
            m, acc, m_tile = carry[g]
            out.append(stage_b(g, tile_b, slot_b, m, acc, m_tile) + (m_next[g],))
        return tuple(out)

    def run(carry, k0, n, last):
        for j in range(n):
            nxt = None if (last and j == n - 1) else order(k0 + j + 1)
            carry = half(carry, order(k0 + j), j % 2, nxt)
        return carry

    init = tuple((jnp.full((1, nl), MASK_NEG, F32), jnp.zeros((V_ROWS, nl), F32), grp[g]["m_tile0"])
                 for g in range(N_KV))
    carry = lax.fori_loop(0, n_sel // SLC_UNROLL, lambda p, c: run(c, SLC_UNROLL * p, SLC_UNROLL, False), init)
    k_tail = SLC_UNROLL * (n_sel // SLC_UNROLL)
    carry = lax.switch(n_sel % SLC_UNROLL, [functools.partial(run, k0=k_tail, n=r + 1, last=True)
                                            for r in range(SLC_UNROLL)], carry)
    for g in range(N_KV):
        _, acc, _ = carry[g]
        o_s = acc[0:HEAD_DIM] * (1.0 / acc[HEAD_DIM:HEAD_DIM + 1])
        out = grp[g]["base"] + grp[g]["g_s"] * o_s
        for h in range(HEADS_PER_KV):
            r0 = g * hd_rows + h * HEAD_DIM
            o_ref[0, r0:r0 + HEAD_DIM, :] = out[:, h * TQ:(h + 1) * TQ].astype(o_ref.dtype)


def _attention(qT, gT, kcmp, ks_aug, vsT4, kw_aug, vwT4):
    b_, _, t_ = qT.shape
    n_cmp = t_ // STRIDE
    n_slc = t_ // L_SLC
    c = jnp.arange(n_cmp, dtype=I32)
    cmp_cols = jnp.concatenate([v.astype(BF16)[:, None] for v in (c // 16, c // 16, c % 16, c % 16)]
                               + [jnp.zeros((n_cmp, HEAD_DIM - 4), BF16)], axis=1)
    kc4 = kcmp[..., 0:KV_W].reshape(b_, n_cmp, N_KV, HEAD_DIM).transpose(0, 2, 1, 3).astype(BF16)
    kc_aug = jnp.concatenate(
        [kc4, jnp.broadcast_to(cmp_cols, (b_, N_KV) + cmp_cols.shape)], axis=-1)
    vcT = kcmp[..., KV_W:2 * KV_W].reshape(b_, n_cmp, N_KV, HEAD_DIM).transpose(0, 2, 3, 1).astype(BF16)
    s = jnp.arange(n_slc, dtype=I32)[:, None]
    cc = c[None, :]
    r_slc = L_SLC // STRIDE
    mt = (((cc >= r_slc * s) & (cc < r_slc * s + r_slc)).astype(F32)
          + ((cc + 1 >= r_slc * s) & (cc + 1 < r_slc * s + r_slc)).astype(F32))
    mt = jnp.where(cc < n_cmp - 1, mt, 0.0).astype(BF16)
    kv_spec = lambda a: pl.BlockSpec((1,) + a.shape[1:], lambda b, i: (b, 0, 0, 0))
    return pl.pallas_call(
        _attn_kernel,
        name="nsa_attention",
        grid=(b_, t_ // TQ),
        in_specs=[pl.BlockSpec((1, D_ATTN, TQ), lambda b, i: (b, 0, i)),
                  pl.BlockSpec((1, 32, TQ), lambda b, i: (b, 0, i)),
                  kv_spec(kc_aug), kv_spec(vcT), kv_spec(ks_aug), kv_spec(vsT4),
                  kv_spec(kw_aug), kv_spec(vwT4),
                  pl.BlockSpec(mt.shape, lambda b, i: (0, 0))],
        out_specs=pl.BlockSpec((1, D_ATTN, TQ), lambda b, i: (b, 0, i)),
        out_shape=jax.ShapeDtypeStruct((b_, D_ATTN, t_), BF16),
        scratch_shapes=[pltpu.VMEM((N_KV, 2, KT_SLC, HEADS_PER_KV * TQ), F32), pltpu.SMEM((32,), I32)],
        compiler_params=_cparams(("parallel", "arbitrary"), V7X_VMEM_LIMIT_BYTES),
    )(qT, gT, kc_aug, vcT, ks_aug, vsT4, kw_aug, vwT4, mt)


def _rglru_seg_pitch(tc):
    return tc // 8 + 8


def _rglru_kernel(xr_ref, yg_ref, cw_ref, cb_ref, wa_ref, ba_ref, wx_ref, bx_ref, lam_ref,
                  o_ref, tail_ref, h_ref, xs_scr, ys_scr, os_scr):
    tc = xr_ref.shape[1]

    @pl.when(pl.program_id(1) == 0)
    def _():
        tail_ref[...] = jnp.zeros_like(tail_ref)
        h_ref[...] = jnp.zeros_like(h_ref)

    n_v = tc // 8
    n_slab = xr_ref.shape[2] // V7X_LANES
    pitch = xs_scr.shape[1] // 8
    for c in range(n_slab):
        for s in range(8):
            lanes = slice(c * V7X_LANES, (c + 1) * V7X_LANES)
            xs_scr[c, s * pitch:s * pitch + n_v, :] = xr_ref[0, s * n_v:(s + 1) * n_v, lanes].astype(F32)
            ys_scr[c, s * pitch:s * pitch + n_v, :] = yg_ref[0, s * n_v:(s + 1) * n_v, lanes].astype(F32)

    def piece(scr, v):
        return jnp.concatenate([scr[c, pl.ds(v, 8, stride=pitch), :] for c in range(n_slab)], axis=1)

    x = [piece(xs_scr, v) for v in range(n_v)]
    row8 = lax.broadcasted_iota(I32, (8, x[0].shape[1]), 0)
    tail = tail_ref[...]
    before = {d: jnp.where(row8 == 0, tail[CONV_W - 1 - d:CONV_W - d, :], pltpu.roll(x[n_v - d], 1, axis=0))
              for d in range(1, CONV_W)}
    tail_ref[0:CONV_W - 1, :] = jnp.concatenate([x[n_v - d][7:8, :] for d in range(CONV_W - 1, 0, -1)], axis=0)
    cw = cw_ref[...]
    cb = cb_ref[...]
    xc = []
    for v in range(n_v):
        acc = x[v] * cw[CONV_W - 1:CONV_W, :] + cb
        for d in range(1, CONV_W):
            acc = acc + (x[v - d] if v >= d else before[d - v]) * cw[CONV_W - 1 - d:CONV_W - d, :]
        xc.append(acc)

    xb = jnp.concatenate(xc, axis=0).astype(BF16)
    r = _sigmoid(jnp.dot(xb, wa_ref[...], preferred_element_type=F32) + ba_ref[...])
    gi = _sigmoid(jnp.dot(xb, wx_ref[...], preferred_element_type=F32) + bx_ref[...])
    z = -lam_ref[...]
    softplus = jnp.maximum(z, 0.0) + jnp.log1p(jnp.exp(-jnp.abs(z)))
    log_a = (-C_LRU * r) * softplus
    a = jnp.exp(log_a)
    th = jnp.tanh(log_a)
    neg_expm1 = 2.0 * th / (th - 1.0)
    bb = jnp.sqrt(neg_expm1) * gi * jnp.concatenate(xc, axis=0)

    a_cum, b_cum = [a[0:8]], [bb[0:8]]
    for v in range(1, n_v):
        a_v = a[v * 8:(v + 1) * 8]
        b_cum.append(a_v * b_cum[-1] + bb[v * 8:(v + 1) * 8])
        a_cum.append(a_v * a_cum[-1])
    h_in = [h_ref[0:1, :]]
    for s in range(7):
        h_in.append(a_cum[-1][s:s + 1, :] * h_in[-1] + b_cum[-1][s:s + 1, :])
    h_ref[0:1, :] = a_cum[-1][7:8, :] * h_in[-1] + b_cum[-1][7:8, :]
    h_in = jnp.concatenate(h_in, axis=0)
    for v in range(n_v):
        out = (a_cum[v] * h_in + b_cum[v]) * _gelu_tanh(piece(ys_scr, v))
        for c in range(n_slab):
            os_scr[c, pl.ds(v, 8, stride=pitch), :] = out[:, c * V7X_LANES:(c + 1) * V7X_LANES]
    for s in range(8):
        o_ref[0, s * n_v:(s + 1) * n_v, :] = jnp.concatenate(
            [os_scr[c, s * pitch:s * pitch + n_v, :] for c in range(n_slab)], axis=1).astype(o_ref.dtype)


def _block_diag(w):
    n, c, d = w.shape
    return jnp.einsum('ncd,nm->ncmd', w, jnp.eye(n, dtype=w.dtype)).reshape(n * c, n * d)


def _rglru(xr, yg, conv_w, conv_b, wa, ba, wx, bx, lam):
    b_, t_, c_ = xr.shape
    tc = min(RGLRU_CHUNK, t_)
    wa_bd = _block_diag(wa).astype(BF16)
    wx_bd = _block_diag(wx).astype(BF16)
    vec = lambda v: v.reshape(1, c_)
    full = lambda a: pl.BlockSpec(a.shape, lambda b, i: (0,) * a.ndim)
    blk = pl.BlockSpec((1, tc, c_), lambda b, i: (b, i, 0))
    args = (xr, yg, conv_w, vec(conv_b), wa_bd, vec(ba), wx_bd, vec(bx), vec(lam))
    return pl.pallas_call(
        _rglru_kernel,
        name="rglru",
        grid=(b_, t_ // tc),
        in_specs=[blk, blk] + [full(a) for a in args[2:]],
        out_specs=blk,
        out_shape=jax.ShapeDtypeStruct((b_, t_, c_), BF16),
        scratch_shapes=[pltpu.VMEM((8, c_), F32), pltpu.VMEM((8, c_), F32)]
                       + [pltpu.VMEM((c_ // V7X_LANES, 8 * _rglru_seg_pitch(tc), V7X_LANES), F32)] * 3,
        compiler_params=_cparams(("parallel", "arbitrary"), V7X_VMEM_LIMIT_BYTES),
    )(*args)


def _out_proj_kernel(oaT_ref, orec_ref, x_ref, ga_ref, gr_ref, wa_ref, wr_ref, o_ref):
    oaT = oaT_ref[0].astype(F32)
    ms = jnp.mean(oaT * oaT, axis=0, keepdims=True)
    na = (oaT * lax.rsqrt(ms + RMS_EPS)).T * ga_ref[...]
    nr = _rms(orec_ref[0].astype(F32), gr_ref[...])
    y = (jnp.dot(na.astype(BF16), wa_ref[...], preferred_element_type=F32)
         + jnp.dot(nr.astype(BF16), wr_ref[...], preferred_element_type=F32))
    o_ref[0] = x_ref[0] + y


def _out_proj(oaT, orec, x, g_attn, g_rec, w_out):
    b_, t_, d_ = x.shape
    tm = min(PROJ_TILE, t_)
    wa = w_out[:D_ATTN].astype(BF16)
    wr = w_out[D_ATTN:].astype(BF16)
    ga = g_attn.reshape(1, -1)
    gr = g_rec.reshape(1, -1)
    full = lambda a: pl.BlockSpec(a.shape, lambda b, i: (0,) * a.ndim)
    return pl.pallas_call(
        _out_proj_kernel,
        name="out_proj",
        grid=(b_, t_ // tm),
        in_specs=[pl.BlockSpec((1, D_ATTN, tm), lambda b, i: (b, 0, i)),
                  pl.BlockSpec((1, tm, orec.shape[2]), lambda b, i: (b, i, 0)),
                  pl.BlockSpec((1, tm, d_), lambda b, i: (b, i, 0)),
                  full(ga), full(gr), full(wa), full(wr)],
        out_specs=pl.BlockSpec((1, tm, d_), lambda b, i: (b, i, 0)),
        out_shape=jax.ShapeDtypeStruct((b_, t_, d_), F32),
        compiler_params=_cparams(("parallel", "parallel"), V7X_VMEM_LIMIT_BYTES),
    )(oaT, orec, x, ga, gr, wa, wr)


def _ple_apply(x, p_ref, g_ref, wg_ref, wp_ref, fg_ref, *, final):
    gate = _sigmoid(jnp.dot(_rms(x, g_ref[...]).astype(BF16), wg_ref[...], preferred_element_type=F32))
    proj = jnp.dot(p_ref[...].astype(BF16), wp_ref[...], preferred_element_type=F32)
    y = x + gate * proj
    return _rms(y, fg_ref[...]) if final else y


def _ple_operands(p, norm_g, w_gate, w_proj, final_g, rows, row_map, const_map):
    d_ = w_gate.shape[0]
    arrays = (p, norm_g.reshape(1, d_), w_gate.astype(BF16), w_proj.astype(BF16), final_g.reshape(1, d_))
    once = dict(pipeline_mode=pl.Buffered(1))
    specs = [pl.BlockSpec((rows, p.shape[1]), row_map)] + [pl.BlockSpec(a.shape, const_map, **once)
                                                            for a in arrays[1:]]
    return arrays, specs


def _ffn_kernel(be_ref, x_ref, g_ref, wg_ref, wu_ref, wd_ref, *rest, dense, final):
    ple_refs, (o_ref, hn_ref, acc_ref) = rest[:-3], rest[-3:]
    f = pl.program_id(1)
    used = True if dense else pl.program_id(0) < be_ref[pl.num_programs(0)]

    @pl.when(f == 0)
    def _():
        if dense:
            x = x_ref[...]
            hn_ref[...] = _rms(x, g_ref[...]).astype(BF16)
            acc_ref[...] = x
        else:
            hn_ref[...] = _unpack_bf16_pairs(x_ref[...])
            acc_ref[...] = jnp.zeros_like(acc_ref)

    def accumulate():
        hn = hn_ref[...]
        gate = jnp.dot(hn, wg_ref[0], preferred_element_type=F32)
        up = jnp.dot(hn, wu_ref[0], preferred_element_type=F32)
        act = (_silu(gate) * up).astype(BF16)
        acc_ref[...] += jnp.dot(act, wd_ref[0], preferred_element_type=F32)

    if dense:
        accumulate()
    else:
        pl.when(used)(accumulate)

    @pl.when(f == pl.num_programs(1) - 1)
    def _():
        if dense:
            o_ref[...] = _ple_apply(acc_ref[...], *ple_refs, final=final)
        else:
            o_ref[...] = acc_ref[...].astype(o_ref.dtype)


def _ffn(x, norm_g, wg, wu, wd, block_e, n_used=None, *, dense, tf, ple=None, final=False):
    n = x.shape[0]
    d_ = wg.shape[1]
    tm = min(ROW_TILE if dense else MOE_ROWS, n)
    ff = wg.shape[2]
    assert ff % tf == 0 and n % tm == 0
    if block_e is None:
        block_e = jnp.zeros((n // tm,), I32)
    if n_used is None:
        n_used = jnp.int32(n // tm)
    n_blk, nf = n // tm, ff // tf
    be_all = jnp.concatenate([block_e, jnp.reshape(n_used, (1,)).astype(I32)])
    wmode = dict(pipeline_mode=pl.Buffered(1)) if (wg.shape[0] == 1 and ff == tf) else {}
    ftile = lambda i, f, be: jnp.where(i < be[n_blk], f, nf - 1)
    ple_arrays, ple_specs = ((), [])
    if dense:
        ple_arrays, ple_specs = _ple_operands(*ple, tm, lambda i, f, be: (i, 0), lambda i, f, be: (0, 0))
    grid_spec = pltpu.PrefetchScalarGridSpec(
        num_scalar_prefetch=1,
        grid=(n_blk, nf),
        in_specs=[pl.BlockSpec((tm, x.shape[1]), lambda i, f, be: (i, 0)),
                  pl.BlockSpec((1, d_), lambda i, f, be: (0, 0)),
                  pl.BlockSpec((1, d_, tf), lambda i, f, be: (be[i], 0, ftile(i, f, be)), **wmode),
                  pl.BlockSpec((1, d_, tf), lambda i, f, be: (be[i], 0, ftile(i, f, be)), **wmode),
                  pl.BlockSpec((1, tf, d_), lambda i, f, be: (be[i], ftile(i, f, be), 0), **wmode)] + ple_specs,
        out_specs=pl.BlockSpec((tm, d_), lambda i, f, be: (i, 0)),
        scratch_shapes=[pltpu.VMEM((tm, d_), BF16), pltpu.VMEM((tm, d_), F32)],
    )
    return pl.pallas_call(
        functools.partial(_ffn_kernel, dense=dense, final=final),
        name="ffn_dense" if dense else "ffn_expert",
        grid_spec=grid_spec,
        out_shape=jax.ShapeDtypeStruct((n, d_), F32 if dense else BF16),
        compiler_params=_cparams(("parallel", "arbitrary"), V7X_VMEM_LIMIT_BYTES),
    )(be_all, x, norm_g.reshape(1, d_), wg, wu, wd, *ple_arrays)


def _pack_bf16_pairs(x):
    w = x.shape[1] // 2
    bits = lax.bitcast_convert_type(x.astype(BF16).astype(F32), jnp.uint32)
    return (bits[:, :w] >> 16) | bits[:, w:]


def _unpack_bf16_pairs(words):
    lo = lax.bitcast_convert_type(words << 16, F32)
    hi = lax.bitcast_convert_type(words & jnp.uint32(0xFFFF0000), F32)
    return jnp.concatenate([lo, hi], axis=1).astype(BF16)


def _router_kernel(x_ref, g_ref, wr_ref, tri_ref, h_ref, e_ref, p_ref, r_ref, cb_ref, tot_ref, cnt_ref):
    @pl.when(pl.program_id(0) == 0)
    def _():
        cnt_ref[...] = jnp.zeros_like(cnt_ref)

    hn = _rms(x_ref[...], g_ref[...])
    h_ref[...] = _pack_bf16_pairs(hn)
    logits = lax.dot_general(wr_ref[...], hn, (((1,), (1,)), ((), ())),
                             precision=lax.Precision.HIGHEST, preferred_element_type=F32)
    eidx = lax.broadcasted_iota(I32, logits.shape, 0)
    m1 = jnp.max(logits, axis=0, keepdims=True)
    i1 = jnp.min(jnp.where(logits == m1, eidx, N_EXPERTS), axis=0, keepdims=True)
    rest = jnp.where(eidx == i1, -jnp.inf, logits)
    m2 = jnp.max(rest, axis=0, keepdims=True)
    i2 = jnp.min(jnp.where(rest == m2, eidx, N_EXPERTS), axis=0, keepdims=True)
    e2 = jnp.exp(m2 - m1)
    inv = 1.0 / (1.0 + e2)
    e_ref[...] = jnp.concatenate([i1, i2], axis=0)
    p_ref[...] = jnp.concatenate([inv, e2 * inv], axis=0)

    tm = logits.shape[1]
    chosen = jnp.where((eidx == i1) | (eidx == i2), 1.0, 0.0)
    incl = jnp.dot(chosen.astype(BF16), tri_ref[...], preferred_element_type=F32)
    before = incl - chosen + cnt_ref[:, 0:1]
    r_ref[...] = jnp.concatenate(
        [jnp.sum(jnp.where(eidx == i1, before, 0.0), axis=0, keepdims=True),
         jnp.sum(jnp.where(eidx == i2, before, 0.0), axis=0, keepdims=True)], axis=0).astype(I32)
    lane = lax.broadcasted_iota(I32, cb_ref.shape, 1)
    cb = jnp.zeros(cb_ref.shape, F32)
    for c in range(tm // CMB_ROWS):
        cb = jnp.where(lane == c, before[:, c * CMB_ROWS:c * CMB_ROWS + 1], cb)
    cb_ref[...] = cb.astype(I32)
    total = cnt_ref[...] + incl[:, tm - 1:tm]
    cnt_ref[...] = total
    tot_ref[...] = total.astype(I32)


def _router(x, norm_g, router_w):
    n, d_ = x.shape
    tm = min(ROW_TILE, n)
    assert tm % CMB_ROWS == 0
    wr = router_w.T
    tri = (jnp.arange(tm)[:, None] <= jnp.arange(tm)[None, :]).astype(BF16)
    h, top_e, top_p, rank, cb, tot = pl.pallas_call(
        _router_kernel,
        name="router",
        grid=(n // tm,),
        in_specs=[pl.BlockSpec((tm, d_), lambda i: (i, 0)),
                  pl.BlockSpec((1, d_), lambda i: (0, 0)),
                  pl.BlockSpec(wr.shape, lambda i: (0, 0)),
                  pl.BlockSpec(tri.shape, lambda i: (0, 0))],
        out_specs=[pl.BlockSpec((tm, d_ // 2), lambda i: (i, 0)),
                   pl.BlockSpec((2, tm), lambda i: (0, i)),
                   pl.BlockSpec((2, tm), lambda i: (0, i)),
                   pl.BlockSpec((2, tm), lambda i: (0, i)),
                   pl.BlockSpec((N_EXPERTS, V7X_LANES), lambda i: (0, i)),
                   pl.BlockSpec((N_EXPERTS, V7X_LANES), lambda i: (0, 0))],
        out_shape=[jax.ShapeDtypeStruct((n, d_ // 2), jnp.uint32),
                   jax.ShapeDtypeStruct((2, n), I32),
                   jax.ShapeDtypeStruct((2, n), F32),
                   jax.ShapeDtypeStruct((2, n), I32),
                   jax.ShapeDtypeStruct((N_EXPERTS, n // tm * V7X_LANES), I32),
                   jax.ShapeDtypeStruct((N_EXPERTS, V7X_LANES), I32)],
        scratch_shapes=[pltpu.VMEM((N_EXPERTS, V7X_LANES), F32)],
        compiler_params=_cparams(("arbitrary",), V7X_VMEM_LIMIT_BYTES),
    )(x, norm_g.reshape(1, d_), wr, tri)
    cb = cb.reshape(N_EXPERTS, n // tm, V7X_LANES)[:, :, :tm // CMB_ROWS].reshape(N_EXPERTS, n // CMB_ROWS).T
    return h, top_e, top_p, rank, cb, tot[:, 0]


SCATTER_UNROLL = 16


def _scatter_kernel(dest_ref, h_ref, zeros_ref, xs_ref, sem):
    del zeros_ref
    n_tok = h_ref.shape[0]

    def row_copy(t, k):
        return pltpu.make_async_copy(h_ref.at[pl.ds(t, 1), :], xs_ref.at[pl.ds(dest_ref[k, t], 1), :], sem)

    def issue(u, carry):
        for r in range(SCATTER_UNROLL):
            t = u * SCATTER_UNROLL + r
            row_copy(t, 0).start(priority=0)
            row_copy(t, 1).start(priority=1)
        return carry

    lax.fori_loop(0, n_tok // SCATTER_UNROLL, issue, 0)
    for _ in range(2):
        pltpu.make_async_copy(h_ref, xs_ref.at[pl.ds(0, n_tok), :], sem).wait()


def _scatter(h, dest, n_slots):
    n, w = h.shape
    tc = min(MOE_TOK_CHUNK, n)
    zeros = jnp.zeros((n_slots, w), h.dtype)
    return pl.pallas_call(
        _scatter_kernel,
        name="moe_scatter",
        grid=(n // tc,),
        in_specs=[pl.BlockSpec((2, tc), lambda c: (0, c), memory_space=pltpu.SMEM),
                  pl.BlockSpec((tc, w), lambda c: (c, 0)),
                  pl.BlockSpec(memory_space=pl.ANY)],
        out_specs=pl.BlockSpec(memory_space=pl.ANY),
        out_shape=jax.ShapeDtypeStruct((n_slots, w), h.dtype),
        scratch_shapes=[pltpu.SemaphoreType.DMA(())],
        input_output_aliases={2: 0},
        compiler_params=_cparams(("arbitrary",), V7X_VMEM_LIMIT_BYTES),
    )(dest, h, zeros)


N_CMB_OPS = 2 * N_EXPERTS


def _combine_kernel(jt_ref, vt_ref, x_ref, dest_ref, gate_ref, *rest, final):
    yb_refs = rest[:N_CMB_OPS]
    ple_refs = rest[N_CMB_OPS:-1]
    o_ref = rest[-1]
    c = pl.program_id(0)
    dest = dest_ref[...]
    gate = gate_ref[...]
    lane = lax.broadcasted_iota(I32, (dest.shape[0], CMB_ROWS), 1)
    acc = x_ref[...]
    for k in range(N_CMB_OPS):
        base = jnp.where(vt_ref[c * N_CMB_OPS + k] == 1, jt_ref[c * N_CMB_OPS + k] * CMB_ROWS, -2 * CMB_ROWS)
        rel = dest - base
        w = (jnp.where(lane == rel[:, 0:1], gate[:, 0:1], 0.0)
             + jnp.where(lane == rel[:, 1:2], gate[:, 1:2], 0.0)).astype(BF16)
        acc = acc + jnp.dot(w, yb_refs[k][...], preferred_element_type=F32)
    o_ref[...] = _ple_apply(acc, *ple_refs, final=final)


def _combine(x, yb, dest_t, gate_t, jt, vt, ple, final):
    n, d_ = x.shape
    ple_arrays, ple_specs = _ple_operands(*ple, CMB_ROWS, lambda c, jt, vt: (c, 0), lambda c, jt, vt: (0, 0))
    yb_spec = lambda k: pl.BlockSpec((CMB_ROWS, d_), lambda c, jt, vt, k=k: (jt[c * N_CMB_OPS + k], 0))
    grid_spec = pltpu.PrefetchScalarGridSpec(
        num_scalar_prefetch=2,
        grid=(n // CMB_ROWS,),
        in_specs=[pl.BlockSpec((CMB_ROWS, d_), lambda c, jt, vt: (c, 0)),
                  pl.BlockSpec((CMB_ROWS, 2), lambda c, jt, vt: (c, 0)),
                  pl.BlockSpec((CMB_ROWS, 2), lambda c, jt, vt: (c, 0))]
                 + [yb_spec(k) for k in range(N_CMB_OPS)] + ple_specs,
        out_specs=pl.BlockSpec((CMB_ROWS, d_), lambda c, jt, vt: (c, 0)),
    )
    return pl.pallas_call(
        functools.partial(_combine_kernel, final=final),
        name="moe_combine",
        grid_spec=grid_spec,
        out_shape=jax.ShapeDtypeStruct((n, d_), F32),
        compiler_params=_cparams(("parallel",), V7X_VMEM_LIMIT_BYTES),
    )(jt, vt, x, dest_t, gate_t, *([yb] * N_CMB_OPS), *ple_arrays)


def _moe_plan(top_e, rank, cb, counts, n_slots):
    padded = (counts + MOE_ROWS - 1) // MOE_ROWS * MOE_ROWS
    pad_ends = jnp.cumsum(padded)
    pad_starts = pad_ends - padded
    start_of = sum(jnp.where(top_e == e, pad_starts[e], 0) for e in range(N_EXPERTS))
    dest = (start_of + rank).astype(I32)
    n_blk = n_slots // MOE_ROWS
    blk_start = jnp.arange(n_blk, dtype=I32) * MOE_ROWS
    block_e = jnp.minimum(jnp.sum((pad_ends[None, :] <= blk_start[:, None]).astype(I32), axis=1),
                          N_EXPERTS - 1).astype(I32)

    cb = jnp.concatenate([cb, counts[None]], axis=0)
    lo = pad_starts[None] + cb[:-1]
    hi = pad_starts[None] + cb[1:]
    ja = lo // CMB_ROWS
    jb = (hi - 1) // CMB_ROWS
    va = hi > lo
    vb = va & (jb > ja)
    jt = jnp.stack([ja, jb], axis=-1).reshape(-1)
    vt = jnp.stack([va, vb], axis=-1).reshape(-1)
    jt = jnp.where(vt, jt, 0).astype(I32)
    return dest, block_e, pad_ends[-1] // MOE_ROWS, jt, vt.astype(I32)


def _moe(x, norm_g, router_w, wg, wu, wd, ple, final):
    n, d_ = x.shape
    n_slots = (2 * n // MOE_ROWS + N_EXPERTS) * MOE_ROWS
    h, top_e, top_p, rank, cb, counts = _router(x, norm_g, router_w)
    dest, block_e, n_used, jt, vt = _moe_plan(top_e, rank, cb, counts, n_slots)
    xs = _scatter(h, dest, n_slots)
    yb = _ffn(xs, norm_g, wg, wu, wd, block_e, n_used, dense=False, tf=wg.shape[2] // 2)
    return _combine(x, yb, dest.T, top_p.T, jt, vt, ple, final)


def kernel(x, p, attn_norm, w_in, cmp_pos, cmp_w1, cmp_w2, conv_w, conv_b, lru_wa, lru_ba, lru_wx, lru_bx,
           lru_lambda, out_norm_attn, out_norm_rec, w_out, ffn_norm, dense_w_gate, dense_w_up, dense_w_down,
           router_w, moe_w_gate, moe_w_up, moe_w_down, ple_norm, ple_w_gate, ple_w_proj, final_norm):
    b_, t_, d_ = x.shape
    depth = w_in.shape[0]
    n = b_ * t_
    slc_cols, win_cols = _key_position_columns(t_)
    for i in range(depth):
        kcvc, ks, kw, xr, yg, qT, vsT, vwT, gT = _proj_in(x, attn_norm[i], w_in[i], slc_cols, win_cols)
        kcmp = _compress(kcvc, cmp_pos[i], cmp_w1[i], cmp_w2[i])
        oaT = _attention(qT, gT, kcmp, ks, vsT, kw, vwT)
        orec = _rglru(xr, yg, conv_w[i], conv_b[i], lru_wa[i], lru_ba[i], lru_wx[i], lru_bx[i], lru_lambda[i])
        x = _out_proj(oaT, orec, x, out_norm_attn[i], out_norm_rec[i], w_out[i])
        x2 = x.reshape(n, d_)
        j = i // 2
        ple = (p[i].reshape(n, -1), ple_norm[i], ple_w_gate[i], ple_w_proj[i], final_norm)
        final = i == depth - 1
        if i % 2 == 0:
            x2 = _ffn(x2, ffn_norm[i], dense_w_gate[j][None].astype(BF16), dense_w_up[j][None].astype(BF16),
                      dense_w_down[j][None].astype(BF16), None, dense=True, tf=dense_w_gate.shape[2],
                      ple=ple, final=final)
        else:
            x2 = _moe(x2, ffn_norm[i], router_w[j], moe_w_gate[j].astype(BF16), moe_w_up[j].astype(BF16),
                      moe_w_down[j].astype(BF16), ple, final)
        x = x2.reshape(b_, t_, d_)
    return x
```

```python
import functools

import jax
import jax.numpy as jnp
from jax import lax
from jax.experimental import pallas as pl
from jax.experimental.pallas import tpu as pltpu

F32 = jnp.float32
BF16 = jnp.bfloat16
I32 = jnp.int32

N_ATTN_HEADS = 8
HEAD_DIM = 64
N_KV = 2
HEADS_PER_KV = N_ATTN_HEADS // N_KV
D_ATTN = N_ATTN_HEADS * HEAD_DIM
KV_W = N_KV * HEAD_DIM
N_GATES = 3
L_CMP = 32
STRIDE = 16
L_SLC = 64
N_SEL = 16
N_LOCAL = 2
W_WIN = 512
CMP_HID = 2 * HEAD_DIM
CONV_W = 4
C_LRU = 8.0
N_EXPERTS = 8
RMS_EPS = 1e-6
ATTN_SCALE = HEAD_DIM ** -0.5
LOG2E = 1.4426950408889634
LOG2E_HI = 1.4453125
LOG2E_LO = LOG2E - LOG2E_HI

V7X_LANES = 128
V7X_VMEM_LIMIT_BYTES = 56 * 1024 * 1024

TQ = 128
KT_SLC = 256
SLC_UNROLL = 4
KT_WIN = 128
N_WIN_TILES = W_WIN // KT_WIN + 1
V_ROWS = HEAD_DIM + 16
MASK_NEG = -1e30
SEL_NEG = -float(2 ** 20)
ROW_TILE = 512
PROJ_TILE = 1024
RGLRU_CHUNK = 512
MOE_ROWS = 512
MOE_TOK_CHUNK = 512
CMB_ROWS = 256


def _cparams(semantics, vmem=None):
    return pltpu.CompilerParams(dimension_semantics=semantics, vmem_limit_bytes=vmem)


def _rms(x, g):
    ms = jnp.mean(x * x, axis=-1, keepdims=True)
    return x * lax.rsqrt(ms + RMS_EPS) * g


def _gelu_tanh(x):
    c = 0.7978845608028654
    return x * (0.5 * (1.0 + jnp.tanh(c * (x + 0.044715 * (x * x * x)))))


def _sigmoid(x):
    return 1.0 / (1.0 + jnp.exp(-x))


def _silu(x):
    return x * _sigmoid(x)


def _proj_in_kernel(x_ref, g_ref, wtok_ref, wtr_ref,
                    slc_ref, win_ref, kcvc_ref, ks_ref, kw_ref, xr_ref, yg_ref, qT_ref, vsT_ref, vwT_ref, gT_ref):
    hn = _rms(x_ref[0], g_ref[...]).astype(BF16)
    tok = jnp.dot(hn, wtok_ref[...], preferred_element_type=F32)
    kcvc_ref[0] = tok[:, 0:256].astype(BF16)
    xr_ref[0] = tok[:, 512:1024].astype(BF16)
    yg_ref[0] = tok[:, 1024:1536].astype(BF16)
    lane = lax.broadcasted_iota(I32, (tok.shape[0], V7X_LANES), 1)
    slc_c = slc_ref[...].astype(F32)
    win_c = win_ref[...].astype(F32)
    for g in range(N_KV):
        in_g = lambda k2: jnp.where(lane < HEAD_DIM, k2 if g == 0 else pltpu.roll(k2, HEAD_DIM, axis=1), 0.0)
        ks_ref[0, g] = jnp.concatenate([in_g(tok[:, 256:384]) + slc_c[:, :V7X_LANES], slc_c[:, V7X_LANES:]],
                                       axis=1).astype(BF16)
        kw_ref[0, g] = (in_g(tok[:, 384:512]) + win_c).astype(BF16)
    tr = lax.dot_general(wtr_ref[...], hn, (((1,), (1,)), ((), ())),
                         preferred_element_type=F32)
    qT_ref[0] = (tr[0:512] * (ATTN_SCALE * LOG2E)).astype(BF16)
    ones = jnp.ones((V_ROWS - HEAD_DIM, tr.shape[1]), BF16)
    for g in range(N_KV):
        for v_ref, r0 in ((vsT_ref, 512), (vwT_ref, 640)):
            v_ref[0, g, 0:HEAD_DIM, :] = tr[r0 + g * HEAD_DIM:r0 + (g + 1) * HEAD_DIM].astype(BF16)
            v_ref[0, g, HEAD_DIM:V_ROWS, :] = ones
    gT_ref[0] = _sigmoid(tr[768:800])


def _key_position_columns(t_):
    pos = jnp.arange(t_, dtype=I32)
    dup = lambda vals: [v.astype(BF16)[:, None] for v in vals for _ in range(2)]
    pad_to = lambda cols, width: jnp.concatenate(
        cols + [jnp.zeros((t_, width - sum(c.shape[1] for c in cols)), BF16)], axis=1)
    lead = jnp.zeros((t_, HEAD_DIM), BF16)
    slc = pad_to([lead, jax.nn.one_hot(pos // L_SLC, t_ // L_SLC, dtype=BF16)]
                 + dup([pos // L_SLC, pos % L_SLC]), 2 * HEAD_DIM + t_ // L_SLC)
    win = pad_to([lead] + dup([pos % KT_WIN]), 2 * HEAD_DIM)
    return slc, win


def _proj_in(x, norm_g, w_in, slc_cols, win_cols):
    b_, t_, d_ = x.shape
    tm = min(PROJ_TILE, t_)
    q, kc, vc, ks, vs, kw, vw, g, xr, yg = jnp.split(
        w_in, [512, 640, 768, 896, 1024, 1152, 1280, 1304, 1816], axis=1)
    wtok = jnp.concatenate([kc, vc, ks, kw, xr, yg], axis=1).astype(BF16)
    g4 = g.reshape(d_, N_KV, HEADS_PER_KV, N_GATES).transpose(0, 1, 3, 2)
    g4 = jnp.pad(g4.reshape(d_, N_KV, 12), ((0, 0), (0, 0), (0, 4))).reshape(d_, 32)
    wtr = jnp.concatenate([q, vs, vw, g4], axis=1).T.astype(BF16)
    nt = t_ // tm
    row = lambda shape: pl.BlockSpec((1, tm, shape), lambda b, i: (b, i, 0))
    col = lambda shape: pl.BlockSpec((1, shape, tm), lambda b, i: (b, 0, i))
    full = lambda a: pl.BlockSpec(a.shape, lambda b, i: (0,) * a.ndim)
    g2 = norm_g.reshape(1, d_)
    ws, ww = slc_cols.shape[1], win_cols.shape[1]
    grow = lambda width: pl.BlockSpec((1, N_KV, tm, width), lambda b, i: (b, 0, i, 0))
    gcol = pl.BlockSpec((1, N_KV, V_ROWS, tm), lambda b, i: (b, 0, 0, i))
    outs = pl.pallas_call(
        _proj_in_kernel,
        name="proj_in",
        grid=(b_, nt),
        in_specs=[row(d_), full(g2), full(wtok), full(wtr),
                  pl.BlockSpec((tm, ws), lambda b, i: (i, 0)), pl.BlockSpec((tm, ww), lambda b, i: (i, 0))],
        out_specs=[row(256), grow(ws), grow(ww), row(512), row(512), col(512), gcol, gcol, col(32)],
        out_shape=[
            jax.ShapeDtypeStruct((b_, t_, 256), BF16),
            jax.ShapeDtypeStruct((b_, N_KV, t_, ws), BF16),
            jax.ShapeDtypeStruct((b_, N_KV, t_, ww), BF16),
            jax.ShapeDtypeStruct((b_, t_, 512), BF16),
            jax.ShapeDtypeStruct((b_, t_, 512), BF16),
            jax.ShapeDtypeStruct((b_, 512, t_), BF16),
            jax.ShapeDtypeStruct((b_, N_KV, V_ROWS, t_), BF16),
            jax.ShapeDtypeStruct((b_, N_KV, V_ROWS, t_), BF16),
            jax.ShapeDtypeStruct((b_, 32, t_), F32),
        ],
        compiler_params=_cparams(("parallel", "parallel"), V7X_VMEM_LIMIT_BYTES),
    )(x, g2, wtok, wtr, slc_cols, win_cols)
    return outs


def _compress_kernel(xc_ref, pa_ref, pb_ref, w1a_ref, w1b_ref, w2_ref, out_ref):
    xc = xc_ref[0].astype(BF16)
    n_chunk = xc.shape[0]
    a = jnp.dot(xc, w1a_ref[...], preferred_element_type=F32)
    bm = jnp.dot(xc, w1b_ref[...], preferred_element_type=F32)
    posc = (jnp.dot(pa_ref[...], w1a_ref[...], preferred_element_type=F32)
            + jnp.dot(pb_ref[...], w1b_ref[...], preferred_element_type=F32))[0:1]
    row = lax.broadcasted_iota(I32, bm.shape, 0)
    bm_up = jnp.where(row < n_chunk - 1, pltpu.roll(bm, n_chunk - 1, axis=0), 0.0)
    hid = _gelu_tanh(a + bm_up + posc).astype(BF16)
    out_ref[0] = jnp.dot(hid, w2_ref[...], preferred_element_type=F32)


def _compress(kcvc, cmp_pos, cmp_w1, cmp_w2):
    b_, t_, _ = kcvc.shape
    n_chunk = t_ // STRIDE
    xc = kcvc.reshape(b_, n_chunk, STRIDE * 256)
    w1 = cmp_w1.reshape(2, 2, STRIDE, HEAD_DIM, CMP_HID)

    def expand(half):
        w = w1[:, half]
        full = jnp.einsum('wldo,wx,gy->lxgdwyo', w, jnp.eye(2, dtype=F32), jnp.eye(2, dtype=F32))
        return full.reshape(STRIDE * 256, 4 * CMP_HID).astype(BF16)

    w1a, w1b = expand(0), expand(1)
    pos = cmp_pos.reshape(2, 2, STRIDE, HEAD_DIM)

    def posrow(half):
        p = jnp.broadcast_to(pos[:, half][:, None], (2, N_KV, STRIDE, HEAD_DIM))
        p = p.transpose(2, 0, 1, 3).reshape(1, STRIDE * 256)
        return jnp.pad(p, ((0, 7), (0, 0))).astype(BF16)

    pa, pb = posrow(0), posrow(1)
    w2 = jnp.einsum('whd,wx,gy->wghxyd', cmp_w2, jnp.eye(2, dtype=F32), jnp.eye(2, dtype=F32))
    w2 = w2.reshape(4 * CMP_HID, 4 * HEAD_DIM).astype(BF16)
    full = lambda a: pl.BlockSpec(a.shape, lambda b: (0,) * a.ndim)
    return pl.pallas_call(
        _compress_kernel,
        name="compress_kv",
        grid=(b_,),
        in_specs=[pl.BlockSpec((1, n_chunk, STRIDE * 256), lambda b: (b, 0, 0)),
                  full(pa), full(pb), full(w1a), full(w1b), full(w2)],
        out_specs=pl.BlockSpec((1, n_chunk, 256), lambda b: (b, 0, 0)),
        out_shape=jax.ShapeDtypeStruct((b_, n_chunk, 256), F32),
        compiler_params=_cparams(("parallel",), V7X_VMEM_LIMIT_BYTES),
    )(xc, pa, pb, w1a, w1b, w2)


def _head_lanes(rows):
    r = rows.shape[0] // HEADS_PER_KV
    return jnp.concatenate([rows[h * r:(h + 1) * r, :] for h in range(HEADS_PER_KV)], axis=1)


def _attn_kernel(qT_ref, gT_ref, kc_ref, vcT_ref, ks_ref, vsT_ref, kw_ref, vwT_ref, mt_ref, o_ref, s_scr,
                 tile_list):
    i = pl.program_id(1)
    n_cmp = kc_ref.shape[2]
    n_slc = mt_ref.shape[0]
    nl = HEADS_PER_KV * TQ
    hd_rows = HEADS_PER_KV * HEAD_DIM

    lane = lax.broadcasted_iota(I32, (1, nl), 1)
    t_loc = lane % TQ
    t_row = i * TQ + t_loc

    def extra_rows(n_rows, rows):
        ridx = lax.broadcasted_iota(I32, (16, nl), 0)
        out = jnp.zeros((16, nl), F32)
        for k, r in enumerate(rows):
            out = jnp.where(ridx == k, r, out)
        return jnp.concatenate([out.astype(BF16), jnp.zeros((n_rows - 16, nl), BF16)], axis=0)

    ok_c = (lax.broadcasted_iota(I32, (n_cmp, nl), 0) * STRIDE + (L_CMP - 1)) <= t_row
    r_idx = lax.broadcasted_iota(I32, (KT_WIN, nl), 0)
    s_idx = lax.broadcasted_iota(I32, (n_slc, TQ), 0)
    cur = (i * TQ + lax.broadcasted_iota(I32, (1, TQ), 1)) // L_SLC
    valid = s_idx <= cur
    forced = valid & ((s_idx == 0) | (s_idx > cur - N_LOCAL))
    s_idx_f = s_idx.astype(F32)
    n_full = (i * TQ) // KT_SLC
    mt = mt_ref[...]

    def pick(_, score):
        m = jnp.max(score, axis=0, keepdims=True)
        idx = jnp.min(jnp.where(score == m, s_idx_f, 1.0e9), axis=0, keepdims=True)
        return jnp.where(s_idx_f == idx, -2.0, score)

    def stage_a(g, bs, tile, slot, causal):
        k0 = pl.multiple_of(tile * KT_SLC, KT_SLC)
        s = jnp.dot(ks_ref[0, g, pl.ds(k0, KT_SLC), :], bs, preferred_element_type=F32)
        if causal:
            pos = k0 + lax.broadcasted_iota(I32, (KT_SLC, nl), 0)
            s = jnp.where(pos <= t_row, s, MASK_NEG)
        s_scr[g, slot] = s
        return jnp.max(s, axis=0, keepdims=True)

    def stage_b(g, tile, slot, m, acc, m_tile):
        k0 = pl.multiple_of(tile * KT_SLC, KT_SLC)
        m_new = jnp.maximum(m, m_tile)
        p = jnp.exp2(s_scr[g, slot] - m_new)
        acc = jnp.exp2(m - m_new) * acc + jnp.dot(vsT_ref[0, g, :, pl.ds(k0, KT_SLC)], p.astype(BF16),
                                                  preferred_element_type=F32)
        return m_new, acc

    groups = range(N_KV)

    def setup(g):
        q4 = _head_lanes(qT_ref[0, g * hd_rows:(g + 1) * hd_rows, :])
        head = g * HEADS_PER_KV + lane // TQ + 1
        slope = lax.bitcast_convert_type((127 - head) << 23, F32)
        return dict(q4=q4, slope=slope, s_hi=slope * LOG2E_HI, s_lo=slope * LOG2E_LO)

    st = [setup(g) for g in groups]

    def cmp_scores(g):
        s_hi, s_lo = st[g]["s_hi"], st[g]["s_lo"]
        bc = jnp.concatenate(
            [st[g]["q4"], extra_rows(64, [256.0 * s_hi, 256.0 * s_lo, 16.0 * s_hi, 16.0 * s_lo])], axis=0)
        return jnp.dot(kc_ref[0, g], bc, preferred_element_type=F32)

    def cmp_softmax(g, s_c):
        s_c = jnp.where(ok_c, s_c, MASK_NEG)
        m_c = jnp.max(s_c, axis=0, keepdims=True)
        e_c = jnp.exp2(s_c - m_c)
        l_c = jnp.sum(e_c, axis=0, keepdims=True)
        inv_c = jnp.where(m_c > 0.5 * MASK_NEG, 1.0 / jnp.maximum(l_c, 1e-30), 0.0)
        p_c = e_c * inv_c
        o_c = jnp.dot(vcT_ref[0, g], p_c.astype(BF16), preferred_element_type=F32)
        p_grp = (p_c[:, 0:TQ] + p_c[:, TQ:2 * TQ]) + (p_c[:, 2 * TQ:3 * TQ] + p_c[:, 3 * TQ:4 * TQ])
        p1 = p_grp.astype(BF16)
        r1 = p_grp - p1.astype(F32)
        p2 = r1.astype(BF16)
        p3 = (r1 - p2.astype(F32)).astype(BF16)
        p_slc = (jnp.dot(mt, p1, preferred_element_type=F32) + jnp.dot(mt, p2, preferred_element_type=F32)
                 + jnp.dot(mt, p3, preferred_element_type=F32))
        return o_c, p_slc

    def win_scores(g):
        slope = st[g]["slope"]
        bw = jnp.concatenate([st[g]["q4"], extra_rows(64, [st[g]["s_hi"], st[g]["s_lo"]])], axis=0)
        s_tiles, c_tiles = [], []
        for a in range(N_WIN_TILES):
            tile = i - (N_WIN_TILES - 1) + a
            k0 = pl.multiple_of(jnp.maximum(tile, 0) * KT_WIN, KT_WIN)
            s = jnp.dot(kw_ref[0, g, pl.ds(k0, KT_WIN), :], bw, preferred_element_type=F32)
            if a == 0:
                s = jnp.where(r_idx > t_loc, s, MASK_NEG)
            elif a == N_WIN_TILES - 1:
                s = jnp.where(r_idx <= t_loc, s, MASK_NEG)
            s_tiles.append(s)
            c_a = slope * (LOG2E * KT_WIN * (a - (N_WIN_TILES - 1)))
            c_tiles.append(jnp.where(tile >= 0, c_a, MASK_NEG))
        return s_tiles, c_tiles

    def win_softmax(g, s_tiles, c_tiles):
        m_w = s_tiles[0].max(axis=0, keepdims=True) + c_tiles[0]
        for s, c_a in zip(s_tiles[1:], c_tiles[1:]):
            m_w = jnp.maximum(m_w, s.max(axis=0, keepdims=True) + c_a)
        acc_w = jnp.zeros((V_ROWS, nl), F32)
        for a, (s, c_a) in enumerate(zip(s_tiles, c_tiles)):
            tile = i - (N_WIN_TILES - 1) + a
            k0 = pl.multiple_of(jnp.maximum(tile, 0) * KT_WIN, KT_WIN)
            p = jnp.exp2(s - (m_w - c_a))
            acc_w = acc_w + jnp.dot(vwT_ref[0, g, :, pl.ds(k0, KT_WIN)], p.astype(BF16),
                                    preferred_element_type=F32)
        return acc_w[0:HEAD_DIM] * (1.0 / acc_w[HEAD_DIM:HEAD_DIM + 1])

    s_cmp = [cmp_scores(g) for g in groups]
    s_win = [win_scores(g) for g in groups]
    cmp_out, scores = [], []
    for g in groups:
        cmp_out.append(cmp_softmax(g, s_cmp[g]))
        score0 = jnp.where(forced, -2.0, jnp.where(valid, cmp_out[g][1], -1.0))
        scores.append(lax.fori_loop(0, N_SEL - (N_LOCAL + 1), pick, score0, unroll=True))
    o_win = [win_softmax(g, *s_win[g]) for g in groups]
    score = jnp.concatenate(scores, axis=1)

    def select(g):
        sel = valid & (score[:, g * TQ:(g + 1) * TQ] < -1.5)
        sel4 = jnp.concatenate([sel] * HEADS_PER_KV, axis=1)
        blk_rows = jnp.where(sel4, 0.0, SEL_NEG).astype(BF16)
        s_hi, s_lo = st[g]["s_hi"], st[g]["s_lo"]
        bs = jnp.concatenate(
            [st[g]["q4"], blk_rows, extra_rows(64, [L_SLC * s_hi, L_SLC * s_lo, s_hi, s_lo])], axis=0)
        m_tile0 = stage_a(g, bs, n_full, 0, True)
        gt = gT_ref[0, g * 16:(g + 1) * 16, :]
        gates = [_head_lanes(gt[k * HEADS_PER_KV:(k + 1) * HEADS_PER_KV, :]) for k in range(N_GATES)]
        return dict(bs=bs, base=gates[0] * cmp_out[g][0] + gates[2] * o_win[g], g_s=gates[1], m_tile0=m_tile0)

    grp = [select(g) for g in groups]

    blocks_per_tile = KT_SLC // L_SLC
    n_tiles = n_slc // blocks_per_tile
    assert n_tiles <= 32
    valid2 = jnp.concatenate([valid] * N_KV, axis=1)
    sel_all = jnp.where(valid2 & (score < -1.5), 1.0, 0.0).astype(BF16)
    blk_cnt = lax.dot_general(jnp.ones((8, N_KV * TQ), BF16), sel_all, (((1,), (1,)), ((), ())),
                              preferred_element_type=F32)
    memb = jnp.where(lax.broadcasted_iota(I32, (n_slc, V7X_LANES), 0) // blocks_per_tile
                     == lax.broadcasted_iota(I32, (n_slc, V7X_LANES), 1), 1.0, 0.0).astype(BF16)
    tile_cnt = jnp.dot(jnp.where(blk_cnt > 0.5, 1.0, 0.0).astype(BF16), memb,
                       preferred_element_type=F32)[0:1]
    lane_t = lax.broadcasted_iota(I32, (1, V7X_LANES), 1)
    need = (tile_cnt > 0.5) & (lane_t < n_full)
    bit = lax.bitcast_convert_type(((lane_t & 15) + 127) << 23, F32)
    lo_bits = jnp.sum(jnp.where(need & (lane_t < 16), bit, 0.0)).astype(I32)
    hi_bits = jnp.sum(jnp.where(need & (lane_t >= 16), bit, 0.0)).astype(I32)
    n_sel = jnp.int32(0)
    for t in range(n_tiles - 1):
        tile_list[n_sel] = t
        n_sel = n_sel + (((lo_bits if t < 16 else hi_bits) >> (t % 16)) & 1)

    order = lambda k: jnp.where(k == 0, n_full, tile_list[jnp.maximum(k - 1, 0)])

    def half(carry, tile_b, slot_b, tile_a):
        m_next = [c[2] if tile_a is None else stage_a(g, grp[g]["bs"], tile_a, 1 - slot_b, False)
                  for g, c in enumerate(carry)]
        out = []
        for g in range(N_KV):
            m, acc, m_tile = carry[g]
            out.append(stage_b(g, tile_b, slot_b, m, acc, m_tile) + (m_next[g],))
        return tuple(out)

    def run(carry, k0, n, last):
        for j in range(n):
            nxt = None if (last and j == n - 1) else order(k0 + j + 1)
            carry = half(carry, order(k0 + j), j % 2, nxt)
        return carry

    init = tuple((jnp.full((1, nl), MASK_NEG, F32), jnp.zeros((V_ROWS, nl), F32), grp[g]["m_tile0"])
                 for g in range(N_KV))
    carry = lax.fori_loop(0, n_sel // SLC_UNROLL, lambda p, c: run(c, SLC_UNROLL * p, SLC_UNROLL, False), init)
    k_tail = SLC_UNROLL * (n_sel // SLC_UNROLL)
    carry = lax.switch(n_sel % SLC_UNROLL, [functools.partial(run, k0=k_tail, n=r + 1, last=True)
                                            for r in range(SLC_UNROLL)], carry)
    for g in range(N_KV):
        _, acc, _ = carry[g]
        o_s = acc[0:HEAD_DIM] * (1.0 / acc[HEAD_DIM:HEAD_DIM + 1])
        out = grp[g]["base"] + grp[g]["g_s"] * o_s
        for h in range(HEADS_PER_KV):
            r0 = g * hd_rows + h * HEAD_DIM
            o_ref[0, r0:r0 + HEAD_DIM, :] = out[:, h * TQ:(h + 1) * TQ].astype(o_ref.dtype)


def _attention(qT, gT, kcmp, ks_aug, vsT4, kw_aug, vwT4):
    b_, _, t_ = qT.shape
    n_cmp = t_ // STRIDE
    n_slc = t_ // L_SLC
    c = jnp.arange(n_cmp, dtype=I32)
    cmp_cols = jnp.concatenate([v.astype(BF16)[:, None] for v in (c // 16, c // 16, c % 16, c % 16)]
                               + [jnp.zeros((n_cmp, HEAD_DIM - 4), BF16)], axis=1)
    kc4 = kcmp[..., 0:KV_W].reshape(b_, n_cmp, N_KV, HEAD_DIM).transpose(0, 2, 1, 3).astype(BF16)
    kc_aug = jnp.concatenate(
        [kc4, jnp.broadcast_to(cmp_cols, (b_, N_KV) + cmp_cols.shape)], axis=-1)
    vcT = kcmp[..., KV_W:2 * KV_W].reshape(b_, n_cmp, N_KV, HEAD_DIM).transpose(0, 2, 3, 1).astype(BF16)
    s = jnp.arange(n_slc, dtype=I32)[:, None]
    cc = c[None, :]
    r_slc = L_SLC // STRIDE
    mt = (((cc >= r_slc * s) & (cc < r_slc * s + r_slc)).astype(F32)
          + ((cc + 1 >= r_slc * s) & (cc + 1 < r_slc * s + r_slc)).astype(F32))
    mt = jnp.where(cc < n_cmp - 1, mt, 0.0).astype(BF16)
    kv_spec = lambda a: pl.BlockSpec((1,) + a.shape[1:], lambda b, i: (b, 0, 0, 0))
    return pl.pallas_call(
        _attn_kernel,
        name="nsa_attention",
        grid=(b_, t_ // TQ),
        in_specs=[pl.BlockSpec((1, D_ATTN, TQ), lambda b, i: (b, 0, i)),
                  pl.BlockSpec((1, 32, TQ), lambda b, i: (b, 0, i)),
                  kv_spec(kc_aug), kv_spec(vcT), kv_spec(ks_aug), kv_spec(vsT4),
                  kv_spec(kw_aug), kv_spec(vwT4),
                  pl.BlockSpec(mt.shape, lambda b, i: (0, 0))],
        out_specs=pl.BlockSpec((1, D_ATTN, TQ), lambda b, i: (b, 0, i)),
        out_shape=jax.ShapeDtypeStruct((b_, D_ATTN, t_), BF16),
        scratch_shapes=[pltpu.VMEM((N_KV, 2, KT_SLC, HEADS_PER_KV * TQ), F32), pltpu.SMEM((32,), I32)],
        compiler_params=_cparams(("parallel", "arbitrary"), V7X_VMEM_LIMIT_BYTES),
    )(qT, gT, kc_aug, vcT, ks_aug, vsT4, kw_aug, vwT4, mt)


def _rglru_seg_pitch(tc):
    return tc // 8 + 8


def _rglru_kernel(xr_ref, yg_ref, cw_ref, cb_ref, wa_ref, ba_ref, wx_ref, bx_ref, lam_ref,
                  o_ref, tail_ref, h_ref, xs_scr, ys_scr, os_scr):
    tc = xr_ref.shape[1]

    @pl.when(pl.program_id(1) == 0)
    def _():
        tail_ref[...] = jnp.zeros_like(tail_ref)
        h_ref[...] = jnp.zeros_like(h_ref)

    n_v = tc // 8
    n_slab = xr_ref.shape[2] // V7X_LANES
    pitch = xs_scr.shape[1] // 8
    for c in range(n_slab):
        for s in range(8):
            lanes = slice(c * V7X_LANES, (c + 1) * V7X_LANES)
            xs_scr[c, s * pitch:s * pitch + n_v, :] = xr_ref[0, s * n_v:(s + 1) * n_v, lanes].astype(F32)
            ys_scr[c, s * pitch:s * pitch + n_v, :] = yg_ref[0, s * n_v:(s + 1) * n_v, lanes].astype(F32)

    def piece(scr, v):
        return jnp.concatenate([scr[c, pl.ds(v, 8, stride=pitch), :] for c in range(n_slab)], axis=1)

    x = [piece(xs_scr, v) for v in range(n_v)]
    row8 = lax.broadcasted_iota(I32, (8, x[0].shape[1]), 0)
    tail = tail_ref[...]
    before = {d: jnp.where(row8 == 0, tail[CONV_W - 1 - d:CONV_W - d, :], pltpu.roll(x[n_v - d], 1, axis=0))
              for d in range(1, CONV_W)}
    tail_ref[0:CONV_W - 1, :] = jnp.concatenate([x[n_v - d][7:8, :] for d in range(CONV_W - 1, 0, -1)], axis=0)
    cw = cw_ref[...]
    cb = cb_ref[...]
    xc = []
    for v in range(n_v):
        acc = x[v] * cw[CONV_W - 1:CONV_W, :] + cb
        for d in range(1, CONV_W):
            acc = acc + (x[v - d] if v >= d else before[d - v]) * cw[CONV_W - 1 - d:CONV_W - d, :]
        xc.append(acc)

    xb = jnp.concatenate(xc, axis=0).astype(BF16)
    r = _sigmoid(jnp.dot(xb, wa_ref[...], preferred_element_type=F32) + ba_ref[...])
    gi = _sigmoid(jnp.dot(xb, wx_ref[...], preferred_element_type=F32) + bx_ref[...])
    z = -lam_ref[...]
    softplus = jnp.maximum(z, 0.0) + jnp.log1p(jnp.exp(-jnp.abs(z)))
    log_a = (-C_LRU * r) * softplus
    a = jnp.exp(log_a)
    th = jnp.tanh(log_a)
    neg_expm1 = 2.0 * th / (th - 1.0)
    bb = jnp.sqrt(neg_expm1) * gi * jnp.concatenate(xc, axis=0)

    a_cum, b_cum = [a[0:8]], [bb[0:8]]
    for v in range(1, n_v):
        a_v = a[v * 8:(v + 1) * 8]
        b_cum.append(a_v * b_cum[-1] + bb[v * 8:(v + 1) * 8])
        a_cum.append(a_v * a_cum[-1])
    h_in = [h_ref[0:1, :]]
    for s in range(7):
        h_in.append(a_cum[-1][s:s + 1, :] * h_in[-1] + b_cum[-1][s:s + 1, :])
    h_ref[0:1, :] = a_cum[-1][7:8, :] * h_in[-1] + b_cum[-1][7:8, :]
    h_in = jnp.concatenate(h_in, axis=0)
    for v in range(n_v):
        out = (a_cum[v] * h_in + b_cum[v]) * _gelu_tanh(piece(ys_scr, v))
        for c in range(n_slab):
            os_scr[c, pl.ds(v, 8, stride=pitch), :] = out[:, c * V7X_LANES:(c + 1) * V7X_LANES]
    for s in range(8):
        o_ref[0, s * n_v:(s + 1) * n_v, :] = jnp.concatenate(
            [os_scr[c, s * pitch:s * pitch + n_v, :] for c in range(n_slab)], axis=1).astype(o_ref.dtype)


def _block_diag(w):
    n, c, d = w.shape
    return jnp.einsum('ncd,nm->ncmd', w, jnp.eye(n, dtype=w.dtype)).reshape(n * c, n * d)


def _rglru(xr, yg, conv_w, conv_b, wa, ba, wx, bx, lam):
    b_, t_, c_ = xr.shape
    tc = min(RGLRU_CHUNK, t_)
    wa_bd = _block_diag(wa).astype(BF16)
    wx_bd = _block_diag(wx).astype(BF16)
    vec = lambda v: v.reshape(1, c_)
    full = lambda a: pl.BlockSpec(a.shape, lambda b, i: (0,) * a.ndim)
    blk = pl.BlockSpec((1, tc, c_), lambda b, i: (b, i, 0))
    args = (xr, yg, conv_w, vec(conv_b), wa_bd, vec(ba), wx_bd, vec(bx), vec(lam))
    return pl.pallas_call(
        _rglru_kernel,
        name="rglru",
        grid=(b_, t_ // tc),
        in_specs=[blk, blk] + [full(a) for a in args[2:]],
        out_specs=blk,
        out_shape=jax.ShapeDtypeStruct((b_, t_, c_), BF16),
        scratch_shapes=[pltpu.VMEM((8, c_), F32), pltpu.VMEM((8, c_), F32)]
                       + [pltpu.VMEM((c_ // V7X_LANES, 8 * _rglru_seg_pitch(tc), V7X_LANES), F32)] * 3,
        compiler_params=_cparams(("parallel", "arbitrary"), V7X_VMEM_LIMIT_BYTES),
    )(*args)


def _out_proj_kernel(oaT_ref, orec_ref, x_ref, ga_ref, gr_ref, wa_ref, wr_ref, o_ref):
    oaT = oaT_ref[0].astype(F32)
    ms = jnp.mean(oaT * oaT, axis=0, keepdims=True)
    na = (oaT * lax.rsqrt(ms + RMS_EPS)).T * ga_ref[...]
    nr = _rms(orec_ref[0].astype(F32), gr_ref[...])
    y = (jnp.dot(na.astype(BF16), wa_ref[...], preferred_element_type=F32)
         + jnp.dot(nr.astype(BF16), wr_ref[...], preferred_element_type=F32))
    o_ref[0] = x_ref[0] + y


def _out_proj(oaT, orec, x, g_attn, g_rec, w_out):
    b_, t_, d_ = x.shape
    tm = min(PROJ_TILE, t_)
    wa = w_out[:D_ATTN].astype(BF16)
    wr = w_out[D_ATTN:].astype(BF16)
    ga = g_attn.reshape(1, -1)
    gr = g_rec.reshape(1, -1)
    full = lambda a: pl.BlockSpec(a.shape, lambda b, i: (0,) * a.ndim)
    return pl.pallas_call(
        _out_proj_kernel,
        name="out_proj",
        grid=(b_, t_ // tm),
        in_specs=[pl.BlockSpec((1, D_ATTN, tm), lambda b, i: (b, 0, i)),
                  pl.BlockSpec((1, tm, orec.shape[2]), lambda b, i: (b, i, 0)),
                  pl.BlockSpec((1, tm, d_), lambda b, i: (b, i, 0)),
                  full(ga), full(gr), full(wa), full(wr)],
        out_specs=pl.BlockSpec((1, tm, d_), lambda b, i: (b, i, 0)),
        out_shape=jax.ShapeDtypeStruct((b_, t_, d_), F32),
        compiler_params=_cparams(("parallel", "parallel"), V7X_VMEM_LIMIT_BYTES),
    )(oaT, orec, x, ga, gr, wa, wr)


def _ple_apply(x, p_ref, g_ref, wg_ref, wp_ref, fg_ref, *, final):
    gate = _sigmoid(jnp.dot(_rms(x, g_ref[...]).astype(BF16), wg_ref[...], preferred_element_type=F32))
    proj = jnp.dot(p_ref[...].astype(BF16), wp_ref[...], preferred_element_type=F32)
    y = x + gate * proj
    return _rms(y, fg_ref[...]) if final else y


def _ple_operands(p, norm_g, w_gate, w_proj, final_g, rows, row_map, const_map):
    d_ = w_gate.shape[0]
    arrays = (p, norm_g.reshape(1, d_), w_gate.astype(BF16), w_proj.astype(BF16), final_g.reshape(1, d_))
    once = dict(pipeline_mode=pl.Buffered(1))
    specs = [pl.BlockSpec((rows, p.shape[1]), row_map)] + [pl.BlockSpec(a.shape, const_map, **once)
                                                            for a in arrays[1:]]
    return arrays, specs


def _ffn_kernel(be_ref, x_ref, g_ref, wg_ref, wu_ref, wd_ref, *rest, dense, final):
    ple_refs, (o_ref, hn_ref, acc_ref) = rest[:-3], rest[-3:]
    f = pl.program_id(1)
    used = True if dense else pl.program_id(0) < be_ref[pl.num_programs(0)]

    @pl.when(f == 0)
    def _():
        if dense:
            x = x_ref[...]
            hn_ref[...] = _rms(x, g_ref[...]).astype(BF16)
            acc_ref[...] = x
        else:
            hn_ref[...] = _unpack_bf16_pairs(x_ref[...])
            acc_ref[...] = jnp.zeros_like(acc_ref)

    def accumulate():
        hn = hn_ref[...]
        gate = jnp.dot(hn, wg_ref[0], preferred_element_type=F32)
        up = jnp.dot(hn, wu_ref[0], preferred_element_type=F32)
        act = (_silu(gate) * up).astype(BF16)
        acc_ref[...] += jnp.dot(act, wd_ref[0], preferred_element_type=F32)

    if dense:
        accumulate()
    else:
        pl.when(used)(accumulate)

    @pl.when(f == pl.num_programs(1) - 1)
    def _():
        if dense:
            o_ref[...] = _ple_apply(acc_ref[...], *ple_refs, final=final)
        else:
            o_ref[...] = acc_ref[...].astype(o_ref.dtype)


def _ffn(x, norm_g, wg, wu, wd, block_e, n_used=None, *, dense, tf, ple=None, final=False):
    n = x.shape[0]
    d_ = wg.shape[1]
    tm = min(ROW_TILE if dense else MOE_ROWS, n)
    ff = wg.shape[2]
    assert ff % tf == 0 and n % tm == 0
    if block_e is None:
        block_e = jnp.zeros((n // tm,), I32)
    if n_used is None:
        n_used = jnp.int32(n // tm)
    n_blk, nf = n // tm, ff // tf
    be_all = jnp.concatenate([block_e, jnp.reshape(n_used, (1,)).astype(I32)])
    wmode = dict(pipeline_mode=pl.Buffered(1)) if (wg.shape[0] == 1 and ff == tf) else {}
    ftile = lambda i, f, be: jnp.where(i < be[n_blk], f, nf - 1)
    ple_arrays, ple_specs = ((), [])
    if dense:
        ple_arrays, ple_specs = _ple_operands(*ple, tm, lambda i, f, be: (i, 0), lambda i, f, be: (0, 0))
    grid_spec = pltpu.PrefetchScalarGridSpec(
        num_scalar_prefetch=1,
        grid=(n_blk, nf),
        in_specs=[pl.BlockSpec((tm, x.shape[1]), lambda i, f, be: (i, 0)),
                  pl.BlockSpec((1, d_), lambda i, f, be: (0, 0)),
                  pl.BlockSpec((1, d_, tf), lambda i, f, be: (be[i], 0, ftile(i, f, be)), **wmode),
                  pl.BlockSpec((1, d_, tf), lambda i, f, be: (be[i], 0, ftile(i, f, be)), **wmode),
                  pl.BlockSpec((1, tf, d_), lambda i, f, be: (be[i], ftile(i, f, be), 0), **wmode)] + ple_specs,
        out_specs=pl.BlockSpec((tm, d_), lambda i, f, be: (i, 0)),
        scratch_shapes=[pltpu.VMEM((tm, d_), BF16), pltpu.VMEM((tm, d_), F32)],
    )
    return pl.pallas_call(
        functools.partial(_ffn_kernel, dense=dense, final=final),
        name="ffn_dense" if dense else "ffn_expert",
        grid_spec=grid_spec,
        out_shape=jax.ShapeDtypeStruct((n, d_), F32 if dense else BF16),
        compiler_params=_cparams(("parallel", "arbitrary"), V7X_VMEM_LIMIT_BYTES),
    )(be_all, x, norm_g.reshape(1, d_), wg, wu, wd, *ple_arrays)


def _pack_bf16_pairs(x):
    w = x.shape[1] // 2
    bits = lax.bitcast_convert_type(x.astype(BF16).astype(F32), jnp.uint32)
    return (bits[:, :w] >> 16) | bits[:, w:]


def _unpack_bf16_pairs(words):
    lo = lax.bitcast_convert_type(words << 16, F32)
    hi = lax.bitcast_convert_type(words & jnp.uint32(0xFFFF0000), F32)
    return jnp.concatenate([lo, hi], axis=1).astype(BF16)


def _router_kernel(x_ref, g_ref, wr_ref, tri_ref, h_ref, e_ref, p_ref, r_ref, cb_ref, tot_ref, cnt_ref):
    @pl.when(pl.program_id(0) == 0)
    def _():
        cnt_ref[...] = jnp.zeros_like(cnt_ref)

    hn = _rms(x_ref[...], g_ref[...])
    h_ref[...] = _pack_bf16_pairs(hn)
    logits = lax.dot_general(wr_ref[...], hn, (((1,), (1,)), ((), ())),
                             precision=lax.Precision.HIGHEST, preferred_element_type=F32)
    eidx = lax.broadcasted_iota(I32, logits.shape, 0)
    m1 = jnp.max(logits, axis=0, keepdims=True)
    i1 = jnp.min(jnp.where(logits == m1, eidx, N_EXPERTS), axis=0, keepdims=True)
    rest = jnp.where(eidx == i1, -jnp.inf, logits)
    m2 = jnp.max(rest, axis=0, keepdims=True)
    i2 = jnp.min(jnp.where(rest == m2, eidx, N_EXPERTS), axis=0, keepdims=True)
    e2 = jnp.exp(m2 - m1)
    inv = 1.0 / (1.0 + e2)
    e_ref[...] = jnp.concatenate([i1, i2], axis=0)
    p_ref[...] = jnp.concatenate([inv, e2 * inv], axis=0)

    tm = logits.shape[1]
    chosen = jnp.where((eidx == i1) | (eidx == i2), 1.0, 0.0)
    incl = jnp.dot(chosen.astype(BF16), tri_ref[...], preferred_element_type=F32)
    before = incl - chosen + cnt_ref[:, 0:1]
    r_ref[...] = jnp.concatenate(
        [jnp.sum(jnp.where(eidx == i1, before, 0.0), axis=0, keepdims=True),
         jnp.sum(jnp.where(eidx == i2, before, 0.0), axis=0, keepdims=True)], axis=0).astype(I32)
    lane = lax.broadcasted_iota(I32, cb_ref.shape, 1)
    cb = jnp.zeros(cb_ref.shape, F32)
    for c in range(tm // CMB_ROWS):
        cb = jnp.where(lane == c, before[:, c * CMB_ROWS:c * CMB_ROWS + 1], cb)
    cb_ref[...] = cb.astype(I32)
    total = cnt_ref[...] + incl[:, tm - 1:tm]
    cnt_ref[...] = total
    tot_ref[...] = total.astype(I32)


def _router(x, norm_g, router_w):
    n, d_ = x.shape
    tm = min(ROW_TILE, n)
    assert tm % CMB_ROWS == 0
    wr = router_w.T
    tri = (jnp.arange(tm)[:, None] <= jnp.arange(tm)[None, :]).astype(BF16)
    h, top_e, top_p, rank, cb, tot = pl.pallas_call(
        _router_kernel,
        name="router",
        grid=(n // tm,),
        in_specs=[pl.BlockSpec((tm, d_), lambda i: (i, 0)),
                  pl.BlockSpec((1, d_), lambda i: (0, 0)),
                  pl.BlockSpec(wr.shape, lambda i: (0, 0)),
                  pl.BlockSpec(tri.shape, lambda i: (0, 0))],
        out_specs=[pl.BlockSpec((tm, d_ // 2), lambda i: (i, 0)),
                   pl.BlockSpec((2, tm), lambda i: (0, i)),
                   pl.BlockSpec((2, tm), lambda i: (0, i)),
                   pl.BlockSpec((2, tm), lambda i: (0, i)),
                   pl.BlockSpec((N_EXPERTS, V7X_LANES), lambda i: (0, i)),
                   pl.BlockSpec((N_EXPERTS, V7X_LANES), lambda i: (0, 0))],
        out_shape=[jax.ShapeDtypeStruct((n, d_ // 2), jnp.uint32),
                   jax.ShapeDtypeStruct((2, n), I32),
                   jax.ShapeDtypeStruct((2, n), F32),
                   jax.ShapeDtypeStruct((2, n), I32),
                   jax.ShapeDtypeStruct((N_EXPERTS, n // tm * V7X_LANES), I32),
                   jax.ShapeDtypeStruct((N_EXPERTS, V7X_LANES), I32)],
        scratch_shapes=[pltpu.VMEM((N_EXPERTS, V7X_LANES), F32)],
        compiler_params=_cparams(("arbitrary",), V7X_VMEM_LIMIT_BYTES),
    )(x, norm_g.reshape(1, d_), wr, tri)
    cb = cb.reshape(N_EXPERTS, n // tm, V7X_LANES)[:, :, :tm // CMB_ROWS].reshape(N_EXPERTS, n // CMB_ROWS).T
    return h, top_e, top_p, rank, cb, tot[:, 0]


SCATTER_UNROLL = 16


def _scatter_kernel(dest_ref, h_ref, zeros_ref, xs_ref, sem):
    del zeros_ref
    n_tok = h_ref.shape[0]

    def row_copy(t, k):
        return pltpu.make_async_copy(h_ref.at[pl.ds(t, 1), :], xs_ref.at[pl.ds(dest_ref[k, t], 1), :], sem)

    def issue(u, carry):
        for r in range(SCATTER_UNROLL):
            t = u * SCATTER_UNROLL + r
            row_copy(t, 0).start(priority=0)
            row_copy(t, 1).start(priority=1)
        return carry

    lax.fori_loop(0, n_tok // SCATTER_UNROLL, issue, 0)
    for _ in range(2):
        pltpu.make_async_copy(h_ref, xs_ref.at[pl.ds(0, n_tok), :], sem).wait()


def _scatter(h, dest, n_slots):
    n, w = h.shape
    tc = min(MOE_TOK_CHUNK, n)
    zeros = jnp.zeros((n_slots, w), h.dtype)
    return pl.pallas_call(
        _scatter_kernel,
        name="moe_scatter",
        grid=(n // tc,),
        in_specs=[pl.BlockSpec((2, tc), lambda c: (0, c), memory_space=pltpu.SMEM),
                  pl.BlockSpec((tc, w), lambda c: (c, 0)),
                  pl.BlockSpec(memory_space=pl.ANY)],
        out_specs=pl.BlockSpec(memory_space=pl.ANY),
        out_shape=jax.ShapeDtypeStruct((n_slots, w), h.dtype),
        scratch_shapes=[pltpu.SemaphoreType.DMA(())],
        input_output_aliases={2: 0},
        compiler_params=_cparams(("arbitrary",), V7X_VMEM_LIMIT_BYTES),
    )(dest, h, zeros)


N_CMB_OPS = 2 * N_EXPERTS


def _combine_kernel(jt_ref, vt_ref, x_ref, dest_ref, gate_ref, *rest, final):
    yb_refs = rest[:N_CMB_OPS]
    ple_refs = rest[N_CMB_OPS:-1]
    o_ref = rest[-1]
    c = pl.program_id(0)
    dest = dest_ref[...]
    gate = gate_ref[...]
    lane = lax.broadcasted_iota(I32, (dest.shape[0], CMB_ROWS), 1)
    acc = x_ref[...]
    for k in range(N_CMB_OPS):
        base = jnp.where(vt_ref[c * N_CMB_OPS + k] == 1, jt_ref[c * N_CMB_OPS + k] * CMB_ROWS, -2 * CMB_ROWS)
        rel = dest - base
        w = (jnp.where(lane == rel[:, 0:1], gate[:, 0:1], 0.0)
             + jnp.where(lane == rel[:, 1:2], gate[:, 1:2], 0.0)).astype(BF16)
        acc = acc + jnp.dot(w, yb_refs[k][...], preferred_element_type=F32)
    o_ref[...] = _ple_apply(acc, *ple_refs, final=final)


def _combine(x, yb, dest_t, gate_t, jt, vt, ple, final):
    n, d_ = x.shape
    ple_arrays, ple_specs = _ple_operands(*ple, CMB_ROWS, lambda c, jt, vt: (c, 0), lambda c, jt, vt: (0, 0))
    yb_spec = lambda k: pl.BlockSpec((CMB_ROWS, d_), lambda c, jt, vt, k=k: (jt[c * N_CMB_OPS + k], 0))
    grid_spec = pltpu.PrefetchScalarGridSpec(
        num_scalar_prefetch=2,
        grid=(n // CMB_ROWS,),
        in_specs=[pl.BlockSpec((CMB_ROWS, d_), lambda c, jt, vt: (c, 0)),
                  pl.BlockSpec((CMB_ROWS, 2), lambda c, jt, vt: (c, 0)),
                  pl.BlockSpec((CMB_ROWS, 2), lambda c, jt, vt: (c, 0))]
                 + [yb_spec(k) for k in range(N_CMB_OPS)] + ple_specs,
        out_specs=pl.BlockSpec((CMB_ROWS, d_), lambda c, jt, vt: (c, 0)),
    )
    return pl.pallas_call(
        functools.partial(_combine_kernel, final=final),
        name="moe_combine",
        grid_spec=grid_spec,
        out_shape=jax.ShapeDtypeStruct((n, d_), F32),
        compiler_params=_cparams(("parallel",), V7X_VMEM_LIMIT_BYTES),
    )(jt, vt, x, dest_t, gate_t, *([yb] * N_CMB_OPS), *ple_arrays)


def _moe_plan(top_e, rank, cb, counts, n_slots):
    padded = (counts + MOE_ROWS - 1) // MOE_ROWS * MOE_ROWS
    pad_ends = jnp.cumsum(padded)
    pad_starts = pad_ends - padded
    start_of = sum(jnp.where(top_e == e, pad_starts[e], 0) for e in range(N_EXPERTS))
    dest = (start_of + rank).astype(I32)
    n_blk = n_slots // MOE_ROWS
    blk_start = jnp.arange(n_blk, dtype=I32) * MOE_ROWS
    block_e = jnp.minimum(jnp.sum((pad_ends[None, :] <= blk_start[:, None]).astype(I32), axis=1),
                          N_EXPERTS - 1).astype(I32)

    cb = jnp.concatenate([cb, counts[None]], axis=0)
    lo = pad_starts[None] + cb[:-1]
    hi = pad_starts[None] + cb[1:]
    ja = lo // CMB_ROWS
    jb = (hi - 1) // CMB_ROWS
    va = hi > lo
    vb = va & (jb > ja)
    jt = jnp.stack([ja, jb], axis=-1).reshape(-1)
    vt = jnp.stack([va, vb], axis=-1).reshape(-1)
    jt = jnp.where(vt, jt, 0).astype(I32)
    return dest, block_e, pad_ends[-1] // MOE_ROWS, jt, vt.astype(I32)


def _moe(x, norm_g, router_w, wg, wu, wd, ple, final):
    n, d_ = x.shape
    n_slots = (2 * n // MOE_ROWS + N_EXPERTS) * MOE_ROWS
    h, top_e, top_p, rank, cb, counts = _router(x, norm_g, router_w)
    dest, block_e, n_used, jt, vt = _moe_plan(top_e, rank, cb, counts, n_slots)
    xs = _scatter(h, dest, n_slots)
    yb = _ffn(xs, norm_g, wg, wu, wd, block_e, n_used, dense=False, tf=wg.shape[2] // 2)
    return _combine(x, yb, dest.T, top_p.T, jt, vt, ple, final)


def kernel(x, p, attn_norm, w_in, cmp_pos, cmp_w1, cmp_w2, conv_w, conv_b, lru_wa, lru_ba, lru_wx, lru_bx,
           lru_lambda, out_norm_attn, out_norm_rec, w_out, ffn_norm, dense_w_gate, dense_w_up, dense_w_down,
           router_w, moe_w_gate, moe_w_up, moe_w_down, ple_norm, ple_w_gate, ple_w_proj, final_norm):
    b_, t_, d_ = x.shape
    depth = w_in.shape[0]
    n = b_ * t_
    slc_cols, win_cols = _key_position_columns(t_)
    for i in range(depth):
        kcvc, ks, kw, xr, yg, qT, vsT, vwT, gT = _proj_in(x, attn_norm[i], w_in[i], slc_cols, win_cols)
        kcmp = _compress(kcvc, cmp_pos[i], cmp_w1[i], cmp_w2[i])
        oaT = _attention(qT, gT, kcmp, ks, vsT, kw, vwT)
        orec = _rglru(xr, yg, conv_w[i], conv_b[i], lru_wa[i], lru_ba[i], lru_wx[i], lru_bx[i], lru_lambda[i])
        x = _out_proj(oaT, orec, x, out_norm_attn[i], out_norm_rec[i], w_out[i])
        x2 = x.reshape(n, d_)
        j = i // 2
        ple = (p[i].reshape(n, -1), ple_norm[i], ple_w_gate[i], ple_w_proj[i], final_norm)
        final = i == depth - 1
        if i % 2 == 0:
            x2 = _ffn(x2, ffn_norm[i], dense_w_gate[j][None].astype(BF16), dense_w_up[j][None].astype(BF16),
                      dense_w_down[j][None].astype(BF16), None, dense=True, tf=dense_w_gate.shape[2],
                      ple=ple, final=final)
        else:
            x2 = _moe(x2, ffn_norm[i], router_w[j], moe_w_gate[j].astype(BF16), moe_w_up[j].astype(BF16),
                      moe_w_down[j].astype(BF16), ple, final)
        x = x2.reshape(b_, t_, d_)
    return x
```
